```python
import math
import jax, jax.numpy as jnp
from jax import lax
import numpy as np

D_MODEL = 1024
BATCH = 4
SEQ = 4096
DEPTH = 4

GRID_W = 64
RET_HEADS = 4
RET_DK = 256
RET_DV = 512
RET_CHUNK = 128
ROPE_BASE = 10000.0
NA_HEADS = 16
NA_HD = 64
NA_KH = 8
NA_KW = 16
N_EXPERTS = 32
N_GROUPS = 8
EXPERTS_PER_GROUP = N_EXPERTS // N_GROUPS
TOP_K = 2
D_EXPERT = 512
MOE_BLOCK = 256
LN_EPS = 1e-5
GN_EPS = 1e-5
DEEPNORM_ALPHA = (2 * DEPTH) ** 0.25
DEEPNORM_BETA = (8 * DEPTH) ** -0.25

RET_QK_W = RET_HEADS * RET_DK
RET_V_W = RET_HEADS * RET_DV
NA_W = NA_HEADS * NA_HD
IN_SIZES = (RET_QK_W, RET_QK_W, RET_V_W, RET_V_W, NA_W, NA_W, NA_W, 2 * D_MODEL)
D_IN = RET_QK_W * 2 + RET_V_W * 2 + NA_W * 3 + 2 * D_MODEL

kernel_name = "hybrid_retention_natten_grouped_moe_deepnorm"


def _split_points():
    pts, acc = [], 0
    for s in IN_SIZES[:-1]:
        acc += s
        pts.append(acc)
    return tuple(pts)


def layernorm(x, g, b):
    xf = x.astype(jnp.float32)
    mu = jnp.mean(xf, axis=-1, keepdims=True)
    var = jnp.mean(jnp.square(xf - mu), axis=-1, keepdims=True)
    y = (xf - mu) * lax.rsqrt(var + LN_EPS) * g.astype(jnp.float32) + b.astype(jnp.float32)
    return y.astype(x.dtype)


def head_norm(y):
    mu = jnp.mean(y, axis=-1, keepdims=True)
    var = jnp.mean(jnp.square(y - mu), axis=-1, keepdims=True)
    return (y - mu) * lax.rsqrt(var + GN_EPS)


def rotary(x):
    S, d = x.shape[1], x.shape[-1]
    half = d // 2
    pos = jnp.arange(S, dtype=jnp.float32)
    freqs = ROPE_BASE ** (-jnp.arange(half, dtype=jnp.float32) / half)
    ang = pos[:, None] * freqs[None, :]
    cos = jnp.cos(ang)[None, :, None, :].astype(x.dtype)
    sin = jnp.sin(ang)[None, :, None, :].astype(x.dtype)
    x1, x2 = x[..., :half], x[..., half:]
    return jnp.concatenate([x1 * cos - x2 * sin, x1 * sin + x2 * cos], axis=-1)


def retention_one_direction(q, k, v, log_g, strict):
    B, H, S, dk = q.shape
    dv = v.shape[-1]
    C = RET_CHUNK
    nc = S // C
    pos = jnp.arange(C, dtype=jnp.float32)
    diff = pos[:, None] - pos[None, :]
    mask = (diff > 0) if strict else (diff >= 0)
    intra = jnp.where(mask[None], jnp.exp(log_g[:, None, None] * jnp.maximum(diff, 0.0)[None]), 0.0)
    qc = q.reshape(B, H, nc, C, dk)
    kc = k.reshape(B, H, nc, C, dk)
    vc = v.reshape(B, H, nc, C, dv)
    scores = jnp.einsum('bhnid,bhnjd->bhnij', qc, kc) * intra[None, :, None]
    inner = jnp.einsum('bhnij,bhnje->bhnie', scores, vc)
    q_dec = jnp.exp(log_g[:, None] * (pos + 1.0))
    k_dec = jnp.exp(log_g[:, None] * (C - 1.0 - pos))
    chunk_dec = jnp.exp(log_g * C)
    xs = (jnp.moveaxis(qc * q_dec[None, :, None, :, None], 2, 0),
          jnp.moveaxis(kc * k_dec[None, :, None, :, None], 2, 0),
          jnp.moveaxis(vc, 2, 0))

    def step(state, inp):
        qi, ki, vi = inp
        cross = jnp.einsum('bhid,bhde->bhie', qi, state)
        state = state * chunk_dec[None, :, None, None] + jnp.einsum('bhjd,bhje->bhde', ki, vi)
        return state, cross

    state0 = jnp.zeros((B, H, dk, dv), jnp.float32)
    _, cross = lax.scan(step, state0, xs)
    out = inner + jnp.moveaxis(cross, 0, 2)
    return out.reshape(B, H, S, dv)


def bidirectional_retention(q, k, v, decay_logit):
    log_g = jax.nn.log_sigmoid(decay_logit.astype(jnp.float32))
    qh = q.astype(jnp.float32).transpose(0, 2, 1, 3)
    kh = k.astype(jnp.float32).transpose(0, 2, 1, 3)
    vh = v.astype(jnp.float32).transpose(0, 2, 1, 3)
    fwd = retention_one_direction(qh, kh, vh, log_g[0], strict=False)
    bwd = jnp.flip(retention_one_direction(jnp.flip(qh, 2), jnp.flip(kh, 2), jnp.flip(vh, 2),
                                           log_g[1], strict=True), 2)
    return (fwd + bwd).transpose(0, 2, 1, 3)


def neighbourhood_attention(q, k, v, rpb):
    B, S, H, hd = q.shape
    rows = S // GRID_W
    kh = min(NA_KH, rows)
    kw = NA_KW
    qg = q.reshape(B, rows, GRID_W, H, hd)
    kg = k.reshape(B, rows, GRID_W, H, hd)
    vg = v.reshape(B, rows, GRID_W, H, hd)
    col = jnp.arange(GRID_W)
    col_start = jnp.clip(col - kw // 2, 0, GRID_W - kw)
    col_idx = col_start[:, None] + jnp.arange(kw)[None, :]
    col_off = col_idx - col[:, None] + (NA_KW - 1)

    def one_row(r):
        row_start = jnp.clip(r - kh // 2, 0, rows - kh)
        k_rows = lax.dynamic_slice_in_dim(kg, row_start, kh, axis=1)
        v_rows = lax.dynamic_slice_in_dim(vg, row_start, kh, axis=1)
        k_nb = k_rows[:, :, col_idx]
        v_nb = v_rows[:, :, col_idx]
        q_r = lax.dynamic_index_in_dim(qg, r, axis=1, keepdims=False)
        row_off = row_start + jnp.arange(kh) - r + (NA_KH - 1)
        bias = rpb[:, row_off[:, None, None], col_off[None, :, :]]
        s = jnp.einsum('bwhd,biwjhd->bhwij', q_r, k_nb).astype(jnp.float32)
        s = s + bias.transpose(0, 2, 1, 3)[None].astype(jnp.float32)
        p = jax.nn.softmax(s.reshape(B, H, GRID_W, kh * kw), axis=-1).reshape(B, H, GRID_W, kh, kw)
        return jnp.einsum('bhwij,biwjhd->bwhd', p.astype(v.dtype), v_nb)

    out = lax.map(one_row, jnp.arange(rows))
    return out.transpose(1, 0, 2, 3, 4).reshape(B, S, H * hd)


def hybrid_mixer(x, w_in, decay_logit, w_ret_o, rpb, w_na_o, w_out):
    B, S, _ = x.shape
    proj = jnp.einsum('bsd,de->bse', x, w_in)
    q_r, k_r, v_r, g_r, q_n, k_n, v_n, gate_logits = jnp.split(proj, _split_points(), axis=-1)
    q_r = rotary(q_r.reshape(B, S, RET_HEADS, RET_DK))
    k_r = rotary(k_r.reshape(B, S, RET_HEADS, RET_DK)) * (RET_DK ** -0.5)
    v_r = v_r.reshape(B, S, RET_HEADS, RET_DV)
    ret = head_norm(bidirectional_retention(q_r, k_r, v_r, decay_logit)).reshape(B, S, RET_V_W)
    y_ret = jnp.einsum('bse,ed->bsd', (jax.nn.silu(g_r.astype(jnp.float32)) * ret).astype(x.dtype), w_ret_o)
    na = neighbourhood_attention(q_n.reshape(B, S, NA_HEADS, NA_HD) * (NA_HD ** -0.5),
                                 k_n.reshape(B, S, NA_HEADS, NA_HD),
                                 v_n.reshape(B, S, NA_HEADS, NA_HD), rpb)
    y_na = jnp.einsum('bse,ed->bsd', na, w_na_o)
    gates = jax.nn.sigmoid(gate_logits)
    merged = gates[..., :D_MODEL] * y_ret + gates[..., D_MODEL:] * y_na
    return jnp.einsum('bsd,de->bse', merged, w_out)


def route(h, router_w, router_bias):
    T = h.shape[0]
    scores = jax.nn.sigmoid(jnp.dot(h.astype(jnp.float32), router_w.astype(jnp.float32)))
    sel = scores + router_bias.astype(jnp.float32)[None, :]
    sel_g = sel.reshape(T, N_GROUPS, EXPERTS_PER_GROUP)
    group_score = jnp.sum(lax.top_k(sel_g, TOP_K)[0], axis=-1)
    g_best = jnp.argmax(group_score, axis=-1).astype(jnp.int32)
    in_group = jnp.take_along_axis(sel_g, g_best[:, None, None], axis=1)[:, 0]
    _, local = lax.top_k(in_group, TOP_K)
    expert_idx = g_best[:, None] * EXPERTS_PER_GROUP + local.astype(jnp.int32)
    w = jnp.take_along_axis(scores, expert_idx, axis=1)
    w = w / jnp.sum(w, axis=-1, keepdims=True)
    return expert_idx, w


def swiglu(xb, wg, wu, wd):
    return jnp.dot(jax.nn.silu(jnp.dot(xb, wg)) * jnp.dot(xb, wu), wd)


def moe(h, router_w, router_bias, w_gate, w_up, w_down):
    T, D = h.shape
    expert_idx, gate_w = route(h, router_w, router_bias)
    TK = T * TOP_K
    flat_e = expert_idx.reshape(TK)
    flat_tok = jnp.repeat(jnp.arange(T, dtype=jnp.int32), TOP_K)
    flat_w = gate_w.reshape(TK)
    order = jnp.argsort(flat_e)
    e_sorted = flat_e[order]
    tok_sorted = flat_tok[order]
    w_sorted = flat_w[order]
    counts = jnp.bincount(flat_e, length=N_EXPERTS)
    starts = jnp.cumsum(counts) - counts
    padded = ((counts + MOE_BLOCK - 1) // MOE_BLOCK) * MOE_BLOCK
    ends_pad = jnp.cumsum(padded)
    starts_pad = ends_pad - padded
    rank = jnp.arange(TK, dtype=jnp.int32) - starts[e_sorted]
    dest = starts_pad[e_sorted] + rank
    n_blocks = (TK + N_EXPERTS * (MOE_BLOCK - 1) + MOE_BLOCK - 1) // MOE_BLOCK
    buf = jnp.zeros((n_blocks * MOE_BLOCK, D), h.dtype).at[dest].set(h[tok_sorted])
    block_e = jnp.clip(jnp.searchsorted(ends_pad, jnp.arange(n_blocks) * MOE_BLOCK, side='right'),
                       0, N_EXPERTS - 1)

    def expert_block(args):
        xb, e = args
        return swiglu(xb, w_gate[e], w_up[e], w_down[e])

    y_blocks = lax.map(expert_block, (buf.reshape(n_blocks, MOE_BLOCK, D), block_e))
    y = y_blocks.reshape(n_blocks * MOE_BLOCK, D)[dest] * w_sorted[:, None].astype(h.dtype)
    return jax.ops.segment_sum(y, tok_sorted, num_segments=T)


def setup_inputs(seed: int = 0) -> dict:
    key = jax.random.key(seed)
    ks = jax.random.split(key, 16)
    f32 = jnp.float32
    nrm = lambda k, shape, scale: jax.random.normal(k, shape, f32) * scale
    base_logit = jnp.log(2.0 ** (5.0 + jnp.arange(RET_HEADS, dtype=f32)) - 1.0)
    decay = base_logit[None, None, :] + nrm(ks[2], (DEPTH, 2, RET_HEADS), 0.1)
    return {
        "x": nrm(ks[0], (BATCH, SEQ, D_MODEL), 1.0),
        "w_in": nrm(ks[1], (DEPTH, D_MODEL, D_IN), D_MODEL ** -0.5),
        "ret_decay_logit": decay,
        "w_ret_o": nrm(ks[3], (DEPTH, RET_V_W, D_MODEL), RET_V_W ** -0.5),
        "na_rpb": nrm(ks[4], (DEPTH, NA_HEADS, 2 * NA_KH - 1, 2 * NA_KW - 1), 0.02),
        "w_na_o": nrm(ks[5], (DEPTH, NA_W, D_MODEL), NA_W ** -0.5),
        "w_out": nrm(ks[6], (DEPTH, D_MODEL, D_MODEL), D_MODEL ** -0.5 * DEEPNORM_BETA),
        "ln_mix_g": 1.0 + nrm(ks[7], (DEPTH, D_MODEL), 0.02),
        "ln_mix_b": nrm(ks[8], (DEPTH, D_MODEL), 0.02),
        "router_w": nrm(ks[9], (D_MODEL, N_EXPERTS), D_MODEL ** -0.5),
        "router_bias": nrm(ks[10], (N_EXPERTS,), 0.01),
        "w_exp_gate": nrm(ks[11], (DEPTH, N_EXPERTS, D_MODEL, D_EXPERT), D_MODEL ** -0.5),
        "w_exp_up": nrm(ks[12], (DEPTH, N_EXPERTS, D_MODEL, D_EXPERT), D_MODEL ** -0.5),
        "w_exp_down": nrm(ks[13], (DEPTH, N_EXPERTS, D_EXPERT, D_MODEL), D_EXPERT ** -0.5 * DEEPNORM_BETA),
        "ln_ffn_g": 1.0 + nrm(ks[14], (DEPTH, D_MODEL), 0.02),
        "ln_ffn_b": nrm(ks[15], (DEPTH, D_MODEL), 0.02),
    }


def reference(x, w_in, ret_decay_logit, w_ret_o, na_rpb, w_na_o, w_out, ln_mix_g, ln_mix_b,
              router_w, router_bias, w_exp_gate, w_exp_up, w_exp_down, ln_ffn_g, ln_ffn_b):
    B, S, D = x.shape
    for l in range(DEPTH):
        mix = hybrid_mixer(x, w_in[l], ret_decay_logit[l], w_ret_o[l], na_rpb[l], w_na_o[l], w_out[l])
        x = layernorm(DEEPNORM_ALPHA * x + mix, ln_mix_g[l], ln_mix_b[l])
        ffn = moe(x.reshape(B * S, D), router_w, router_bias,
                  w_exp_gate[l], w_exp_up[l], w_exp_down[l]).reshape(B, S, D)
        x = layernorm(DEEPNORM_ALPHA * x + ffn, ln_ffn_g[l], ln_ffn_b[l])
    return x
```

```python
import functools

import numpy as np
import jax
import jax.numpy as jnp
from jax import lax
from jax.experimental import pallas as pl
from jax.experimental.pallas import tpu as pltpu

F32 = jnp.float32
BF16 = jnp.bfloat16

D_MODEL = 1024
GRID_W = 64
RET_HEADS = 4
RET_DK = 256
RET_DV = 512
ROPE_BASE = 10000.0
NA_HEADS = 16
NA_HD = 64
NA_KH = 8
NA_KW = 16
N_EXPERTS = 32
N_GROUPS = 8
EXPERTS_PER_GROUP = 4
D_EXPERT = 512
LN_EPS = 1e-5
GN_EPS = 1e-5
MODEL_DEPTH = 4
DEEPNORM_ALPHA = (2 * MODEL_DEPTH) ** 0.25

COL_Q_R = 0
COL_K_R = 1024
COL_V_R = 2048
COL_G_R = 4096
COL_Q_N = 6144
COL_K_N = 7168
COL_V_N = 8192
COL_GATE = 9216
D_IN = 11264

RET_CHUNK = 256
NA_QROWS = 4
NA_KROWS = 12
NA_NEG = -1e30
MOE_BLOCK = 256
N_PAIRS = 6
N_CLASSES = N_GROUPS * N_PAIRS
VMEM_LIMIT = 56 * 1024 * 1024


def _cparams(sem):
    return pltpu.CompilerParams(dimension_semantics=sem, vmem_limit_bytes=VMEM_LIMIT)


IN_BM = 1024
IN_BN = 1024
_ROTARY_BLOCKS = {0: 1.0, 1: RET_DK ** -0.5}
_SCALE_BLOCKS = {2: 1.0, 3: 1.0, 6: NA_HD ** -0.5, 7: 1.0, 8: 1.0}
_SILU_BLOCKS = (4, 5)
_SIGMOID_BLOCKS = (9, 10)


def _inproj_kernel(x_ref, w_ref, cos_ref, sin_ref, o_ref):
    j = pl.program_id(0)
    acc = jnp.dot(x_ref[...].astype(BF16), w_ref[...], preferred_element_type=F32)

    for blk, scale in _ROTARY_BLOCKS.items():
        @pl.when(j == blk)
        def _(scale=scale):
            cos = cos_ref[...]
            sin = sin_ref[...]
            half = RET_DK // 2
            for h in range(IN_BN // RET_DK):
                x1 = acc[:, h * RET_DK:h * RET_DK + half]
                x2 = acc[:, h * RET_DK + half:(h + 1) * RET_DK]
                o_ref[:, h * RET_DK:h * RET_DK + half] = ((x1 * cos - x2 * sin) * scale).astype(o_ref.dtype)
                o_ref[:, h * RET_DK + half:(h + 1) * RET_DK] = ((x1 * sin + x2 * cos) * scale).astype(o_ref.dtype)

    for blk, scale in _SCALE_BLOCKS.items():
        @pl.when(j == blk)
        def _(scale=scale):
            o_ref[...] = (acc * scale).astype(o_ref.dtype)

    @pl.when((j == _SILU_BLOCKS[0]) | (j == _SILU_BLOCKS[1]))
    def _():
        o_ref[...] = (acc * jax.nn.sigmoid(acc)).astype(o_ref.dtype)

    @pl.when((j == _SIGMOID_BLOCKS[0]) | (j == _SIGMOID_BLOCKS[1]))
    def _():
        o_ref[...] = jax.nn.sigmoid(acc).astype(o_ref.dtype)


def _inproj(x2d, w_bf, cos, sin, seq):
    t = x2d.shape[0]
    pos_blocks = seq // IN_BM
    return pl.pallas_call(
        _inproj_kernel,
        out_shape=jax.ShapeDtypeStruct((t, D_IN), BF16),
        grid=(D_IN // IN_BN, t // IN_BM),
        in_specs=[
            pl.BlockSpec((IN_BM, D_MODEL), lambda j, i: (i, 0)),
            pl.BlockSpec((D_MODEL, IN_BN), lambda j, i: (0, j)),
            pl.BlockSpec((IN_BM, RET_DK // 2), lambda j, i: (i % pos_blocks, 0)),
            pl.BlockSpec((IN_BM, RET_DK // 2), lambda j, i: (i % pos_blocks, 0)),
        ],
        out_specs=pl.BlockSpec((IN_BM, IN_BN), lambda j, i: (i, j)),
        compiler_params=_cparams(("arbitrary", "arbitrary")),
        name="inproj",
    )(x2d, w_bf, cos, sin)


def _log_sigmoid(x):
    return jnp.minimum(x, 0.0) - jnp.log1p(jnp.exp(-jnp.abs(x)))


def _retention_kernel(dl_ref, q_ref, k_ref, v_ref, g_ref, o_ref,
                      sf_ref, st_ref, dm_ref, qdf_ref, qdb_ref, kdf_ref, kdb_ref, *, nc):
    c_len = RET_CHUNK
    h = pl.program_id(1)
    lgf = _log_sigmoid(jnp.full((c_len, RET_DK), dl_ref[0, h], F32))
    lgb = _log_sigmoid(jnp.full((c_len, RET_DK), dl_ref[1, h], F32))
    ri = lax.broadcasted_iota(jnp.int32, (c_len, RET_DK), 0).astype(F32)
    qdf_ref[...] = jnp.exp(lgf * (ri + 1.0))
    qdb_ref[...] = jnp.exp(lgb * (c_len - ri))
    kdf_ref[...] = jnp.exp(lgf * (c_len - 1.0 - ri))
    kdb_ref[...] = jnp.exp(lgb * ri)
    rr = lax.broadcasted_iota(jnp.int32, (c_len, c_len), 0).astype(F32)
    cc = lax.broadcasted_iota(jnp.int32, (c_len, c_len), 1).astype(F32)
    diff = rr - cc
    lgf_cc = _log_sigmoid(jnp.full((c_len, c_len), dl_ref[0, h], F32))
    lgb_cc = _log_sigmoid(jnp.full((c_len, c_len), dl_ref[1, h], F32))
    dm_ref[...] = jnp.where(diff >= 0.0, jnp.exp(lgf_cc * jnp.maximum(diff, 0.0)),
                            jnp.exp(lgb_cc * jnp.maximum(-diff, 0.0)))
    chunk_f = jnp.exp(lgf[:1, :1] * float(c_len))
    chunk_b = jnp.exp(lgb[:1, :1] * float(c_len))

    tn_dims = (((0,), (0,)), ((), ()))
    nt_dims = (((1,), (1,)), ((), ()))

    st_ref[...] = jnp.zeros_like(st_ref)

    def fwd_body(c, carry):
        off = pl.multiple_of(c * c_len, c_len)
        sf_ref[c] = st_ref[...].astype(BF16)
        kc = k_ref[0, pl.ds(off, c_len), :].astype(F32)
        vc = v_ref[0, pl.ds(off, c_len), :]
        kd = (kc * kdf_ref[...]).astype(BF16)
        upd = lax.dot_general(kd, vc, tn_dims, preferred_element_type=F32)
        st_ref[...] = st_ref[...] * chunk_f + upd
        return carry

    lax.fori_loop(0, nc, fwd_body, 0)

    st_ref[...] = jnp.zeros_like(st_ref)

    def bwd_body(i, carry):
        c = nc - 1 - i
        off = pl.multiple_of(c * c_len, c_len)
        qb = q_ref[0, pl.ds(off, c_len), :]
        kb = k_ref[0, pl.ds(off, c_len), :]
        vc = v_ref[0, pl.ds(off, c_len), :]
        qc = qb.astype(F32)
        kc = kb.astype(F32)
        s = lax.dot_general(qb, kb, nt_dims, preferred_element_type=F32)
        p = (s * dm_ref[...]).astype(BF16)
        out = jnp.dot(p, vc, preferred_element_type=F32)
        out = out + jnp.dot((qc * qdf_ref[...]).astype(BF16), sf_ref[c], preferred_element_type=F32)
        out = out + jnp.dot((qc * qdb_ref[...]).astype(BF16), st_ref[...].astype(BF16),
                            preferred_element_type=F32)
        mu = jnp.mean(out, axis=-1, keepdims=True)
        cen = out - mu
        var = jnp.mean(cen * cen, axis=-1, keepdims=True)
        y = cen * lax.rsqrt(var + GN_EPS)
        gate = g_ref[0, pl.ds(off, c_len), :].astype(F32)
        o_ref[0, pl.ds(off, c_len), :] = (gate * y).astype(o_ref.dtype)
        kd = (kc * kdb_ref[...]).astype(BF16)
        upd = lax.dot_general(kd, vc, tn_dims, preferred_element_type=F32)
        st_ref[...] = st_ref[...] * chunk_b + upd
        return carry

    lax.fori_loop(0, nc, bwd_body, 0)


def _retention(proj3, decay_logit):
    b, s, _ = proj3.shape
    nc = s // RET_CHUNK
    kq = COL_K_R // RET_DK
    kv = COL_V_R // RET_DV
    kg = COL_G_R // RET_DV
    return pl.pallas_call(
        functools.partial(_retention_kernel, nc=nc),
        out_shape=jax.ShapeDtypeStruct((b, s, RET_HEADS * RET_DV), BF16),
        grid=(b, RET_HEADS),
        in_specs=[
            pl.BlockSpec(memory_space=pltpu.SMEM),
            pl.BlockSpec((1, s, RET_DK), lambda bi, h: (bi, 0, h)),
            pl.BlockSpec((1, s, RET_DK), lambda bi, h: (bi, 0, kq + h)),
            pl.BlockSpec((1, s, RET_DV), lambda bi, h: (bi, 0, kv + h)),
            pl.BlockSpec((1, s, RET_DV), lambda bi, h: (bi, 0, kg + h)),
        ],
        out_specs=pl.BlockSpec((1, s, RET_DV), lambda bi, h: (bi, 0, h)),
        scratch_shapes=[
            pltpu.VMEM((nc, RET_DK, RET_DV), BF16),
            pltpu.VMEM((RET_DK, RET_DV), F32),
            pltpu.VMEM((RET_CHUNK, RET_CHUNK), F32),
            pltpu.VMEM((RET_CHUNK, RET_DK), F32),
            pltpu.VMEM((RET_CHUNK, RET_DK), F32),
            pltpu.VMEM((RET_CHUNK, RET_DK), F32),
            pltpu.VMEM((RET_CHUNK, RET_DK), F32),
        ],
        compiler_params=_cparams(("arbitrary", "arbitrary")),
        name="retention",
    )(decay_logit, proj3, proj3, proj3, proj3)


def _na_bias_indices(rows):
    nq = NA_QROWS * GRID_W
    nk = NA_KROWS * GRID_W
    ro = np.zeros((3, nq, nk), np.int32)
    co = np.zeros((3, nq, nk), np.int32)
    ok = np.zeros((3, nq, nk), bool)
    n_tiles = rows // NA_QROWS
    rr, c = np.divmod(np.arange(nq), GRID_W)
    ki, kc = np.divmod(np.arange(nk), GRID_W)
    for p, t in enumerate((0, 1, n_tiles - 1)):
        kstart = np.clip(NA_QROWS * t - NA_KH // 2, 0, rows - NA_KROWS)
        r = NA_QROWS * t + rr
        rs = np.clip(r - NA_KH // 2, 0, rows - NA_KH)
        krow = kstart + ki
        cs = np.clip(c - NA_KW // 2, 0, GRID_W - NA_KW)
        ok_r = (krow[None, :] >= rs[:, None]) & (krow[None, :] < rs[:, None] + NA_KH)
        ok_c = (kc[None, :] >= cs[:, None]) & (kc[None, :] < cs[:, None] + NA_KW)
        ok[p] = ok_r & ok_c
        ro[p] = np.clip(krow[None, :] - r[:, None] + (NA_KH - 1), 0, 2 * NA_KH - 2)
        co[p] = np.clip(kc[None, :] - c[:, None] + (NA_KW - 1), 0, 2 * NA_KW - 2)
    return ro, co, ok


def _na_bias_table(rpb, rows):
    ro, co, ok = _na_bias_indices(rows)
    tab = rpb[:, ro, co]
    return jnp.where(ok[None], tab, NA_NEG).astype(F32)


def _na_kernel(q_ref, k_ref, v_ref, bias_ref, o_ref, *, n_tiles, rows):
    nq = NA_QROWS * GRID_W
    nk = NA_KROWS * GRID_W
    nt_dims = (((1,), (1,)), ((), ()))
    lane = lax.broadcasted_iota(jnp.int32, (nq, 2 * NA_HD), 1)
    first = lane < NA_HD

    def body(t, carry):
        qoff = pl.multiple_of(t * nq, nq)
        krow0 = jnp.clip(NA_QROWS * t - NA_KH // 2, 0, rows - NA_KROWS)
        koff = pl.multiple_of(krow0 * GRID_W, nq)
        pat = jnp.where(t == 0, 0, jnp.where(t == n_tiles - 1, 2, 1))
        q2 = q_ref[0, pl.ds(qoff, nq), :]
        k2 = k_ref[0, pl.ds(koff, nk), :]
        v2 = v_ref[0, pl.ds(koff, nk), :]
        outs = []
        for hh in range(2):
            qm = jnp.where(first if hh == 0 else jnp.logical_not(first), q2, jnp.zeros_like(q2))
            s = lax.dot_general(qm, k2, nt_dims, preferred_element_type=F32)
            s = s + bias_ref[hh, pat]
            m = jnp.max(s, axis=-1, keepdims=True)
            e = jnp.exp(s - m)
            l = jnp.sum(e, axis=-1, keepdims=True)
            o = jnp.dot(e.astype(BF16), v2, preferred_element_type=F32)
            outs.append(o / l)
        o_ref[0, pl.ds(qoff, nq), :] = jnp.where(first, outs[0], outs[1]).astype(o_ref.dtype)
        return carry

    lax.fori_loop(0, n_tiles, body, 0)


def _na(proj3, bias_tab):
    b, s, _ = proj3.shape
    rows = s // GRID_W
    n_tiles = rows // NA_QROWS
    w2 = 2 * NA_HD
    nq = NA_QROWS * GRID_W
    nk = NA_KROWS * GRID_W
    return pl.pallas_call(
        functools.partial(_na_kernel, n_tiles=n_tiles, rows=rows),
        out_shape=jax.ShapeDtypeStruct((b, s, NA_HEADS * NA_HD), BF16),
        grid=(NA_HEADS // 2, b),
        in_specs=[
            pl.BlockSpec((1, s, w2), lambda hp, bi: (bi, 0, COL_Q_N // w2 + hp)),
            pl.BlockSpec((1, s, w2), lambda hp, bi: (bi, 0, COL_K_N // w2 + hp)),
            pl.BlockSpec((1, s, w2), lambda hp, bi: (bi, 0, COL_V_N // w2 + hp)),
            pl.BlockSpec((2, 3, nq, nk), lambda hp, bi: (hp, 0, 0, 0)),
        ],
        out_specs=pl.BlockSpec((1, s, w2), lambda hp, bi: (bi, 0, hp)),
        compiler_params=_cparams(("arbitrary", "arbitrary")),
        name="natten",
    )(proj3, proj3, proj3, bias_tab)


MIX_BM = 512


def _layernorm_rows(z, g, b):
    mu = jnp.mean(z, axis=-1, keepdims=True)
    cen = z - mu
    var = jnp.mean(cen * cen, axis=-1, keepdims=True)
    return cen * lax.rsqrt(var + LN_EPS) * g + b


def _mixout_kernel(ret_ref, na_ref, g1_ref, g2_ref, x_ref, wr_ref, wn_ref, wo_ref, lg_ref, lb_ref, o_ref):
    y_ret = jnp.dot(ret_ref[...], wr_ref[...], preferred_element_type=F32)
    y_na = jnp.dot(na_ref[...], wn_ref[...], preferred_element_type=F32)
    merged = g1_ref[...].astype(F32) * y_ret + g2_ref[...].astype(F32) * y_na
    mix = jnp.dot(merged.astype(BF16), wo_ref[...], preferred_element_type=F32)
    z = DEEPNORM_ALPHA * x_ref[...] + mix
    o_ref[...] = _layernorm_rows(z, lg_ref[...], lb_ref[...])


def _mixout(ret2d, na2d, proj2d, x2d, wr, wn, wo, lg, lb):
    t = x2d.shape[0]
    gate_blk = COL_GATE // D_MODEL
    const = lambda i: (0, 0)
    return pl.pallas_call(
        _mixout_kernel,
        out_shape=jax.ShapeDtypeStruct((t, D_MODEL), F32),
        grid=(t // MIX_BM,),
        in_specs=[
            pl.BlockSpec((MIX_BM, RET_HEADS * RET_DV), lambda i: (i, 0)),
            pl.BlockSpec((MIX_BM, NA_HEADS * NA_HD), lambda i: (i, 0)),
            pl.BlockSpec((MIX_BM, D_MODEL), lambda i: (i, gate_blk)),
            pl.BlockSpec((MIX_BM, D_MODEL), lambda i: (i, gate_blk + 1)),
            pl.BlockSpec((MIX_BM, D_MODEL), lambda i: (i, 0)),
            pl.BlockSpec((RET_HEADS * RET_DV, D_MODEL), const),
            pl.BlockSpec((NA_HEADS * NA_HD, D_MODEL), const),
            pl.BlockSpec((D_MODEL, D_MODEL), const),
            pl.BlockSpec((1, D_MODEL), const),
            pl.BlockSpec((1, D_MODEL), const),
        ],
        out_specs=pl.BlockSpec((MIX_BM, D_MODEL), lambda i: (i, 0)),
        compiler_params=_cparams(("arbitrary",)),
        name="mixout",
    )(ret2d, na2d, proj2d, proj2d, x2d, wr, wn, wo, lg, lb)


ROUTE_BM = 512


def _router_kernel(x_ref, rw_ref, rb_ref, cls_ref, w_ref):
    nt_dims = (((1,), (1,)), ((), ()))
    logits = lax.dot_general(rw_ref[...], x_ref[...], nt_dims, preferred_element_type=F32,
                             precision=lax.Precision.HIGHEST)
    scores = jax.nn.sigmoid(logits)
    sel = scores + rb_ref[...]
    p = [scores[m * N_GROUPS:(m + 1) * N_GROUPS] for m in range(EXPERTS_PER_GROUP)]
    s = [sel[m * N_GROUPS:(m + 1) * N_GROUPS] for m in range(EXPERTS_PER_GROUP)]
    one = jnp.ones_like(s[0])
    zero = jnp.zeros_like(s[0])
    chosen = []
    for m in range(EXPERTS_PER_GROUP):
        rank = zero
        for j in range(EXPERTS_PER_GROUP):
            if j == m:
                continue
            beats = (s[j] >= s[m]) if j < m else (s[j] > s[m])
            rank = rank + jnp.where(beats, one, zero)
        chosen.append(rank < 2.0)
    group_score = zero
    for m in range(EXPERTS_PER_GROUP):
        group_score = group_score + jnp.where(chosen[m], s[m], zero)
    gid = lax.broadcasted_iota(jnp.int32, group_score.shape, 0)
    gmax = jnp.max(group_score, axis=0, keepdims=True)
    gbest = jnp.min(jnp.where(group_score == gmax, gid, N_GROUPS), axis=0, keepdims=True)
    in_best = gid == gbest
    picked = [chosen[m] & in_best for m in range(EXPERTS_PER_GROUP)]
    before = zero
    wa = zero
    wb = zero
    ma = zero
    mb = zero
    for m in range(EXPERTS_PER_GROUP):
        is_a = picked[m] & (before == 0.0)
        is_b = picked[m] & (before == 1.0)
        wa = wa + jnp.where(is_a, p[m], zero)
        wb = wb + jnp.where(is_b, p[m], zero)
        ma = ma + jnp.where(is_a, float(m), 0.0)
        mb = mb + jnp.where(is_b, float(m), 0.0)
        before = before + jnp.where(chosen[m], one, zero)
    wa = jnp.sum(wa, axis=0, keepdims=True)
    wb = jnp.sum(wb, axis=0, keepdims=True)
    ma = jnp.sum(ma, axis=0, keepdims=True)
    mb = jnp.sum(mb, axis=0, keepdims=True)
    denom = wa + wb
    base = jnp.where(ma == 0.0, 0.0, jnp.where(ma == 1.0, 3.0, 5.0))
    pair = base + mb - ma - 1.0
    cls_ref[0] = gbest * N_PAIRS + pair.astype(jnp.int32)
    w_ref[0] = jnp.concatenate([wa / denom, wb / denom], axis=0)


def _router(x2d, rw_t, rb_t):
    t = x2d.shape[0]
    nt = t // ROUTE_BM
    return pl.pallas_call(
        _router_kernel,
        out_shape=(jax.ShapeDtypeStruct((nt, 1, ROUTE_BM), jnp.int32),
                   jax.ShapeDtypeStruct((nt, 2, ROUTE_BM), F32)),
        grid=(nt,),
        in_specs=[
            pl.BlockSpec((ROUTE_BM, D_MODEL), lambda i: (i, 0)),
            pl.BlockSpec((N_EXPERTS, D_MODEL), lambda i: (0, 0)),
            pl.BlockSpec((N_EXPERTS, 1), lambda i: (0, 0)),
        ],
        out_specs=(pl.BlockSpec((1, 1, ROUTE_BM), lambda i: (i, 0, 0)),
                   pl.BlockSpec((1, 2, ROUTE_BM), lambda i: (i, 0, 0))),
        compiler_params=_cparams(("arbitrary",)),
        name="router",
    )(x2d, rw_t, rb_t)


_PAIR_LO = np.array([0, 0, 0, 1, 1, 2], np.int32)
_PAIR_HI = np.array([1, 2, 3, 2, 3, 3], np.int32)


def _experts_kernel(ea_ref, eb_ref, nb_ref, x_ref, gw_ref, wga_ref, wua_ref, wda_ref,
                    wgb_ref, wub_ref, wdb_ref, o_ref):
    j = pl.program_id(0)

    @pl.when(j < nb_ref[0])
    def _():
        x = x_ref[...].astype(BF16)
        gw = gw_ref[...]
        acts = []
        for wg_ref, wu_ref, col in ((wga_ref, wua_ref, 0), (wgb_ref, wub_ref, 1)):
            hg = jnp.dot(x, wg_ref[0], preferred_element_type=F32)
            hu = jnp.dot(x, wu_ref[0], preferred_element_type=F32)
            act = hg * jax.nn.sigmoid(hg) * hu * gw[:, col:col + 1]
            acts.append(act.astype(BF16))
        o_ref[...] = (jnp.dot(acts[0], wda_ref[0], preferred_element_type=F32)
                      + jnp.dot(acts[1], wdb_ref[0], preferred_element_type=F32))

    @pl.when(j >= nb_ref[0])
    def _():
        o_ref[...] = jnp.zeros_like(o_ref)


def _experts(blk_a, blk_b, n_used, xs, gws, wg, wu, wd):
    n_rows = xs.shape[0]
    n_blocks = n_rows // MOE_BLOCK
    wa_map = lambda j, ea, eb, nb: (ea[j], 0, 0)
    wb_map = lambda j, ea, eb, nb: (eb[j], 0, 0)
    row_map = lambda j, ea, eb, nb: (j, 0)
    grid_spec = pltpu.PrefetchScalarGridSpec(
        num_scalar_prefetch=3,
        grid=(n_blocks,),
        in_specs=[
            pl.BlockSpec((MOE_BLOCK, D_MODEL), row_map),
            pl.BlockSpec((MOE_BLOCK, 128), row_map),
            pl.BlockSpec((1, D_MODEL, D_EXPERT), wa_map),
            pl.BlockSpec((1, D_MODEL, D_EXPERT), wa_map),
            pl.BlockSpec((1, D_EXPERT, D_MODEL), wa_map),
            pl.BlockSpec((1, D_MODEL, D_EXPERT), wb_map),
            pl.BlockSpec((1, D_MODEL, D_EXPERT), wb_map),
            pl.BlockSpec((1, D_EXPERT, D_MODEL), wb_map),
        ],
        out_specs=pl.BlockSpec((MOE_BLOCK, D_MODEL), row_map),
    )
    return pl.pallas_call(
        _experts_kernel,
        out_shape=jax.ShapeDtypeStruct((n_rows, D_MODEL), F32),
        grid_spec=grid_spec,
        compiler_params=_cparams(("arbitrary",)),
        name="experts",
    )(blk_a, blk_b, n_used, xs, gws, wg, wu, wd, wg, wu, wd)


LN_BM = 1024


def _add_ln_kernel(x_ref, y_ref, lg_ref, lb_ref, o_ref):
    z = DEEPNORM_ALPHA * x_ref[...] + y_ref[...]
    o_ref[...] = _layernorm_rows(z, lg_ref[...], lb_ref[...])


def _add_ln(x2d, y2d, lg, lb):
    t = x2d.shape[0]
    return pl.pallas_call(
        _add_ln_kernel,
        out_shape=jax.ShapeDtypeStruct((t, D_MODEL), F32),
        grid=(t // LN_BM,),
        in_specs=[
            pl.BlockSpec((LN_BM, D_MODEL), lambda i: (i, 0)),
            pl.BlockSpec((LN_BM, D_MODEL), lambda i: (i, 0)),
            pl.BlockSpec((1, D_MODEL), lambda i: (0, 0)),
            pl.BlockSpec((1, D_MODEL), lambda i: (0, 0)),
        ],
        out_specs=pl.BlockSpec((LN_BM, D_MODEL), lambda i: (i, 0)),
        compiler_params=_cparams(("arbitrary",)),
        name="add_ln",
    )(x2d, y2d, lg, lb)


def _moe(x2d, rw_t, rb_t, wg, wu, wd, lg, lb):
    t = x2d.shape[0]
    cls3, w3 = _router(x2d, rw_t, rb_t)
    cls = cls3.reshape(t)
    w_lo = w3[:, 0, :].reshape(t)
    w_hi = w3[:, 1, :].reshape(t)
    counts = jnp.bincount(cls, length=N_CLASSES)
    padded = ((counts + MOE_BLOCK - 1) // MOE_BLOCK) * MOE_BLOCK
    ends_pad = jnp.cumsum(padded)
    starts_pad = ends_pad - padded
    starts = jnp.cumsum(counts) - counts
    order = jnp.argsort(cls)
    cls_sorted = cls[order]
    dest_sorted = starts_pad[cls_sorted] + jnp.arange(t, dtype=jnp.int32) - starts[cls_sorted]
    n_blocks = (t + N_CLASSES * (MOE_BLOCK - 1)) // MOE_BLOCK
    n_rows = n_blocks * MOE_BLOCK
    xs = jnp.zeros((n_rows, D_MODEL), F32).at[dest_sorted].set(x2d[order])
    gw = jnp.zeros((n_rows, 128), F32).at[dest_sorted, 0].set(w_lo[order]).at[dest_sorted, 1].set(w_hi[order])
    n_used = (ends_pad[-1] // MOE_BLOCK).astype(jnp.int32)
    blk = jnp.arange(n_blocks, dtype=jnp.int32)
    blk_cls = jnp.searchsorted(ends_pad, jnp.minimum(blk, n_used - 1) * MOE_BLOCK, side='right')
    blk_cls = jnp.clip(blk_cls, 0, N_CLASSES - 1).astype(jnp.int32)
    grp = blk_cls // N_PAIRS
    pair = blk_cls % N_PAIRS
    blk_a = grp * EXPERTS_PER_GROUP + jnp.asarray(_PAIR_LO)[pair]
    blk_b = grp * EXPERTS_PER_GROUP + jnp.asarray(_PAIR_HI)[pair]
    ys = _experts(blk_a, blk_b, n_used.reshape(1), xs, gw, wg, wu, wd)
    dest = jnp.zeros((t,), jnp.int32).at[order].set(dest_sorted)
    y = ys[dest]
    return _add_ln(x2d, y, lg, lb)


def _rope_tables(seq):
    half = RET_DK // 2
    pos = jnp.arange(seq, dtype=F32)
    freqs = ROPE_BASE ** (-jnp.arange(half, dtype=F32) / half)
    ang = pos[:, None] * freqs[None, :]
    return jnp.cos(ang), jnp.sin(ang)


def kernel(x, w_in, ret_decay_logit, w_ret_o, na_rpb, w_na_o, w_out, ln_mix_g, ln_mix_b, router_w, router_bias,
           w_exp_gate, w_exp_up, w_exp_down, ln_ffn_g, ln_ffn_b):
    b, s, d = x.shape
    depth = w_in.shape[0]
    t = b * s
    rows = s // GRID_W
    cos, sin = _rope_tables(s)
    rw_t = router_w.astype(F32).T.reshape(N_GROUPS, EXPERTS_PER_GROUP, d).transpose(1, 0, 2).reshape(N_EXPERTS, d)
    rb_t = router_bias.astype(F32).reshape(N_GROUPS, EXPERTS_PER_GROUP).T.reshape(N_EXPERTS, 1)
    x2d = x.reshape(t, d)
    for l in range(depth):
        proj = _inproj(x2d, w_in[l].astype(BF16), cos, sin, s)
        proj3 = proj.reshape(b, s, D_IN)
        ret = _retention(proj3, ret_decay_logit[l].astype(F32))
        bias_tab = _na_bias_table(na_rpb[l].astype(F32), rows)
        na = _na(proj3, bias_tab)
        x2d = _mixout(ret.reshape(t, -1), na.reshape(t, -1), proj, x2d,
                      w_ret_o[l].astype(BF16), w_na_o[l].astype(BF16), w_out[l].astype(BF16),
                      ln_mix_g[l].reshape(1, d).astype(F32), ln_mix_b[l].reshape(1, d).astype(F32))
        x2d = _moe(x2d, rw_t, rb_t, w_exp_gate[l].astype(BF16), w_exp_up[l].astype(BF16),
                   w_exp_down[l].astype(BF16),
                   ln_ffn_g[l].reshape(1, d).astype(F32), ln_ffn_b[l].reshape(1, d).astype(F32))
    return x2d.reshape(b, s, d)
```

```python
import functools

import numpy as np
import jax
import jax.numpy as jnp
from jax import lax
from jax.experimental import pallas as pl
from jax.experimental.pallas import tpu as pltpu

F32 = jnp.float32
BF16 = jnp.bfloat16

D_MODEL = 1024
GRID_W = 64
RET_HEADS = 4
RET_DK = 256
RET_DV = 512
ROPE_BASE = 10000.0
NA_HEADS = 16
NA_HD = 64
NA_KH = 8
NA_KW = 16
N_EXPERTS = 32
N_GROUPS = 8
EXPERTS_PER_GROUP = 4
D_EXPERT = 512
LN_EPS = 1e-5
GN_EPS = 1e-5
MODEL_DEPTH = 4
DEEPNORM_ALPHA = (2 * MODEL_DEPTH) ** 0.25

COL_Q_R = 0
COL_K_R = 1024
COL_V_R = 2048
COL_G_R = 4096
COL_Q_N = 6144
COL_K_N = 7168
COL_V_N = 8192
COL_GATE = 9216
D_IN = 11264

RET_CHUNK = 256
NA_QROWS = 4
NA_KROWS = 12
NA_NEG = -1e30
MOE_BLOCK = 256
N_PAIRS = 6
N_CLASSES = N_GROUPS * N_PAIRS
VMEM_LIMIT = 56 * 1024 * 1024


def _cparams(sem):
    return pltpu.CompilerParams(dimension_semantics=sem, vmem_limit_bytes=VMEM_LIMIT)


IN_BM = 1024
IN_BN = 1024
_ROTARY_BLOCKS = {0: 1.0, 1: RET_DK ** -0.5}
_SCALE_BLOCKS = {2: 1.0, 3: 1.0, 6: NA_HD ** -0.5, 7: 1.0, 8: 1.0}
_SILU_BLOCKS = (4, 5)
_SIGMOID_BLOCKS = (9, 10)


def _inproj_kernel(x_ref, w_ref, cos_ref, sin_ref, o_ref):
    j = pl.program_id(0)
    acc = jnp.dot(x_ref[...].astype(BF16), w_ref[...], preferred_element_type=F32)

    for blk, scale in _ROTARY_BLOCKS.items():
        @pl.when(j == blk)
        def _(scale=scale):
            cos = cos_ref[...]
            sin = sin_ref[...]
            half = RET_DK // 2
            for h in range(IN_BN // RET_DK):
                x1 = acc[:, h * RET_DK:h * RET_DK + half]
                x2 = acc[:, h * RET_DK + half:(h + 1) * RET_DK]
                o_ref[:, h * RET_DK:h * RET_DK + half] = ((x1 * cos - x2 * sin) * scale).astype(o_ref.dtype)
                o_ref[:, h * RET_DK + half:(h + 1) * RET_DK] = ((x1 * sin + x2 * cos) * scale).astype(o_ref.dtype)

    for blk, scale in _SCALE_BLOCKS.items():
        @pl.when(j == blk)
        def _(scale=scale):
            o_ref[...] = (acc * scale).astype(o_ref.dtype)

    @pl.when((j == _SILU_BLOCKS[0]) | (j == _SILU_BLOCKS[1]))
    def _():
        o_ref[...] = (acc * jax.nn.sigmoid(acc)).astype(o_ref.dtype)

    @pl.when((j == _SIGMOID_BLOCKS[0]) | (j == _SIGMOID_BLOCKS[1]))
    def _():
        o_ref[...] = jax.nn.sigmoid(acc).astype(o_ref.dtype)


def _inproj(x2d, w_bf, cos, sin, seq):
    t = x2d.shape[0]
    pos_blocks = seq // IN_BM
    return pl.pallas_call(
        _inproj_kernel,
        out_shape=jax.ShapeDtypeStruct((t, D_IN), BF16),
        grid=(D_IN // IN_BN, t // IN_BM),
        in_specs=[
            pl.BlockSpec((IN_BM, D_MODEL), lambda j, i: (i, 0)),
            pl.BlockSpec((D_MODEL, IN_BN), lambda j, i: (0, j)),
            pl.BlockSpec((IN_BM, RET_DK // 2), lambda j, i: (i % pos_blocks, 0)),
            pl.BlockSpec((IN_BM, RET_DK // 2), lambda j, i: (i % pos_blocks, 0)),
        ],
        out_specs=pl.BlockSpec((IN_BM, IN_BN), lambda j, i: (i, j)),
        compiler_params=_cparams(("arbitrary", "arbitrary")),
        name="inproj",
    )(x2d, w_bf, cos, sin)


def _log_sigmoid(x):
    return jnp.minimum(x, 0.0) - jnp.log1p(jnp.exp(-jnp.abs(x)))


def _retention_kernel(dl_ref, q_ref, k_ref, v_ref, g_ref, o_ref,
                      sf_ref, st_ref, dm_ref, qdf_ref, qdb_ref, kdf_ref, kdb_ref, *, nc):
    c_len = RET_CHUNK
    h = pl.program_id(1)
    lgf = _log_sigmoid(jnp.full((c_len, RET_DK), dl_ref[0, h], F32))
    lgb = _log_sigmoid(jnp.full((c_len, RET_DK), dl_ref[1, h], F32))
    ri = lax.broadcasted_iota(jnp.int32, (c_len, RET_DK), 0).astype(F32)
    qdf_ref[...] = jnp.exp(lgf * (ri + 1.0))
    qdb_ref[...] = jnp.exp(lgb * (c_len - ri))
    kdf_ref[...] = jnp.exp(lgf * (c_len - 1.0 - ri))
    kdb_ref[...] = jnp.exp(lgb * ri)
    rr = lax.broadcasted_iota(jnp.int32, (c_len, c_len), 0).astype(F32)
    cc = lax.broadcasted_iota(jnp.int32, (c_len, c_len), 1).astype(F32)
    diff = rr - cc
    lgf_cc = _log_sigmoid(jnp.full((c_len, c_len), dl_ref[0, h], F32))
    lgb_cc = _log_sigmoid(jnp.full((c_len, c_len), dl_ref[1, h], F32))
    dm_ref[...] = jnp.where(diff >= 0.0, jnp.exp(lgf_cc * jnp.maximum(diff, 0.0)),
                            jnp.exp(lgb_cc * jnp.maximum(-diff, 0.0)))
    chunk_f = jnp.exp(lgf[:1, :1] * float(c_len))
    chunk_b = jnp.exp(lgb[:1, :1] * float(c_len))

    tn_dims = (((0,), (0,)), ((), ()))
    nt_dims = (((1,), (1,)), ((), ()))

    st_ref[...] = jnp.zeros_like(st_ref)

    def fwd_body(c, carry):
        off = pl.multiple_of(c * c_len, c_len)
        sf_ref[c] = st_ref[...].astype(BF16)
        kc = k_ref[0, pl.ds(off, c_len), :].astype(F32)
        vc = v_ref[0, pl.ds(off, c_len), :]
        kd = (kc * kdf_ref[...]).astype(BF16)
        upd = lax.dot_general(kd, vc, tn_dims, preferred_element_type=F32)
        st_ref[...] = st_ref[...] * chunk_f + upd
        return carry

    lax.fori_loop(0, nc, fwd_body, 0)

    st_ref[...] = jnp.zeros_like(st_ref)

    def bwd_body(i, carry):
        c = nc - 1 - i
        off = pl.multiple_of(c * c_len, c_len)
        qb = q_ref[0, pl.ds(off, c_len), :]
        kb = k_ref[0, pl.ds(off, c_len), :]
        vc = v_ref[0, pl.ds(off, c_len), :]
        qc = qb.astype(F32)
        kc = kb.astype(F32)
        s = lax.dot_general(qb, kb, nt_dims, preferred_element_type=F32)
        p = (s * dm_ref[...]).astype(BF16)
        out = jnp.dot(p, vc, preferred_element_type=F32)
        out = out + jnp.dot((qc * qdf_ref[...]).astype(BF16), sf_ref[c], preferred_element_type=F32)
        out = out + jnp.dot((qc * qdb_ref[...]).astype(BF16), st_ref[...].astype(BF16),
                            preferred_element_type=F32)
        mu = jnp.mean(out, axis=-1, keepdims=True)
        cen = out - mu
        var = jnp.mean(cen * cen, axis=-1, keepdims=True)
        y = cen * lax.rsqrt(var + GN_EPS)
        gate = g_ref[0, pl.ds(off, c_len), :].astype(F32)
        o_ref[0, pl.ds(off, c_len), :] = (gate * y).astype(o_ref.dtype)
        kd = (kc * kdb_ref[...]).astype(BF16)
        upd = lax.dot_general(kd, vc, tn_dims, preferred_element_type=F32)
        st_ref[...] = st_ref[...] * chunk_b + upd
        return carry

    lax.fori_loop(0, nc, bwd_body, 0)


def _retention(proj3, decay_logit):
    b, s, _ = proj3.shape
    nc = s // RET_CHUNK
    kq = COL_K_R // RET_DK
    kv = COL_V_R // RET_DV
    kg = COL_G_R // RET_DV
    return pl.pallas_call(
        functools.partial(_retention_kernel, nc=nc),
        out_shape=jax.ShapeDtypeStruct((b, s, RET_HEADS * RET_DV), BF16),
        grid=(b, RET_HEADS),
        in_specs=[
            pl.BlockSpec(memory_space=pltpu.SMEM),
            pl.BlockSpec((1, s, RET_DK), lambda bi, h: (bi, 0, h)),
            pl.BlockSpec((1, s, RET_DK), lambda bi, h: (bi, 0, kq + h)),
            pl.BlockSpec((1, s, RET_DV), lambda bi, h: (bi, 0, kv + h)),
            pl.BlockSpec((1, s, RET_DV), lambda bi, h: (bi, 0, kg + h)),
        ],
        out_specs=pl.BlockSpec((1, s, RET_DV), lambda bi, h: (bi, 0, h)),
        scratch_shapes=[
            pltpu.VMEM((nc, RET_DK, RET_DV), BF16),
            pltpu.VMEM((RET_DK, RET_DV), F32),
            pltpu.VMEM((RET_CHUNK, RET_CHUNK), F32),
            pltpu.VMEM((RET_CHUNK, RET_DK), F32),
            pltpu.VMEM((RET_CHUNK, RET_DK), F32),
            pltpu.VMEM((RET_CHUNK, RET_DK), F32),
            pltpu.VMEM((RET_CHUNK, RET_DK), F32),
        ],
        compiler_params=_cparams(("arbitrary", "arbitrary")),
        name="retention",
    )(decay_logit, proj3, proj3, proj3, proj3)


NA_ROW_OFFS = 2 * NA_KH - 1
NA_COL_OFFS = 2 * NA_KW - 1
NA_PATTERNS = 3


def _na_row_offsets(rows):
    n_tiles = rows // NA_QROWS
    offs = np.full((NA_PATTERNS, NA_QROWS, NA_KROWS), NA_ROW_OFFS, np.int32)
    for p, t in enumerate((0, 1, n_tiles - 1)):
        kstart = int(np.clip(NA_QROWS * t - NA_KH // 2, 0, rows - NA_KROWS))
        for rr in range(NA_QROWS):
            r = NA_QROWS * t + rr
            rs = int(np.clip(r - NA_KH // 2, 0, rows - NA_KH))
            for i in range(NA_KROWS):
                krow = kstart + i
                if rs <= krow < rs + NA_KH:
                    offs[p, rr, i] = krow - r + (NA_KH - 1)
    return offs


def _na_build_bias(rpb_ref, e_ref, bias_ref, head0, rows):
    wide = 2 * GRID_W
    lane = lax.broadcasted_iota(jnp.int32, (GRID_W, wide), 1)
    col = lax.broadcasted_iota(jnp.int32, (GRID_W, wide), 0)
    kcol = jnp.where(lane < GRID_W, lane, lane - GRID_W)
    cstart = jnp.clip(col - NA_KW // 2, 0, GRID_W - NA_KW)
    col_ok = (kcol >= cstart) & (kcol < cstart + NA_KW)
    coff = kcol - col + (NA_KW - 1)
    neg = jnp.full((GRID_W, wide), NA_NEG, F32)
    left = lane < GRID_W
    offs = _na_row_offsets(rows)
    for hh in range(2):
        base = (head0 + hh) * (NA_ROW_OFFS * NA_COL_OFFS)
        for ro in range(NA_ROW_OFFS):
            acc = neg
            for d in range(NA_COL_OFFS):
                acc = jnp.where(coff == d, rpb_ref[base + ro * NA_COL_OFFS + d], acc)
            e_ref[hh, ro] = jnp.where(col_ok, acc, neg)
        e_ref[hh, NA_ROW_OFFS] = neg
        for p in range(NA_PATTERNS):
            for rr in range(NA_QROWS):
                for ip in range(NA_KROWS // 2):
                    tile = jnp.where(left, e_ref[hh, int(offs[p, rr, 2 * ip])], e_ref[hh, int(offs[p, rr, 2 * ip + 1])])
                    bias_ref[hh, p, rr * GRID_W:(rr + 1) * GRID_W, ip * wide:(ip + 1) * wide] = tile


def _na_kernel(rpb_ref, q_ref, k_ref, v_ref, o_ref, bias_ref, e_ref, *, n_tiles, rows):
    nq = NA_QROWS * GRID_W
    nk = NA_KROWS * GRID_W
    nt_dims = (((1,), (1,)), ((), ()))
    lane = lax.broadcasted_iota(jnp.int32, (nq, 2 * NA_HD), 1)
    first = lane < NA_HD

    @pl.when(pl.program_id(1) == 0)
    def _():
        _na_build_bias(rpb_ref, e_ref, bias_ref, 2 * pl.program_id(0), rows)

    def body(t, carry):
        qoff = pl.multiple_of(t * nq, nq)
        krow0 = jnp.clip(NA_QROWS * t - NA_KH // 2, 0, rows - NA_KROWS)
        koff = pl.multiple_of(krow0 * GRID_W, nq)
        pat = jnp.where(t == 0, 0, jnp.where(t == n_tiles - 1, 2, 1))
        q2 = q_ref[0, pl.ds(qoff, nq), :]
        k2 = k_ref[0, pl.ds(koff, nk), :]
        v2 = v_ref[0, pl.ds(koff, nk), :]
        outs = []
        for hh in range(2):
            qm = jnp.where(first if hh == 0 else jnp.logical_not(first), q2, jnp.zeros_like(q2))
            s = lax.dot_general(qm, k2, nt_dims, preferred_element_type=F32)
            s = s + bias_ref[hh, pat]
            m = jnp.max(s, axis=-1, keepdims=True)
            e = jnp.exp(s - m)
            l = jnp.sum(e, axis=-1, keepdims=True)
            o = jnp.dot(e.astype(BF16), v2, preferred_element_type=F32)
            outs.append(o / l)
        o_ref[0, pl.ds(qoff, nq), :] = jnp.where(first, outs[0], outs[1]).astype(o_ref.dtype)
        return carry

    lax.fori_loop(0, n_tiles, body, 0)


def _na(proj3, rpb_flat):
    b, s, _ = proj3.shape
    rows = s // GRID_W
    n_tiles = rows // NA_QROWS
    w2 = 2 * NA_HD
    nq = NA_QROWS * GRID_W
    nk = NA_KROWS * GRID_W
    return pl.pallas_call(
        functools.partial(_na_kernel, n_tiles=n_tiles, rows=rows),
        out_shape=jax.ShapeDtypeStruct((b, s, NA_HEADS * NA_HD), BF16),
        grid=(NA_HEADS // 2, b),
        in_specs=[
            pl.BlockSpec(memory_space=pltpu.SMEM),
            pl.BlockSpec((1, s, w2), lambda hp, bi: (bi, 0, COL_Q_N // w2 + hp)),
            pl.BlockSpec((1, s, w2), lambda hp, bi: (bi, 0, COL_K_N // w2 + hp)),
            pl.BlockSpec((1, s, w2), lambda hp, bi: (bi, 0, COL_V_N // w2 + hp)),
        ],
        out_specs=pl.BlockSpec((1, s, w2), lambda hp, bi: (bi, 0, hp)),
        scratch_shapes=[
            pltpu.VMEM((2, NA_PATTERNS, nq, nk), F32),
            pltpu.VMEM((2, NA_ROW_OFFS + 1, GRID_W, 2 * GRID_W), F32),
        ],
        compiler_params=_cparams(("arbitrary", "arbitrary")),
        name="natten",
    )(rpb_flat, proj3, proj3, proj3)


MIX_BM = 512


def _layernorm_rows(z, g, b):
    mu = jnp.mean(z, axis=-1, keepdims=True)
    cen = z - mu
    var = jnp.mean(cen * cen, axis=-1, keepdims=True)
    return cen * lax.rsqrt(var + LN_EPS) * g + b


def _mixout_kernel(ret_ref, na_ref, g1_ref, g2_ref, x_ref, wr_ref, wn_ref, wo_ref, lg_ref, lb_ref, o_ref):
    y_ret = jnp.dot(ret_ref[...], wr_ref[...], preferred_element_type=F32)
    y_na = jnp.dot(na_ref[...], wn_ref[...], preferred_element_type=F32)
    merged = g1_ref[...].astype(F32) * y_ret + g2_ref[...].astype(F32) * y_na
    mix = jnp.dot(merged.astype(BF16), wo_ref[...], preferred_element_type=F32)
    z = DEEPNORM_ALPHA * x_ref[...] + mix
    o_ref[...] = _layernorm_rows(z, lg_ref[...], lb_ref[...])


def _mixout(ret2d, na2d, proj2d, x2d, wr, wn, wo, lg, lb):
    t = x2d.shape[0]
    gate_blk = COL_GATE // D_MODEL
    const = lambda i: (0, 0)
    return pl.pallas_call(
        _mixout_kernel,
        out_shape=jax.ShapeDtypeStruct((t, D_MODEL), F32),
        grid=(t // MIX_BM,),
        in_specs=[
            pl.BlockSpec((MIX_BM, RET_HEADS * RET_DV), lambda i: (i, 0)),
            pl.BlockSpec((MIX_BM, NA_HEADS * NA_HD), lambda i: (i, 0)),
            pl.BlockSpec((MIX_BM, D_MODEL), lambda i: (i, gate_blk)),
            pl.BlockSpec((MIX_BM, D_MODEL), lambda i: (i, gate_blk + 1)),
            pl.BlockSpec((MIX_BM, D_MODEL), lambda i: (i, 0)),
            pl.BlockSpec((RET_HEADS * RET_DV, D_MODEL), const),
            pl.BlockSpec((NA_HEADS * NA_HD, D_MODEL), const),
            pl.BlockSpec((D_MODEL, D_MODEL), const),
            pl.BlockSpec((1, D_MODEL), const),
            pl.BlockSpec((1, D_MODEL), const),
        ],
        out_specs=pl.BlockSpec((MIX_BM, D_MODEL), lambda i: (i, 0)),
        compiler_params=_cparams(("arbitrary",)),
        name="mixout",
    )(ret2d, na2d, proj2d, proj2d, x2d, wr, wn, wo, lg, lb)


ROUTE_BM = 512


def _route_tile(x, rw, rb):
    nt_dims = (((1,), (1,)), ((), ()))
    logits = lax.dot_general(rw, x, nt_dims, preferred_element_type=F32, precision=lax.Precision.HIGHEST)
    scores = jax.nn.sigmoid(logits)
    sel = scores + rb
    p = [scores[m * N_GROUPS:(m + 1) * N_GROUPS] for m in range(EXPERTS_PER_GROUP)]
    s = [sel[m * N_GROUPS:(m + 1) * N_GROUPS] for m in range(EXPERTS_PER_GROUP)]
    one = jnp.ones_like(s[0])
    zero = jnp.zeros_like(s[0])
    chosen = []
    for m in range(EXPERTS_PER_GROUP):
        rank = zero
        for j in range(EXPERTS_PER_GROUP):
            if j == m:
                continue
            beats = (s[j] >= s[m]) if j < m else (s[j] > s[m])
            rank = rank + jnp.where(beats, one, zero)
        chosen.append(rank < 2.0)
    group_score = zero
    for m in range(EXPERTS_PER_GROUP):
        group_score = group_score + jnp.where(chosen[m], s[m], zero)
    gid = lax.broadcasted_iota(jnp.int32, group_score.shape, 0)
    gmax = jnp.max(group_score, axis=0, keepdims=True)
    gbest = jnp.min(jnp.where(group_score == gmax, gid, N_GROUPS), axis=0, keepdims=True)
    in_best = gid == gbest
    picked = [chosen[m] & in_best for m in range(EXPERTS_PER_GROUP)]
    before = zero
    wa = zero
    wb = zero
    ma = zero
    mb = zero
    for m in range(EXPERTS_PER_GROUP):
        is_a = picked[m] & (before == 0.0)
        is_b = picked[m] & (before == 1.0)
        wa = wa + jnp.where(is_a, p[m], zero)
        wb = wb + jnp.where(is_b, p[m], zero)
        ma = ma + jnp.where(is_a, float(m), 0.0)
        mb = mb + jnp.where(is_b, float(m), 0.0)
        before = before + jnp.where(chosen[m], one, zero)
    wa = jnp.sum(wa, axis=0, keepdims=True)
    wb = jnp.sum(wb, axis=0, keepdims=True)
    ma = jnp.sum(ma, axis=0, keepdims=True)
    mb = jnp.sum(mb, axis=0, keepdims=True)
    denom = wa + wb
    base = jnp.where(ma == 0.0, 0.0, jnp.where(ma == 1.0, 3.0, 5.0))
    pair = base + mb - ma - 1.0
    return gbest * N_PAIRS + pair.astype(jnp.int32), wa / denom, wb / denom


CLASS_ROWS = 64
META_LANES = 128


def _router_kernel(x_ref, rw_ref, rb_ref, tri_ref, dest_ref, wtok_ref, meta_ref, cls_s, rank_s, cnt_s):
    phase = pl.program_id(0)
    i = pl.program_id(1)
    bm = ROUTE_BM
    reps = bm // META_LANES
    cid = lax.broadcasted_iota(jnp.int32, (CLASS_ROWS, bm), 0)

    @pl.when((phase == 0) & (i == 0))
    def _():
        cnt_s[...] = jnp.zeros_like(cnt_s)

    @pl.when(phase == 0)
    def _():
        cls, w_lo, w_hi = _route_tile(x_ref[...], rw_ref[...], rb_ref[...])
        cls_s[i] = cls
        w_rows = jnp.concatenate([w_lo, w_hi, jnp.zeros((META_LANES - 2, bm), F32)], axis=0)
        wtok_ref[...] = w_rows.T
        onehot = (cid == cls).astype(BF16)
        before = jnp.dot(onehot, tri_ref[...], preferred_element_type=F32)
        carry = jnp.concatenate([cnt_s[...]] * reps, axis=1)
        rank = jnp.sum(jnp.where(cid == cls, before + carry, 0.0), axis=0, keepdims=True)
        rank_s[i] = rank.astype(jnp.int32)
        cnt_s[...] = cnt_s[...] + jnp.dot(onehot, jnp.ones((bm, META_LANES), BF16), preferred_element_type=F32)

    @pl.when(phase == 1)
    def _():
        cnt = cnt_s[...]
        nblk = jnp.floor((cnt + (MOE_BLOCK - 1.0)) * (1.0 / MOE_BLOCK))
        rr = lax.broadcasted_iota(jnp.int32, (CLASS_ROWS, CLASS_ROWS), 0)
        cc = lax.broadcasted_iota(jnp.int32, (CLASS_ROWS, CLASS_ROWS), 1)
        lower = (cc < rr).astype(BF16)
        start_blk = jnp.dot(lower, nblk.astype(BF16), preferred_element_type=F32)
        cls = cls_s[i]
        start_t = jnp.concatenate([start_blk] * reps, axis=1)
        start = jnp.sum(jnp.where(cid == cls, start_t, 0.0), axis=0, keepdims=True)
        dest_ref[0] = (start * float(MOE_BLOCK)).astype(jnp.int32) + rank_s[i]

        @pl.when(i == 0)
        def _():
            end_blk = start_blk + nblk
            n_used = jnp.max(end_blk, axis=0, keepdims=True)
            blk = lax.broadcasted_iota(jnp.int32, (CLASS_ROWS, META_LANES), 1).astype(F32)
            blk = jnp.minimum(blk, n_used - 1.0)
            bcls = jnp.sum(jnp.where(end_blk <= blk, 1.0, 0.0), axis=0, keepdims=True)
            grp = jnp.zeros_like(bcls)
            for g in range(1, N_GROUPS):
                grp = grp + jnp.where(bcls >= float(g * N_PAIRS), 1.0, 0.0)
            pair = bcls - grp * float(N_PAIRS)
            lo = jnp.where(pair < 3.0, 0.0, jnp.where(pair < 5.0, 1.0, 2.0))
            hi = jnp.where(pair == 0.0, 1.0, jnp.where((pair == 1.0) | (pair == 3.0), 2.0, 3.0))
            rows = [grp * float(EXPERTS_PER_GROUP) + lo, grp * float(EXPERTS_PER_GROUP) + hi, n_used]
            rows = rows + [jnp.zeros_like(bcls)] * (8 - len(rows))
            meta_ref[...] = jnp.concatenate(rows, axis=0).astype(jnp.int32)


def _router(x2d, rw_t, rb_t):
    t = x2d.shape[0]
    nt = t // ROUTE_BM
    tri = jnp.asarray(np.triu(np.ones((ROUTE_BM, ROUTE_BM), np.float32), 1), BF16)
    hold = lambda ph, i: ((1 - ph) * i + ph * (nt - 1), 0)
    const = lambda ph, i: (0, 0)
    return pl.pallas_call(
        _router_kernel,
        out_shape=(jax.ShapeDtypeStruct((nt, 1, ROUTE_BM), jnp.int32),
                   jax.ShapeDtypeStruct((t, META_LANES), F32),
                   jax.ShapeDtypeStruct((8, META_LANES), jnp.int32)),
        grid=(2, nt),
        in_specs=[
            pl.BlockSpec((ROUTE_BM, D_MODEL), hold),
            pl.BlockSpec((N_EXPERTS, D_MODEL), const),
            pl.BlockSpec((N_EXPERTS, 1), const),
            pl.BlockSpec((ROUTE_BM, ROUTE_BM), const),
        ],
        out_specs=(pl.BlockSpec((1, 1, ROUTE_BM), lambda ph, i: (ph * i, 0, 0)),
                   pl.BlockSpec((ROUTE_BM, META_LANES), hold),
                   pl.BlockSpec((8, META_LANES), const)),
        scratch_shapes=[
            pltpu.VMEM((nt, 1, ROUTE_BM), jnp.int32),
            pltpu.VMEM((nt, 1, ROUTE_BM), jnp.int32),
            pltpu.VMEM((CLASS_ROWS, META_LANES), F32),
        ],
        compiler_params=_cparams(("arbitrary", "arbitrary")),
        name="router",
    )(x2d, rw_t, rb_t, tri)


ROW_BM = 256


def _dispatch_kernel(dest_ref, x_ref, zeros_ref, xs_ref, sem):
    del zeros_ref
    base = pl.program_id(0) * ROW_BM

    def row_copy(r):
        return pltpu.make_async_copy(x_ref.at[pl.ds(r, 1), :], xs_ref.at[pl.ds(dest_ref[base + r], 1), :], sem)

    def start(r, carry):
        row_copy(r).start()
        return carry

    def wait(r, carry):
        row_copy(r).wait()
        return carry

    lax.fori_loop(0, ROW_BM, start, 0)
    lax.fori_loop(0, ROW_BM, wait, 0)


def _dispatch(dest, x2d, n_rows):
    t = x2d.shape[0]
    grid_spec = pltpu.PrefetchScalarGridSpec(
        num_scalar_prefetch=1,
        grid=(t // ROW_BM,),
        in_specs=[
            pl.BlockSpec((ROW_BM, D_MODEL), lambda i, dest: (i, 0)),
            pl.BlockSpec(memory_space=pl.ANY),
        ],
        out_specs=pl.BlockSpec(memory_space=pl.ANY),
        scratch_shapes=[pltpu.SemaphoreType.DMA],
    )
    return pl.pallas_call(
        _dispatch_kernel,
        out_shape=jax.ShapeDtypeStruct((n_rows, D_MODEL), F32),
        grid_spec=grid_spec,
        input_output_aliases={2: 0},
        compiler_params=_cparams(("arbitrary",)),
        name="dispatch",
    )(dest, x2d, jnp.zeros((n_rows, D_MODEL), F32))


_PAIR_LO = np.array([0, 0, 0, 1, 1, 2], np.int32)
_PAIR_HI = np.array([1, 2, 3, 2, 3, 3], np.int32)


def _experts_kernel(ea_ref, eb_ref, nb_ref, x_ref, wga_ref, wua_ref, wda_ref, wgb_ref, wub_ref, wdb_ref, o_ref):
    @pl.when(pl.program_id(0) < nb_ref[0])
    def _():
        x = x_ref[...].astype(BF16)
        for wg_ref, wu_ref, wd_ref, col in ((wga_ref, wua_ref, wda_ref, 0), (wgb_ref, wub_ref, wdb_ref, D_MODEL)):
            hg = jnp.dot(x, wg_ref[0], preferred_element_type=F32)
            hu = jnp.dot(x, wu_ref[0], preferred_element_type=F32)
            act = (hg * jax.nn.sigmoid(hg) * hu).astype(BF16)
            o_ref[:, col:col + D_MODEL] = jnp.dot(act, wd_ref[0], preferred_element_type=F32)

    @pl.when(pl.program_id(0) >= nb_ref[0])
    def _():
        o_ref[...] = jnp.zeros_like(o_ref)


def _experts(blk_a, blk_b, n_used, xs, wg, wu, wd):
    n_rows = xs.shape[0]
    n_blocks = n_rows // MOE_BLOCK
    wa_map = lambda j, ea, eb, nb: (ea[j], 0, 0)
    wb_map = lambda j, ea, eb, nb: (eb[j], 0, 0)
    in_row_map = lambda j, ea, eb, nb: (jnp.minimum(j, nb[0] - 1), 0)
    row_map = lambda j, ea, eb, nb: (j, 0)
    grid_spec = pltpu.PrefetchScalarGridSpec(
        num_scalar_prefetch=3,
        grid=(n_blocks,),
        in_specs=[
            pl.BlockSpec((MOE_BLOCK, D_MODEL), in_row_map),
            pl.BlockSpec((1, D_MODEL, D_EXPERT), wa_map),
            pl.BlockSpec((1, D_MODEL, D_EXPERT), wa_map),
            pl.BlockSpec((1, D_EXPERT, D_MODEL), wa_map),
            pl.BlockSpec((1, D_MODEL, D_EXPERT), wb_map),
            pl.BlockSpec((1, D_MODEL, D_EXPERT), wb_map),
            pl.BlockSpec((1, D_EXPERT, D_MODEL), wb_map),
        ],
        out_specs=pl.BlockSpec((MOE_BLOCK, 2 * D_MODEL), row_map),
    )
    return pl.pallas_call(
        _experts_kernel,
        out_shape=jax.ShapeDtypeStruct((n_rows, 2 * D_MODEL), F32),
        grid_spec=grid_spec,
        compiler_params=_cparams(("arbitrary",)),
        name="experts",
    )(blk_a, blk_b, n_used, xs, wg, wu, wd, wg, wu, wd)


def _combine_ln_kernel(dest_ref, x_ref, wtok_ref, lg_ref, lb_ref, ys_ref, o_ref, ybuf, sem):
    base = pl.program_id(0) * ROW_BM

    def row_copy(r):
        return pltpu.make_async_copy(ys_ref.at[pl.ds(dest_ref[base + r], 1), :], ybuf.at[pl.ds(r, 1), :], sem)

    def start(r, carry):
        row_copy(r).start()
        return carry

    def wait(r, carry):
        row_copy(r).wait()
        return carry

    lax.fori_loop(0, ROW_BM, start, 0)
    lax.fori_loop(0, ROW_BM, wait, 0)
    w = wtok_ref[...]
    y = w[:, 0:1] * ybuf[:, :D_MODEL] + w[:, 1:2] * ybuf[:, D_MODEL:]
    z = DEEPNORM_ALPHA * x_ref[...] + y
    o_ref[...] = _layernorm_rows(z, lg_ref[...], lb_ref[...])


def _combine_ln(dest, x2d, wtok, ys, lg, lb):
    t = x2d.shape[0]
    const = lambda i, dest: (0, 0)
    grid_spec = pltpu.PrefetchScalarGridSpec(
        num_scalar_prefetch=1,
        grid=(t // ROW_BM,),
        in_specs=[
            pl.BlockSpec((ROW_BM, D_MODEL), lambda i, dest: (i, 0)),
            pl.BlockSpec((ROW_BM, META_LANES), lambda i, dest: (i, 0)),
            pl.BlockSpec((1, D_MODEL), const),
            pl.BlockSpec((1, D_MODEL), const),
            pl.BlockSpec(memory_space=pl.ANY),
        ],
        out_specs=pl.BlockSpec((ROW_BM, D_MODEL), lambda i, dest: (i, 0)),
        scratch_shapes=[pltpu.VMEM((ROW_BM, 2 * D_MODEL), F32), pltpu.SemaphoreType.DMA],
    )
    return pl.pallas_call(
        _combine_ln_kernel,
        out_shape=jax.ShapeDtypeStruct((t, D_MODEL), F32),
        grid_spec=grid_spec,
        compiler_params=_cparams(("arbitrary",)),
        name="combine_ln",
    )(dest, x2d, wtok, lg, lb, ys)


def _moe(x2d, rw_t, rb_t, wg, wu, wd, lg, lb):
    t = x2d.shape[0]
    n_blocks = (t + N_CLASSES * (MOE_BLOCK - 1)) // MOE_BLOCK
    assert n_blocks <= META_LANES
    dest3, wtok, meta = _router(x2d, rw_t, rb_t)
    dest = dest3.reshape(t)
    xs = _dispatch(dest, x2d, n_blocks * MOE_BLOCK)
    ys = _experts(meta[0, :n_blocks], meta[1, :n_blocks], meta[2, :1], xs, wg, wu, wd)
    return _combine_ln(dest, x2d, wtok, ys, lg, lb)


def _rope_tables(seq):
    half = RET_DK // 2
    pos = jnp.arange(seq, dtype=F32)
    freqs = ROPE_BASE ** (-jnp.arange(half, dtype=F32) / half)
    ang = pos[:, None] * freqs[None, :]
    return jnp.cos(ang), jnp.sin(ang)


def kernel(x, w_in, ret_decay_logit, w_ret_o, na_rpb, w_na_o, w_out, ln_mix_g, ln_mix_b, router_w, router_bias,
           w_exp_gate, w_exp_up, w_exp_down, ln_ffn_g, ln_ffn_b):
    b, s, d = x.shape
    depth = w_in.shape[0]
    t = b * s
    rows = s // GRID_W
    cos, sin = _rope_tables(s)
    rw_t = router_w.astype(F32).T.reshape(N_GROUPS, EXPERTS_PER_GROUP, d).transpose(1, 0, 2).reshape(N_EXPERTS, d)
    rb_t = router_bias.astype(F32).reshape(N_GROUPS, EXPERTS_PER_GROUP).T.reshape(N_EXPERTS, 1)
    x2d = x.reshape(t, d)
    for l in range(depth):
        proj = _inproj(x2d, w_in[l].astype(BF16), cos, sin, s)
        proj3 = proj.reshape(b, s, D_IN)
        ret = _retention(proj3, ret_decay_logit[l].astype(F32))
        na = _na(proj3, na_rpb[l].astype(F32).reshape(-1))
        x2d = _mixout(ret.reshape(t, -1), na.reshape(t, -1), proj, x2d,
                      w_ret_o[l].astype(BF16), w_na_o[l].astype(BF16), w_out[l].astype(BF16),
                      ln_mix_g[l].reshape(1, d).astype(F32), ln_mix_b[l].reshape(1, d).astype(F32))
        x2d = _moe(x2d, rw_t, rb_t, w_exp_gate[l].astype(BF16), w_exp_up[l].astype(BF16),
                   w_exp_down[l].astype(BF16),
                   ln_ffn_g[l].reshape(1, d).astype(F32), ln_ffn_b[l].reshape(1, d).astype(F32))
    return x2d.reshape(b, s, d)
```

```python
import functools

import numpy as np
import jax
import jax.numpy as jnp
from jax import lax
from jax.experimental import pallas as pl
from jax.experimental.pallas import tpu as pltpu

F32 = jnp.float32
BF16 = jnp.bfloat16

D_MODEL = 1024
GRID_W = 64
RET_HEADS = 4
RET_DK = 256
RET_DV = 512
ROPE_BASE = 10000.0
NA_HEADS = 16
NA_HD = 64
NA_KH = 8
NA_KW = 16
N_EXPERTS = 32
N_GROUPS = 8
EXPERTS_PER_GROUP = 4
D_EXPERT = 512
LN_EPS = 1e-5
GN_EPS = 1e-5
MODEL_DEPTH = 4
DEEPNORM_ALPHA = (2 * MODEL_DEPTH) ** 0.25

COL_Q_R = 0
COL_K_R = 1024
COL_V_R = 2048
COL_G_R = 4096
COL_Q_N = 6144
COL_K_N = 7168
COL_V_N = 8192
COL_GATE = 9216
D_IN = 11264

RET_CHUNK = 256
NA_QROWS = 4
NA_KROWS = 12
NA_NEG = -1e30
MOE_BLOCK = 256
N_PAIRS = 6
N_CLASSES = N_GROUPS * N_PAIRS
VMEM_LIMIT = 56 * 1024 * 1024


def _cparams(sem):
    return pltpu.CompilerParams(dimension_semantics=sem, vmem_limit_bytes=VMEM_LIMIT)


IN_BM = 1024
IN_BN = 1024
IN_CHUNK = 256
LOG2E = 1.4426950408889634

IN_GROUPS = {
    "rotary": (0, 1),
    "scale": (2, 3, 6, 7, 8),
    "silu": (4, 5),
    "sigmoid": (9, 10),
}
VN_Q = 2048
VN_K = 3072
VN_V = 4096


def _inproj_kernel(x_ref, w_ref, cos_ref, sin_ref, o_ref, wbf_ref, *, kind):
    j = pl.program_id(0)

    @pl.when(pl.program_id(1) == 0)
    def _():
        wbf_ref[...] = w_ref[0].astype(BF16)

    x = x_ref[...]
    half = RET_DK // 2
    for c in range(IN_BN // IN_CHUNK):
        lo = c * IN_CHUNK
        acc = jnp.dot(x, wbf_ref[:, lo:lo + IN_CHUNK], preferred_element_type=F32)
        if kind == "rotary":
            scale = jnp.where(j == 0, 1.0, RET_DK ** -0.5)
            cos = cos_ref[...] * scale
            sin = sin_ref[...] * scale
            x1 = acc[:, :half]
            x2 = acc[:, half:]
            o_ref[:, lo:lo + half] = (x1 * cos - x2 * sin).astype(o_ref.dtype)
            o_ref[:, lo + half:lo + IN_CHUNK] = (x1 * sin + x2 * cos).astype(o_ref.dtype)
        elif kind == "scale":
            scale = jnp.where(j == 2, NA_HD ** -0.5 * LOG2E, 1.0)
            o_ref[:, lo:lo + IN_CHUNK] = (acc * scale).astype(o_ref.dtype)
        elif kind == "silu":
            o_ref[:, lo:lo + IN_CHUNK] = (acc * jax.nn.sigmoid(acc)).astype(o_ref.dtype)
        else:
            o_ref[:, lo:lo + IN_CHUNK] = jax.nn.sigmoid(acc).astype(o_ref.dtype)


def _inproj(x_bf, w_in, layer, cos, sin, seq, kind):
    t = x_bf.shape[0]
    pos_blocks = seq // IN_BM
    blocks = IN_GROUPS[kind]
    first, n_first = blocks[0], sum(1 for k, b in enumerate(blocks) if b == blocks[0] + k)
    jump = blocks[n_first] - (first + n_first) if n_first < len(blocks) else 0
    wcol = lambda j: first + j + jnp.where(j >= n_first, jump, 0)
    pos_map = (lambda j, i: (i % pos_blocks, 0)) if kind == "rotary" else (lambda j, i: (0, 0))
    return pl.pallas_call(
        functools.partial(_inproj_kernel, kind=kind),
        out_shape=jax.ShapeDtypeStruct((t, len(blocks) * IN_BN), BF16),
        grid=(len(blocks), t // IN_BM),
        in_specs=[
            pl.BlockSpec((IN_BM, D_MODEL), lambda j, i: (i, 0)),
            pl.BlockSpec((1, D_MODEL, IN_BN), lambda j, i: (layer, 0, wcol(j))),
            pl.BlockSpec((IN_BM, RET_DK // 2), pos_map),
            pl.BlockSpec((IN_BM, RET_DK // 2), pos_map),
        ],
        out_specs=pl.BlockSpec((IN_BM, IN_BN), lambda j, i: (i, j)),
        scratch_shapes=[pltpu.VMEM((D_MODEL, IN_BN), BF16)],
        compiler_params=_cparams(("arbitrary", "arbitrary")),
        name="inproj_" + kind,
    )(x_bf, w_in, cos, sin)


def _log_sigmoid(x):
    return jnp.minimum(x, 0.0) - jnp.log1p(jnp.exp(-jnp.abs(x)))


def _retention_kernel(dl_ref, q_ref, k_ref, v_ref, g_ref, o_ref,
                      sf_ref, st_ref, dm_ref, qdf_ref, qdb_ref, kdf_ref, kdb_ref, *, nc):
    c_len = RET_CHUNK
    h = pl.program_id(1)
    lgf = _log_sigmoid(jnp.full((c_len, RET_DK), dl_ref[0, h], F32))
    lgb = _log_sigmoid(jnp.full((c_len, RET_DK), dl_ref[1, h], F32))
    ri = lax.broadcasted_iota(jnp.int32, (c_len, RET_DK), 0).astype(F32)
    qdf_ref[...] = jnp.exp(lgf * (ri + 1.0))
    qdb_ref[...] = jnp.exp(lgb * (c_len - ri))
    kdf_ref[...] = jnp.exp(lgf * (c_len - 1.0 - ri))
    kdb_ref[...] = jnp.exp(lgb * ri)
    rr = lax.broadcasted_iota(jnp.int32, (c_len, c_len), 0).astype(F32)
    cc = lax.broadcasted_iota(jnp.int32, (c_len, c_len), 1).astype(F32)
    diff = rr - cc
    lgf_cc = _log_sigmoid(jnp.full((c_len, c_len), dl_ref[0, h], F32))
    lgb_cc = _log_sigmoid(jnp.full((c_len, c_len), dl_ref[1, h], F32))
    dm_ref[...] = jnp.where(diff >= 0.0, jnp.exp(lgf_cc * jnp.maximum(diff, 0.0)),
                            jnp.exp(lgb_cc * jnp.maximum(-diff, 0.0)))
    chunk_f = jnp.exp(lgf[:1, :1] * float(c_len))
    chunk_b = jnp.exp(lgb[:1, :1] * float(c_len))

    tn_dims = (((0,), (0,)), ((), ()))
    nt_dims = (((1,), (1,)), ((), ()))

    st_ref[...] = jnp.zeros_like(st_ref)

    def fwd_body(c, carry):
        off = pl.multiple_of(c * c_len, c_len)
        sf_ref[c] = st_ref[...].astype(BF16)
        kc = k_ref[0, pl.ds(off, c_len), :].astype(F32)
        vc = v_ref[0, pl.ds(off, c_len), :]
        kd = (kc * kdf_ref[...]).astype(BF16)
        upd = lax.dot_general(kd, vc, tn_dims, preferred_element_type=F32)
        st_ref[...] = st_ref[...] * chunk_f + upd
        return carry

    lax.fori_loop(0, nc, fwd_body, 0, unroll=4)

    st_ref[...] = jnp.zeros_like(st_ref)

    def bwd_body(i, carry):
        c = nc - 1 - i
        off = pl.multiple_of(c * c_len, c_len)
        qb = q_ref[0, pl.ds(off, c_len), :]
        kb = k_ref[0, pl.ds(off, c_len), :]
        vc = v_ref[0, pl.ds(off, c_len), :]
        qc = qb.astype(F32)
        kc = kb.astype(F32)
        s = lax.dot_general(qb, kb, nt_dims, preferred_element_type=F32)
        p = (s * dm_ref[...]).astype(BF16)
        out = jnp.dot(p, vc, preferred_element_type=F32)
        out = out + jnp.dot((qc * qdf_ref[...]).astype(BF16), sf_ref[c], preferred_element_type=F32)
        out = out + jnp.dot((qc * qdb_ref[...]).astype(BF16), st_ref[...].astype(BF16),
                            preferred_element_type=F32)
        mu = jnp.mean(out, axis=-1, keepdims=True)
        cen = out - mu
        var = jnp.mean(cen * cen, axis=-1, keepdims=True)
        y = cen * lax.rsqrt(var + GN_EPS)
        gate = g_ref[0, pl.ds(off, c_len), :].astype(F32)
        o_ref[0, pl.ds(off, c_len), :] = (gate * y).astype(o_ref.dtype)
        kd = (kc * kdb_ref[...]).astype(BF16)
        upd = lax.dot_general(kd, vc, tn_dims, preferred_element_type=F32)
        st_ref[...] = st_ref[...] * chunk_b + upd
        return carry

    lax.fori_loop(0, nc, bwd_body, 0, unroll=4)


def _retention(qk3, vn3, gr3, decay_logit):
    b, s, _ = qk3.shape
    nc = s // RET_CHUNK
    return pl.pallas_call(
        functools.partial(_retention_kernel, nc=nc),
        out_shape=jax.ShapeDtypeStruct((b, s, RET_HEADS * RET_DV), BF16),
        grid=(b, RET_HEADS),
        in_specs=[
            pl.BlockSpec(memory_space=pltpu.SMEM),
            pl.BlockSpec((1, s, RET_DK), lambda bi, h: (bi, 0, h)),
            pl.BlockSpec((1, s, RET_DK), lambda bi, h: (bi, 0, RET_HEADS + h)),
            pl.BlockSpec((1, s, RET_DV), lambda bi, h: (bi, 0, h)),
            pl.BlockSpec((1, s, RET_DV), lambda bi, h: (bi, 0, h)),
        ],
        out_specs=pl.BlockSpec((1, s, RET_DV), lambda bi, h: (bi, 0, h)),
        scratch_shapes=[
            pltpu.VMEM((nc, RET_DK, RET_DV), BF16),
            pltpu.VMEM((RET_DK, RET_DV), F32),
            pltpu.VMEM((RET_CHUNK, RET_CHUNK), F32),
            pltpu.VMEM((RET_CHUNK, RET_DK), F32),
            pltpu.VMEM((RET_CHUNK, RET_DK), F32),
            pltpu.VMEM((RET_CHUNK, RET_DK), F32),
            pltpu.VMEM((RET_CHUNK, RET_DK), F32),
        ],
        compiler_params=_cparams(("arbitrary", "arbitrary")),
        name="retention",
    )(decay_logit, qk3, qk3, vn3, gr3)


NA_ROW_OFFS = 2 * NA_KH - 1
NA_COL_OFFS = 2 * NA_KW - 1
NA_PATTERNS = 3


def _na_row_offsets(rows):
    n_tiles = rows // NA_QROWS
    offs = np.full((NA_PATTERNS, NA_QROWS, NA_KROWS), NA_ROW_OFFS, np.int32)
    for p, t in enumerate((0, 1, n_tiles - 1)):
        kstart = int(np.clip(NA_QROWS * t - NA_KH // 2, 0, rows - NA_KROWS))
        for rr in range(NA_QROWS):
            r = NA_QROWS * t + rr
            rs = int(np.clip(r - NA_KH // 2, 0, rows - NA_KH))
            for i in range(NA_KROWS):
                krow = kstart + i
                if rs <= krow < rs + NA_KH:
                    offs[p, rr, i] = krow - r + (NA_KH - 1)
    return offs


def _na_build_bias(rpb_ref, e_ref, bias_ref, head0, rows):
    wide = 2 * GRID_W
    lane = lax.broadcasted_iota(jnp.int32, (GRID_W, wide), 1)
    col = lax.broadcasted_iota(jnp.int32, (GRID_W, wide), 0)
    kcol = jnp.where(lane < GRID_W, lane, lane - GRID_W)
    cstart = jnp.clip(col - NA_KW // 2, 0, GRID_W - NA_KW)
    col_ok = (kcol >= cstart) & (kcol < cstart + NA_KW)
    coff = kcol - col + (NA_KW - 1)
    neg = jnp.full((GRID_W, wide), NA_NEG, F32)
    left = lane < GRID_W
    offs = _na_row_offsets(rows)
    for hh in range(2):
        base = (head0 + hh) * (NA_ROW_OFFS * NA_COL_OFFS)
        for ro in range(NA_ROW_OFFS):
            acc = neg
            for d in range(NA_COL_OFFS):
                acc = jnp.where(coff == d, rpb_ref[base + ro * NA_COL_OFFS + d] * LOG2E, acc)
            e_ref[hh, ro] = jnp.where(col_ok, acc, neg)
        e_ref[hh, NA_ROW_OFFS] = neg
        for p in range(NA_PATTERNS):
            for rr in range(NA_QROWS):
                for ip in range(NA_KROWS // 2):
                    tile = jnp.where(left, e_ref[hh, int(offs[p, rr, 2 * ip])], e_ref[hh, int(offs[p, rr, 2 * ip + 1])])
                    bias_ref[hh, p, rr * GRID_W:(rr + 1) * GRID_W, ip * wide:(ip + 1) * wide] = tile


def _na_kernel(rpb_ref, q_ref, k_ref, v_ref, o_ref, bias_ref, e_ref, *, n_tiles, rows):
    nq = NA_QROWS * GRID_W
    nk = NA_KROWS * GRID_W
    nt_dims = (((1,), (1,)), ((), ()))
    first = lax.broadcasted_iota(jnp.int32, (nq, 2 * NA_HD), 1) < NA_HD
    kfirst = lax.broadcasted_iota(jnp.int32, (nk, 2 * NA_HD), 1) < NA_HD

    @pl.when(pl.program_id(1) == 0)
    def _():
        _na_build_bias(rpb_ref, e_ref, bias_ref, 2 * pl.program_id(0), rows)

    def body(t, carry):
        qoff = pl.multiple_of(t * nq, nq)
        krow0 = jnp.clip(NA_QROWS * t - NA_KH // 2, 0, rows - NA_KROWS)
        koff = pl.multiple_of(krow0 * GRID_W, nq)
        pat = jnp.where(t == 0, 0, jnp.where(t == n_tiles - 1, 2, 1))
        q2 = q_ref[0, pl.ds(qoff, nq), :]
        k2 = k_ref[0, pl.ds(koff, nk), :]
        v2 = v_ref[0, pl.ds(koff, nk), :]
        outs = []
        for hh in range(2):
            qm = jnp.where(first if hh == 0 else jnp.logical_not(first), q2, jnp.zeros_like(q2))
            s = lax.dot_general(qm, k2, nt_dims, preferred_element_type=F32)
            s = s + bias_ref[hh, pat]
            m = jnp.max(s, axis=-1, keepdims=True)
            e = jnp.exp2(s - m).astype(BF16)
            vv = jnp.where(kfirst if hh == 0 else jnp.logical_not(kfirst), v2, jnp.ones_like(v2))
            o = jnp.dot(e, vv, preferred_element_type=F32)
            outs.append(o / pltpu.roll(o, NA_HD, axis=1))
        o_ref[0, pl.ds(qoff, nq), :] = jnp.where(first, outs[0], outs[1]).astype(o_ref.dtype)
        return carry

    lax.fori_loop(0, n_tiles, body, 0, unroll=4)


def _na(vn3, rpb_flat):
    b, s, _ = vn3.shape
    rows = s // GRID_W
    n_tiles = rows // NA_QROWS
    w2 = 2 * NA_HD
    nq = NA_QROWS * GRID_W
    nk = NA_KROWS * GRID_W
    return pl.pallas_call(
        functools.partial(_na_kernel, n_tiles=n_tiles, rows=rows),
        out_shape=jax.ShapeDtypeStruct((b, s, NA_HEADS * NA_HD), BF16),
        grid=(NA_HEADS // 2, b),
        in_specs=[
            pl.BlockSpec(memory_space=pltpu.SMEM),
            pl.BlockSpec((1, s, w2), lambda hp, bi: (bi, 0, VN_Q // w2 + hp)),
            pl.BlockSpec((1, s, w2), lambda hp, bi: (bi, 0, VN_K // w2 + hp)),
            pl.BlockSpec((1, s, w2), lambda hp, bi: (bi, 0, VN_V // w2 + hp)),
        ],
        out_specs=pl.BlockSpec((1, s, w2), lambda hp, bi: (bi, 0, hp)),
        scratch_shapes=[
            pltpu.VMEM((2, NA_PATTERNS, nq, nk), F32),
            pltpu.VMEM((2, NA_ROW_OFFS + 1, GRID_W, 2 * GRID_W), F32),
        ],
        compiler_params=_cparams(("arbitrary", "arbitrary")),
        name="natten",
    )(rpb_flat, vn3, vn3, vn3)


MIX_BM = 512


def _layernorm_rows(z, g, b):
    mu = jnp.mean(z, axis=-1, keepdims=True)
    cen = z - mu
    var = jnp.mean(cen * cen, axis=-1, keepdims=True)
    return cen * lax.rsqrt(var + LN_EPS) * g + b


def _mixout_kernel(ret_ref, na_ref, g1_ref, g2_ref, x_ref, wr_ref, wn_ref, wo_ref, lg_ref, lb_ref, o_ref):
    y_ret = jnp.dot(ret_ref[...], wr_ref[...], preferred_element_type=F32)
    y_na = jnp.dot(na_ref[...], wn_ref[...], preferred_element_type=F32)
    merged = g1_ref[...].astype(F32) * y_ret + g2_ref[...].astype(F32) * y_na
    mix = jnp.dot(merged.astype(BF16), wo_ref[...], preferred_element_type=F32)
    z = DEEPNORM_ALPHA * x_ref[...] + mix
    o_ref[...] = _layernorm_rows(z, lg_ref[...], lb_ref[...])


def _mixout(ret2d, na2d, gates2d, x2d, wr, wn, wo, lg, lb):
    t = x2d.shape[0]
    gate_blk = 0
    const = lambda i: (0, 0)
    return pl.pallas_call(
        _mixout_kernel,
        out_shape=jax.ShapeDtypeStruct((t, D_MODEL), F32),
        grid=(t // MIX_BM,),
        in_specs=[
            pl.BlockSpec((MIX_BM, RET_HEADS * RET_DV), lambda i: (i, 0)),
            pl.BlockSpec((MIX_BM, NA_HEADS * NA_HD), lambda i: (i, 0)),
            pl.BlockSpec((MIX_BM, D_MODEL), lambda i: (i, gate_blk)),
            pl.BlockSpec((MIX_BM, D_MODEL), lambda i: (i, gate_blk + 1)),
            pl.BlockSpec((MIX_BM, D_MODEL), lambda i: (i, 0)),
            pl.BlockSpec((RET_HEADS * RET_DV, D_MODEL), const),
            pl.BlockSpec((NA_HEADS * NA_HD, D_MODEL), const),
            pl.BlockSpec((D_MODEL, D_MODEL), const),
            pl.BlockSpec((1, D_MODEL), const),
            pl.BlockSpec((1, D_MODEL), const),
        ],
        out_specs=pl.BlockSpec((MIX_BM, D_MODEL), lambda i: (i, 0)),
        compiler_params=_cparams(("arbitrary",)),
        name="mixout",
    )(ret2d, na2d, gates2d, gates2d, x2d, wr, wn, wo, lg, lb)


ROUTE_BM = 512


def _route_tile(x, rw, rb):
    nt_dims = (((1,), (1,)), ((), ()))
    logits = lax.dot_general(rw, x, nt_dims, preferred_element_type=F32, precision=lax.Precision.HIGHEST)
    scores = jax.nn.sigmoid(logits)
    sel = scores + rb
    p = [scores[m * N_GROUPS:(m + 1) * N_GROUPS] for m in range(EXPERTS_PER_GROUP)]
    s = [sel[m * N_GROUPS:(m + 1) * N_GROUPS] for m in range(EXPERTS_PER_GROUP)]
    one = jnp.ones_like(s[0])
    zero = jnp.zeros_like(s[0])
    chosen = []
    for m in range(EXPERTS_PER_GROUP):
        rank = zero
        for j in range(EXPERTS_PER_GROUP):
            if j == m:
                continue
            beats = (s[j] >= s[m]) if j < m else (s[j] > s[m])
            rank = rank + jnp.where(beats, one, zero)
        chosen.append(rank < 2.0)
    group_score = zero
    for m in range(EXPERTS_PER_GROUP):
        group_score = group_score + jnp.where(chosen[m], s[m], zero)
    gid = lax.broadcasted_iota(jnp.int32, group_score.shape, 0)
    gmax = jnp.max(group_score, axis=0, keepdims=True)
    gbest = jnp.min(jnp.where(group_score == gmax, gid, N_GROUPS), axis=0, keepdims=True)
    in_best = gid == gbest
    picked = [chosen[m] & in_best for m in range(EXPERTS_PER_GROUP)]
    before = zero
    wa = zero
    wb = zero
    ma = zero
    mb = zero
    for m in range(EXPERTS_PER_GROUP):
        is_a = picked[m] & (before == 0.0)
        is_b = picked[m] & (before == 1.0)
        wa = wa + jnp.where(is_a, p[m], zero)
        wb = wb + jnp.where(is_b, p[m], zero)
        ma = ma + jnp.where(is_a, float(m), 0.0)
        mb = mb + jnp.where(is_b, float(m), 0.0)
        before = before + jnp.where(chosen[m], one, zero)
    wa = jnp.sum(wa, axis=0, keepdims=True)
    wb = jnp.sum(wb, axis=0, keepdims=True)
    ma = jnp.sum(ma, axis=0, keepdims=True)
    mb = jnp.sum(mb, axis=0, keepdims=True)
    denom = wa + wb
    base = jnp.where(ma == 0.0, 0.0, jnp.where(ma == 1.0, 3.0, 5.0))
    pair = base + mb - ma - 1.0
    return gbest * N_PAIRS + pair.astype(jnp.int32), wa / denom, wb / denom


CLASS_ROWS = 64
META_LANES = 128


def _router_kernel(x_ref, rw_ref, rb_ref, tri_ref, dest_ref, wtok_ref, meta_ref, cls_s, rank_s, cnt_s):
    phase = pl.program_id(0)
    i = pl.program_id(1)
    bm = ROUTE_BM
    reps = bm // META_LANES
    cid = lax.broadcasted_iota(jnp.int32, (CLASS_ROWS, bm), 0)

    @pl.when((phase == 0) & (i == 0))
    def _():
        cnt_s[...] = jnp.zeros_like(cnt_s)

    @pl.when(phase == 0)
    def _():
        cls, w_lo, w_hi = _route_tile(x_ref[...], rw_ref[...], rb_ref[...])
        cls_s[i] = cls
        w_rows = jnp.concatenate([w_lo, w_hi, jnp.zeros((META_LANES - 2, bm), F32)], axis=0)
        wtok_ref[...] = w_rows.T
        onehot = (cid == cls).astype(BF16)
        before = jnp.dot(onehot, tri_ref[...], preferred_element_type=F32)
        carry = jnp.concatenate([cnt_s[...]] * reps, axis=1)
        rank = jnp.sum(jnp.where(cid == cls, before + carry, 0.0), axis=0, keepdims=True)
        rank_s[i] = rank.astype(jnp.int32)
        cnt_s[...] = cnt_s[...] + jnp.dot(onehot, jnp.ones((bm, META_LANES), BF16), preferred_element_type=F32)

    @pl.when(phase == 1)
    def _():
        cnt = cnt_s[...]
        nblk = jnp.floor((cnt + (MOE_BLOCK - 1.0)) * (1.0 / MOE_BLOCK))
        rr = lax.broadcasted_iota(jnp.int32, (CLASS_ROWS, CLASS_ROWS), 0)
        cc = lax.broadcasted_iota(jnp.int32, (CLASS_ROWS, CLASS_ROWS), 1)
        lower = (cc < rr).astype(BF16)
        start_blk = jnp.dot(lower, nblk.astype(BF16), preferred_element_type=F32)
        cls = cls_s[i]
        start_t = jnp.concatenate([start_blk] * reps, axis=1)
        start = jnp.sum(jnp.where(cid == cls, start_t, 0.0), axis=0, keepdims=True)
        dest_ref[0] = (start * float(MOE_BLOCK)).astype(jnp.int32) + rank_s[i]

        @pl.when(i == 0)
        def _():
            end_blk = start_blk + nblk
            n_used = jnp.max(end_blk, axis=0, keepdims=True)
            blk = lax.broadcasted_iota(jnp.int32, (CLASS_ROWS, META_LANES), 1).astype(F32)
            blk = jnp.minimum(blk, n_used - 1.0)
            bcls = jnp.sum(jnp.where(end_blk <= blk, 1.0, 0.0), axis=0, keepdims=True)
            grp = jnp.zeros_like(bcls)
            for g in range(1, N_GROUPS):
                grp = grp + jnp.where(bcls >= float(g * N_PAIRS), 1.0, 0.0)
            pair = bcls - grp * float(N_PAIRS)
            lo = jnp.where(pair < 3.0, 0.0, jnp.where(pair < 5.0, 1.0, 2.0))
            hi = jnp.where(pair == 0.0, 1.0, jnp.where((pair == 1.0) | (pair == 3.0), 2.0, 3.0))
            rows = [grp * float(EXPERTS_PER_GROUP) + lo, grp * float(EXPERTS_PER_GROUP) + hi, n_used]
            rows = rows + [jnp.zeros_like(bcls)] * (8 - len(rows))
            meta_ref[...] = jnp.concatenate(rows, axis=0).astype(jnp.int32)


def _router(x2d, rw_t, rb_t):
    t = x2d.shape[0]
    nt = t // ROUTE_BM
    tri = jnp.asarray(np.triu(np.ones((ROUTE_BM, ROUTE_BM), np.float32), 1), BF16)
    hold = lambda ph, i: ((1 - ph) * i + ph * (nt - 1), 0)
    const = lambda ph, i: (0, 0)
    return pl.pallas_call(
        _router_kernel,
        out_shape=(jax.ShapeDtypeStruct((nt, 1, ROUTE_BM), jnp.int32),
                   jax.ShapeDtypeStruct((t, META_LANES), F32),
                   jax.ShapeDtypeStruct((8, META_LANES), jnp.int32)),
        grid=(2, nt),
        in_specs=[
            pl.BlockSpec((ROUTE_BM, D_MODEL), hold),
            pl.BlockSpec((N_EXPERTS, D_MODEL), const),
            pl.BlockSpec((N_EXPERTS, 1), const),
            pl.BlockSpec((ROUTE_BM, ROUTE_BM), const),
        ],
        out_specs=(pl.BlockSpec((1, 1, ROUTE_BM), lambda ph, i: (ph * i, 0, 0)),
                   pl.BlockSpec((ROUTE_BM, META_LANES), hold),
                   pl.BlockSpec((8, META_LANES), const)),
        scratch_shapes=[
            pltpu.VMEM((nt, 1, ROUTE_BM), jnp.int32),
            pltpu.VMEM((nt, 1, ROUTE_BM), jnp.int32),
            pltpu.VMEM((CLASS_ROWS, META_LANES), F32),
        ],
        compiler_params=_cparams(("arbitrary", "arbitrary")),
        name="router",
    )(x2d, rw_t, rb_t, tri)


ROW_BM = 256


ROW_UNROLL = 8


def _start_rows(make_copy):
    def body(g, carry):
        for u in range(ROW_UNROLL):
            make_copy(g * ROW_UNROLL + u).start(priority=u % 2)
        return carry

    lax.fori_loop(0, ROW_BM // ROW_UNROLL, body, 0)


def _dispatch_kernel(dest_ref, x_ref, zeros_ref, xs_ref, sem):
    del zeros_ref
    i = pl.program_id(0)
    base = i * ROW_BM
    _start_rows(lambda r: pltpu.make_async_copy(x_ref.at[pl.ds(base + r, 1), :],
                                                xs_ref.at[pl.ds(dest_ref[base + r], 1), :], sem))

    def wait_one_step():
        pltpu.make_async_copy(x_ref.at[pl.ds(0, ROW_BM), :], xs_ref.at[pl.ds(0, ROW_BM), :], sem).wait()

    @pl.when(i > 0)
    def _():
        wait_one_step()

    @pl.when(i == pl.num_programs(0) - 1)
    def _():
        wait_one_step()


def _dispatch(dest, x2d, n_rows):
    t = x2d.shape[0]
    grid_spec = pltpu.PrefetchScalarGridSpec(
        num_scalar_prefetch=1,
        grid=(t // ROW_BM,),
        in_specs=[
            pl.BlockSpec(memory_space=pl.ANY),
            pl.BlockSpec(memory_space=pl.ANY),
        ],
        out_specs=pl.BlockSpec(memory_space=pl.ANY),
        scratch_shapes=[pltpu.SemaphoreType.DMA],
    )
    return pl.pallas_call(
        _dispatch_kernel,
        out_shape=jax.ShapeDtypeStruct((n_rows, D_MODEL), F32),
        grid_spec=grid_spec,
        input_output_aliases={2: 0},
        compiler_params=_cparams(("arbitrary",)),
        name="dispatch",
    )(dest, x2d, jnp.zeros((n_rows, D_MODEL), F32))


_PAIR_LO = np.array([0, 0, 0, 1, 1, 2], np.int32)
_PAIR_HI = np.array([1, 2, 3, 2, 3, 3], np.int32)


def _experts_kernel(ea_ref, eb_ref, nb_ref, x_ref, wga_ref, wua_ref, wda_ref, wgb_ref, wub_ref, wdb_ref, o_ref):
    @pl.when(pl.program_id(0) < nb_ref[0])
    def _():
        x = x_ref[...].astype(BF16)
        for wg_ref, wu_ref, wd_ref, col in ((wga_ref, wua_ref, wda_ref, 0), (wgb_ref, wub_ref, wdb_ref, D_MODEL)):
            hg = jnp.dot(x, wg_ref[0], preferred_element_type=F32)
            hu = jnp.dot(x, wu_ref[0], preferred_element_type=F32)
            act = (hg * jax.nn.sigmoid(hg) * hu).astype(BF16)
            o_ref[:, col:col + D_MODEL] = jnp.dot(act, wd_ref[0], preferred_element_type=F32)

    @pl.when(pl.program_id(0) >= nb_ref[0])
    def _():
        o_ref[...] = jnp.zeros_like(o_ref)


def _experts(blk_a, blk_b, n_used, xs, wg, wu, wd):
    n_rows = xs.shape[0]
    n_blocks = n_rows // MOE_BLOCK
    wa_map = lambda j, ea, eb, nb: (ea[j], 0, 0)
    wb_map = lambda j, ea, eb, nb: (eb[j], 0, 0)
    in_row_map = lambda j, ea, eb, nb: (jnp.maximum(jnp.minimum(j, nb[0] - 1), 0), 0)
    row_map = lambda j, ea, eb, nb: (j, 0)
    grid_spec = pltpu.PrefetchScalarGridSpec(
        num_scalar_prefetch=3,
        grid=(n_blocks,),
        in_specs=[
            pl.BlockSpec((MOE_BLOCK, D_MODEL), in_row_map),
            pl.BlockSpec((1, D_MODEL, D_EXPERT), wa_map),
            pl.BlockSpec((1, D_MODEL, D_EXPERT), wa_map),
            pl.BlockSpec((1, D_EXPERT, D_MODEL), wa_map),
            pl.BlockSpec((1, D_MODEL, D_EXPERT), wb_map),
            pl.BlockSpec((1, D_MODEL, D_EXPERT), wb_map),
            pl.BlockSpec((1, D_EXPERT, D_MODEL), wb_map),
        ],
        out_specs=pl.BlockSpec((MOE_BLOCK, 2 * D_MODEL), row_map),
    )
    return pl.pallas_call(
        _experts_kernel,
        out_shape=jax.ShapeDtypeStruct((n_rows, 2 * D_MODEL), F32),
        grid_spec=grid_spec,
        compiler_params=_cparams(("arbitrary",)),
        name="experts",
    )(blk_a, blk_b, n_used, xs, wg, wu, wd, wg, wu, wd)


def _combine_ln_kernel(dest_ref, x_ref, wtok_ref, lg_ref, lb_ref, ys_ref, o_ref, obf_ref, ybuf, sem):
    i = pl.program_id(0)
    slot = i % 2

    def start_tile(tile, to_slot):
        base = tile * ROW_BM
        _start_rows(lambda r: pltpu.make_async_copy(ys_ref.at[pl.ds(dest_ref[base + r], 1), :],
                                                    ybuf.at[to_slot, pl.ds(r, 1), :], sem.at[to_slot]))

    @pl.when(i == 0)
    def _():
        start_tile(0, 0)

    @pl.when(i + 1 < pl.num_programs(0))
    def _():
        start_tile(i + 1, 1 - slot)

    pltpu.make_async_copy(ys_ref.at[pl.ds(0, ROW_BM), :], ybuf.at[slot], sem.at[slot]).wait()
    w = wtok_ref[...]
    y = w[:, 0:1] * ybuf[slot, :, :D_MODEL] + w[:, 1:2] * ybuf[slot, :, D_MODEL:]
    z = DEEPNORM_ALPHA * x_ref[...] + y
    out = _layernorm_rows(z, lg_ref[...], lb_ref[...])
    o_ref[...] = out
    obf_ref[...] = out.astype(BF16)


def _combine_ln(dest, x2d, wtok, ys, lg, lb):
    t = x2d.shape[0]
    const = lambda i, dest: (0, 0)
    grid_spec = pltpu.PrefetchScalarGridSpec(
        num_scalar_prefetch=1,
        grid=(t // ROW_BM,),
        in_specs=[
            pl.BlockSpec((ROW_BM, D_MODEL), lambda i, dest: (i, 0)),
            pl.BlockSpec((ROW_BM, META_LANES), lambda i, dest: (i, 0)),
            pl.BlockSpec((1, D_MODEL), const),
            pl.BlockSpec((1, D_MODEL), const),
            pl.BlockSpec(memory_space=pl.ANY),
        ],
        out_specs=(pl.BlockSpec((ROW_BM, D_MODEL), lambda i, dest: (i, 0)),
                   pl.BlockSpec((ROW_BM, D_MODEL), lambda i, dest: (i, 0))),
        scratch_shapes=[pltpu.VMEM((2, ROW_BM, 2 * D_MODEL), F32), pltpu.SemaphoreType.DMA((2,))],
    )
    return pl.pallas_call(
        _combine_ln_kernel,
        out_shape=(jax.ShapeDtypeStruct((t, D_MODEL), F32), jax.ShapeDtypeStruct((t, D_MODEL), BF16)),
        grid_spec=grid_spec,
        compiler_params=_cparams(("arbitrary",)),
        name="combine_ln",
    )(dest, x2d, wtok, lg, lb, ys)


def _moe(x2d, rw_t, rb_t, wg, wu, wd, lg, lb):
    t = x2d.shape[0]
    n_blocks = (t + N_CLASSES * (MOE_BLOCK - 1)) // MOE_BLOCK
    assert n_blocks <= META_LANES
    dest3, wtok, meta = _router(x2d, rw_t, rb_t)
    dest = dest3.reshape(t)
    xs = _dispatch(dest, x2d, n_blocks * MOE_BLOCK)
    ys = _experts(meta[0, :n_blocks], meta[1, :n_blocks], meta[2, :1], xs, wg, wu, wd)
    return _combine_ln(dest, x2d, wtok, ys, lg, lb)


def _rope_tables(seq):
    half = RET_DK // 2
    pos = jnp.arange(seq, dtype=F32)
    freqs = ROPE_BASE ** (-jnp.arange(half, dtype=F32) / half)
    ang = pos[:, None] * freqs[None, :]
    return jnp.cos(ang), jnp.sin(ang)


def kernel(x, w_in, ret_decay_logit, w_ret_o, na_rpb, w_na_o, w_out, ln_mix_g, ln_mix_b, router_w, router_bias,
           w_exp_gate, w_exp_up, w_exp_down, ln_ffn_g, ln_ffn_b):
    b, s, d = x.shape
    depth = w_in.shape[0]
    t = b * s
    rows = s // GRID_W
    cos, sin = _rope_tables(s)
    rw_t = router_w.astype(F32).T.reshape(N_GROUPS, EXPERTS_PER_GROUP, d).transpose(1, 0, 2).reshape(N_EXPERTS, d)
    rb_t = router_bias.astype(F32).reshape(N_GROUPS, EXPERTS_PER_GROUP).T.reshape(N_EXPERTS, 1)
    x2d = x.reshape(t, d)
    x_bf = x2d.astype(BF16)
    w_in = w_in.astype(F32)
    for l in range(depth):
        qk = _inproj(x_bf, w_in, l, cos, sin, s, "rotary").reshape(b, s, -1)
        vn = _inproj(x_bf, w_in, l, cos, sin, s, "scale").reshape(b, s, -1)
        gr = _inproj(x_bf, w_in, l, cos, sin, s, "silu").reshape(b, s, -1)
        gates = _inproj(x_bf, w_in, l, cos, sin, s, "sigmoid")
        ret = _retention(qk, vn, gr, ret_decay_logit[l].astype(F32))
        na = _na(vn, na_rpb[l].astype(F32).reshape(-1))
        x2d = _mixout(ret.reshape(t, -1), na.reshape(t, -1), gates, x2d,
                      w_ret_o[l].astype(BF16), w_na_o[l].astype(BF16), w_out[l].astype(BF16),
                      ln_mix_g[l].reshape(1, d).astype(F32), ln_mix_b[l].reshape(1, d).astype(F32))
        x2d, x_bf = _moe(x2d, rw_t, rb_t, w_exp_gate[l].astype(BF16), w_exp_up[l].astype(BF16),
                         w_exp_down[l].astype(BF16),
                         ln_ffn_g[l].reshape(1, d).astype(F32), ln_ffn_b[l].reshape(1, d).astype(F32))
    return x2d.reshape(b, s, d)
```

```python
import functools

import numpy as np
import jax
import jax.numpy as jnp
from jax import lax
from jax.experimental import pallas as pl
from jax.experimental.pallas import tpu as pltpu

F32 = jnp.float32
BF16 = jnp.bfloat16

D_MODEL = 1024
GRID_W = 64
RET_HEADS = 4
RET_DK = 256
RET_DV = 512
ROPE_BASE = 10000.0
NA_HEADS = 16
NA_HD = 64
NA_KH = 8
NA_KW = 16
N_EXPERTS = 32
N_GROUPS = 8
EXPERTS_PER_GROUP = 4
D_EXPERT = 512
LN_EPS = 1e-5
GN_EPS = 1e-5
MODEL_DEPTH = 4
DEEPNORM_ALPHA = (2 * MODEL_DEPTH) ** 0.25

COL_Q_R = 0
COL_K_R = 1024
COL_V_R = 2048
COL_G_R = 4096
COL_Q_N = 6144
COL_K_N = 7168
COL_V_N = 8192
COL_GATE = 9216
D_IN = 11264

RET_CHUNK = 256
NA_QROWS = 4
NA_KROWS = 12
NA_NEG = -1e30
MOE_BLOCK = 256
N_PAIRS = 6
N_CLASSES = N_GROUPS * N_PAIRS
VMEM_LIMIT = 56 * 1024 * 1024


def _cparams(sem):
    return pltpu.CompilerParams(dimension_semantics=sem, vmem_limit_bytes=VMEM_LIMIT)


IN_BM = 1024
IN_BN = 1024
IN_CHUNK = 256
LOG2E = 1.4426950408889634

IN_GROUPS = {
    "rotary": (0, 1),
    "scale": (2, 3, 6, 7, 8),
    "silu": (4, 5),
    "sigmoid": (9, 10),
}
VN_Q = 2048
VN_K = 3072
VN_V = 4096


def _inproj_kernel(x_ref, w_ref, cos_ref, sin_ref, o_ref, wbf_ref, *, kind):
    j = pl.program_id(0)

    @pl.when(pl.program_id(1) == 0)
    def _():
        wbf_ref[...] = w_ref[0].astype(BF16)

    x = x_ref[...]
    half = RET_DK // 2
    for c in range(IN_BN // IN_CHUNK):
        lo = c * IN_CHUNK
        acc = jnp.dot(x, wbf_ref[:, lo:lo + IN_CHUNK], preferred_element_type=F32)
        if kind == "rotary":
            scale = jnp.where(j == 0, 1.0, RET_DK ** -0.5)
            cos = cos_ref[...] * scale
            sin = sin_ref[...] * scale
            x1 = acc[:, :half]
            x2 = acc[:, half:]
            o_ref[:, lo:lo + half] = (x1 * cos - x2 * sin).astype(o_ref.dtype)
            o_ref[:, lo + half:lo + IN_CHUNK] = (x1 * sin + x2 * cos).astype(o_ref.dtype)
        elif kind == "scale":
            scale = jnp.where(j == 2, NA_HD ** -0.5 * LOG2E, 1.0)
            o_ref[:, lo:lo + IN_CHUNK] = (acc * scale).astype(o_ref.dtype)
        elif kind == "silu":
            o_ref[:, lo:lo + IN_CHUNK] = (acc * jax.nn.sigmoid(acc)).astype(o_ref.dtype)
        else:
            o_ref[:, lo:lo + IN_CHUNK] = jax.nn.sigmoid(acc).astype(o_ref.dtype)


def _inproj(x_bf, w_in, layer, cos, sin, seq, kind):
    t = x_bf.shape[0]
    pos_blocks = seq // IN_BM
    blocks = IN_GROUPS[kind]
    first, n_first = blocks[0], sum(1 for k, b in enumerate(blocks) if b == blocks[0] + k)
    jump = blocks[n_first] - (first + n_first) if n_first < len(blocks) else 0
    wcol = lambda j: first + j + jnp.where(j >= n_first, jump, 0)
    pos_map = (lambda j, i: (i % pos_blocks, 0)) if kind == "rotary" else (lambda j, i: (0, 0))
    return pl.pallas_call(
        functools.partial(_inproj_kernel, kind=kind),
        out_shape=jax.ShapeDtypeStruct((t, len(blocks) * IN_BN), BF16),
        grid=(len(blocks), t // IN_BM),
        in_specs=[
            pl.BlockSpec((IN_BM, D_MODEL), lambda j, i: (i, 0)),
            pl.BlockSpec((1, D_MODEL, IN_BN), lambda j, i: (layer, 0, wcol(j))),
            pl.BlockSpec((IN_BM, RET_DK // 2), pos_map),
            pl.BlockSpec((IN_BM, RET_DK // 2), pos_map),
        ],
        out_specs=pl.BlockSpec((IN_BM, IN_BN), lambda j, i: (i, j)),
        scratch_shapes=[pltpu.VMEM((D_MODEL, IN_BN), BF16)],
        compiler_params=_cparams(("arbitrary", "arbitrary")),
        name="inproj_" + kind,
    )(x_bf, w_in, cos, sin)


def _log_sigmoid(x):
    return jnp.minimum(x, 0.0) - jnp.log1p(jnp.exp(-jnp.abs(x)))


def _retention_kernel(dl_ref, q_ref, k_ref, v_ref, g_ref, o_ref,
                      sf_ref, st_ref, dm_ref, qdf_ref, qdb_ref, kdf_ref, kdb_ref, *, nc):
    c_len = RET_CHUNK
    h = pl.program_id(1)
    lgf = _log_sigmoid(jnp.full((c_len, RET_DK), dl_ref[0, h], F32))
    lgb = _log_sigmoid(jnp.full((c_len, RET_DK), dl_ref[1, h], F32))
    ri = lax.broadcasted_iota(jnp.int32, (c_len, RET_DK), 0).astype(F32)
    qdf_ref[...] = jnp.exp(lgf * (ri + 1.0))
    qdb_ref[...] = jnp.exp(lgb * (c_len - ri))
    kdf_ref[...] = jnp.exp(lgf * (c_len - 1.0 - ri))
    kdb_ref[...] = jnp.exp(lgb * ri)
    rr = lax.broadcasted_iota(jnp.int32, (c_len, c_len), 0).astype(F32)
    cc = lax.broadcasted_iota(jnp.int32, (c_len, c_len), 1).astype(F32)
    diff = rr - cc
    lgf_cc = _log_sigmoid(jnp.full((c_len, c_len), dl_ref[0, h], F32))
    lgb_cc = _log_sigmoid(jnp.full((c_len, c_len), dl_ref[1, h], F32))
    dm_ref[...] = jnp.where(diff >= 0.0, jnp.exp(lgf_cc * jnp.maximum(diff, 0.0)),
                            jnp.exp(lgb_cc * jnp.maximum(-diff, 0.0)))
    chunk_f = jnp.exp(lgf[:1, :1] * float(c_len))
    chunk_b = jnp.exp(lgb[:1, :1] * float(c_len))

    tn_dims = (((0,), (0,)), ((), ()))
    nt_dims = (((1,), (1,)), ((), ()))

    st_ref[...] = jnp.zeros_like(st_ref)

    def fwd_body(c, carry):
        off = pl.multiple_of(c * c_len, c_len)
        sf_ref[c] = st_ref[...].astype(BF16)
        kc = k_ref[0, pl.ds(off, c_len), :].astype(F32)
        vc = v_ref[0, pl.ds(off, c_len), :]
        kd = (kc * kdf_ref[...]).astype(BF16)
        upd = lax.dot_general(kd, vc, tn_dims, preferred_element_type=F32)
        st_ref[...] = st_ref[...] * chunk_f + upd
        return carry

    lax.fori_loop(0, nc, fwd_body, 0, unroll=4)

    st_ref[...] = jnp.zeros_like(st_ref)

    def bwd_body(i, carry):
        c = nc - 1 - i
        off = pl.multiple_of(c * c_len, c_len)
        qb = q_ref[0, pl.ds(off, c_len), :]
        kb = k_ref[0, pl.ds(off, c_len), :]
        vc = v_ref[0, pl.ds(off, c_len), :]
        qc = qb.astype(F32)
        kc = kb.astype(F32)
        s = lax.dot_general(qb, kb, nt_dims, preferred_element_type=F32)
        p = (s * dm_ref[...]).astype(BF16)
        out = jnp.dot(p, vc, preferred_element_type=F32)
        out = out + jnp.dot((qc * qdf_ref[...]).astype(BF16), sf_ref[c], preferred_element_type=F32)
        out = out + jnp.dot((qc * qdb_ref[...]).astype(BF16), st_ref[...].astype(BF16),
                            preferred_element_type=F32)
        mu = jnp.mean(out, axis=-1, keepdims=True)
        cen = out - mu
        var = jnp.mean(cen * cen, axis=-1, keepdims=True)
        y = cen * lax.rsqrt(var + GN_EPS)
        gate = g_ref[0, pl.ds(off, c_len), :].astype(F32)
        o_ref[0, pl.ds(off, c_len), :] = (gate * y).astype(o_ref.dtype)
        kd = (kc * kdb_ref[...]).astype(BF16)
        upd = lax.dot_general(kd, vc, tn_dims, preferred_element_type=F32)
        st_ref[...] = st_ref[...] * chunk_b + upd
        return carry

    lax.fori_loop(0, nc, bwd_body, 0, unroll=4)


def _retention(qk3, vn3, gr3, decay_logit):
    b, s, _ = qk3.shape
    nc = s // RET_CHUNK
    return pl.pallas_call(
        functools.partial(_retention_kernel, nc=nc),
        out_shape=jax.ShapeDtypeStruct((b, s, RET_HEADS * RET_DV), BF16),
        grid=(b, RET_HEADS),
        in_specs=[
            pl.BlockSpec(memory_space=pltpu.SMEM),
            pl.BlockSpec((1, s, RET_DK), lambda bi, h: (bi, 0, h)),
            pl.BlockSpec((1, s, RET_DK), lambda bi, h: (bi, 0, RET_HEADS + h)),
            pl.BlockSpec((1, s, RET_DV), lambda bi, h: (bi, 0, h)),
            pl.BlockSpec((1, s, RET_DV), lambda bi, h: (bi, 0, h)),
        ],
        out_specs=pl.BlockSpec((1, s, RET_DV), lambda bi, h: (bi, 0, h)),
        scratch_shapes=[
            pltpu.VMEM((nc, RET_DK, RET_DV), BF16),
            pltpu.VMEM((RET_DK, RET_DV), F32),
            pltpu.VMEM((RET_CHUNK, RET_CHUNK), F32),
            pltpu.VMEM((RET_CHUNK, RET_DK), F32),
            pltpu.VMEM((RET_CHUNK, RET_DK), F32),
            pltpu.VMEM((RET_CHUNK, RET_DK), F32),
            pltpu.VMEM((RET_CHUNK, RET_DK), F32),
        ],
        compiler_params=_cparams(("arbitrary", "arbitrary")),
        name="retention",
    )(decay_logit, qk3, qk3, vn3, gr3)


NA_ROW_OFFS = 2 * NA_KH - 1
NA_COL_OFFS = 2 * NA_KW - 1
NA_PATTERNS = 3


def _na_row_offsets(rows):
    n_tiles = rows // NA_QROWS
    offs = np.full((NA_PATTERNS, NA_QROWS, NA_KROWS), NA_ROW_OFFS, np.int32)
    for p, t in enumerate((0, 1, n_tiles - 1)):
        kstart = int(np.clip(NA_QROWS * t - NA_KH // 2, 0, rows - NA_KROWS))
        for rr in range(NA_QROWS):
            r = NA_QROWS * t + rr
            rs = int(np.clip(r - NA_KH // 2, 0, rows - NA_KH))
            for i in range(NA_KROWS):
                krow = kstart + i
                if rs <= krow < rs + NA_KH:
                    offs[p, rr, i] = krow - r + (NA_KH - 1)
    return offs


def _na_build_bias(rpb_ref, e_ref, bias_ref, head0, rows):
    wide = 2 * GRID_W
    lane = lax.broadcasted_iota(jnp.int32, (GRID_W, wide), 1)
    col = lax.broadcasted_iota(jnp.int32, (GRID_W, wide), 0)
    kcol = jnp.where(lane < GRID_W, lane, lane - GRID_W)
    cstart = jnp.clip(col - NA_KW // 2, 0, GRID_W - NA_KW)
    col_ok = (kcol >= cstart) & (kcol < cstart + NA_KW)
    coff = kcol - col + (NA_KW - 1)
    neg = jnp.full((GRID_W, wide), NA_NEG, F32)
    left = lane < GRID_W
    offs = _na_row_offsets(rows)
    for hh in range(2):
        base = (head0 + hh) * (NA_ROW_OFFS * NA_COL_OFFS)
        for ro in range(NA_ROW_OFFS):
            acc = neg
            for d in range(NA_COL_OFFS):
                acc = jnp.where(coff == d, rpb_ref[base + ro * NA_COL_OFFS + d] * LOG2E, acc)
            e_ref[hh, ro] = jnp.where(col_ok, acc, neg)
        e_ref[hh, NA_ROW_OFFS] = neg
        for p in range(NA_PATTERNS):
            for rr in range(NA_QROWS):
                for ip in range(NA_KROWS // 2):
                    tile = jnp.where(left, e_ref[hh, int(offs[p, rr, 2 * ip])], e_ref[hh, int(offs[p, rr, 2 * ip + 1])])
                    bias_ref[hh, p, rr * GRID_W:(rr + 1) * GRID_W, ip * wide:(ip + 1) * wide] = tile


def _na_kernel(rpb_ref, q_ref, k_ref, v_ref, o_ref, bias_ref, e_ref, *, n_tiles, rows):
    nq = NA_QROWS * GRID_W
    nk = NA_KROWS * GRID_W
    nt_dims = (((1,), (1,)), ((), ()))
    first = lax.broadcasted_iota(jnp.int32, (nq, 2 * NA_HD), 1) < NA_HD
    kfirst = lax.broadcasted_iota(jnp.int32, (nk, 2 * NA_HD), 1) < NA_HD

    @pl.when(pl.program_id(1) == 0)
    def _():
        _na_build_bias(rpb_ref, e_ref, bias_ref, 2 * pl.program_id(0), rows)

    def body(t, carry):
        qoff = pl.multiple_of(t * nq, nq)
        krow0 = jnp.clip(NA_QROWS * t - NA_KH // 2, 0, rows - NA_KROWS)
        koff = pl.multiple_of(krow0 * GRID_W, nq)
        pat = jnp.where(t == 0, 0, jnp.where(t == n_tiles - 1, 2, 1))
        q2 = q_ref[0, pl.ds(qoff, nq), :]
        k2 = k_ref[0, pl.ds(koff, nk), :]
        v2 = v_ref[0, pl.ds(koff, nk), :]
        outs = []
        for hh in range(2):
            qm = jnp.where(first if hh == 0 else jnp.logical_not(first), q2, jnp.zeros_like(q2))
            s = lax.dot_general(qm, k2, nt_dims, preferred_element_type=F32)
            s = s + bias_ref[hh, pat]
            m = jnp.max(s, axis=-1, keepdims=True)
            e = jnp.exp2(s - m).astype(BF16)
            vv = jnp.where(kfirst if hh == 0 else jnp.logical_not(kfirst), v2, jnp.ones_like(v2))
            o = jnp.dot(e, vv, preferred_element_type=F32)
            outs.append(o / pltpu.roll(o, NA_HD, axis=1))
        o_ref[0, pl.ds(qoff, nq), :] = jnp.where(first, outs[0], outs[1]).astype(o_ref.dtype)
        return carry

    lax.fori_loop(0, n_tiles, body, 0, unroll=4)


def _na(vn3, rpb_flat):
    b, s, _ = vn3.shape
    rows = s // GRID_W
    n_tiles = rows // NA_QROWS
    w2 = 2 * NA_HD
    nq = NA_QROWS * GRID_W
    nk = NA_KROWS * GRID_W
    return pl.pallas_call(
        functools.partial(_na_kernel, n_tiles=n_tiles, rows=rows),
        out_shape=jax.ShapeDtypeStruct((b, s, NA_HEADS * NA_HD), BF16),
        grid=(NA_HEADS // 2, b),
        in_specs=[
            pl.BlockSpec(memory_space=pltpu.SMEM),
            pl.BlockSpec((1, s, w2), lambda hp, bi: (bi, 0, VN_Q // w2 + hp)),
            pl.BlockSpec((1, s, w2), lambda hp, bi: (bi, 0, VN_K // w2 + hp)),
            pl.BlockSpec((1, s, w2), lambda hp, bi: (bi, 0, VN_V // w2 + hp)),
        ],
        out_specs=pl.BlockSpec((1, s, w2), lambda hp, bi: (bi, 0, hp)),
        scratch_shapes=[
            pltpu.VMEM((2, NA_PATTERNS, nq, nk), F32),
            pltpu.VMEM((2, NA_ROW_OFFS + 1, GRID_W, 2 * GRID_W), F32),
        ],
        compiler_params=_cparams(("arbitrary", "arbitrary")),
        name="natten",
    )(rpb_flat, vn3, vn3, vn3)


MIX_BM = 512


def _layernorm_rows(z, g, b):
    mu = jnp.mean(z, axis=-1, keepdims=True)
    cen = z - mu
    var = jnp.mean(cen * cen, axis=-1, keepdims=True)
    return cen * lax.rsqrt(var + LN_EPS) * g + b


def _mixout_kernel(ret_ref, na_ref, g1_ref, g2_ref, x_ref, wr_ref, wn_ref, wo_ref, lg_ref, lb_ref, o_ref):
    y_ret = jnp.dot(ret_ref[...], wr_ref[...], preferred_element_type=F32)
    y_na = jnp.dot(na_ref[...], wn_ref[...], preferred_element_type=F32)
    merged = g1_ref[...].astype(F32) * y_ret + g2_ref[...].astype(F32) * y_na
    mix = jnp.dot(merged.astype(BF16), wo_ref[...], preferred_element_type=F32)
    z = DEEPNORM_ALPHA * x_ref[...] + mix
    o_ref[...] = _layernorm_rows(z, lg_ref[...], lb_ref[...])


def _mixout(ret2d, na2d, gates2d, x2d, wr, wn, wo, lg, lb):
    t = x2d.shape[0]
    gate_blk = 0
    const = lambda i: (0, 0)
    return pl.pallas_call(
        _mixout_kernel,
        out_shape=jax.ShapeDtypeStruct((t, D_MODEL), F32),
        grid=(t // MIX_BM,),
        in_specs=[
            pl.BlockSpec((MIX_BM, RET_HEADS * RET_DV), lambda i: (i, 0)),
            pl.BlockSpec((MIX_BM, NA_HEADS * NA_HD), lambda i: (i, 0)),
            pl.BlockSpec((MIX_BM, D_MODEL), lambda i: (i, gate_blk)),
            pl.BlockSpec((MIX_BM, D_MODEL), lambda i: (i, gate_blk + 1)),
            pl.BlockSpec((MIX_BM, D_MODEL), lambda i: (i, 0)),
            pl.BlockSpec((RET_HEADS * RET_DV, D_MODEL), const),
            pl.BlockSpec((NA_HEADS * NA_HD, D_MODEL), const),
            pl.BlockSpec((D_MODEL, D_MODEL), const),
            pl.BlockSpec((1, D_MODEL), const),
            pl.BlockSpec((1, D_MODEL), const),
        ],
        out_specs=pl.BlockSpec((MIX_BM, D_MODEL), lambda i: (i, 0)),
        compiler_params=_cparams(("arbitrary",)),
        name="mixout",
    )(ret2d, na2d, gates2d, gates2d, x2d, wr, wn, wo, lg, lb)


ROUTE_BM = 512


def _route_tile(x, rw, rb):
    nt_dims = (((1,), (1,)), ((), ()))
    logits = lax.dot_general(rw, x, nt_dims, preferred_element_type=F32, precision=lax.Precision.HIGHEST)
    scores = jax.nn.sigmoid(logits)
    sel = scores + rb
    p = [scores[m * N_GROUPS:(m + 1) * N_GROUPS] for m in range(EXPERTS_PER_GROUP)]
    s = [sel[m * N_GROUPS:(m + 1) * N_GROUPS] for m in range(EXPERTS_PER_GROUP)]
    one = jnp.ones_like(s[0])
    zero = jnp.zeros_like(s[0])
    chosen = []
    for m in range(EXPERTS_PER_GROUP):
        rank = zero
        for j in range(EXPERTS_PER_GROUP):
            if j == m:
                continue
            beats = (s[j] >= s[m]) if j < m else (s[j] > s[m])
            rank = rank + jnp.where(beats, one, zero)
        chosen.append(rank < 2.0)
    group_score = zero
    for m in range(EXPERTS_PER_GROUP):
        group_score = group_score + jnp.where(chosen[m], s[m], zero)
    gid = lax.broadcasted_iota(jnp.int32, group_score.shape, 0)
    gmax = jnp.max(group_score, axis=0, keepdims=True)
    gbest = jnp.min(jnp.where(group_score == gmax, gid, N_GROUPS), axis=0, keepdims=True)
    in_best = gid == gbest
    picked = [chosen[m] & in_best for m in range(EXPERTS_PER_GROUP)]
    before = zero
    wa = zero
    wb = zero
    ma = zero
    mb = zero
    for m in range(EXPERTS_PER_GROUP):
        is_a = picked[m] & (before == 0.0)
        is_b = picked[m] & (before == 1.0)
        wa = wa + jnp.where(is_a, p[m], zero)
        wb = wb + jnp.where(is_b, p[m], zero)
        ma = ma + jnp.where(is_a, float(m), 0.0)
        mb = mb + jnp.where(is_b, float(m), 0.0)
        before = before + jnp.where(chosen[m], one, zero)
    wa = jnp.sum(wa, axis=0, keepdims=True)
    wb = jnp.sum(wb, axis=0, keepdims=True)
    ma = jnp.sum(ma, axis=0, keepdims=True)
    mb = jnp.sum(mb, axis=0, keepdims=True)
    denom = wa + wb
    base = jnp.where(ma == 0.0, 0.0, jnp.where(ma == 1.0, 3.0, 5.0))
    pair = base + mb - ma - 1.0
    return gbest * N_PAIRS + pair.astype(jnp.int32), wa / denom, wb / denom


CLASS_ROWS = 64
META_LANES = 128


def _router_kernel(x_ref, rw_ref, rb_ref, tri_ref, dest_ref, wtok_ref, meta_ref, cls_s, rank_s, cnt_s):
    phase = pl.program_id(0)
    i = pl.program_id(1)
    bm = ROUTE_BM
    reps = bm // META_LANES
    cid = lax.broadcasted_iota(jnp.int32, (CLASS_ROWS, bm), 0)

    @pl.when((phase == 0) & (i == 0))
    def _():
        cnt_s[...] = jnp.zeros_like(cnt_s)

    @pl.when(phase == 0)
    def _():
        cls, w_lo, w_hi = _route_tile(x_ref[...], rw_ref[...], rb_ref[...])
        cls_s[i] = cls
        w_rows = jnp.concatenate([w_lo, w_hi, jnp.zeros((META_LANES - 2, bm), F32)], axis=0)
        wtok_ref[...] = w_rows.T
        onehot = (cid == cls).astype(BF16)
        before = jnp.dot(onehot, tri_ref[...], preferred_element_type=F32)
        carry = jnp.concatenate([cnt_s[...]] * reps, axis=1)
        rank = jnp.sum(jnp.where(cid == cls, before + carry, 0.0), axis=0, keepdims=True)
        rank_s[i] = rank.astype(jnp.int32)
        cnt_s[...] = cnt_s[...] + jnp.dot(onehot, jnp.ones((bm, META_LANES), BF16), preferred_element_type=F32)

    @pl.when(phase == 1)
    def _():
        cnt = cnt_s[...]
        nblk = jnp.floor((cnt + (MOE_BLOCK - 1.0)) * (1.0 / MOE_BLOCK))
        rr = lax.broadcasted_iota(jnp.int32, (CLASS_ROWS, CLASS_ROWS), 0)
        cc = lax.broadcasted_iota(jnp.int32, (CLASS_ROWS, CLASS_ROWS), 1)
        lower = (cc < rr).astype(BF16)
        start_blk = jnp.dot(lower, nblk.astype(BF16), preferred_element_type=F32)
        cls = cls_s[i]
        start_t = jnp.concatenate([start_blk] * reps, axis=1)
        start = jnp.sum(jnp.where(cid == cls, start_t, 0.0), axis=0, keepdims=True)
        dest_ref[0] = (start * float(MOE_BLOCK)).astype(jnp.int32) + rank_s[i]

        @pl.when(i == 0)
        def _():
            end_blk = start_blk + nblk
            n_used = jnp.max(end_blk, axis=0, keepdims=True)
            blk = lax.broadcasted_iota(jnp.int32, (CLASS_ROWS, META_LANES), 1).astype(F32)
            blk = jnp.minimum(blk, n_used - 1.0)
            bcls = jnp.sum(jnp.where(end_blk <= blk, 1.0, 0.0), axis=0, keepdims=True)
            grp = jnp.zeros_like(bcls)
            for g in range(1, N_GROUPS):
                grp = grp + jnp.where(bcls >= float(g * N_PAIRS), 1.0, 0.0)
            pair = bcls - grp * float(N_PAIRS)
            lo = jnp.where(pair < 3.0, 0.0, jnp.where(pair < 5.0, 1.0, 2.0))
            hi = jnp.where(pair == 0.0, 1.0, jnp.where((pair == 1.0) | (pair == 3.0), 2.0, 3.0))
            rows = [grp * float(EXPERTS_PER_GROUP) + lo, grp * float(EXPERTS_PER_GROUP) + hi, n_used]
            rows = rows + [jnp.zeros_like(bcls)] * (8 - len(rows))
            meta_ref[...] = jnp.concatenate(rows, axis=0).astype(jnp.int32)


def _router(x2d, rw_t, rb_t):
    t = x2d.shape[0]
    nt = t // ROUTE_BM
    tri = jnp.asarray(np.triu(np.ones((ROUTE_BM, ROUTE_BM), np.float32), 1), BF16)
    hold = lambda ph, i: ((1 - ph) * i + ph * (nt - 1), 0)
    const = lambda ph, i: (0, 0)
    return pl.pallas_call(
        _router_kernel,
        out_shape=(jax.ShapeDtypeStruct((nt, 1, ROUTE_BM), jnp.int32),
                   jax.ShapeDtypeStruct((t, META_LANES), F32),
                   jax.ShapeDtypeStruct((8, META_LANES), jnp.int32)),
        grid=(2, nt),
        in_specs=[
            pl.BlockSpec((ROUTE_BM, D_MODEL), hold),
            pl.BlockSpec((N_EXPERTS, D_MODEL), const),
            pl.BlockSpec((N_EXPERTS, 1), const),
            pl.BlockSpec((ROUTE_BM, ROUTE_BM), const),
        ],
        out_specs=(pl.BlockSpec((1, 1, ROUTE_BM), lambda ph, i: (ph * i, 0, 0)),
                   pl.BlockSpec((ROUTE_BM, META_LANES), hold),
                   pl.BlockSpec((8, META_LANES), const)),
        scratch_shapes=[
            pltpu.VMEM((nt, 1, ROUTE_BM), jnp.int32),
            pltpu.VMEM((nt, 1, ROUTE_BM), jnp.int32),
            pltpu.VMEM((CLASS_ROWS, META_LANES), F32),
        ],
        compiler_params=_cparams(("arbitrary", "arbitrary")),
        name="router",
    )(x2d, rw_t, rb_t, tri)


ROW_BM = 256


ROW_UNROLL = 8


def _start_rows(make_copy):
    def body(g, carry):
        for u in range(ROW_UNROLL):
            make_copy(g * ROW_UNROLL + u).start(priority=u % 2)
        return carry

    lax.fori_loop(0, ROW_BM // ROW_UNROLL, body, 0)


def _dispatch_kernel(dest_ref, x_ref, zeros_ref, xs_ref, sem):
    del zeros_ref
    base = pl.program_id(0) * ROW_BM
    _start_rows(lambda r: pltpu.make_async_copy(x_ref.at[pl.ds(r, 1), :],
                                                xs_ref.at[pl.ds(dest_ref[base + r], 1), :], sem))
    pltpu.make_async_copy(x_ref, xs_ref.at[pl.ds(0, ROW_BM), :], sem).wait()


def _dispatch(dest, x2d, n_rows):
    t = x2d.shape[0]
    grid_spec = pltpu.PrefetchScalarGridSpec(
        num_scalar_prefetch=1,
        grid=(t // ROW_BM,),
        in_specs=[
            pl.BlockSpec((ROW_BM, D_MODEL), lambda i, dest: (i, 0)),
            pl.BlockSpec(memory_space=pl.ANY),
        ],
        out_specs=pl.BlockSpec(memory_space=pl.ANY),
        scratch_shapes=[pltpu.SemaphoreType.DMA],
    )
    return pl.pallas_call(
        _dispatch_kernel,
        out_shape=jax.ShapeDtypeStruct((n_rows, D_MODEL), F32),
        grid_spec=grid_spec,
        input_output_aliases={2: 0},
        compiler_params=_cparams(("arbitrary",)),
        name="dispatch",
    )(dest, x2d, jnp.zeros((n_rows, D_MODEL), F32))


_PAIR_LO = np.array([0, 0, 0, 1, 1, 2], np.int32)
_PAIR_HI = np.array([1, 2, 3, 2, 3, 3], np.int32)


def _experts_kernel(ea_ref, eb_ref, nb_ref, x_ref, wga_ref, wua_ref, wda_ref, wgb_ref, wub_ref, wdb_ref, o_ref):
    @pl.when(pl.program_id(0) < nb_ref[0])
    def _():
        x = x_ref[...].astype(BF16)
        for wg_ref, wu_ref, wd_ref, col in ((wga_ref, wua_ref, wda_ref, 0), (wgb_ref, wub_ref, wdb_ref, D_MODEL)):
            hg = jnp.dot(x, wg_ref[0, 0].astype(BF16), preferred_element_type=F32)
            hu = jnp.dot(x, wu_ref[0, 0].astype(BF16), preferred_element_type=F32)
            act = (hg * jax.nn.sigmoid(hg) * hu).astype(BF16)
            o_ref[:, col:col + D_MODEL] = jnp.dot(act, wd_ref[0, 0].astype(BF16), preferred_element_type=F32)

    @pl.when(pl.program_id(0) >= nb_ref[0])
    def _():
        o_ref[...] = jnp.zeros_like(o_ref)


def _experts(blk_a, blk_b, n_used, xs, wg, wu, wd, layer):
    n_rows = xs.shape[0]
    n_blocks = n_rows // MOE_BLOCK
    wa_map = lambda j, ea, eb, nb: (layer, ea[j], 0, 0)
    wb_map = lambda j, ea, eb, nb: (layer, eb[j], 0, 0)
    in_row_map = lambda j, ea, eb, nb: (jnp.maximum(jnp.minimum(j, nb[0] - 1), 0), 0)
    row_map = lambda j, ea, eb, nb: (j, 0)
    grid_spec = pltpu.PrefetchScalarGridSpec(
        num_scalar_prefetch=3,
        grid=(n_blocks,),
        in_specs=[
            pl.BlockSpec((MOE_BLOCK, D_MODEL), in_row_map),
            pl.BlockSpec((1, 1, D_MODEL, D_EXPERT), wa_map),
            pl.BlockSpec((1, 1, D_MODEL, D_EXPERT), wa_map),
            pl.BlockSpec((1, 1, D_EXPERT, D_MODEL), wa_map),
            pl.BlockSpec((1, 1, D_MODEL, D_EXPERT), wb_map),
            pl.BlockSpec((1, 1, D_MODEL, D_EXPERT), wb_map),
            pl.BlockSpec((1, 1, D_EXPERT, D_MODEL), wb_map),
        ],
        out_specs=pl.BlockSpec((MOE_BLOCK, 2 * D_MODEL), row_map),
    )
    return pl.pallas_call(
        _experts_kernel,
        out_shape=jax.ShapeDtypeStruct((n_rows, 2 * D_MODEL), F32),
        grid_spec=grid_spec,
        compiler_params=_cparams(("arbitrary",)),
        name="experts",
    )(blk_a, blk_b, n_used, xs, wg, wu, wd, wg, wu, wd)


def _combine_ln_kernel(dest_ref, x_ref, wtok_ref, lg_ref, lb_ref, ys_ref, o_ref, obf_ref, ybuf, sem):
    i = pl.program_id(0)
    slot = i % 2

    def start_tile(tile, to_slot):
        base = tile * ROW_BM
        _start_rows(lambda r: pltpu.make_async_copy(ys_ref.at[pl.ds(dest_ref[base + r], 1), :],
                                                    ybuf.at[to_slot, pl.ds(r, 1), :], sem.at[to_slot]))

    @pl.when(i == 0)
    def _():
        start_tile(0, 0)

    @pl.when(i + 1 < pl.num_programs(0))
    def _():
        start_tile(i + 1, 1 - slot)

    pltpu.make_async_copy(ys_ref.at[pl.ds(0, ROW_BM), :], ybuf.at[slot], sem.at[slot]).wait()
    w = wtok_ref[...]
    y = w[:, 0:1] * ybuf[slot, :, :D_MODEL] + w[:, 1:2] * ybuf[slot, :, D_MODEL:]
    z = DEEPNORM_ALPHA * x_ref[...] + y
    out = _layernorm_rows(z, lg_ref[...], lb_ref[...])
    o_ref[...] = out
    obf_ref[...] = out.astype(BF16)


def _combine_ln(dest, x2d, wtok, ys, lg, lb):
    t = x2d.shape[0]
    const = lambda i, dest: (0, 0)
    grid_spec = pltpu.PrefetchScalarGridSpec(
        num_scalar_prefetch=1,
        grid=(t // ROW_BM,),
        in_specs=[
            pl.BlockSpec((ROW_BM, D_MODEL), lambda i, dest: (i, 0)),
            pl.BlockSpec((ROW_BM, META_LANES), lambda i, dest: (i, 0)),
            pl.BlockSpec((1, D_MODEL), const),
            pl.BlockSpec((1, D_MODEL), const),
            pl.BlockSpec(memory_space=pl.ANY),
        ],
        out_specs=(pl.BlockSpec((ROW_BM, D_MODEL), lambda i, dest: (i, 0)),
                   pl.BlockSpec((ROW_BM, D_MODEL), lambda i, dest: (i, 0))),
        scratch_shapes=[pltpu.VMEM((2, ROW_BM, 2 * D_MODEL), F32), pltpu.SemaphoreType.DMA((2,))],
    )
    return pl.pallas_call(
        _combine_ln_kernel,
        out_shape=(jax.ShapeDtypeStruct((t, D_MODEL), F32), jax.ShapeDtypeStruct((t, D_MODEL), BF16)),
        grid_spec=grid_spec,
        compiler_params=_cparams(("arbitrary",)),
        name="combine_ln",
    )(dest, x2d, wtok, lg, lb, ys)


def _moe(x2d, rw_t, rb_t, wg, wu, wd, layer, lg, lb):
    t = x2d.shape[0]
    n_blocks = (t + N_CLASSES * (MOE_BLOCK - 1)) // MOE_BLOCK
    assert n_blocks <= META_LANES
    dest3, wtok, meta = _router(x2d, rw_t, rb_t)
    dest = dest3.reshape(t)
    xs = _dispatch(dest, x2d, n_blocks * MOE_BLOCK)
    ys = _experts(meta[0, :n_blocks], meta[1, :n_blocks], meta[2, :1], xs, wg, wu, wd, layer)
    return _combine_ln(dest, x2d, wtok, ys, lg, lb)


def _rope_tables(seq):
    half = RET_DK // 2
    pos = jnp.arange(seq, dtype=F32)
    freqs = ROPE_BASE ** (-jnp.arange(half, dtype=F32) / half)
    ang = pos[:, None] * freqs[None, :]
    return jnp.cos(ang), jnp.sin(ang)


def kernel(x, w_in, ret_decay_logit, w_ret_o, na_rpb, w_na_o, w_out, ln_mix_g, ln_mix_b, router_w, router_bias,
           w_exp_gate, w_exp_up, w_exp_down, ln_ffn_g, ln_ffn_b):
    b, s, d = x.shape
    depth = w_in.shape[0]
    t = b * s
    rows = s // GRID_W
    cos, sin = _rope_tables(s)
    rw_t = router_w.astype(F32).T.reshape(N_GROUPS, EXPERTS_PER_GROUP, d).transpose(1, 0, 2).reshape(N_EXPERTS, d)
    rb_t = router_bias.astype(F32).reshape(N_GROUPS, EXPERTS_PER_GROUP).T.reshape(N_EXPERTS, 1)
    x2d = x.reshape(t, d)
    x_bf = x2d.astype(BF16)
    w_in = w_in.astype(F32)
    for l in range(depth):
        qk = _inproj(x_bf, w_in, l, cos, sin, s, "rotary").reshape(b, s, -1)
        vn = _inproj(x_bf, w_in, l, cos, sin, s, "scale").reshape(b, s, -1)
        gr = _inproj(x_bf, w_in, l, cos, sin, s, "silu").reshape(b, s, -1)
        gates = _inproj(x_bf, w_in, l, cos, sin, s, "sigmoid")
        ret = _retention(qk, vn, gr, ret_decay_logit[l].astype(F32))
        na = _na(vn, na_rpb[l].astype(F32).reshape(-1))
        x2d = _mixout(ret.reshape(t, -1), na.reshape(t, -1), gates, x2d,
                      w_ret_o[l].astype(BF16), w_na_o[l].astype(BF16), w_out[l].astype(BF16),
                      ln_mix_g[l].reshape(1, d).astype(F32), ln_mix_b[l].reshape(1, d).astype(F32))
        x2d, x_bf = _moe(x2d, rw_t, rb_t, w_exp_gate, w_exp_up, w_exp_down, l,
                         ln_ffn_g[l].reshape(1, d).astype(F32), ln_ffn_b[l].reshape(1, d).astype(F32))
    return x2d.reshape(b, s, d)
```

```python
import functools

import numpy as np
import jax
import jax.numpy as jnp
from jax import lax
from jax.experimental import pallas as pl
from jax.experimental.pallas import tpu as pltpu

F32 = jnp.float32
BF16 = jnp.bfloat16

D_MODEL = 1024
GRID_W = 64
RET_HEADS = 4
RET_DK = 256
RET_DV = 512
ROPE_BASE = 10000.0
NA_HEADS = 16
NA_HD = 64
NA_KH = 8
NA_KW = 16
N_EXPERTS = 32
N_GROUPS = 8
EXPERTS_PER_GROUP = 4
D_EXPERT = 512
LN_EPS = 1e-5
GN_EPS = 1e-5
MODEL_DEPTH = 4
DEEPNORM_ALPHA = (2 * MODEL_DEPTH) ** 0.25

D_IN = 11264

RET_CHUNK = 256
NA_QROWS = 4
NA_KROWS = 12
NA_NEG = -1e30
MOE_BLOCK = 256
N_PAIRS = 6
PAIR_SLOT_A = (0, 2, 2, 3, 3, 3)
PAIR_SLOT_B = (1, 1, 0, 0, 1, 2)
N_CLASSES = N_GROUPS * N_PAIRS
VMEM_LIMIT = 56 * 1024 * 1024


def _cparams(sem):
    return pltpu.CompilerParams(dimension_semantics=sem, vmem_limit_bytes=VMEM_LIMIT)


IN_BM = 1024
IN_BN = 1024
IN_CHUNK = 256
LOG2E = 1.4426950408889634

IN_GROUPS = {
    "rotary": (0, 1),
    "scale": (2, 3, 6, 7, 8),
    "silu": (4, 5),
    "sigmoid": (9, 10),
}
VN_Q = 2048
VN_K = 3072
VN_V = 4096


def _inproj_kernel(x_ref, w_ref, cos_ref, sin_ref, o_ref, wbf_ref, *, kind):
    j = pl.program_id(0)

    @pl.when(pl.program_id(1) == 0)
    def _():
        wbf_ref[...] = w_ref[0].astype(BF16)

    x = x_ref[...]
    half = RET_DK // 2
    for c in range(IN_BN // IN_CHUNK):
        lo = c * IN_CHUNK
        acc = jnp.dot(x, wbf_ref[:, lo:lo + IN_CHUNK], preferred_element_type=F32)
        if kind == "rotary":
            scale = jnp.where(j == 0, 1.0, RET_DK ** -0.5)
            cos = cos_ref[...] * scale
            sin = sin_ref[...] * scale
            x1 = acc[:, :half]
            x2 = acc[:, half:]
            o_ref[:, lo:lo + half] = (x1 * cos - x2 * sin).astype(o_ref.dtype)
            o_ref[:, lo + half:lo + IN_CHUNK] = (x1 * sin + x2 * cos).astype(o_ref.dtype)
        elif kind == "scale":
            scale = jnp.where(j == 2, NA_HD ** -0.5 * LOG2E, 1.0)
            o_ref[:, lo:lo + IN_CHUNK] = (acc * scale).astype(o_ref.dtype)
        elif kind == "silu":
            o_ref[:, lo:lo + IN_CHUNK] = (acc * jax.nn.sigmoid(acc)).astype(o_ref.dtype)
        else:
            o_ref[:, lo:lo + IN_CHUNK] = jax.nn.sigmoid(acc).astype(o_ref.dtype)


def _inproj(x_bf, w_in, layer, cos, sin, seq, kind):
    t = x_bf.shape[0]
    pos_blocks = seq // IN_BM
    blocks = IN_GROUPS[kind]

    def wcol(j):
        col = blocks[0]
        for k in range(1, len(blocks)):
            col = jnp.where(j == k, blocks[k], col)
        return col

    pos_map = (lambda j, i: (i % pos_blocks, 0)) if kind == "rotary" else (lambda j, i: (0, 0))
    return pl.pallas_call(
        functools.partial(_inproj_kernel, kind=kind),
        out_shape=jax.ShapeDtypeStruct((t, len(blocks) * IN_BN), BF16),
        grid=(len(blocks), t // IN_BM),
        in_specs=[
            pl.BlockSpec((IN_BM, D_MODEL), lambda j, i: (i, 0)),
            pl.BlockSpec((1, D_MODEL, IN_BN), lambda j, i: (layer, 0, wcol(j))),
            pl.BlockSpec((IN_BM, RET_DK // 2), pos_map),
            pl.BlockSpec((IN_BM, RET_DK // 2), pos_map),
        ],
        out_specs=pl.BlockSpec((IN_BM, IN_BN), lambda j, i: (i, j)),
        scratch_shapes=[pltpu.VMEM((D_MODEL, IN_BN), BF16)],
        compiler_params=_cparams(("arbitrary", "arbitrary")),
        name="inproj_" + kind,
    )(x_bf, w_in, cos, sin)


def _log_sigmoid(x):
    return jnp.minimum(x, 0.0) - jnp.log1p(jnp.exp(-jnp.abs(x)))


def _retention_kernel(dl_ref, q_ref, k_ref, v_ref, g_ref, o_ref,
                      sf_ref, st_ref, dm_ref, qdf_ref, qdb_ref, kdf_ref, kdb_ref, *, nc):
    c_len = RET_CHUNK
    h = pl.program_id(1)
    lgf = _log_sigmoid(jnp.full((c_len, RET_DK), dl_ref[0, h], F32))
    lgb = _log_sigmoid(jnp.full((c_len, RET_DK), dl_ref[1, h], F32))
    ri = lax.broadcasted_iota(jnp.int32, (c_len, RET_DK), 0).astype(F32)
    qdf_ref[...] = jnp.exp(lgf * (ri + 1.0))
    qdb_ref[...] = jnp.exp(lgb * (c_len - ri))
    kdf_ref[...] = jnp.exp(lgf * (c_len - 1.0 - ri))
    kdb_ref[...] = jnp.exp(lgb * ri)
    rr = lax.broadcasted_iota(jnp.int32, (c_len, c_len), 0).astype(F32)
    cc = lax.broadcasted_iota(jnp.int32, (c_len, c_len), 1).astype(F32)
    diff = rr - cc
    lgf_cc = _log_sigmoid(jnp.full((c_len, c_len), dl_ref[0, h], F32))
    lgb_cc = _log_sigmoid(jnp.full((c_len, c_len), dl_ref[1, h], F32))
    dm_ref[...] = jnp.where(diff >= 0.0, jnp.exp(lgf_cc * jnp.maximum(diff, 0.0)),
                            jnp.exp(lgb_cc * jnp.maximum(-diff, 0.0)))
    chunk_f = jnp.exp(lgf[:1, :1] * float(c_len))
    chunk_b = jnp.exp(lgb[:1, :1] * float(c_len))

    tn_dims = (((0,), (0,)), ((), ()))
    nt_dims = (((1,), (1,)), ((), ()))

    st_ref[...] = jnp.zeros_like(st_ref)

    def fwd_body(c, carry):
        off = pl.multiple_of(c * c_len, c_len)
        sf_ref[c] = st_ref[...].astype(BF16)
        kc = k_ref[0, pl.ds(off, c_len), :].astype(F32)
        vc = v_ref[0, pl.ds(off, c_len), :]
        kd = (kc * kdf_ref[...]).astype(BF16)
        upd = lax.dot_general(kd, vc, tn_dims, preferred_element_type=F32)
        st_ref[...] = st_ref[...] * chunk_f + upd
        return carry

    lax.fori_loop(0, nc, fwd_body, 0, unroll=4)

    st_ref[...] = jnp.zeros_like(st_ref)

    def bwd_body(i, carry):
        c = nc - 1 - i
        off = pl.multiple_of(c * c_len, c_len)
        qb = q_ref[0, pl.ds(off, c_len), :]
        kb = k_ref[0, pl.ds(off, c_len), :]
        vc = v_ref[0, pl.ds(off, c_len), :]
        qc = qb.astype(F32)
        kc = kb.astype(F32)
        s = lax.dot_general(qb, kb, nt_dims, preferred_element_type=F32)
        p = (s * dm_ref[...]).astype(BF16)
        out = jnp.dot(p, vc, preferred_element_type=F32)
        out = out + jnp.dot((qc * qdf_ref[...]).astype(BF16), sf_ref[c], preferred_element_type=F32)
        out = out + jnp.dot((qc * qdb_ref[...]).astype(BF16), st_ref[...].astype(BF16),
                            preferred_element_type=F32)
        mu = jnp.mean(out, axis=-1, keepdims=True)
        cen = out - mu
        var = jnp.mean(cen * cen, axis=-1, keepdims=True)
        y = cen * lax.rsqrt(var + GN_EPS)
        gate = g_ref[0, pl.ds(off, c_len), :].astype(F32)
        o_ref[0, pl.ds(off, c_len), :] = (gate * y).astype(o_ref.dtype)
        kd = (kc * kdb_ref[...]).astype(BF16)
        upd = lax.dot_general(kd, vc, tn_dims, preferred_element_type=F32)
        st_ref[...] = st_ref[...] * chunk_b + upd
        return carry

    lax.fori_loop(0, nc, bwd_body, 0, unroll=4)


def _retention(qk3, vn3, gr3, decay_logit):
    b, s, _ = qk3.shape
    nc = s // RET_CHUNK
    return pl.pallas_call(
        functools.partial(_retention_kernel, nc=nc),
        out_shape=jax.ShapeDtypeStruct((b, s, RET_HEADS * RET_DV), BF16),
        grid=(b, RET_HEADS),
        in_specs=[
            pl.BlockSpec(memory_space=pltpu.SMEM),
            pl.BlockSpec((1, s, RET_DK), lambda bi, h: (bi, 0, h)),
            pl.BlockSpec((1, s, RET_DK), lambda bi, h: (bi, 0, RET_HEADS + h)),
            pl.BlockSpec((1, s, RET_DV), lambda bi, h: (bi, 0, h)),
            pl.BlockSpec((1, s, RET_DV), lambda bi, h: (bi, 0, h)),
        ],
        out_specs=pl.BlockSpec((1, s, RET_DV), lambda bi, h: (bi, 0, h)),
        scratch_shapes=[
            pltpu.VMEM((nc, RET_DK, RET_DV), BF16),
            pltpu.VMEM((RET_DK, RET_DV), F32),
            pltpu.VMEM((RET_CHUNK, RET_CHUNK), F32),
            pltpu.VMEM((RET_CHUNK, RET_DK), F32),
            pltpu.VMEM((RET_CHUNK, RET_DK), F32),
            pltpu.VMEM((RET_CHUNK, RET_DK), F32),
            pltpu.VMEM((RET_CHUNK, RET_DK), F32),
        ],
        compiler_params=_cparams(("arbitrary", "arbitrary")),
        name="retention",
    )(decay_logit, qk3, qk3, vn3, gr3)


NA_ROW_OFFS = 2 * NA_KH - 1
NA_COL_OFFS = 2 * NA_KW - 1
NA_PATTERNS = 3


def _na_row_offsets(rows):
    n_tiles = rows // NA_QROWS
    offs = np.full((NA_PATTERNS, NA_QROWS, NA_KROWS), NA_ROW_OFFS, np.int32)
    for p, t in enumerate((0, 1, n_tiles - 1)):
        kstart = int(np.clip(NA_QROWS * t - NA_KH // 2, 0, rows - NA_KROWS))
        for rr in range(NA_QROWS):
            r = NA_QROWS * t + rr
            rs = int(np.clip(r - NA_KH // 2, 0, rows - NA_KH))
            for i in range(NA_KROWS):
                krow = kstart + i
                if rs <= krow < rs + NA_KH:
                    offs[p, rr, i] = krow - r + (NA_KH - 1)
    return offs


def _na_build_bias(rpb_ref, e_ref, bias_ref, head0, rows):
    wide = 2 * GRID_W
    lane = lax.broadcasted_iota(jnp.int32, (GRID_W, wide), 1)
    col = lax.broadcasted_iota(jnp.int32, (GRID_W, wide), 0)
    kcol = jnp.where(lane < GRID_W, lane, lane - GRID_W)
    cstart = jnp.clip(col - NA_KW // 2, 0, GRID_W - NA_KW)
    col_ok = (kcol >= cstart) & (kcol < cstart + NA_KW)
    coff = kcol - col + (NA_KW - 1)
    neg = jnp.full((GRID_W, wide), NA_NEG, F32)
    left = lane < GRID_W
    offs = _na_row_offsets(rows)
    for hh in range(2):
        base = (head0 + hh) * (NA_ROW_OFFS * NA_COL_OFFS)
        for ro in range(NA_ROW_OFFS):
            acc = neg
            for d in range(NA_COL_OFFS):
                acc = jnp.where(coff == d, rpb_ref[base + ro * NA_COL_OFFS + d] * LOG2E, acc)
            e_ref[hh, ro] = jnp.where(col_ok, acc, neg)
        e_ref[hh, NA_ROW_OFFS] = neg
        for p in range(NA_PATTERNS):
            for rr in range(NA_QROWS):
                for ip in range(NA_KROWS // 2):
                    tile = jnp.where(left, e_ref[hh, int(offs[p, rr, 2 * ip])], e_ref[hh, int(offs[p, rr, 2 * ip + 1])])
                    bias_ref[hh, p, rr * GRID_W:(rr + 1) * GRID_W, ip * wide:(ip + 1) * wide] = tile


def _na_kernel(rpb_ref, q_ref, k_ref, v_ref, o_ref, bias_ref, e_ref, *, n_tiles, rows):
    nq = NA_QROWS * GRID_W
    nk = NA_KROWS * GRID_W
    nt_dims = (((1,), (1,)), ((), ()))
    first = lax.broadcasted_iota(jnp.int32, (nq, 2 * NA_HD), 1) < NA_HD
    kfirst = lax.broadcasted_iota(jnp.int32, (nk, 2 * NA_HD), 1) < NA_HD

    @pl.when(pl.program_id(1) == 0)
    def _():
        _na_build_bias(rpb_ref, e_ref, bias_ref, 2 * pl.program_id(0), rows)

    def body(t, carry):
        qoff = pl.multiple_of(t * nq, nq)
        krow0 = jnp.clip(NA_QROWS * t - NA_KH // 2, 0, rows - NA_KROWS)
        koff = pl.multiple_of(krow0 * GRID_W, nq)
        pat = jnp.where(t == 0, 0, jnp.where(t == n_tiles - 1, 2, 1))
        q2 = q_ref[0, pl.ds(qoff, nq), :]
        k2 = k_ref[0, pl.ds(koff, nk), :]
        v2 = v_ref[0, pl.ds(koff, nk), :]
        outs = []
        for hh in range(2):
            qm = jnp.where(first if hh == 0 else jnp.logical_not(first), q2, jnp.zeros_like(q2))
            s = lax.dot_general(qm, k2, nt_dims, preferred_element_type=F32)
            s = s + bias_ref[hh, pat]
            m = jnp.max(s, axis=-1, keepdims=True)
            e = jnp.exp2(s - m).astype(BF16)
            vv = jnp.where(kfirst if hh == 0 else jnp.logical_not(kfirst), v2, jnp.ones_like(v2))
            o = jnp.dot(e, vv, preferred_element_type=F32)
            outs.append(o / pltpu.roll(o, NA_HD, axis=1))
        o_ref[0, pl.ds(qoff, nq), :] = jnp.where(first, outs[0], outs[1]).astype(o_ref.dtype)
        return carry

    lax.fori_loop(0, n_tiles, body, 0, unroll=4)


def _na(vn3, rpb_flat):
    b, s, _ = vn3.shape
    rows = s // GRID_W
    n_tiles = rows // NA_QROWS
    w2 = 2 * NA_HD
    nq = NA_QROWS * GRID_W
    nk = NA_KROWS * GRID_W
    return pl.pallas_call(
        functools.partial(_na_kernel, n_tiles=n_tiles, rows=rows),
        out_shape=jax.ShapeDtypeStruct((b, s, NA_HEADS * NA_HD), BF16),
        grid=(NA_HEADS // 2, b),
        in_specs=[
            pl.BlockSpec(memory_space=pltpu.SMEM),
            pl.BlockSpec((1, s, w2), lambda hp, bi: (bi, 0, VN_Q // w2 + hp)),
            pl.BlockSpec((1, s, w2), lambda hp, bi: (bi, 0, VN_K // w2 + hp)),
            pl.BlockSpec((1, s, w2), lambda hp, bi: (bi, 0, VN_V // w2 + hp)),
        ],
        out_specs=pl.BlockSpec((1, s, w2), lambda hp, bi: (bi, 0, hp)),
        scratch_shapes=[
            pltpu.VMEM((2, NA_PATTERNS, nq, nk), F32),
            pltpu.VMEM((2, NA_ROW_OFFS + 1, GRID_W, 2 * GRID_W), F32),
        ],
        compiler_params=_cparams(("arbitrary", "arbitrary")),
        name="natten",
    )(rpb_flat, vn3, vn3, vn3)


MIX_BM = 512


def _layernorm_rows(z, g, b):
    mu = jnp.mean(z, axis=-1, keepdims=True)
    cen = z - mu
    var = jnp.mean(cen * cen, axis=-1, keepdims=True)
    return cen * lax.rsqrt(var + LN_EPS) * g + b


def _mixout_kernel(ret_ref, na_ref, g1_ref, g2_ref, x_ref, wr_ref, wn_ref, wo_ref, lg_ref, lb_ref, o_ref):
    y_ret = jnp.dot(ret_ref[...], wr_ref[...], preferred_element_type=F32)
    y_na = jnp.dot(na_ref[...], wn_ref[...], preferred_element_type=F32)
    merged = g1_ref[...].astype(F32) * y_ret + g2_ref[...].astype(F32) * y_na
    mix = jnp.dot(merged.astype(BF16), wo_ref[...], preferred_element_type=F32)
    z = DEEPNORM_ALPHA * x_ref[...] + mix
    o_ref[...] = _layernorm_rows(z, lg_ref[...], lb_ref[...])


def _mixout(ret2d, na2d, gates2d, x2d, wr, wn, wo, lg, lb):
    t = x2d.shape[0]
    gate_blk = 0
    const = lambda i: (0, 0)
    return pl.pallas_call(
        _mixout_kernel,
        out_shape=jax.ShapeDtypeStruct((t, D_MODEL), F32),
        grid=(t // MIX_BM,),
        in_specs=[
            pl.BlockSpec((MIX_BM, RET_HEADS * RET_DV), lambda i: (i, 0)),
            pl.BlockSpec((MIX_BM, NA_HEADS * NA_HD), lambda i: (i, 0)),
            pl.BlockSpec((MIX_BM, D_MODEL), lambda i: (i, gate_blk)),
            pl.BlockSpec((MIX_BM, D_MODEL), lambda i: (i, gate_blk + 1)),
            pl.BlockSpec((MIX_BM, D_MODEL), lambda i: (i, 0)),
            pl.BlockSpec((RET_HEADS * RET_DV, D_MODEL), const),
            pl.BlockSpec((NA_HEADS * NA_HD, D_MODEL), const),
            pl.BlockSpec((D_MODEL, D_MODEL), const),
            pl.BlockSpec((1, D_MODEL), const),
            pl.BlockSpec((1, D_MODEL), const),
        ],
        out_specs=pl.BlockSpec((MIX_BM, D_MODEL), lambda i: (i, 0)),
        compiler_params=_cparams(("arbitrary",)),
        name="mixout",
    )(ret2d, na2d, gates2d, gates2d, x2d, wr, wn, wo, lg, lb)


ROUTE_BM = 512


def _route_tile(x, rw, rb):
    nt_dims = (((1,), (1,)), ((), ()))
    x_hi = x.astype(BF16)
    x_lo = (x - x_hi.astype(F32)).astype(BF16)
    rw_hi = rw.astype(BF16)
    rw_lo = (rw - rw_hi.astype(F32)).astype(BF16)
    both = lax.dot_general(jnp.concatenate([rw_hi, rw_lo], axis=0), x_hi, nt_dims, preferred_element_type=F32)
    logits = (both[:N_EXPERTS] + both[N_EXPERTS:]
              + lax.dot_general(rw_hi, x_lo, nt_dims, preferred_element_type=F32))
    scores = jax.nn.sigmoid(logits)
    sel = scores + rb
    p = [scores[m * N_GROUPS:(m + 1) * N_GROUPS] for m in range(EXPERTS_PER_GROUP)]
    s = [sel[m * N_GROUPS:(m + 1) * N_GROUPS] for m in range(EXPERTS_PER_GROUP)]
    one = jnp.ones_like(s[0])
    zero = jnp.zeros_like(s[0])
    chosen = []
    for m in range(EXPERTS_PER_GROUP):
        rank = zero
        for j in range(EXPERTS_PER_GROUP):
            if j == m:
                continue
            beats = (s[j] >= s[m]) if j < m else (s[j] > s[m])
            rank = rank + jnp.where(beats, one, zero)
        chosen.append(rank < 2.0)
    group_score = zero
    for m in range(EXPERTS_PER_GROUP):
        group_score = group_score + jnp.where(chosen[m], s[m], zero)
    gid = lax.broadcasted_iota(jnp.int32, group_score.shape, 0)
    gmax = jnp.max(group_score, axis=0, keepdims=True)
    gbest = jnp.min(jnp.where(group_score == gmax, gid, N_GROUPS), axis=0, keepdims=True)
    in_best = gid == gbest
    picked = [chosen[m] & in_best for m in range(EXPERTS_PER_GROUP)]
    before = zero
    wa = zero
    wb = zero
    ma = zero
    mb = zero
    for m in range(EXPERTS_PER_GROUP):
        is_a = picked[m] & (before == 0.0)
        is_b = picked[m] & (before == 1.0)
        wa = wa + jnp.where(is_a, p[m], zero)
        wb = wb + jnp.where(is_b, p[m], zero)
        ma = ma + jnp.where(is_a, float(m), 0.0)
        mb = mb + jnp.where(is_b, float(m), 0.0)
        before = before + jnp.where(chosen[m], one, zero)
    wa = jnp.sum(wa, axis=0, keepdims=True)
    wb = jnp.sum(wb, axis=0, keepdims=True)
    ma = jnp.sum(ma, axis=0, keepdims=True)
    mb = jnp.sum(mb, axis=0, keepdims=True)
    denom = wa + wb
    w_lo = wa / denom
    w_hi = wb / denom
    pair = jnp.where(ma == 0.0, jnp.where(mb == 1.0, 0.0, mb),
                     jnp.where(ma == 1.0, jnp.where(mb == 2.0, 1.0, 4.0), 5.0))
    keep = pair == 0.0
    return (gbest * N_PAIRS + pair.astype(jnp.int32),
            jnp.where(keep, w_lo, w_hi), jnp.where(keep, w_hi, w_lo))


CLASS_ROWS = 64
META_LANES = 128


def _router_kernel(x_ref, rw_ref, rb_ref, tri_ref, dest_ref, wtok_ref, meta_ref, cls_s, rank_s, cnt_s):
    phase = pl.program_id(0)
    i = pl.program_id(1)
    bm = ROUTE_BM
    reps = bm // META_LANES
    cid = lax.broadcasted_iota(jnp.int32, (CLASS_ROWS, bm), 0)

    @pl.when((phase == 0) & (i == 0))
    def _():
        cnt_s[...] = jnp.zeros_like(cnt_s)

    @pl.when(phase == 0)
    def _():
        cls, w_lo, w_hi = _route_tile(x_ref[...], rw_ref[...], rb_ref[...])
        cls_s[i] = cls
        w_rows = jnp.concatenate([w_lo, w_hi, jnp.zeros((META_LANES - 2, bm), F32)], axis=0)
        wtok_ref[...] = w_rows.T
        onehot = (cid == cls).astype(BF16)
        before = jnp.dot(onehot, tri_ref[...], preferred_element_type=F32)
        carry = jnp.concatenate([cnt_s[...]] * reps, axis=1)
        rank = jnp.sum(jnp.where(cid == cls, before + carry, 0.0), axis=0, keepdims=True)
        rank_s[i] = rank.astype(jnp.int32)
        cnt_s[...] = cnt_s[...] + jnp.dot(onehot, jnp.ones((bm, META_LANES), BF16), preferred_element_type=F32)

    @pl.when(phase == 1)
    def _():
        cnt = cnt_s[...]
        nblk = jnp.floor((cnt + (MOE_BLOCK - 1.0)) * (1.0 / MOE_BLOCK))
        rr = lax.broadcasted_iota(jnp.int32, (CLASS_ROWS, CLASS_ROWS), 0)
        cc = lax.broadcasted_iota(jnp.int32, (CLASS_ROWS, CLASS_ROWS), 1)
        lower = (cc < rr).astype(BF16)
        start_blk = jnp.dot(lower, nblk.astype(BF16), preferred_element_type=F32)
        cls = cls_s[i]
        start_t = jnp.concatenate([start_blk] * reps, axis=1)
        start = jnp.sum(jnp.where(cid == cls, start_t, 0.0), axis=0, keepdims=True)
        dest_ref[0] = (start * float(MOE_BLOCK)).astype(jnp.int32) + rank_s[i]

        @pl.when(i == 0)
        def _():
            end_blk = start_blk + nblk
            n_used = jnp.max(end_blk, axis=0, keepdims=True)
            blk = lax.broadcasted_iota(jnp.int32, (CLASS_ROWS, META_LANES), 1).astype(F32)
            blk = jnp.minimum(blk, n_used - 1.0)
            bcls = jnp.sum(jnp.where(end_blk <= blk, 1.0, 0.0), axis=0, keepdims=True)
            grp = jnp.zeros_like(bcls)
            for g in range(1, N_GROUPS):
                grp = grp + jnp.where(bcls >= float(g * N_PAIRS), 1.0, 0.0)
            pair = bcls - grp * float(N_PAIRS)
            slot_a = jnp.zeros_like(pair)
            slot_b = jnp.zeros_like(pair)
            for k in range(N_PAIRS):
                slot_a = jnp.where(pair == float(k), float(PAIR_SLOT_A[k]), slot_a)
                slot_b = jnp.where(pair == float(k), float(PAIR_SLOT_B[k]), slot_b)
            rows = [grp * float(EXPERTS_PER_GROUP) + slot_a, grp * float(EXPERTS_PER_GROUP) + slot_b, n_used]
            rows = rows + [jnp.zeros_like(bcls)] * (8 - len(rows))
            meta_ref[...] = jnp.concatenate(rows, axis=0).astype(jnp.int32)


def _router(x2d, rw_t, rb_t):
    t = x2d.shape[0]
    nt = t // ROUTE_BM
    tri = jnp.asarray(np.triu(np.ones((ROUTE_BM, ROUTE_BM), np.float32), 1), BF16)
    hold = lambda ph, i: ((1 - ph) * i + ph * (nt - 1), 0)
    const = lambda ph, i: (0, 0)
    return pl.pallas_call(
        _router_kernel,
        out_shape=(jax.ShapeDtypeStruct((nt, 1, ROUTE_BM), jnp.int32),
                   jax.ShapeDtypeStruct((t, META_LANES), F32),
                   jax.ShapeDtypeStruct((8, META_LANES), jnp.int32)),
        grid=(2, nt),
        in_specs=[
            pl.BlockSpec((ROUTE_BM, D_MODEL), hold),
            pl.BlockSpec((N_EXPERTS, D_MODEL), const),
            pl.BlockSpec((N_EXPERTS, 1), const),
            pl.BlockSpec((ROUTE_BM, ROUTE_BM), const),
        ],
        out_specs=(pl.BlockSpec((1, 1, ROUTE_BM), lambda ph, i: (ph * i, 0, 0)),
                   pl.BlockSpec((ROUTE_BM, META_LANES), hold),
                   pl.BlockSpec((8, META_LANES), const)),
        scratch_shapes=[
            pltpu.VMEM((nt, 1, ROUTE_BM), jnp.int32),
            pltpu.VMEM((nt, 1, ROUTE_BM), jnp.int32),
            pltpu.VMEM((CLASS_ROWS, META_LANES), F32),
        ],
        compiler_params=_cparams(("arbitrary", "arbitrary")),
        name="router",
    )(x2d, rw_t, rb_t, tri)


ROW_BM = 256


def _start_rows(make_copy):
    for r in range(ROW_BM):
        make_copy(r).start(priority=r % 2)


def _dispatch_kernel(dest_ref, x_ref, zeros_ref, xs_ref, sem):
    del zeros_ref
    base = pl.program_id(0) * ROW_BM
    _start_rows(lambda r: pltpu.make_async_copy(x_ref.at[pl.ds(r, 1), :],
                                                xs_ref.at[pl.ds(dest_ref[base + r], 1), :], sem))
    pltpu.make_async_copy(x_ref, xs_ref.at[pl.ds(0, ROW_BM), :], sem).wait()


def _dispatch(dest, x2d, n_rows):
    t = x2d.shape[0]
    grid_spec = pltpu.PrefetchScalarGridSpec(
        num_scalar_prefetch=1,
        grid=(t // ROW_BM,),
        in_specs=[
            pl.BlockSpec((ROW_BM, D_MODEL), lambda i, dest: (i, 0)),
            pl.BlockSpec(memory_space=pl.ANY),
        ],
        out_specs=pl.BlockSpec(memory_space=pl.ANY),
        scratch_shapes=[pltpu.SemaphoreType.DMA],
    )
    return pl.pallas_call(
        _dispatch_kernel,
        out_shape=jax.ShapeDtypeStruct((n_rows, D_MODEL), F32),
        grid_spec=grid_spec,
        input_output_aliases={2: 0},
        compiler_params=_cparams(("arbitrary",)),
        name="dispatch",
    )(dest, x2d, jnp.zeros((n_rows, D_MODEL), F32))


def _experts_kernel(ea_ref, eb_ref, nb_ref, x_ref, wga_ref, wua_ref, wda_ref, wgb_ref, wub_ref, wdb_ref, o_ref):
    @pl.when(pl.program_id(0) < nb_ref[0])
    def _():
        x = x_ref[...].astype(BF16)
        for wg_ref, wu_ref, wd_ref, col in ((wga_ref, wua_ref, wda_ref, 0), (wgb_ref, wub_ref, wdb_ref, D_MODEL)):
            hg = jnp.dot(x, wg_ref[0, 0].astype(BF16), preferred_element_type=F32)
            hu = jnp.dot(x, wu_ref[0, 0].astype(BF16), preferred_element_type=F32)
            act = (hg * jax.nn.sigmoid(hg) * hu).astype(BF16)
            o_ref[:, col:col + D_MODEL] = jnp.dot(act, wd_ref[0, 0].astype(BF16), preferred_element_type=F32)

    @pl.when(pl.program_id(0) >= nb_ref[0])
    def _():
        o_ref[...] = jnp.zeros_like(o_ref)


def _experts(blk_a, blk_b, n_used, xs, wg, wu, wd, layer):
    n_rows = xs.shape[0]
    n_blocks = n_rows // MOE_BLOCK
    wa_map = lambda j, ea, eb, nb: (layer, ea[j], 0, 0)
    wb_map = lambda j, ea, eb, nb: (layer, eb[j], 0, 0)
    in_row_map = lambda j, ea, eb, nb: (jnp.maximum(jnp.minimum(j, nb[0] - 1), 0), 0)
    row_map = lambda j, ea, eb, nb: (j, 0)
    grid_spec = pltpu.PrefetchScalarGridSpec(
        num_scalar_prefetch=3,
        grid=(n_blocks,),
        in_specs=[
            pl.BlockSpec((MOE_BLOCK, D_MODEL), in_row_map),
            pl.BlockSpec((1, 1, D_MODEL, D_EXPERT), wa_map),
            pl.BlockSpec((1, 1, D_MODEL, D_EXPERT), wa_map),
            pl.BlockSpec((1, 1, D_EXPERT, D_MODEL), wa_map),
            pl.BlockSpec((1, 1, D_MODEL, D_EXPERT), wb_map),
            pl.BlockSpec((1, 1, D_MODEL, D_EXPERT), wb_map),
            pl.BlockSpec((1, 1, D_EXPERT, D_MODEL), wb_map),
        ],
        out_specs=pl.BlockSpec((MOE_BLOCK, 2 * D_MODEL), row_map),
    )
    return pl.pallas_call(
        _experts_kernel,
        out_shape=jax.ShapeDtypeStruct((n_rows, 2 * D_MODEL), F32),
        grid_spec=grid_spec,
        compiler_params=_cparams(("arbitrary",)),
        name="experts",
    )(blk_a, blk_b, n_used, xs, wg, wu, wd, wg, wu, wd)


def _combine_ln_kernel(dest_ref, x_ref, wtok_ref, lg_ref, lb_ref, ys_ref, o_ref, obf_ref, ybuf, sem):
    i = pl.program_id(0)
    slot = i % 2

    def start_tile(tile, to_slot):
        base = tile * ROW_BM
        _start_rows(lambda r: pltpu.make_async_copy(ys_ref.at[pl.ds(dest_ref[base + r], 1), :],
                                                    ybuf.at[to_slot, pl.ds(r, 1), :], sem.at[to_slot]))

    @pl.when(i == 0)
    def _():
        start_tile(0, 0)

    @pl.when(i + 1 < pl.num_programs(0))
    def _():
        start_tile(i + 1, 1 - slot)

    pltpu.make_async_copy(ys_ref.at[pl.ds(0, ROW_BM), :], ybuf.at[slot], sem.at[slot]).wait()
    w = wtok_ref[...]
    y = w[:, 0:1] * ybuf[slot, :, :D_MODEL] + w[:, 1:2] * ybuf[slot, :, D_MODEL:]
    z = DEEPNORM_ALPHA * x_ref[...] + y
    out = _layernorm_rows(z, lg_ref[...], lb_ref[...])
    o_ref[...] = out
    obf_ref[...] = out.astype(BF16)


def _combine_ln(dest, x2d, wtok, ys, lg, lb):
    t = x2d.shape[0]
    const = lambda i, dest: (0, 0)
    grid_spec = pltpu.PrefetchScalarGridSpec(
        num_scalar_prefetch=1,
        grid=(t // ROW_BM,),
        in_specs=[
            pl.BlockSpec((ROW_BM, D_MODEL), lambda i, dest: (i, 0)),
            pl.BlockSpec((ROW_BM, META_LANES), lambda i, dest: (i, 0)),
            pl.BlockSpec((1, D_MODEL), const),
            pl.BlockSpec((1, D_MODEL), const),
            pl.BlockSpec(memory_space=pl.ANY),
        ],
        out_specs=(pl.BlockSpec((ROW_BM, D_MODEL), lambda i, dest: (i, 0)),
                   pl.BlockSpec((ROW_BM, D_MODEL), lambda i, dest: (i, 0))),
        scratch_shapes=[pltpu.VMEM((2, ROW_BM, 2 * D_MODEL), F32), pltpu.SemaphoreType.DMA((2,))],
    )
    return pl.pallas_call(
        _combine_ln_kernel,
        out_shape=(jax.ShapeDtypeStruct((t, D_MODEL), F32), jax.ShapeDtypeStruct((t, D_MODEL), BF16)),
        grid_spec=grid_spec,
        compiler_params=_cparams(("arbitrary",)),
        name="combine_ln",
    )(dest, x2d, wtok, lg, lb, ys)


def _moe(x2d, rw_t, rb_t, wg, wu, wd, layer, lg, lb):
    t = x2d.shape[0]
    n_blocks = (t + N_CLASSES * (MOE_BLOCK - 1)) // MOE_BLOCK
    assert n_blocks <= META_LANES
    dest3, wtok, meta = _router(x2d, rw_t, rb_t)
    dest = dest3.reshape(t)
    xs = _dispatch(dest, x2d, n_blocks * MOE_BLOCK)
    ys = _experts(meta[0, :n_blocks], meta[1, :n_blocks], meta[2, :1], xs, wg, wu, wd, layer)
    return _combine_ln(dest, x2d, wtok, ys, lg, lb)


def _rope_tables(seq):
    half = RET_DK // 2
    pos = jnp.arange(seq, dtype=F32)
    freqs = ROPE_BASE ** (-jnp.arange(half, dtype=F32) / half)
    ang = pos[:, None] * freqs[None, :]
    return jnp.cos(ang), jnp.sin(ang)


def kernel(x, w_in, ret_decay_logit, w_ret_o, na_rpb, w_na_o, w_out, ln_mix_g, ln_mix_b, router_w, router_bias,
           w_exp_gate, w_exp_up, w_exp_down, ln_ffn_g, ln_ffn_b):
    b, s, d = x.shape
    depth = w_in.shape[0]
    t = b * s
    rows = s // GRID_W
    cos, sin = _rope_tables(s)
    rw_t = router_w.astype(F32).T.reshape(N_GROUPS, EXPERTS_PER_GROUP, d).transpose(1, 0, 2).reshape(N_EXPERTS, d)
    rb_t = router_bias.astype(F32).reshape(N_GROUPS, EXPERTS_PER_GROUP).T.reshape(N_EXPERTS, 1)
    x2d = x.reshape(t, d)
    x_bf = x2d.astype(BF16)
    w_in = w_in.astype(F32)
    for l in range(depth):
        qk = _inproj(x_bf, w_in, l, cos, sin, s, "rotary").reshape(b, s, -1)
        vn = _inproj(x_bf, w_in, l, cos, sin, s, "scale").reshape(b, s, -1)
        gr = _inproj(x_bf, w_in, l, cos, sin, s, "silu").reshape(b, s, -1)
        gates = _inproj(x_bf, w_in, l, cos, sin, s, "sigmoid")
        ret = _retention(qk, vn, gr, ret_decay_logit[l].astype(F32))
        na = _na(vn, na_rpb[l].astype(F32).reshape(-1))
        x2d = _mixout(ret.reshape(t, -1), na.reshape(t, -1), gates, x2d,
                      w_ret_o[l].astype(BF16), w_na_o[l].astype(BF16), w_out[l].astype(BF16),
                      ln_mix_g[l].reshape(1, d).astype(F32), ln_mix_b[l].reshape(1, d).astype(F32))
        x2d, x_bf = _moe(x2d, rw_t, rb_t, w_exp_gate, w_exp_up, w_exp_down, l,
                         ln_ffn_g[l].reshape(1, d).astype(F32), ln_ffn_b[l].reshape(1, d).astype(F32))
    return x2d.reshape(b, s, d)
```

```python
import functools

import numpy as np
import jax
import jax.numpy as jnp
from jax import lax
from jax.experimental import pallas as pl
from jax.experimental.pallas import tpu as pltpu

F32 = jnp.float32
BF16 = jnp.bfloat16

D_MODEL = 1024
GRID_W = 64
RET_HEADS = 4
RET_DK = 256
RET_DV = 512
ROPE_BASE = 10000.0
NA_HEADS = 16
NA_HD = 64
NA_KH = 8
NA_KW = 16
N_EXPERTS = 32
N_GROUPS = 8
EXPERTS_PER_GROUP = 4
D_EXPERT = 512
LN_EPS = 1e-5
GN_EPS = 1e-5
MODEL_DEPTH = 4
DEEPNORM_ALPHA = (2 * MODEL_DEPTH) ** 0.25

D_IN = 11264

RET_CHUNK = 256
NA_QROWS = 4
NA_KROWS = 12
NA_NEG = -1e30
MOE_BLOCK = 256
N_PAIRS = 6
PAIR_SLOT_A = (0, 2, 2, 3, 3, 3)
PAIR_SLOT_B = (1, 1, 0, 0, 1, 2)
N_CLASSES = N_GROUPS * N_PAIRS
VMEM_LIMIT = 56 * 1024 * 1024


def _cparams(sem):
    return pltpu.CompilerParams(dimension_semantics=sem, vmem_limit_bytes=VMEM_LIMIT)


U32 = jnp.uint32
_HIGH_HALF = 0xFFFF0000


def _pack_bf16_pair(hi, lo):
    hi_bits = lax.bitcast_convert_type(hi.astype(jnp.bfloat16).astype(F32), U32)
    lo_bits = lax.bitcast_convert_type(lo.astype(jnp.bfloat16).astype(F32), U32)
    return (hi_bits & jnp.uint32(_HIGH_HALF)) | (lo_bits >> 16)


def _unpack_bf16_pair(packed):
    hi = lax.bitcast_convert_type(packed & jnp.uint32(_HIGH_HALF), F32)
    lo = lax.bitcast_convert_type(packed << 16, F32)
    return hi, lo


IN_BM = 1024
IN_BN = 1024
IN_CHUNK = 256
LOG2E = 1.4426950408889634

IN_GROUPS = {
    "rotary": (0, 1),
    "scale": (2, 3, 6, 7, 8),
    "silu": (4, 5),
    "sigmoid": (9, 10),
}
VN_Q = 2048
VN_K = 3072
VN_V = 4096


def _inproj_kernel(x_ref, w_ref, cos_ref, sin_ref, o_ref, wbf_ref, *, kind):
    j = pl.program_id(0)

    @pl.when(pl.program_id(1) == 0)
    def _():
        wbf_ref[...] = w_ref[0].astype(BF16)

    x = x_ref[...]
    half = RET_DK // 2
    for c in range(IN_BN // IN_CHUNK):
        lo = c * IN_CHUNK
        acc = jnp.dot(x, wbf_ref[:, lo:lo + IN_CHUNK], preferred_element_type=F32)
        if kind == "rotary":
            scale = jnp.where(j == 0, 1.0, RET_DK ** -0.5)
            cos = cos_ref[...] * scale
            sin = sin_ref[...] * scale
            x1 = acc[:, :half]
            x2 = acc[:, half:]
            o_ref[:, lo:lo + half] = (x1 * cos - x2 * sin).astype(o_ref.dtype)
            o_ref[:, lo + half:lo + IN_CHUNK] = (x1 * sin + x2 * cos).astype(o_ref.dtype)
        elif kind == "scale":
            scale = jnp.where(j == 2, NA_HD ** -0.5 * LOG2E, 1.0)
            o_ref[:, lo:lo + IN_CHUNK] = (acc * scale).astype(o_ref.dtype)
        elif kind == "silu":
            o_ref[:, lo:lo + IN_CHUNK] = (acc * jax.nn.sigmoid(acc)).astype(o_ref.dtype)
        else:
            o_ref[:, lo:lo + IN_CHUNK] = jax.nn.sigmoid(acc).astype(o_ref.dtype)


def _inproj(x_bf, w_in, layer, cos, sin, seq, kind):
    t = x_bf.shape[0]
    pos_blocks = seq // IN_BM
    blocks = IN_GROUPS[kind]

    def wcol(j):
        col = blocks[0]
        for k in range(1, len(blocks)):
            col = jnp.where(j == k, blocks[k], col)
        return col

    pos_map = (lambda j, i: (i % pos_blocks, 0)) if kind == "rotary" else (lambda j, i: (0, 0))
    return pl.pallas_call(
        functools.partial(_inproj_kernel, kind=kind),
        out_shape=jax.ShapeDtypeStruct((t, len(blocks) * IN_BN), BF16),
        grid=(len(blocks), t // IN_BM),
        in_specs=[
            pl.BlockSpec((IN_BM, D_MODEL), lambda j, i: (i, 0)),
            pl.BlockSpec((1, D_MODEL, IN_BN), lambda j, i: (layer, 0, wcol(j))),
            pl.BlockSpec((IN_BM, RET_DK // 2), pos_map),
            pl.BlockSpec((IN_BM, RET_DK // 2), pos_map),
        ],
        out_specs=pl.BlockSpec((IN_BM, IN_BN), lambda j, i: (i, j)),
        scratch_shapes=[pltpu.VMEM((D_MODEL, IN_BN), BF16)],
        compiler_params=_cparams(("arbitrary", "arbitrary")),
        name="inproj_" + kind,
    )(x_bf, w_in, cos, sin)


def _log_sigmoid(x):
    return jnp.minimum(x, 0.0) - jnp.log1p(jnp.exp(-jnp.abs(x)))


def _retention_kernel(dl_ref, q_ref, k_ref, v_ref, g_ref, o_ref,
                      sf_ref, st_ref, dm_ref, qdf_ref, qdb_ref, kdf_ref, kdb_ref, *, nc):
    c_len = RET_CHUNK
    h = pl.program_id(1)
    lgf = _log_sigmoid(jnp.full((c_len, RET_DK), dl_ref[0, h], F32))
    lgb = _log_sigmoid(jnp.full((c_len, RET_DK), dl_ref[1, h], F32))
    ri = lax.broadcasted_iota(jnp.int32, (c_len, RET_DK), 0).astype(F32)
    qdf_ref[...] = jnp.exp(lgf * (ri + 1.0))
    qdb_ref[...] = jnp.exp(lgb * (c_len - ri))
    kdf_ref[...] = jnp.exp(lgf * (c_len - 1.0 - ri))
    kdb_ref[...] = jnp.exp(lgb * ri)
    rr = lax.broadcasted_iota(jnp.int32, (c_len, c_len), 0).astype(F32)
    cc = lax.broadcasted_iota(jnp.int32, (c_len, c_len), 1).astype(F32)
    diff = rr - cc
    lgf_cc = _log_sigmoid(jnp.full((c_len, c_len), dl_ref[0, h], F32))
    lgb_cc = _log_sigmoid(jnp.full((c_len, c_len), dl_ref[1, h], F32))
    dm_ref[...] = jnp.where(diff >= 0.0, jnp.exp(lgf_cc * jnp.maximum(diff, 0.0)),
                            jnp.exp(lgb_cc * jnp.maximum(-diff, 0.0)))
    chunk_f = jnp.exp(lgf[:1, :1] * float(c_len))
    chunk_b = jnp.exp(lgb[:1, :1] * float(c_len))

    tn_dims = (((0,), (0,)), ((), ()))
    nt_dims = (((1,), (1,)), ((), ()))

    st_ref[...] = jnp.zeros_like(st_ref)

    def fwd_body(c, carry):
        off = pl.multiple_of(c * c_len, c_len)
        sf_ref[c] = st_ref[...].astype(BF16)
        kc = k_ref[0, pl.ds(off, c_len), :].astype(F32)
        vc = v_ref[0, pl.ds(off, c_len), :]
        kd = (kc * kdf_ref[...]).astype(BF16)
        upd = lax.dot_general(kd, vc, tn_dims, preferred_element_type=F32)
        st_ref[...] = st_ref[...] * chunk_f + upd
        return carry

    lax.fori_loop(0, nc, fwd_body, 0, unroll=4)

    st_ref[...] = jnp.zeros_like(st_ref)

    def bwd_body(i, carry):
        c = nc - 1 - i
        off = pl.multiple_of(c * c_len, c_len)
        qb = q_ref[0, pl.ds(off, c_len), :]
        kb = k_ref[0, pl.ds(off, c_len), :]
        vc = v_ref[0, pl.ds(off, c_len), :]
        qc = qb.astype(F32)
        kc = kb.astype(F32)
        s = lax.dot_general(qb, kb, nt_dims, preferred_element_type=F32)
        p = (s * dm_ref[...]).astype(BF16)
        out = jnp.dot(p, vc, preferred_element_type=F32)
        out = out + jnp.dot((qc * qdf_ref[...]).astype(BF16), sf_ref[c], preferred_element_type=F32)
        out = out + jnp.dot((qc * qdb_ref[...]).astype(BF16), st_ref[...].astype(BF16),
                            preferred_element_type=F32)
        mu = jnp.mean(out, axis=-1, keepdims=True)
        cen = out - mu
        var = jnp.mean(cen * cen, axis=-1, keepdims=True)
        y = cen * lax.rsqrt(var + GN_EPS)
        gate = g_ref[0, pl.ds(off, c_len), :].astype(F32)
        o_ref[0, pl.ds(off, c_len), :] = (gate * y).astype(o_ref.dtype)
        kd = (kc * kdb_ref[...]).astype(BF16)
        upd = lax.dot_general(kd, vc, tn_dims, preferred_element_type=F32)
        st_ref[...] = st_ref[...] * chunk_b + upd
        return carry

    lax.fori_loop(0, nc, bwd_body, 0, unroll=4)


def _retention(qk3, vn3, gr3, decay_logit):
    b, s, _ = qk3.shape
    nc = s // RET_CHUNK
    return pl.pallas_call(
        functools.partial(_retention_kernel, nc=nc),
        out_shape=jax.ShapeDtypeStruct((b, s, RET_HEADS * RET_DV), BF16),
        grid=(b, RET_HEADS),
        in_specs=[
            pl.BlockSpec(memory_space=pltpu.SMEM),
            pl.BlockSpec((1, s, RET_DK), lambda bi, h: (bi, 0, h)),
            pl.BlockSpec((1, s, RET_DK), lambda bi, h: (bi, 0, RET_HEADS + h)),
            pl.BlockSpec((1, s, RET_DV), lambda bi, h: (bi, 0, h)),
            pl.BlockSpec((1, s, RET_DV), lambda bi, h: (bi, 0, h)),
        ],
        out_specs=pl.BlockSpec((1, s, RET_DV), lambda bi, h: (bi, 0, h)),
        scratch_shapes=[
            pltpu.VMEM((nc, RET_DK, RET_DV), BF16),
            pltpu.VMEM((RET_DK, RET_DV), F32),
            pltpu.VMEM((RET_CHUNK, RET_CHUNK), F32),
            pltpu.VMEM((RET_CHUNK, RET_DK), F32),
            pltpu.VMEM((RET_CHUNK, RET_DK), F32),
            pltpu.VMEM((RET_CHUNK, RET_DK), F32),
            pltpu.VMEM((RET_CHUNK, RET_DK), F32),
        ],
        compiler_params=_cparams(("arbitrary", "arbitrary")),
        name="retention",
    )(decay_logit, qk3, qk3, vn3, gr3)


NA_ROW_OFFS = 2 * NA_KH - 1
NA_COL_OFFS = 2 * NA_KW - 1
NA_PATTERNS = 3


def _na_row_offsets(rows):
    n_tiles = rows // NA_QROWS
    offs = np.full((NA_PATTERNS, NA_QROWS, NA_KROWS), NA_ROW_OFFS, np.int32)
    for p, t in enumerate((0, 1, n_tiles - 1)):
        kstart = int(np.clip(NA_QROWS * t - NA_KH // 2, 0, rows - NA_KROWS))
        for rr in range(NA_QROWS):
            r = NA_QROWS * t + rr
            rs = int(np.clip(r - NA_KH // 2, 0, rows - NA_KH))
            for i in range(NA_KROWS):
                krow = kstart + i
                if rs <= krow < rs + NA_KH:
                    offs[p, rr, i] = krow - r + (NA_KH - 1)
    return offs


def _na_build_bias(rpb_ref, e_ref, bias_ref, head0, rows):
    wide = 2 * GRID_W
    lane = lax.broadcasted_iota(jnp.int32, (GRID_W, wide), 1)
    col = lax.broadcasted_iota(jnp.int32, (GRID_W, wide), 0)
    kcol = jnp.where(lane < GRID_W, lane, lane - GRID_W)
    cstart = jnp.clip(col - NA_KW // 2, 0, GRID_W - NA_KW)
    col_ok = (kcol >= cstart) & (kcol < cstart + NA_KW)
    coff = kcol - col + (NA_KW - 1)
    neg = jnp.full((GRID_W, wide), NA_NEG, F32)
    left = lane < GRID_W
    offs = _na_row_offsets(rows)
    for hh in range(2):
        base = (head0 + hh) * (NA_ROW_OFFS * NA_COL_OFFS)
        for ro in range(NA_ROW_OFFS):
            acc = neg
            for d in range(NA_COL_OFFS):
                acc = jnp.where(coff == d, rpb_ref[base + ro * NA_COL_OFFS + d] * LOG2E, acc)
            e_ref[hh, ro] = jnp.where(col_ok, acc, neg)
        e_ref[hh, NA_ROW_OFFS] = neg
        for p in range(NA_PATTERNS):
            for rr in range(NA_QROWS):
                for ip in range(NA_KROWS // 2):
                    tile = jnp.where(left, e_ref[hh, int(offs[p, rr, 2 * ip])], e_ref[hh, int(offs[p, rr, 2 * ip + 1])])
                    bias_ref[hh, p, rr * GRID_W:(rr + 1) * GRID_W, ip * wide:(ip + 1) * wide] = tile


def _na_kernel(rpb_ref, q_ref, k_ref, v_ref, o_ref, bias_ref, e_ref, *, n_tiles, rows):
    nq = NA_QROWS * GRID_W
    nk = NA_KROWS * GRID_W
    nt_dims = (((1,), (1,)), ((), ()))
    first = lax.broadcasted_iota(jnp.int32, (nq, 2 * NA_HD), 1) < NA_HD
    kfirst = lax.broadcasted_iota(jnp.int32, (nk, 2 * NA_HD), 1) < NA_HD

    @pl.when(pl.program_id(1) == 0)
    def _():
        _na_build_bias(rpb_ref, e_ref, bias_ref, 2 * pl.program_id(0), rows)

    def body(t, carry):
        qoff = pl.multiple_of(t * nq, nq)
        krow0 = jnp.clip(NA_QROWS * t - NA_KH // 2, 0, rows - NA_KROWS)
        koff = pl.multiple_of(krow0 * GRID_W, nq)
        pat = jnp.where(t == 0, 0, jnp.where(t == n_tiles - 1, 2, 1))
        q2 = q_ref[0, pl.ds(qoff, nq), :]
        k2 = k_ref[0, pl.ds(koff, nk), :]
        v2 = v_ref[0, pl.ds(koff, nk), :]
        outs = []
        for hh in range(2):
            qm = jnp.where(first if hh == 0 else jnp.logical_not(first), q2, jnp.zeros_like(q2))
            s = lax.dot_general(qm, k2, nt_dims, preferred_element_type=F32)
            s = s + bias_ref[hh, pat]
            m = jnp.max(s, axis=-1, keepdims=True)
            e = jnp.exp2(s - m).astype(BF16)
            vv = jnp.where(kfirst if hh == 0 else jnp.logical_not(kfirst), v2, jnp.ones_like(v2))
            o = jnp.dot(e, vv, preferred_element_type=F32)
            outs.append(o / pltpu.roll(o, NA_HD, axis=1))
        o_ref[0, pl.ds(qoff, nq), :] = jnp.where(first, outs[0], outs[1]).astype(o_ref.dtype)
        return carry

    lax.fori_loop(0, n_tiles, body, 0, unroll=4)


def _na(vn3, rpb_flat):
    b, s, _ = vn3.shape
    rows = s // GRID_W
    n_tiles = rows // NA_QROWS
    w2 = 2 * NA_HD
    nq = NA_QROWS * GRID_W
    nk = NA_KROWS * GRID_W
    return pl.pallas_call(
        functools.partial(_na_kernel, n_tiles=n_tiles, rows=rows),
        out_shape=jax.ShapeDtypeStruct((b, s, NA_HEADS * NA_HD), BF16),
        grid=(NA_HEADS // 2, b),
        in_specs=[
            pl.BlockSpec(memory_space=pltpu.SMEM),
            pl.BlockSpec((1, s, w2), lambda hp, bi: (bi, 0, VN_Q // w2 + hp)),
            pl.BlockSpec((1, s, w2), lambda hp, bi: (bi, 0, VN_K // w2 + hp)),
            pl.BlockSpec((1, s, w2), lambda hp, bi: (bi, 0, VN_V // w2 + hp)),
        ],
        out_specs=pl.BlockSpec((1, s, w2), lambda hp, bi: (bi, 0, hp)),
        scratch_shapes=[
            pltpu.VMEM((2, NA_PATTERNS, nq, nk), F32),
            pltpu.VMEM((2, NA_ROW_OFFS + 1, GRID_W, 2 * GRID_W), F32),
        ],
        compiler_params=_cparams(("arbitrary", "arbitrary")),
        name="natten",
    )(rpb_flat, vn3, vn3, vn3)


MIX_BM = 512


def _layernorm_rows(z, g, b):
    mu = jnp.mean(z, axis=-1, keepdims=True)
    cen = z - mu
    var = jnp.mean(cen * cen, axis=-1, keepdims=True)
    return cen * lax.rsqrt(var + LN_EPS) * g + b


def _mixout_kernel(ret_ref, na_ref, g1_ref, g2_ref, x_ref, wr_ref, wn_ref, wo_ref, lg_ref, lb_ref, o_ref, op_ref):
    y_ret = jnp.dot(ret_ref[...], wr_ref[...], preferred_element_type=F32)
    y_na = jnp.dot(na_ref[...], wn_ref[...], preferred_element_type=F32)
    merged = g1_ref[...].astype(F32) * y_ret + g2_ref[...].astype(F32) * y_na
    mix = jnp.dot(merged.astype(BF16), wo_ref[...], preferred_element_type=F32)
    z = DEEPNORM_ALPHA * x_ref[...] + mix
    out = _layernorm_rows(z, lg_ref[...], lb_ref[...])
    o_ref[...] = out
    op_ref[...] = _pack_bf16_pair(out[:, :D_MODEL // 2], out[:, D_MODEL // 2:])


def _mixout(ret2d, na2d, gates2d, x2d, wr, wn, wo, lg, lb):
    t = x2d.shape[0]
    const = lambda i: (0, 0)
    return pl.pallas_call(
        _mixout_kernel,
        out_shape=(jax.ShapeDtypeStruct((t, D_MODEL), F32), jax.ShapeDtypeStruct((t, D_MODEL // 2), U32)),
        grid=(t // MIX_BM,),
        in_specs=[
            pl.BlockSpec((MIX_BM, RET_HEADS * RET_DV), lambda i: (i, 0)),
            pl.BlockSpec((MIX_BM, NA_HEADS * NA_HD), lambda i: (i, 0)),
            pl.BlockSpec((MIX_BM, D_MODEL), lambda i: (i, 0)),
            pl.BlockSpec((MIX_BM, D_MODEL), lambda i: (i, 1)),
            pl.BlockSpec((MIX_BM, D_MODEL), lambda i: (i, 0)),
            pl.BlockSpec((RET_HEADS * RET_DV, D_MODEL), const),
            pl.BlockSpec((NA_HEADS * NA_HD, D_MODEL), const),
            pl.BlockSpec((D_MODEL, D_MODEL), const),
            pl.BlockSpec((1, D_MODEL), const),
            pl.BlockSpec((1, D_MODEL), const),
        ],
        out_specs=(pl.BlockSpec((MIX_BM, D_MODEL), lambda i: (i, 0)),
                   pl.BlockSpec((MIX_BM, D_MODEL // 2), lambda i: (i, 0))),
        compiler_params=_cparams(("arbitrary",)),
        name="mixout",
    )(ret2d, na2d, gates2d, gates2d, x2d, wr, wn, wo, lg, lb)


ROUTE_BM = 512


def _route_tile(x, rw, rb):
    nt_dims = (((1,), (1,)), ((), ()))
    x_hi = x.astype(BF16)
    x_lo = (x - x_hi.astype(F32)).astype(BF16)
    rw_hi = rw.astype(BF16)
    rw_lo = (rw - rw_hi.astype(F32)).astype(BF16)
    both = lax.dot_general(jnp.concatenate([rw_hi, rw_lo], axis=0), x_hi, nt_dims, preferred_element_type=F32)
    logits = (both[:N_EXPERTS] + both[N_EXPERTS:]
              + lax.dot_general(rw_hi, x_lo, nt_dims, preferred_element_type=F32))
    scores = jax.nn.sigmoid(logits)
    sel = scores + rb
    p = [scores[m * N_GROUPS:(m + 1) * N_GROUPS] for m in range(EXPERTS_PER_GROUP)]
    s = [sel[m * N_GROUPS:(m + 1) * N_GROUPS] for m in range(EXPERTS_PER_GROUP)]
    one = jnp.ones_like(s[0])
    zero = jnp.zeros_like(s[0])
    chosen = []
    for m in range(EXPERTS_PER_GROUP):
        rank = zero
        for j in range(EXPERTS_PER_GROUP):
            if j == m:
                continue
            beats = (s[j] >= s[m]) if j < m else (s[j] > s[m])
            rank = rank + jnp.where(beats, one, zero)
        chosen.append(rank < 2.0)
    group_score = zero
    for m in range(EXPERTS_PER_GROUP):
        group_score = group_score + jnp.where(chosen[m], s[m], zero)
    gid = lax.broadcasted_iota(jnp.int32, group_score.shape, 0)
    gmax = jnp.max(group_score, axis=0, keepdims=True)
    gbest = jnp.min(jnp.where(group_score == gmax, gid, N_GROUPS), axis=0, keepdims=True)
    in_best = gid == gbest
    picked = [chosen[m] & in_best for m in range(EXPERTS_PER_GROUP)]
    before = zero
    wa = zero
    wb = zero
    ma = zero
    mb = zero
    for m in range(EXPERTS_PER_GROUP):
        is_a = picked[m] & (before == 0.0)
        is_b = picked[m] & (before == 1.0)
        wa = wa + jnp.where(is_a, p[m], zero)
        wb = wb + jnp.where(is_b, p[m], zero)
        ma = ma + jnp.where(is_a, float(m), 0.0)
        mb = mb + jnp.where(is_b, float(m), 0.0)
        before = before + jnp.where(chosen[m], one, zero)
    wa = jnp.sum(wa, axis=0, keepdims=True)
    wb = jnp.sum(wb, axis=0, keepdims=True)
    ma = jnp.sum(ma, axis=0, keepdims=True)
    mb = jnp.sum(mb, axis=0, keepdims=True)
    denom = wa + wb
    w_lo = wa / denom
    w_hi = wb / denom
    pair = jnp.where(ma == 0.0, jnp.where(mb == 1.0, 0.0, mb),
                     jnp.where(ma == 1.0, jnp.where(mb == 2.0, 1.0, 4.0), 5.0))
    keep = pair == 0.0
    return (gbest * N_PAIRS + pair.astype(jnp.int32),
            jnp.where(keep, w_lo, w_hi), jnp.where(keep, w_hi, w_lo))


CLASS_ROWS = 64
META_LANES = 128


def _router_kernel(x_ref, rw_ref, rb_ref, tri_ref, dest_ref, wtok_ref, meta_ref, cls_s, rank_s, cnt_s):
    phase = pl.program_id(0)
    i = pl.program_id(1)
    bm = ROUTE_BM
    reps = bm // META_LANES
    cid = lax.broadcasted_iota(jnp.int32, (CLASS_ROWS, bm), 0)

    @pl.when((phase == 0) & (i == 0))
    def _():
        cnt_s[...] = jnp.zeros_like(cnt_s)

    @pl.when(phase == 0)
    def _():
        cls, w_lo, w_hi = _route_tile(x_ref[...], rw_ref[...], rb_ref[...])
        cls_s[i] = cls
        w_rows = jnp.concatenate([w_lo, w_hi, jnp.zeros((META_LANES - 2, bm), F32)], axis=0)
        wtok_ref[...] = w_rows.T
        onehot = (cid == cls).astype(BF16)
        before = jnp.dot(onehot, tri_ref[...], preferred_element_type=F32)
        carry = jnp.concatenate([cnt_s[...]] * reps, axis=1)
        rank = jnp.sum(jnp.where(cid == cls, before + carry, 0.0), axis=0, keepdims=True)
        rank_s[i] = rank.astype(jnp.int32)
        cnt_s[...] = cnt_s[...] + jnp.dot(onehot, jnp.ones((bm, META_LANES), BF16), preferred_element_type=F32)

    @pl.when(phase == 1)
    def _():
        cnt = cnt_s[...]
        nblk = jnp.floor((cnt + (MOE_BLOCK - 1.0)) * (1.0 / MOE_BLOCK))
        rr = lax.broadcasted_iota(jnp.int32, (CLASS_ROWS, CLASS_ROWS), 0)
        cc = lax.broadcasted_iota(jnp.int32, (CLASS_ROWS, CLASS_ROWS), 1)
        lower = (cc < rr).astype(BF16)
        start_blk = jnp.dot(lower, nblk.astype(BF16), preferred_element_type=F32)
        cls = cls_s[i]
        start_t = jnp.concatenate([start_blk] * reps, axis=1)
        start = jnp.sum(jnp.where(cid == cls, start_t, 0.0), axis=0, keepdims=True)
        dest_ref[0] = (start * float(MOE_BLOCK)).astype(jnp.int32) + rank_s[i]

        @pl.when(i == 0)
        def _():
            end_blk = start_blk + nblk
            n_used = jnp.max(end_blk, axis=0, keepdims=True)
            blk = lax.broadcasted_iota(jnp.int32, (CLASS_ROWS, META_LANES), 1).astype(F32)
            blk = jnp.minimum(blk, n_used - 1.0)
            bcls = jnp.sum(jnp.where(end_blk <= blk, 1.0, 0.0), axis=0, keepdims=True)
            grp = jnp.zeros_like(bcls)
            for g in range(1, N_GROUPS):
                grp = grp + jnp.where(bcls >= float(g * N_PAIRS), 1.0, 0.0)
            pair = bcls - grp * float(N_PAIRS)
            slot_a = jnp.zeros_like(pair)
            slot_b = jnp.zeros_like(pair)
            for k in range(N_PAIRS):
                slot_a = jnp.where(pair == float(k), float(PAIR_SLOT_A[k]), slot_a)
                slot_b = jnp.where(pair == float(k), float(PAIR_SLOT_B[k]), slot_b)
            rows = [grp * float(EXPERTS_PER_GROUP) + slot_a, grp * float(EXPERTS_PER_GROUP) + slot_b, n_used]
            rows = rows + [jnp.zeros_like(bcls)] * (8 - len(rows))
            meta_ref[...] = jnp.concatenate(rows, axis=0).astype(jnp.int32)


def _router(x2d, rw_t, rb_t):
    t = x2d.shape[0]
    nt = t // ROUTE_BM
    tri = jnp.asarray(np.triu(np.ones((ROUTE_BM, ROUTE_BM), np.float32), 1), BF16)
    hold = lambda ph, i: ((1 - ph) * i + ph * (nt - 1), 0)
    const = lambda ph, i: (0, 0)
    return pl.pallas_call(
        _router_kernel,
        out_shape=(jax.ShapeDtypeStruct((nt, 1, ROUTE_BM), jnp.int32),
                   jax.ShapeDtypeStruct((t, META_LANES), F32),
                   jax.ShapeDtypeStruct((8, META_LANES), jnp.int32)),
        grid=(2, nt),
        in_specs=[
            pl.BlockSpec((ROUTE_BM, D_MODEL), hold),
            pl.BlockSpec((N_EXPERTS, D_MODEL), const),
            pl.BlockSpec((N_EXPERTS, 1), const),
            pl.BlockSpec((ROUTE_BM, ROUTE_BM), const),
        ],
        out_specs=(pl.BlockSpec((1, 1, ROUTE_BM), lambda ph, i: (ph * i, 0, 0)),
                   pl.BlockSpec((ROUTE_BM, META_LANES), hold),
                   pl.BlockSpec((8, META_LANES), const)),
        scratch_shapes=[
            pltpu.VMEM((nt, 1, ROUTE_BM), jnp.int32),
            pltpu.VMEM((nt, 1, ROUTE_BM), jnp.int32),
            pltpu.VMEM((CLASS_ROWS, META_LANES), F32),
        ],
        compiler_params=_cparams(("arbitrary", "arbitrary")),
        name="router",
    )(x2d, rw_t, rb_t, tri)


ROW_BM = 256


def _start_rows(make_copy):
    for r in range(ROW_BM):
        make_copy(r).start(priority=r % 2)


def _dispatch_kernel(dest_ref, x_ref, zeros_ref, xs_ref, sem):
    del zeros_ref
    base = pl.program_id(0) * ROW_BM
    _start_rows(lambda r: pltpu.make_async_copy(x_ref.at[pl.ds(r, 1), :],
                                                xs_ref.at[pl.ds(dest_ref[base + r], 1), :], sem))
    pltpu.make_async_copy(x_ref, xs_ref.at[pl.ds(0, ROW_BM), :], sem).wait()


def _dispatch(dest, x_rows, n_rows):
    t, width = x_rows.shape
    grid_spec = pltpu.PrefetchScalarGridSpec(
        num_scalar_prefetch=1,
        grid=(t // ROW_BM,),
        in_specs=[
            pl.BlockSpec((ROW_BM, width), lambda i, dest: (i, 0)),
            pl.BlockSpec(memory_space=pl.ANY),
        ],
        out_specs=pl.BlockSpec(memory_space=pl.ANY),
        scratch_shapes=[pltpu.SemaphoreType.DMA],
    )
    return pl.pallas_call(
        _dispatch_kernel,
        out_shape=jax.ShapeDtypeStruct((n_rows, width), x_rows.dtype),
        grid_spec=grid_spec,
        input_output_aliases={2: 0},
        compiler_params=_cparams(("arbitrary",)),
        name="dispatch",
    )(dest, x_rows, jnp.zeros((n_rows, width), x_rows.dtype))


def _experts_kernel(ea_ref, eb_ref, nb_ref, x_ref, wga_ref, wua_ref, wda_ref, wgb_ref, wub_ref, wdb_ref, o_ref):
    @pl.when(pl.program_id(0) < nb_ref[0])
    def _():
        x = jnp.concatenate(_unpack_bf16_pair(x_ref[...]), axis=1).astype(BF16)
        ys = []
        for wg_ref, wu_ref, wd_ref in ((wga_ref, wua_ref, wda_ref), (wgb_ref, wub_ref, wdb_ref)):
            hg = jnp.dot(x, wg_ref[0, 0].astype(BF16), preferred_element_type=F32)
            hu = jnp.dot(x, wu_ref[0, 0].astype(BF16), preferred_element_type=F32)
            act = (hg * jax.nn.sigmoid(hg) * hu).astype(BF16)
            ys.append(jnp.dot(act, wd_ref[0, 0].astype(BF16), preferred_element_type=F32))
        o_ref[...] = _pack_bf16_pair(ys[0], ys[1])

    @pl.when(pl.program_id(0) >= nb_ref[0])
    def _():
        o_ref[...] = jnp.zeros_like(o_ref)


def _experts(blk_a, blk_b, n_used, xs, wg, wu, wd, layer):
    n_rows = xs.shape[0]
    n_blocks = n_rows // MOE_BLOCK
    wa_map = lambda j, ea, eb, nb: (layer, ea[j], 0, 0)
    wb_map = lambda j, ea, eb, nb: (layer, eb[j], 0, 0)
    in_row_map = lambda j, ea, eb, nb: (jnp.maximum(jnp.minimum(j, nb[0] - 1), 0), 0)
    row_map = lambda j, ea, eb, nb: (j, 0)
    grid_spec = pltpu.PrefetchScalarGridSpec(
        num_scalar_prefetch=3,
        grid=(n_blocks,),
        in_specs=[
            pl.BlockSpec((MOE_BLOCK, D_MODEL // 2), in_row_map),
            pl.BlockSpec((1, 1, D_MODEL, D_EXPERT), wa_map),
            pl.BlockSpec((1, 1, D_MODEL, D_EXPERT), wa_map),
            pl.BlockSpec((1, 1, D_EXPERT, D_MODEL), wa_map),
            pl.BlockSpec((1, 1, D_MODEL, D_EXPERT), wb_map),
            pl.BlockSpec((1, 1, D_MODEL, D_EXPERT), wb_map),
            pl.BlockSpec((1, 1, D_EXPERT, D_MODEL), wb_map),
        ],
        out_specs=pl.BlockSpec((MOE_BLOCK, D_MODEL), row_map),
    )
    return pl.pallas_call(
        _experts_kernel,
        out_shape=jax.ShapeDtypeStruct((n_rows, D_MODEL), U32),
        grid_spec=grid_spec,
        compiler_params=_cparams(("arbitrary",)),
        name="experts",
    )(blk_a, blk_b, n_used, xs, wg, wu, wd, wg, wu, wd)


def _combine_ln_kernel(dest_ref, x_ref, wtok_ref, lg_ref, lb_ref, ys_ref, o_ref, obf_ref, ybuf, sem):
    i = pl.program_id(0)
    slot = i % 2

    def start_tile(tile, to_slot):
        base = tile * ROW_BM
        _start_rows(lambda r: pltpu.make_async_copy(ys_ref.at[pl.ds(dest_ref[base + r], 1), :],
                                                    ybuf.at[to_slot, pl.ds(r, 1), :], sem.at[to_slot]))

    @pl.when(i == 0)
    def _():
        start_tile(0, 0)

    @pl.when(i + 1 < pl.num_programs(0))
    def _():
        start_tile(i + 1, 1 - slot)

    pltpu.make_async_copy(ys_ref.at[pl.ds(0, ROW_BM), :], ybuf.at[slot], sem.at[slot]).wait()
    w = wtok_ref[...]
    y_a, y_b = _unpack_bf16_pair(ybuf[slot])
    y = w[:, 0:1] * y_a + w[:, 1:2] * y_b
    z = DEEPNORM_ALPHA * x_ref[...] + y
    out = _layernorm_rows(z, lg_ref[...], lb_ref[...])
    o_ref[...] = out
    obf_ref[...] = out.astype(BF16)


def _combine_ln(dest, x2d, wtok, ys, lg, lb):
    t = x2d.shape[0]
    const = lambda i, dest: (0, 0)
    grid_spec = pltpu.PrefetchScalarGridSpec(
        num_scalar_prefetch=1,
        grid=(t // ROW_BM,),
        in_specs=[
            pl.BlockSpec((ROW_BM, D_MODEL), lambda i, dest: (i, 0)),
            pl.BlockSpec((ROW_BM, META_LANES), lambda i, dest: (i, 0)),
            pl.BlockSpec((1, D_MODEL), const),
            pl.BlockSpec((1, D_MODEL), const),
            pl.BlockSpec(memory_space=pl.ANY),
        ],
        out_specs=(pl.BlockSpec((ROW_BM, D_MODEL), lambda i, dest: (i, 0)),
                   pl.BlockSpec((ROW_BM, D_MODEL), lambda i, dest: (i, 0))),
        scratch_shapes=[pltpu.VMEM((2, ROW_BM, D_MODEL), U32), pltpu.SemaphoreType.DMA((2,))],
    )
    return pl.pallas_call(
        _combine_ln_kernel,
        out_shape=(jax.ShapeDtypeStruct((t, D_MODEL), F32), jax.ShapeDtypeStruct((t, D_MODEL), BF16)),
        grid_spec=grid_spec,
        compiler_params=_cparams(("arbitrary",)),
        name="combine_ln",
    )(dest, x2d, wtok, lg, lb, ys)


def _moe(x2d, x_packed, rw_t, rb_t, wg, wu, wd, layer, lg, lb):
    t = x2d.shape[0]
    n_blocks = (t + N_CLASSES * (MOE_BLOCK - 1)) // MOE_BLOCK
    assert n_blocks <= META_LANES
    dest3, wtok, meta = _router(x2d, rw_t, rb_t)
    dest = dest3.reshape(t)
    xs = _dispatch(dest, x_packed, n_blocks * MOE_BLOCK)
    ys = _experts(meta[0, :n_blocks], meta[1, :n_blocks], meta[2, :1], xs, wg, wu, wd, layer)
    return _combine_ln(dest, x2d, wtok, ys, lg, lb)


def _rope_tables(seq):
    half = RET_DK // 2
    pos = jnp.arange(seq, dtype=F32)
    freqs = ROPE_BASE ** (-jnp.arange(half, dtype=F32) / half)
    ang = pos[:, None] * freqs[None, :]
    return jnp.cos(ang), jnp.sin(ang)


def kernel(x, w_in, ret_decay_logit, w_ret_o, na_rpb, w_na_o, w_out, ln_mix_g, ln_mix_b, router_w, router_bias,
           w_exp_gate, w_exp_up, w_exp_down, ln_ffn_g, ln_ffn_b):
    b, s, d = x.shape
    depth = w_in.shape[0]
    t = b * s
    rows = s // GRID_W
    cos, sin = _rope_tables(s)
    rw_t = router_w.astype(F32).T.reshape(N_GROUPS, EXPERTS_PER_GROUP, d).transpose(1, 0, 2).reshape(N_EXPERTS, d)
    rb_t = router_bias.astype(F32).reshape(N_GROUPS, EXPERTS_PER_GROUP).T.reshape(N_EXPERTS, 1)
    x2d = x.reshape(t, d)
    x_bf = x2d.astype(BF16)
    w_in = w_in.astype(F32)
    for l in range(depth):
        qk = _inproj(x_bf, w_in, l, cos, sin, s, "rotary").reshape(b, s, -1)
        vn = _inproj(x_bf, w_in, l, cos, sin, s, "scale").reshape(b, s, -1)
        gr = _inproj(x_bf, w_in, l, cos, sin, s, "silu").reshape(b, s, -1)
        gates = _inproj(x_bf, w_in, l, cos, sin, s, "sigmoid")
        ret = _retention(qk, vn, gr, ret_decay_logit[l].astype(F32))
        na = _na(vn, na_rpb[l].astype(F32).reshape(-1))
        x2d, x_packed = _mixout(ret.reshape(t, -1), na.reshape(t, -1), gates, x2d,
                                w_ret_o[l].astype(BF16), w_na_o[l].astype(BF16), w_out[l].astype(BF16),
                                ln_mix_g[l].reshape(1, d).astype(F32), ln_mix_b[l].reshape(1, d).astype(F32))
        x2d, x_bf = _moe(x2d, x_packed, rw_t, rb_t, w_exp_gate, w_exp_up, w_exp_down, l,
                         ln_ffn_g[l].reshape(1, d).astype(F32), ln_ffn_b[l].reshape(1, d).astype(F32))
    return x2d.reshape(b, s, d)
```

```python
import functools

import numpy as np
import jax
import jax.numpy as jnp
from jax import lax
from jax.experimental import pallas as pl
from jax.experimental.pallas import tpu as pltpu

F32 = jnp.float32
BF16 = jnp.bfloat16

D_MODEL = 1024
GRID_W = 64
RET_HEADS = 4
RET_DK = 256
RET_DV = 512
ROPE_BASE = 10000.0
NA_HEADS = 16
NA_HD = 64
NA_KH = 8
NA_KW = 16
N_EXPERTS = 32
N_GROUPS = 8
EXPERTS_PER_GROUP = 4
D_EXPERT = 512
LN_EPS = 1e-5
GN_EPS = 1e-5
MODEL_DEPTH = 4
DEEPNORM_ALPHA = (2 * MODEL_DEPTH) ** 0.25

D_IN = 11264

RET_CHUNK = 256
NA_QROWS = 4
NA_KROWS = 12
NA_NEG = -1e30
MOE_BLOCK = 256
N_PAIRS = 6
PAIR_SLOT_A = (0, 2, 2, 3, 3, 3)
PAIR_SLOT_B = (1, 1, 0, 0, 1, 2)
N_CLASSES = N_GROUPS * N_PAIRS
VMEM_LIMIT = 56 * 1024 * 1024


def _cparams(sem):
    return pltpu.CompilerParams(dimension_semantics=sem, vmem_limit_bytes=VMEM_LIMIT)


U32 = jnp.uint32
_HIGH_HALF = 0xFFFF0000


def _pack_bf16_pair(hi, lo):
    hi_bits = lax.bitcast_convert_type(hi.astype(jnp.bfloat16).astype(F32), U32)
    lo_bits = lax.bitcast_convert_type(lo.astype(jnp.bfloat16).astype(F32), U32)
    return (hi_bits & jnp.uint32(_HIGH_HALF)) | (lo_bits >> 16)


def _unpack_bf16_pair(packed):
    hi = lax.bitcast_convert_type(packed & jnp.uint32(_HIGH_HALF), F32)
    lo = lax.bitcast_convert_type(packed << 16, F32)
    return hi, lo


IN_BM = 1024
IN_CHUNK = 256
LOG2E = 1.4426950408889634

IN_GROUPS = {
    "rotary": (2048, (0,)),
    "scale": (1024, (2, 3, 6, 7, 8)),
    "silu": (2048, (2,)),
    "sigmoid": (1024, (9, 10)),
}
VN_Q = 2048
VN_K = 3072
VN_V = 4096


def _inproj_kernel(x_ref, w_ref, cos_ref, sin_ref, o_ref, wbf_ref, *, kind, bn):
    j = pl.program_id(0)

    @pl.when(pl.program_id(1) == 0)
    def _():
        wbf_ref[...] = w_ref[0].astype(BF16)

    x = x_ref[...]
    half = RET_DK // 2
    for c in range(bn // IN_CHUNK):
        lo = c * IN_CHUNK
        acc = jnp.dot(x, wbf_ref[:, lo:lo + IN_CHUNK], preferred_element_type=F32)
        if kind == "rotary":
            scale = 1.0 if lo < RET_HEADS * RET_DK else RET_DK ** -0.5
            cos = cos_ref[...] * scale
            sin = sin_ref[...] * scale
            x1 = acc[:, :half]
            x2 = acc[:, half:]
            o_ref[:, lo:lo + half] = (x1 * cos - x2 * sin).astype(o_ref.dtype)
            o_ref[:, lo + half:lo + IN_CHUNK] = (x1 * sin + x2 * cos).astype(o_ref.dtype)
        elif kind == "scale":
            scale = jnp.where(j == 2, NA_HD ** -0.5 * LOG2E, 1.0)
            o_ref[:, lo:lo + IN_CHUNK] = (acc * scale).astype(o_ref.dtype)
        elif kind == "silu":
            o_ref[:, lo:lo + IN_CHUNK] = (acc * jax.nn.sigmoid(acc)).astype(o_ref.dtype)
        else:
            o_ref[:, lo:lo + IN_CHUNK] = jax.nn.sigmoid(acc).astype(o_ref.dtype)


def _inproj(x_bf, w_in, layer, cos, sin, seq, kind):
    t = x_bf.shape[0]
    pos_blocks = seq // IN_BM
    bn, blocks = IN_GROUPS[kind]

    def wcol(j):
        col = blocks[0]
        for k in range(1, len(blocks)):
            col = jnp.where(j == k, blocks[k], col)
        return col

    pos_map = (lambda j, i: (i % pos_blocks, 0)) if kind == "rotary" else (lambda j, i: (0, 0))
    return pl.pallas_call(
        functools.partial(_inproj_kernel, kind=kind, bn=bn),
        out_shape=jax.ShapeDtypeStruct((t, len(blocks) * bn), BF16),
        grid=(len(blocks), t // IN_BM),
        in_specs=[
            pl.BlockSpec((IN_BM, D_MODEL), lambda j, i: (i, 0)),
            pl.BlockSpec((1, D_MODEL, bn), lambda j, i: (layer, 0, wcol(j))),
            pl.BlockSpec((IN_BM, RET_DK // 2), pos_map),
            pl.BlockSpec((IN_BM, RET_DK // 2), pos_map),
        ],
        out_specs=pl.BlockSpec((IN_BM, bn), lambda j, i: (i, j)),
        scratch_shapes=[pltpu.VMEM((D_MODEL, bn), BF16)],
        compiler_params=_cparams(("arbitrary", "arbitrary")),
        name="inproj_" + kind,
    )(x_bf, w_in, cos, sin)


def _log_sigmoid(x):
    return jnp.minimum(x, 0.0) - jnp.log1p(jnp.exp(-jnp.abs(x)))


def _retention_kernel(dl_ref, q_ref, k_ref, v_ref, g_ref, o_ref,
                      sf_ref, st_ref, dm_ref, qdf_ref, qdb_ref, kdf_ref, kdb_ref, *, nc):
    c_len = RET_CHUNK
    h = pl.program_id(1)
    lgf = _log_sigmoid(jnp.full((c_len, RET_DK), dl_ref[0, h], F32))
    lgb = _log_sigmoid(jnp.full((c_len, RET_DK), dl_ref[1, h], F32))
    ri = lax.broadcasted_iota(jnp.int32, (c_len, RET_DK), 0).astype(F32)
    qdf_ref[...] = jnp.exp(lgf * (ri + 1.0))
    qdb_ref[...] = jnp.exp(lgb * (c_len - ri))
    kdf_ref[...] = jnp.exp(lgf * (c_len - 1.0 - ri))
    kdb_ref[...] = jnp.exp(lgb * ri)
    rr = lax.broadcasted_iota(jnp.int32, (c_len, c_len), 0).astype(F32)
    cc = lax.broadcasted_iota(jnp.int32, (c_len, c_len), 1).astype(F32)
    diff = rr - cc
    lgf_cc = _log_sigmoid(jnp.full((c_len, c_len), dl_ref[0, h], F32))
    lgb_cc = _log_sigmoid(jnp.full((c_len, c_len), dl_ref[1, h], F32))
    dm_ref[...] = jnp.where(diff >= 0.0, jnp.exp(lgf_cc * jnp.maximum(diff, 0.0)),
                            jnp.exp(lgb_cc * jnp.maximum(-diff, 0.0)))
    chunk_f = jnp.exp(lgf[:1, :1] * float(c_len))
    chunk_b = jnp.exp(lgb[:1, :1] * float(c_len))

    tn_dims = (((0,), (0,)), ((), ()))
    nt_dims = (((1,), (1,)), ((), ()))

    st_ref[...] = jnp.zeros_like(st_ref)

    def fwd_body(c, carry):
        off = pl.multiple_of(c * c_len, c_len)
        sf_ref[c] = st_ref[...].astype(BF16)
        kc = k_ref[0, pl.ds(off, c_len), :].astype(F32)
        vc = v_ref[0, pl.ds(off, c_len), :]
        kd = (kc * kdf_ref[...]).astype(BF16)
        upd = lax.dot_general(kd, vc, tn_dims, preferred_element_type=F32)
        st_ref[...] = st_ref[...] * chunk_f + upd
        return carry

    lax.fori_loop(0, nc, fwd_body, 0, unroll=8)

    st_ref[...] = jnp.zeros_like(st_ref)

    def bwd_body(i, carry):
        c = nc - 1 - i
        off = pl.multiple_of(c * c_len, c_len)
        qb = q_ref[0, pl.ds(off, c_len), :]
        kb = k_ref[0, pl.ds(off, c_len), :]
        vc = v_ref[0, pl.ds(off, c_len), :]
        qc = qb.astype(F32)
        kc = kb.astype(F32)
        s = lax.dot_general(qb, kb, nt_dims, preferred_element_type=F32)
        p = (s * dm_ref[...]).astype(BF16)
        out = jnp.dot(p, vc, preferred_element_type=F32)
        out = out + jnp.dot((qc * qdf_ref[...]).astype(BF16), sf_ref[c], preferred_element_type=F32)
        out = out + jnp.dot((qc * qdb_ref[...]).astype(BF16), st_ref[...].astype(BF16),
                            preferred_element_type=F32)
        mu = jnp.mean(out, axis=-1, keepdims=True)
        cen = out - mu
        var = jnp.mean(cen * cen, axis=-1, keepdims=True)
        y = cen * lax.rsqrt(var + GN_EPS)
        gate = g_ref[0, pl.ds(off, c_len), :].astype(F32)
        o_ref[0, pl.ds(off, c_len), :] = (gate * y).astype(o_ref.dtype)
        kd = (kc * kdb_ref[...]).astype(BF16)
        upd = lax.dot_general(kd, vc, tn_dims, preferred_element_type=F32)
        st_ref[...] = st_ref[...] * chunk_b + upd
        return carry

    lax.fori_loop(0, nc, bwd_body, 0, unroll=8)


def _retention(qk3, vn3, gr3, decay_logit):
    b, s, _ = qk3.shape
    nc = s // RET_CHUNK
    return pl.pallas_call(
        functools.partial(_retention_kernel, nc=nc),
        out_shape=jax.ShapeDtypeStruct((b, s, RET_HEADS * RET_DV), BF16),
        grid=(b, RET_HEADS),
        in_specs=[
            pl.BlockSpec(memory_space=pltpu.SMEM),
            pl.BlockSpec((1, s, RET_DK), lambda bi, h: (bi, 0, h)),
            pl.BlockSpec((1, s, RET_DK), lambda bi, h: (bi, 0, RET_HEADS + h)),
            pl.BlockSpec((1, s, RET_DV), lambda bi, h: (bi, 0, h)),
            pl.BlockSpec((1, s, RET_DV), lambda bi, h: (bi, 0, h)),
        ],
        out_specs=pl.BlockSpec((1, s, RET_DV), lambda bi, h: (bi, 0, h)),
        scratch_shapes=[
            pltpu.VMEM((nc, RET_DK, RET_DV), BF16),
            pltpu.VMEM((RET_DK, RET_DV), F32),
            pltpu.VMEM((RET_CHUNK, RET_CHUNK), F32),
            pltpu.VMEM((RET_CHUNK, RET_DK), F32),
            pltpu.VMEM((RET_CHUNK, RET_DK), F32),
            pltpu.VMEM((RET_CHUNK, RET_DK), F32),
            pltpu.VMEM((RET_CHUNK, RET_DK), F32),
        ],
        compiler_params=_cparams(("arbitrary", "arbitrary")),
        name="retention",
    )(decay_logit, qk3, qk3, vn3, gr3)


NA_ROW_OFFS = 2 * NA_KH - 1
NA_COL_OFFS = 2 * NA_KW - 1
NA_PATTERNS = 3


def _na_row_offsets(rows):
    n_tiles = rows // NA_QROWS
    offs = np.full((NA_PATTERNS, NA_QROWS, NA_KROWS), NA_ROW_OFFS, np.int32)
    for p, t in enumerate((0, 1, n_tiles - 1)):
        kstart = int(np.clip(NA_QROWS * t - NA_KH // 2, 0, rows - NA_KROWS))
        for rr in range(NA_QROWS):
            r = NA_QROWS * t + rr
            rs = int(np.clip(r - NA_KH // 2, 0, rows - NA_KH))
            for i in range(NA_KROWS):
                krow = kstart + i
                if rs <= krow < rs + NA_KH:
                    offs[p, rr, i] = krow - r + (NA_KH - 1)
    return offs


def _na_build_bias(rpb_ref, e_ref, bias_ref, head0, rows):
    wide = 2 * GRID_W
    lane = lax.broadcasted_iota(jnp.int32, (GRID_W, wide), 1)
    col = lax.broadcasted_iota(jnp.int32, (GRID_W, wide), 0)
    kcol = jnp.where(lane < GRID_W, lane, lane - GRID_W)
    cstart = jnp.clip(col - NA_KW // 2, 0, GRID_W - NA_KW)
    col_ok = (kcol >= cstart) & (kcol < cstart + NA_KW)
    coff = kcol - col + (NA_KW - 1)
    neg = jnp.full((GRID_W, wide), NA_NEG, F32)
    left = lane < GRID_W
    offs = _na_row_offsets(rows)
    for hh in range(2):
        base = (head0 + hh) * (NA_ROW_OFFS * NA_COL_OFFS)
        for ro in range(NA_ROW_OFFS):
            acc = neg
            for d in range(NA_COL_OFFS):
                acc = jnp.where(coff == d, rpb_ref[base + ro * NA_COL_OFFS + d] * LOG2E, acc)
            e_ref[hh, ro] = jnp.where(col_ok, acc, neg)
        e_ref[hh, NA_ROW_OFFS] = neg
        for p in range(NA_PATTERNS):
            for rr in range(NA_QROWS):
                for ip in range(NA_KROWS // 2):
                    tile = jnp.where(left, e_ref[hh, int(offs[p, rr, 2 * ip])], e_ref[hh, int(offs[p, rr, 2 * ip + 1])])
                    bias_ref[hh, p, rr * GRID_W:(rr + 1) * GRID_W, ip * wide:(ip + 1) * wide] = tile


def _na_kernel(rpb_ref, q_ref, k_ref, v_ref, o_ref, bias_ref, e_ref, *, n_tiles, rows):
    nq = NA_QROWS * GRID_W
    nk = NA_KROWS * GRID_W
    nt_dims = (((1,), (1,)), ((), ()))
    first = lax.broadcasted_iota(jnp.int32, (nq, 2 * NA_HD), 1) < NA_HD
    kfirst = lax.broadcasted_iota(jnp.int32, (nk, 2 * NA_HD), 1) < NA_HD

    @pl.when(pl.program_id(1) == 0)
    def _():
        _na_build_bias(rpb_ref, e_ref, bias_ref, 2 * pl.program_id(0), rows)

    def body(t, carry):
        qoff = pl.multiple_of(t * nq, nq)
        krow0 = jnp.clip(NA_QROWS * t - NA_KH // 2, 0, rows - NA_KROWS)
        koff = pl.multiple_of(krow0 * GRID_W, nq)
        pat = jnp.where(t == 0, 0, jnp.where(t == n_tiles - 1, 2, 1))
        q2 = q_ref[0, pl.ds(qoff, nq), :]
        k2 = k_ref[0, pl.ds(koff, nk), :]
        v2 = v_ref[0, pl.ds(koff, nk), :]
        outs = []
        for hh in range(2):
            qm = jnp.where(first if hh == 0 else jnp.logical_not(first), q2, jnp.zeros_like(q2))
            s = lax.dot_general(qm, k2, nt_dims, preferred_element_type=F32)
            s = s + bias_ref[hh, pat]
            m = jnp.max(s, axis=-1, keepdims=True)
            e = jnp.exp2(s - m).astype(BF16)
            vv = jnp.where(kfirst if hh == 0 else jnp.logical_not(kfirst), v2, jnp.ones_like(v2))
            o = jnp.dot(e, vv, preferred_element_type=F32)
            outs.append(o / pltpu.roll(o, NA_HD, axis=1))
        o_ref[0, pl.ds(qoff, nq), :] = jnp.where(first, outs[0], outs[1]).astype(o_ref.dtype)
        return carry

    lax.fori_loop(0, n_tiles, body, 0, unroll=4)


def _na(vn3, rpb_flat):
    b, s, _ = vn3.shape
    rows = s // GRID_W
    n_tiles = rows // NA_QROWS
    w2 = 2 * NA_HD
    nq = NA_QROWS * GRID_W
    nk = NA_KROWS * GRID_W
    return pl.pallas_call(
        functools.partial(_na_kernel, n_tiles=n_tiles, rows=rows),
        out_shape=jax.ShapeDtypeStruct((b, s, NA_HEADS * NA_HD), BF16),
        grid=(NA_HEADS // 2, b),
        in_specs=[
            pl.BlockSpec(memory_space=pltpu.SMEM),
            pl.BlockSpec((1, s, w2), lambda hp, bi: (bi, 0, VN_Q // w2 + hp)),
            pl.BlockSpec((1, s, w2), lambda hp, bi: (bi, 0, VN_K // w2 + hp)),
            pl.BlockSpec((1, s, w2), lambda hp, bi: (bi, 0, VN_V // w2 + hp)),
        ],
        out_specs=pl.BlockSpec((1, s, w2), lambda hp, bi: (bi, 0, hp)),
        scratch_shapes=[
            pltpu.VMEM((2, NA_PATTERNS, nq, nk), F32),
            pltpu.VMEM((2, NA_ROW_OFFS + 1, GRID_W, 2 * GRID_W), F32),
        ],
        compiler_params=_cparams(("arbitrary", "arbitrary")),
        name="natten",
    )(rpb_flat, vn3, vn3, vn3)


MIX_BM = 512
MIX_SUB = 256


def _layernorm_rows(z, g, b):
    mu = jnp.mean(z, axis=-1, keepdims=True)
    cen = z - mu
    var = jnp.mean(cen * cen, axis=-1, keepdims=True)
    return cen * lax.rsqrt(var + LN_EPS) * g + b


def _mixout_kernel(ret_ref, na_ref, g1_ref, g2_ref, x_ref, wr_ref, wn_ref, wo_ref, lg_ref, lb_ref, o_ref, op_ref):
    for r in range(0, MIX_BM, MIX_SUB):
        rows = slice(r, r + MIX_SUB)
        y_ret = jnp.dot(ret_ref[rows, :], wr_ref[...], preferred_element_type=F32)
        y_na = jnp.dot(na_ref[rows, :], wn_ref[...], preferred_element_type=F32)
        merged = g1_ref[rows, :].astype(F32) * y_ret + g2_ref[rows, :].astype(F32) * y_na
        mix = jnp.dot(merged.astype(BF16), wo_ref[...], preferred_element_type=F32)
        z = DEEPNORM_ALPHA * x_ref[rows, :] + mix
        out = _layernorm_rows(z, lg_ref[...], lb_ref[...])
        o_ref[rows, :] = out
        op_ref[rows, :] = _pack_bf16_pair(out[:, :D_MODEL // 2], out[:, D_MODEL // 2:])


def _mixout(ret2d, na2d, gates2d, x2d, wr, wn, wo, lg, lb):
    t = x2d.shape[0]
    const = lambda i: (0, 0)
    return pl.pallas_call(
        _mixout_kernel,
        out_shape=(jax.ShapeDtypeStruct((t, D_MODEL), F32), jax.ShapeDtypeStruct((t, D_MODEL // 2), U32)),
        grid=(t // MIX_BM,),
        in_specs=[
            pl.BlockSpec((MIX_BM, RET_HEADS * RET_DV), lambda i: (i, 0)),
            pl.BlockSpec((MIX_BM, NA_HEADS * NA_HD), lambda i: (i, 0)),
            pl.BlockSpec((MIX_BM, D_MODEL), lambda i: (i, 0)),
            pl.BlockSpec((MIX_BM, D_MODEL), lambda i: (i, 1)),
            pl.BlockSpec((MIX_BM, D_MODEL), lambda i: (i, 0)),
            pl.BlockSpec((RET_HEADS * RET_DV, D_MODEL), const),
            pl.BlockSpec((NA_HEADS * NA_HD, D_MODEL), const),
            pl.BlockSpec((D_MODEL, D_MODEL), const),
            pl.BlockSpec((1, D_MODEL), const),
            pl.BlockSpec((1, D_MODEL), const),
        ],
        out_specs=(pl.BlockSpec((MIX_BM, D_MODEL), lambda i: (i, 0)),
                   pl.BlockSpec((MIX_BM, D_MODEL // 2), lambda i: (i, 0))),
        compiler_params=_cparams(("arbitrary",)),
        name="mixout",
    )(ret2d, na2d, gates2d, gates2d, x2d, wr, wn, wo, lg, lb)


ROUTE_BM = 512


def _route_tile(x, rw, rb):
    nt_dims = (((1,), (1,)), ((), ()))
    x_hi = x.astype(BF16)
    x_lo = (x - x_hi.astype(F32)).astype(BF16)
    rw_hi = rw.astype(BF16)
    rw_lo = (rw - rw_hi.astype(F32)).astype(BF16)
    both = lax.dot_general(jnp.concatenate([rw_hi, rw_lo], axis=0), x_hi, nt_dims, preferred_element_type=F32)
    logits = (both[:N_EXPERTS] + both[N_EXPERTS:]
              + lax.dot_general(rw_hi, x_lo, nt_dims, preferred_element_type=F32))
    scores = jax.nn.sigmoid(logits)
    sel = scores + rb
    p = [scores[m * N_GROUPS:(m + 1) * N_GROUPS] for m in range(EXPERTS_PER_GROUP)]
    s = [sel[m * N_GROUPS:(m + 1) * N_GROUPS] for m in range(EXPERTS_PER_GROUP)]
    one = jnp.ones_like(s[0])
    zero = jnp.zeros_like(s[0])
    chosen = []
    for m in range(EXPERTS_PER_GROUP):
        rank = zero
        for j in range(EXPERTS_PER_GROUP):
            if j == m:
                continue
            beats = (s[j] >= s[m]) if j < m else (s[j] > s[m])
            rank = rank + jnp.where(beats, one, zero)
        chosen.append(rank < 2.0)
    group_score = zero
    for m in range(EXPERTS_PER_GROUP):
        group_score = group_score + jnp.where(chosen[m], s[m], zero)
    gid = lax.broadcasted_iota(jnp.int32, group_score.shape, 0)
    gmax = jnp.max(group_score, axis=0, keepdims=True)
    gbest = jnp.min(jnp.where(group_score == gmax, gid, N_GROUPS), axis=0, keepdims=True)
    in_best = gid == gbest
    picked = [chosen[m] & in_best for m in range(EXPERTS_PER_GROUP)]
    before = zero
    wa = zero
    wb = zero
    ma = zero
    mb = zero
    for m in range(EXPERTS_PER_GROUP):
        is_a = picked[m] & (before == 0.0)
        is_b = picked[m] & (before == 1.0)
        wa = wa + jnp.where(is_a, p[m], zero)
        wb = wb + jnp.where(is_b, p[m], zero)
        ma = ma + jnp.where(is_a, float(m), 0.0)
        mb = mb + jnp.where(is_b, float(m), 0.0)
        before = before + jnp.where(chosen[m], one, zero)
    wa = jnp.sum(wa, axis=0, keepdims=True)
    wb = jnp.sum(wb, axis=0, keepdims=True)
    ma = jnp.sum(ma, axis=0, keepdims=True)
    mb = jnp.sum(mb, axis=0, keepdims=True)
    denom = wa + wb
    w_lo = wa / denom
    w_hi = wb / denom
    pair = jnp.where(ma == 0.0, jnp.where(mb == 1.0, 0.0, mb),
                     jnp.where(ma == 1.0, jnp.where(mb == 2.0, 1.0, 4.0), 5.0))
    keep = pair == 0.0
    return (gbest * N_PAIRS + pair.astype(jnp.int32),
            jnp.where(keep, w_lo, w_hi), jnp.where(keep, w_hi, w_lo))


CLASS_ROWS = 64
META_LANES = 128


def _router_kernel(x_ref, rw_ref, rb_ref, tri_ref, dest_ref, wtok_ref, meta_ref, cls_s, rank_s, cnt_s):
    phase = pl.program_id(0)
    i = pl.program_id(1)
    bm = ROUTE_BM
    reps = bm // META_LANES
    cid = lax.broadcasted_iota(jnp.int32, (CLASS_ROWS, bm), 0)

    @pl.when((phase == 0) & (i == 0))
    def _():
        cnt_s[...] = jnp.zeros_like(cnt_s)

    @pl.when(phase == 0)
    def _():
        cls, w_lo, w_hi = _route_tile(x_ref[...], rw_ref[...], rb_ref[...])
        cls_s[i] = cls
        w_rows = jnp.concatenate([w_lo, w_hi, jnp.zeros((META_LANES - 2, bm), F32)], axis=0)
        wtok_ref[...] = w_rows.T
        onehot = (cid == cls).astype(BF16)
        before = jnp.dot(onehot, tri_ref[...], preferred_element_type=F32)
        carry = jnp.concatenate([cnt_s[...]] * reps, axis=1)
        rank = jnp.sum(jnp.where(cid == cls, before + carry, 0.0), axis=0, keepdims=True)
        rank_s[i] = rank.astype(jnp.int32)
        cnt_s[...] = cnt_s[...] + jnp.dot(onehot, jnp.ones((bm, META_LANES), BF16), preferred_element_type=F32)

    @pl.when(phase == 1)
    def _():
        cnt = cnt_s[...]
        nblk = jnp.floor((cnt + (MOE_BLOCK - 1.0)) * (1.0 / MOE_BLOCK))
        rr = lax.broadcasted_iota(jnp.int32, (CLASS_ROWS, CLASS_ROWS), 0)
        cc = lax.broadcasted_iota(jnp.int32, (CLASS_ROWS, CLASS_ROWS), 1)
        lower = (cc < rr).astype(BF16)
        start_blk = jnp.dot(lower, nblk.astype(BF16), preferred_element_type=F32)
        cls = cls_s[i]
        start_t = jnp.concatenate([start_blk] * reps, axis=1)
        start = jnp.sum(jnp.where(cid == cls, start_t, 0.0), axis=0, keepdims=True)
        dest_ref[0] = (start * float(MOE_BLOCK)).astype(jnp.int32) + rank_s[i]

        @pl.when(i == 0)
        def _():
            end_blk = start_blk + nblk
            n_used = jnp.max(end_blk, axis=0, keepdims=True)
            blk = lax.broadcasted_iota(jnp.int32, (CLASS_ROWS, META_LANES), 1).astype(F32)
            blk = jnp.minimum(blk, n_used - 1.0)
            bcls = jnp.sum(jnp.where(end_blk <= blk, 1.0, 0.0), axis=0, keepdims=True)
            grp = jnp.zeros_like(bcls)
            for g in range(1, N_GROUPS):
                grp = grp + jnp.where(bcls >= float(g * N_PAIRS), 1.0, 0.0)
            pair = bcls - grp * float(N_PAIRS)
            slot_a = jnp.zeros_like(pair)
            slot_b = jnp.zeros_like(pair)
            for k in range(N_PAIRS):
                slot_a = jnp.where(pair == float(k), float(PAIR_SLOT_A[k]), slot_a)
                slot_b = jnp.where(pair == float(k), float(PAIR_SLOT_B[k]), slot_b)
            rows = [grp * float(EXPERTS_PER_GROUP) + slot_a, grp * float(EXPERTS_PER_GROUP) + slot_b, n_used]
            rows = rows + [jnp.zeros_like(bcls)] * (8 - len(rows))
            meta_ref[...] = jnp.concatenate(rows, axis=0).astype(jnp.int32)


def _router(x2d, rw_t, rb_t):
    t = x2d.shape[0]
    nt = t // ROUTE_BM
    tri = jnp.asarray(np.triu(np.ones((ROUTE_BM, ROUTE_BM), np.float32), 1), BF16)
    hold = lambda ph, i: ((1 - ph) * i + ph * (nt - 1), 0)
    const = lambda ph, i: (0, 0)
    return pl.pallas_call(
        _router_kernel,
        out_shape=(jax.ShapeDtypeStruct((nt, 1, ROUTE_BM), jnp.int32),
                   jax.ShapeDtypeStruct((t, META_LANES), F32),
                   jax.ShapeDtypeStruct((8, META_LANES), jnp.int32)),
        grid=(2, nt),
        in_specs=[
            pl.BlockSpec((ROUTE_BM, D_MODEL), hold),
            pl.BlockSpec((N_EXPERTS, D_MODEL), const),
            pl.BlockSpec((N_EXPERTS, 1), const),
            pl.BlockSpec((ROUTE_BM, ROUTE_BM), const),
        ],
        out_specs=(pl.BlockSpec((1, 1, ROUTE_BM), lambda ph, i: (ph * i, 0, 0)),
                   pl.BlockSpec((ROUTE_BM, META_LANES), hold),
                   pl.BlockSpec((8, META_LANES), const)),
        scratch_shapes=[
            pltpu.VMEM((nt, 1, ROUTE_BM), jnp.int32),
            pltpu.VMEM((nt, 1, ROUTE_BM), jnp.int32),
            pltpu.VMEM((CLASS_ROWS, META_LANES), F32),
        ],
        compiler_params=_cparams(("arbitrary", "arbitrary")),
        name="router",
    )(x2d, rw_t, rb_t, tri)


ROW_BM = 512


def _start_rows(make_copy):
    for r in range(ROW_BM):
        make_copy(r).start(priority=r % 2)


def _dispatch_kernel(dest_ref, x_ref, zeros_ref, xs_ref, sem):
    del zeros_ref
    base = pl.program_id(0) * ROW_BM
    _start_rows(lambda r: pltpu.make_async_copy(x_ref.at[pl.ds(r, 1), :],
                                                xs_ref.at[pl.ds(dest_ref[base + r], 1), :], sem))
    pltpu.make_async_copy(x_ref, xs_ref.at[pl.ds(0, ROW_BM), :], sem).wait()


def _dispatch(dest, x_rows, n_rows):
    t, width = x_rows.shape
    grid_spec = pltpu.PrefetchScalarGridSpec(
        num_scalar_prefetch=1,
        grid=(t // ROW_BM,),
        in_specs=[
            pl.BlockSpec((ROW_BM, width), lambda i, dest: (i, 0)),
            pl.BlockSpec(memory_space=pl.ANY),
        ],
        out_specs=pl.BlockSpec(memory_space=pl.ANY),
        scratch_shapes=[pltpu.SemaphoreType.DMA],
    )
    return pl.pallas_call(
        _dispatch_kernel,
        out_shape=jax.ShapeDtypeStruct((n_rows, width), x_rows.dtype),
        grid_spec=grid_spec,
        input_output_aliases={2: 0},
        compiler_params=_cparams(("arbitrary",)),
        name="dispatch",
    )(dest, x_rows, jnp.zeros((n_rows, width), x_rows.dtype))


def _experts_kernel(ea_ref, eb_ref, nb_ref, x_ref, wga_ref, wua_ref, wda_ref, wgb_ref, wub_ref, wdb_ref, o_ref):
    @pl.when(pl.program_id(0) < nb_ref[0])
    def _():
        x = jnp.concatenate(_unpack_bf16_pair(x_ref[...]), axis=1).astype(BF16)
        ys = []
        for wg_ref, wu_ref, wd_ref in ((wga_ref, wua_ref, wda_ref), (wgb_ref, wub_ref, wdb_ref)):
            hg = jnp.dot(x, wg_ref[0, 0].astype(BF16), preferred_element_type=F32)
            hu = jnp.dot(x, wu_ref[0, 0].astype(BF16), preferred_element_type=F32)
            act = (hg * jax.nn.sigmoid(hg) * hu).astype(BF16)
            ys.append(jnp.dot(act, wd_ref[0, 0].astype(BF16), preferred_element_type=F32))
        o_ref[...] = _pack_bf16_pair(ys[0], ys[1])

    @pl.when(pl.program_id(0) >= nb_ref[0])
    def _():
        o_ref[...] = jnp.zeros_like(o_ref)


def _experts(blk_a, blk_b, n_used, xs, wg, wu, wd, layer):
    n_rows = xs.shape[0]
    n_blocks = n_rows // MOE_BLOCK
    wa_map = lambda j, ea, eb, nb: (layer, ea[j], 0, 0)
    wb_map = lambda j, ea, eb, nb: (layer, eb[j], 0, 0)
    in_row_map = lambda j, ea, eb, nb: (jnp.maximum(jnp.minimum(j, nb[0] - 1), 0), 0)
    row_map = lambda j, ea, eb, nb: (j, 0)
    grid_spec = pltpu.PrefetchScalarGridSpec(
        num_scalar_prefetch=3,
        grid=(n_blocks,),
        in_specs=[
            pl.BlockSpec((MOE_BLOCK, D_MODEL // 2), in_row_map),
            pl.BlockSpec((1, 1, D_MODEL, D_EXPERT), wa_map),
            pl.BlockSpec((1, 1, D_MODEL, D_EXPERT), wa_map),
            pl.BlockSpec((1, 1, D_EXPERT, D_MODEL), wa_map),
            pl.BlockSpec((1, 1, D_MODEL, D_EXPERT), wb_map),
            pl.BlockSpec((1, 1, D_MODEL, D_EXPERT), wb_map),
            pl.BlockSpec((1, 1, D_EXPERT, D_MODEL), wb_map),
        ],
        out_specs=pl.BlockSpec((MOE_BLOCK, D_MODEL), row_map),
    )
    return pl.pallas_call(
        _experts_kernel,
        out_shape=jax.ShapeDtypeStruct((n_rows, D_MODEL), U32),
        grid_spec=grid_spec,
        compiler_params=_cparams(("arbitrary",)),
        name="experts",
    )(blk_a, blk_b, n_used, xs, wg, wu, wd, wg, wu, wd)


def _combine_ln_kernel(dest_ref, x_ref, wtok_ref, lg_ref, lb_ref, ys_ref, o_ref, obf_ref, ybuf, sem):
    i = pl.program_id(0)
    slot = i % 2

    def start_tile(tile, to_slot):
        base = tile * ROW_BM
        _start_rows(lambda r: pltpu.make_async_copy(ys_ref.at[pl.ds(dest_ref[base + r], 1), :],
                                                    ybuf.at[to_slot, pl.ds(r, 1), :], sem.at[to_slot]))

    @pl.when(i == 0)
    def _():
        start_tile(0, 0)

    @pl.when(i + 1 < pl.num_programs(0))
    def _():
        start_tile(i + 1, 1 - slot)

    pltpu.make_async_copy(ys_ref.at[pl.ds(0, ROW_BM), :], ybuf.at[slot], sem.at[slot]).wait()
    w = wtok_ref[...]
    y_a, y_b = _unpack_bf16_pair(ybuf[slot])
    y = w[:, 0:1] * y_a + w[:, 1:2] * y_b
    z = DEEPNORM_ALPHA * x_ref[...] + y
    out = _layernorm_rows(z, lg_ref[...], lb_ref[...])
    o_ref[...] = out
    obf_ref[...] = out.astype(BF16)


def _combine_ln(dest, x2d, wtok, ys, lg, lb):
    t = x2d.shape[0]
    const = lambda i, dest: (0, 0)
    grid_spec = pltpu.PrefetchScalarGridSpec(
        num_scalar_prefetch=1,
        grid=(t // ROW_BM,),
        in_specs=[
            pl.BlockSpec((ROW_BM, D_MODEL), lambda i, dest: (i, 0)),
            pl.BlockSpec((ROW_BM, META_LANES), lambda i, dest: (i, 0)),
            pl.BlockSpec((1, D_MODEL), const),
            pl.BlockSpec((1, D_MODEL), const),
            pl.BlockSpec(memory_space=pl.ANY),
        ],
        out_specs=(pl.BlockSpec((ROW_BM, D_MODEL), lambda i, dest: (i, 0)),
                   pl.BlockSpec((ROW_BM, D_MODEL), lambda i, dest: (i, 0))),
        scratch_shapes=[pltpu.VMEM((2, ROW_BM, D_MODEL), U32), pltpu.SemaphoreType.DMA((2,))],
    )
    return pl.pallas_call(
        _combine_ln_kernel,
        out_shape=(jax.ShapeDtypeStruct((t, D_MODEL), F32), jax.ShapeDtypeStruct((t, D_MODEL), BF16)),
        grid_spec=grid_spec,
        compiler_params=_cparams(("arbitrary",)),
        name="combine_ln",
    )(dest, x2d, wtok, lg, lb, ys)


def _moe(x2d, x_packed, rw_t, rb_t, wg, wu, wd, layer, lg, lb):
    t = x2d.shape[0]
    n_blocks = (t + N_CLASSES * (MOE_BLOCK - 1)) // MOE_BLOCK
    assert n_blocks <= META_LANES
    dest3, wtok, meta = _router(x2d, rw_t, rb_t)
    dest = dest3.reshape(t)
    xs = _dispatch(dest, x_packed, n_blocks * MOE_BLOCK)
    ys = _experts(meta[0, :n_blocks], meta[1, :n_blocks], meta[2, :1], xs, wg, wu, wd, layer)
    return _combine_ln(dest, x2d, wtok, ys, lg, lb)


def _rope_tables(seq):
    half = RET_DK // 2
    pos = jnp.arange(seq, dtype=F32)
    freqs = ROPE_BASE ** (-jnp.arange(half, dtype=F32) / half)
    ang = pos[:, None] * freqs[None, :]
    return jnp.cos(ang), jnp.sin(ang)


def kernel(x, w_in, ret_decay_logit, w_ret_o, na_rpb, w_na_o, w_out, ln_mix_g, ln_mix_b, router_w, router_bias,
           w_exp_gate, w_exp_up, w_exp_down, ln_ffn_g, ln_ffn_b):
    b, s, d = x.shape
    depth = w_in.shape[0]
    t = b * s
    rows = s // GRID_W
    cos, sin = _rope_tables(s)
    rw_t = router_w.astype(F32).T.reshape(N_GROUPS, EXPERTS_PER_GROUP, d).transpose(1, 0, 2).reshape(N_EXPERTS, d)
    rb_t = router_bias.astype(F32).reshape(N_GROUPS, EXPERTS_PER_GROUP).T.reshape(N_EXPERTS, 1)
    x2d = x.reshape(t, d)
    x_bf = x2d.astype(BF16)
    w_in = w_in.astype(F32)
    for l in range(depth):
        qk = _inproj(x_bf, w_in, l, cos, sin, s, "rotary").reshape(b, s, -1)
        vn = _inproj(x_bf, w_in, l, cos, sin, s, "scale").reshape(b, s, -1)
        gr = _inproj(x_bf, w_in, l, cos, sin, s, "silu").reshape(b, s, -1)
        gates = _inproj(x_bf, w_in, l, cos, sin, s, "sigmoid")
        ret = _retention(qk, vn, gr, ret_decay_logit[l].astype(F32))
        na = _na(vn, na_rpb[l].astype(F32).reshape(-1))
        x2d, x_packed = _mixout(ret.reshape(t, -1), na.reshape(t, -1), gates, x2d,
                                w_ret_o[l].astype(BF16), w_na_o[l].astype(BF16), w_out[l].astype(BF16),
                                ln_mix_g[l].reshape(1, d).astype(F32), ln_mix_b[l].reshape(1, d).astype(F32))
        x2d, x_bf = _moe(x2d, x_packed, rw_t, rb_t, w_exp_gate, w_exp_up, w_exp_down, l,
                         ln_ffn_g[l].reshape(1, d).astype(F32), ln_ffn_b[l].reshape(1, d).astype(F32))
    return x2d.reshape(b, s, d)
```

```python
import functools

import numpy as np
import jax
import jax.numpy as jnp
from jax import lax
from jax.experimental import pallas as pl
from jax.experimental.pallas import tpu as pltpu

F32 = jnp.float32
BF16 = jnp.bfloat16

D_MODEL = 1024
GRID_W = 64
RET_HEADS = 4
RET_DK = 256
RET_DV = 512
ROPE_BASE = 10000.0
NA_HEADS = 16
NA_HD = 64
NA_KH = 8
NA_KW = 16
N_EXPERTS = 32
N_GROUPS = 8
EXPERTS_PER_GROUP = 4
D_EXPERT = 512
LN_EPS = 1e-5
GN_EPS = 1e-5
MODEL_DEPTH = 4
DEEPNORM_ALPHA = (2 * MODEL_DEPTH) ** 0.25

D_IN = 11264

RET_CHUNK = 256
NA_QROWS = 4
NA_KROWS = 12
NA_NEG = -1e30
MOE_BLOCK = 256
N_PAIRS = 6
PAIR_SLOT_A = (0, 2, 2, 3, 3, 3)
PAIR_SLOT_B = (1, 1, 0, 0, 1, 2)
N_CLASSES = N_GROUPS * N_PAIRS
VMEM_LIMIT = 56 * 1024 * 1024


def _cparams(sem):
    return pltpu.CompilerParams(dimension_semantics=sem, vmem_limit_bytes=VMEM_LIMIT)


U32 = jnp.uint32
_HIGH_HALF = 0xFFFF0000


def _pack_bf16_pair(hi, lo):
    hi_bits = lax.bitcast_convert_type(hi.astype(jnp.bfloat16).astype(F32), U32)
    lo_bits = lax.bitcast_convert_type(lo.astype(jnp.bfloat16).astype(F32), U32)
    return (hi_bits & jnp.uint32(_HIGH_HALF)) | (lo_bits >> 16)


def _unpack_bf16_pair(packed):
    hi = lax.bitcast_convert_type(packed & jnp.uint32(_HIGH_HALF), F32)
    lo = lax.bitcast_convert_type(packed << 16, F32)
    return hi, lo


IN_BM = 1024
IN_CHUNK = 256
LOG2E = 1.4426950408889634

IN_GROUPS = {
    "rotary": (2048, (0,)),
    "scale": (1024, (2, 3, 6, 7, 8)),
    "silu": (2048, (2,)),
    "sigmoid": (1024, (9, 10)),
}
VN_Q = 2048
VN_K = 3072
VN_V = 4096


def _inproj_kernel(x_ref, w_ref, cos_ref, sin_ref, o_ref, wbf_ref, *, kind, bn):
    j = pl.program_id(0)

    @pl.when(pl.program_id(1) == 0)
    def _():
        wbf_ref[...] = w_ref[0].astype(BF16)

    x = x_ref[...]
    half = RET_DK // 2
    for c in range(bn // IN_CHUNK):
        lo = c * IN_CHUNK
        acc = jnp.dot(x, wbf_ref[:, lo:lo + IN_CHUNK], preferred_element_type=F32)
        if kind == "rotary":
            scale = 1.0 if lo < RET_HEADS * RET_DK else RET_DK ** -0.5
            cos = cos_ref[...] * scale
            sin = sin_ref[...] * scale
            x1 = acc[:, :half]
            x2 = acc[:, half:]
            o_ref[:, lo:lo + half] = (x1 * cos - x2 * sin).astype(o_ref.dtype)
            o_ref[:, lo + half:lo + IN_CHUNK] = (x1 * sin + x2 * cos).astype(o_ref.dtype)
        elif kind == "scale":
            scale = jnp.where(j == 2, NA_HD ** -0.5 * LOG2E, 1.0)
            o_ref[:, lo:lo + IN_CHUNK] = (acc * scale).astype(o_ref.dtype)
        elif kind == "silu":
            o_ref[:, lo:lo + IN_CHUNK] = (acc * jax.nn.sigmoid(acc)).astype(o_ref.dtype)
        else:
            o_ref[:, lo:lo + IN_CHUNK] = jax.nn.sigmoid(acc).astype(o_ref.dtype)


def _inproj(x_bf, w_in, layer, cos, sin, seq, kind):
    t = x_bf.shape[0]
    pos_blocks = seq // IN_BM
    bn, blocks = IN_GROUPS[kind]

    def wcol(j):
        col = blocks[0]
        for k in range(1, len(blocks)):
            col = jnp.where(j == k, blocks[k], col)
        return col

    pos_map = (lambda j, i: (i % pos_blocks, 0)) if kind == "rotary" else (lambda j, i: (0, 0))
    return pl.pallas_call(
        functools.partial(_inproj_kernel, kind=kind, bn=bn),
        out_shape=jax.ShapeDtypeStruct((t, len(blocks) * bn), BF16),
        grid=(len(blocks), t // IN_BM),
        in_specs=[
            pl.BlockSpec((IN_BM, D_MODEL), lambda j, i: (i, 0)),
            pl.BlockSpec((1, D_MODEL, bn), lambda j, i: (layer, 0, wcol(j))),
            pl.BlockSpec((IN_BM, RET_DK // 2), pos_map),
            pl.BlockSpec((IN_BM, RET_DK // 2), pos_map),
        ],
        out_specs=pl.BlockSpec((IN_BM, bn), lambda j, i: (i, j)),
        scratch_shapes=[pltpu.VMEM((D_MODEL, bn), BF16)],
        compiler_params=_cparams(("arbitrary", "arbitrary")),
        name="inproj_" + kind,
    )(x_bf, w_in, cos, sin)


def _log_sigmoid(x):
    return jnp.minimum(x, 0.0) - jnp.log1p(jnp.exp(-jnp.abs(x)))


def _retention_kernel(dl_ref, q_ref, k_ref, v_ref, g_ref, o_ref,
                      sf_ref, st_ref, dm_ref, qdf_ref, qdb_ref, kdf_ref, kdb_ref, *, nc):
    c_len = RET_CHUNK
    h = pl.program_id(1)
    lgf = _log_sigmoid(jnp.full((c_len, RET_DK), dl_ref[0, h], F32))
    lgb = _log_sigmoid(jnp.full((c_len, RET_DK), dl_ref[1, h], F32))
    ri = lax.broadcasted_iota(jnp.int32, (c_len, RET_DK), 0).astype(F32)
    qdf_ref[...] = jnp.exp(lgf * (ri + 1.0))
    qdb_ref[...] = jnp.exp(lgb * (c_len - ri))
    kdf_ref[...] = jnp.exp(lgf * (c_len - 1.0 - ri))
    kdb_ref[...] = jnp.exp(lgb * ri)
    rr = lax.broadcasted_iota(jnp.int32, (c_len, c_len), 0).astype(F32)
    cc = lax.broadcasted_iota(jnp.int32, (c_len, c_len), 1).astype(F32)
    diff = rr - cc
    lgf_cc = _log_sigmoid(jnp.full((c_len, c_len), dl_ref[0, h], F32))
    lgb_cc = _log_sigmoid(jnp.full((c_len, c_len), dl_ref[1, h], F32))
    dm_ref[...] = jnp.where(diff >= 0.0, jnp.exp(lgf_cc * jnp.maximum(diff, 0.0)),
                            jnp.exp(lgb_cc * jnp.maximum(-diff, 0.0)))
    chunk_f = jnp.exp(lgf[:1, :1] * float(c_len))
    chunk_b = jnp.exp(lgb[:1, :1] * float(c_len))

    tn_dims = (((0,), (0,)), ((), ()))
    nt_dims = (((1,), (1,)), ((), ()))

    st_ref[...] = jnp.zeros_like(st_ref)

    def fwd_body(c, carry):
        off = pl.multiple_of(c * c_len, c_len)
        sf_ref[c] = st_ref[...].astype(BF16)
        kc = k_ref[0, pl.ds(off, c_len), :].astype(F32)
        vc = v_ref[0, pl.ds(off, c_len), :]
        kd = (kc * kdf_ref[...]).astype(BF16)
        upd = lax.dot_general(kd, vc, tn_dims, preferred_element_type=F32)
        st_ref[...] = st_ref[...] * chunk_f + upd
        return carry

    lax.fori_loop(0, nc, fwd_body, 0, unroll=8)

    st_ref[...] = jnp.zeros_like(st_ref)

    def bwd_body(i, carry):
        c = nc - 1 - i
        off = pl.multiple_of(c * c_len, c_len)
        qb = q_ref[0, pl.ds(off, c_len), :]
        kb = k_ref[0, pl.ds(off, c_len), :]
        vc = v_ref[0, pl.ds(off, c_len), :]
        qc = qb.astype(F32)
        kc = kb.astype(F32)
        s = lax.dot_general(qb, kb, nt_dims, preferred_element_type=F32)
        p = (s * dm_ref[...]).astype(BF16)
        out = jnp.dot(p, vc, preferred_element_type=F32)
        out = out + jnp.dot((qc * qdf_ref[...]).astype(BF16), sf_ref[c], preferred_element_type=F32)
        out = out + jnp.dot((qc * qdb_ref[...]).astype(BF16), st_ref[...].astype(BF16),
                            preferred_element_type=F32)
        mu = jnp.mean(out, axis=-1, keepdims=True)
        cen = out - mu
        var = jnp.mean(cen * cen, axis=-1, keepdims=True)
        y = cen * lax.rsqrt(var + GN_EPS)
        gate = g_ref[0, pl.ds(off, c_len), :].astype(F32)
        o_ref[0, pl.ds(off, c_len), :] = (gate * y).astype(o_ref.dtype)
        kd = (kc * kdb_ref[...]).astype(BF16)
        upd = lax.dot_general(kd, vc, tn_dims, preferred_element_type=F32)
        st_ref[...] = st_ref[...] * chunk_b + upd
        return carry

    lax.fori_loop(0, nc, bwd_body, 0, unroll=8)


def _retention(qk3, vn3, gr3, decay_logit):
    b, s, _ = qk3.shape
    nc = s // RET_CHUNK
    return pl.pallas_call(
        functools.partial(_retention_kernel, nc=nc),
        out_shape=jax.ShapeDtypeStruct((b, s, RET_HEADS * RET_DV), BF16),
        grid=(b, RET_HEADS),
        in_specs=[
            pl.BlockSpec(memory_space=pltpu.SMEM),
            pl.BlockSpec((1, s, RET_DK), lambda bi, h: (bi, 0, h)),
            pl.BlockSpec((1, s, RET_DK), lambda bi, h: (bi, 0, RET_HEADS + h)),
            pl.BlockSpec((1, s, RET_DV), lambda bi, h: (bi, 0, h)),
            pl.BlockSpec((1, s, RET_DV), lambda bi, h: (bi, 0, h)),
        ],
        out_specs=pl.BlockSpec((1, s, RET_DV), lambda bi, h: (bi, 0, h)),
        scratch_shapes=[
            pltpu.VMEM((nc, RET_DK, RET_DV), BF16),
            pltpu.VMEM((RET_DK, RET_DV), F32),
            pltpu.VMEM((RET_CHUNK, RET_CHUNK), F32),
            pltpu.VMEM((RET_CHUNK, RET_DK), F32),
            pltpu.VMEM((RET_CHUNK, RET_DK), F32),
            pltpu.VMEM((RET_CHUNK, RET_DK), F32),
            pltpu.VMEM((RET_CHUNK, RET_DK), F32),
        ],
        compiler_params=_cparams(("arbitrary", "arbitrary")),
        name="retention",
    )(decay_logit, qk3, qk3, vn3, gr3)


NA_ROW_OFFS = 2 * NA_KH - 1
NA_COL_OFFS = 2 * NA_KW - 1
NA_PATTERNS = 3


def _na_row_offsets(rows):
    n_tiles = rows // NA_QROWS
    offs = np.full((NA_PATTERNS, NA_QROWS, NA_KROWS), NA_ROW_OFFS, np.int32)
    for p, t in enumerate((0, 1, n_tiles - 1)):
        kstart = int(np.clip(NA_QROWS * t - NA_KH // 2, 0, rows - NA_KROWS))
        for rr in range(NA_QROWS):
            r = NA_QROWS * t + rr
            rs = int(np.clip(r - NA_KH // 2, 0, rows - NA_KH))
            for i in range(NA_KROWS):
                krow = kstart + i
                if rs <= krow < rs + NA_KH:
                    offs[p, rr, i] = krow - r + (NA_KH - 1)
    return offs


def _na_build_bias(rpb_ref, e_ref, bias_ref, head0, rows):
    wide = 2 * GRID_W
    lane = lax.broadcasted_iota(jnp.int32, (GRID_W, wide), 1)
    col = lax.broadcasted_iota(jnp.int32, (GRID_W, wide), 0)
    kcol = jnp.where(lane < GRID_W, lane, lane - GRID_W)
    cstart = jnp.clip(col - NA_KW // 2, 0, GRID_W - NA_KW)
    col_ok = (kcol >= cstart) & (kcol < cstart + NA_KW)
    coff = kcol - col + (NA_KW - 1)
    neg = jnp.full((GRID_W, wide), NA_NEG, F32)
    left = lane < GRID_W
    offs = _na_row_offsets(rows)
    for hh in range(2):
        base = (head0 + hh) * (NA_ROW_OFFS * NA_COL_OFFS)
        for ro in range(NA_ROW_OFFS):
            acc = neg
            for d in range(NA_COL_OFFS):
                acc = jnp.where(coff == d, rpb_ref[base + ro * NA_COL_OFFS + d] * LOG2E, acc)
            e_ref[hh, ro] = jnp.where(col_ok, acc, neg)
        e_ref[hh, NA_ROW_OFFS] = neg
        for p in range(NA_PATTERNS):
            for rr in range(NA_QROWS):
                for ip in range(NA_KROWS // 2):
                    tile = jnp.where(left, e_ref[hh, int(offs[p, rr, 2 * ip])], e_ref[hh, int(offs[p, rr, 2 * ip + 1])])
                    bias_ref[hh, p, rr * GRID_W:(rr + 1) * GRID_W, ip * wide:(ip + 1) * wide] = tile.astype(BF16)


def _na_kernel(rpb_ref, q_ref, k_ref, v_ref, o_ref, bias_ref, e_ref, *, n_tiles, rows):
    nq = NA_QROWS * GRID_W
    nk = NA_KROWS * GRID_W
    nt_dims = (((1,), (1,)), ((), ()))
    first = lax.broadcasted_iota(jnp.int32, (nq, 2 * NA_HD), 1) < NA_HD
    kfirst = lax.broadcasted_iota(jnp.int32, (nk, 2 * NA_HD), 1) < NA_HD

    @pl.when(pl.program_id(1) == 0)
    def _():
        _na_build_bias(rpb_ref, e_ref, bias_ref, 2 * pl.program_id(0), rows)

    def body(t, carry):
        qoff = pl.multiple_of(t * nq, nq)
        krow0 = jnp.clip(NA_QROWS * t - NA_KH // 2, 0, rows - NA_KROWS)
        koff = pl.multiple_of(krow0 * GRID_W, nq)
        pat = jnp.where(t == 0, 0, jnp.where(t == n_tiles - 1, 2, 1))
        q2 = q_ref[0, pl.ds(qoff, nq), :]
        k2 = k_ref[0, pl.ds(koff, nk), :]
        v2 = v_ref[0, pl.ds(koff, nk), :]
        outs = []
        for hh in range(2):
            qm = jnp.where(first if hh == 0 else jnp.logical_not(first), q2, jnp.zeros_like(q2))
            s = lax.dot_general(qm, k2, nt_dims, preferred_element_type=F32).astype(BF16)
            s = s + bias_ref[hh, pat]
            m = jnp.max(s, axis=-1, keepdims=True)
            e = jnp.exp2(s - m)
            vv = jnp.where(kfirst if hh == 0 else jnp.logical_not(kfirst), v2, jnp.ones_like(v2))
            outs.append(jnp.dot(e, vv, preferred_element_type=F32))
        num = jnp.where(first, outs[0], outs[1])
        den = pltpu.roll(jnp.where(first, outs[1], outs[0]), NA_HD, axis=1)
        o_ref[0, pl.ds(qoff, nq), :] = (num / den).astype(o_ref.dtype)
        return carry

    lax.fori_loop(0, n_tiles, body, 0, unroll=4)


def _na(vn3, rpb_flat):
    b, s, _ = vn3.shape
    rows = s // GRID_W
    n_tiles = rows // NA_QROWS
    w2 = 2 * NA_HD
    nq = NA_QROWS * GRID_W
    nk = NA_KROWS * GRID_W
    return pl.pallas_call(
        functools.partial(_na_kernel, n_tiles=n_tiles, rows=rows),
        out_shape=jax.ShapeDtypeStruct((b, s, NA_HEADS * NA_HD), BF16),
        grid=(NA_HEADS // 2, b),
        in_specs=[
            pl.BlockSpec(memory_space=pltpu.SMEM),
            pl.BlockSpec((1, s, w2), lambda hp, bi: (bi, 0, VN_Q // w2 + hp)),
            pl.BlockSpec((1, s, w2), lambda hp, bi: (bi, 0, VN_K // w2 + hp)),
            pl.BlockSpec((1, s, w2), lambda hp, bi: (bi, 0, VN_V // w2 + hp)),
        ],
        out_specs=pl.BlockSpec((1, s, w2), lambda hp, bi: (bi, 0, hp)),
        scratch_shapes=[
            pltpu.VMEM((2, NA_PATTERNS, nq, nk), BF16),
            pltpu.VMEM((2, NA_ROW_OFFS + 1, GRID_W, 2 * GRID_W), F32),
        ],
        compiler_params=_cparams(("arbitrary", "arbitrary")),
        name="natten",
    )(rpb_flat, vn3, vn3, vn3)


MIX_BM = 512
MIX_SUB = 256


def _layernorm_rows(z, g, b):
    mu = jnp.mean(z, axis=-1, keepdims=True)
    cen = z - mu
    var = jnp.mean(cen * cen, axis=-1, keepdims=True)
    return cen * lax.rsqrt(var + LN_EPS) * g + b


def _mixout_kernel(ret_ref, na_ref, g1_ref, g2_ref, x_ref, wr_ref, wn_ref, wo_ref, lg_ref, lb_ref, o_ref, op_ref):
    for r in range(0, MIX_BM, MIX_SUB):
        rows = slice(r, r + MIX_SUB)
        y_ret = jnp.dot(ret_ref[rows, :], wr_ref[...], preferred_element_type=F32)
        y_na = jnp.dot(na_ref[rows, :], wn_ref[...], preferred_element_type=F32)
        merged = g1_ref[rows, :].astype(F32) * y_ret + g2_ref[rows, :].astype(F32) * y_na
        mix = jnp.dot(merged.astype(BF16), wo_ref[...], preferred_element_type=F32)
        z = DEEPNORM_ALPHA * x_ref[rows, :] + mix
        out = _layernorm_rows(z, lg_ref[...], lb_ref[...])
        o_ref[rows, :] = out
        op_ref[rows, :] = _pack_bf16_pair(out[:, :D_MODEL // 2], out[:, D_MODEL // 2:])


def _mixout(ret2d, na2d, gates2d, x2d, wr, wn, wo, lg, lb):
    t = x2d.shape[0]
    const = lambda i: (0, 0)
    return pl.pallas_call(
        _mixout_kernel,
        out_shape=(jax.ShapeDtypeStruct((t, D_MODEL), F32), jax.ShapeDtypeStruct((t, D_MODEL // 2), U32)),
        grid=(t // MIX_BM,),
        in_specs=[
            pl.BlockSpec((MIX_BM, RET_HEADS * RET_DV), lambda i: (i, 0)),
            pl.BlockSpec((MIX_BM, NA_HEADS * NA_HD), lambda i: (i, 0)),
            pl.BlockSpec((MIX_BM, D_MODEL), lambda i: (i, 0)),
            pl.BlockSpec((MIX_BM, D_MODEL), lambda i: (i, 1)),
            pl.BlockSpec((MIX_BM, D_MODEL), lambda i: (i, 0)),
            pl.BlockSpec((RET_HEADS * RET_DV, D_MODEL), const),
            pl.BlockSpec((NA_HEADS * NA_HD, D_MODEL), const),
            pl.BlockSpec((D_MODEL, D_MODEL), const),
            pl.BlockSpec((1, D_MODEL), const),
            pl.BlockSpec((1, D_MODEL), const),
        ],
        out_specs=(pl.BlockSpec((MIX_BM, D_MODEL), lambda i: (i, 0)),
                   pl.BlockSpec((MIX_BM, D_MODEL // 2), lambda i: (i, 0))),
        compiler_params=_cparams(("arbitrary",)),
        name="mixout",
    )(ret2d, na2d, gates2d, gates2d, x2d, wr, wn, wo, lg, lb)


ROUTE_BM = 512


def _route_tile(x, rw, rb):
    nt_dims = (((1,), (1,)), ((), ()))
    x_hi = x.astype(BF16)
    x_lo = (x - x_hi.astype(F32)).astype(BF16)
    rw_hi = rw.astype(BF16)
    rw_lo = (rw - rw_hi.astype(F32)).astype(BF16)
    both = lax.dot_general(jnp.concatenate([rw_hi, rw_lo], axis=0), x_hi, nt_dims, preferred_element_type=F32)
    logits = (both[:N_EXPERTS] + both[N_EXPERTS:]
              + lax.dot_general(rw_hi, x_lo, nt_dims, preferred_element_type=F32))
    scores = jax.nn.sigmoid(logits)
    sel = scores + rb
    p = [scores[m * N_GROUPS:(m + 1) * N_GROUPS] for m in range(EXPERTS_PER_GROUP)]
    s = [sel[m * N_GROUPS:(m + 1) * N_GROUPS] for m in range(EXPERTS_PER_GROUP)]
    one = jnp.ones_like(s[0])
    zero = jnp.zeros_like(s[0])
    chosen = []
    for m in range(EXPERTS_PER_GROUP):
        rank = zero
        for j in range(EXPERTS_PER_GROUP):
            if j == m:
                continue
            beats = (s[j] >= s[m]) if j < m else (s[j] > s[m])
            rank = rank + jnp.where(beats, one, zero)
        chosen.append(rank < 2.0)
    group_score = zero
    for m in range(EXPERTS_PER_GROUP):
        group_score = group_score + jnp.where(chosen[m], s[m], zero)
    gid = lax.broadcasted_iota(jnp.int32, group_score.shape, 0)
    gmax = jnp.max(group_score, axis=0, keepdims=True)
    gbest = jnp.min(jnp.where(group_score == gmax, gid, N_GROUPS), axis=0, keepdims=True)
    in_best = gid == gbest
    picked = [chosen[m] & in_best for m in range(EXPERTS_PER_GROUP)]
    before = zero
    wa = zero
    wb = zero
    ma = zero
    mb = zero
    for m in range(EXPERTS_PER_GROUP):
        is_a = picked[m] & (before == 0.0)
        is_b = picked[m] & (before == 1.0)
        wa = wa + jnp.where(is_a, p[m], zero)
        wb = wb + jnp.where(is_b, p[m], zero)
        ma = ma + jnp.where(is_a, float(m), 0.0)
        mb = mb + jnp.where(is_b, float(m), 0.0)
        before = before + jnp.where(chosen[m], one, zero)
    wa = jnp.sum(wa, axis=0, keepdims=True)
    wb = jnp.sum(wb, axis=0, keepdims=True)
    ma = jnp.sum(ma, axis=0, keepdims=True)
    mb = jnp.sum(mb, axis=0, keepdims=True)
    denom = wa + wb
    w_lo = wa / denom
    w_hi = wb / denom
    pair = jnp.where(ma == 0.0, jnp.where(mb == 1.0, 0.0, mb),
                     jnp.where(ma == 1.0, jnp.where(mb == 2.0, 1.0, 4.0), 5.0))
    keep = pair == 0.0
    return (gbest * N_PAIRS + pair.astype(jnp.int32),
            jnp.where(keep, w_lo, w_hi), jnp.where(keep, w_hi, w_lo))


CLASS_ROWS = 64
META_LANES = 128


def _router_kernel(x_ref, rw_ref, rb_ref, tri_ref, dest_ref, wtok_ref, meta_ref, cls_s, rank_s, cnt_s):
    phase = pl.program_id(0)
    i = pl.program_id(1)
    bm = ROUTE_BM
    reps = bm // META_LANES
    cid = lax.broadcasted_iota(jnp.int32, (CLASS_ROWS, bm), 0)

    @pl.when((phase == 0) & (i == 0))
    def _():
        cnt_s[...] = jnp.zeros_like(cnt_s)

    @pl.when(phase == 0)
    def _():
        cls, w_lo, w_hi = _route_tile(x_ref[...], rw_ref[...], rb_ref[...])
        cls_s[i] = cls
        w_rows = jnp.concatenate([w_lo, w_hi, jnp.zeros((META_LANES - 2, bm), F32)], axis=0)
        wtok_ref[...] = w_rows.T
        onehot = (cid == cls).astype(BF16)
        before = jnp.dot(onehot, tri_ref[...], preferred_element_type=F32)
        carry = jnp.concatenate([cnt_s[...]] * reps, axis=1)
        rank = jnp.sum(jnp.where(cid == cls, before + carry, 0.0), axis=0, keepdims=True)
        rank_s[i] = rank.astype(jnp.int32)
        cnt_s[...] = cnt_s[...] + jnp.dot(onehot, jnp.ones((bm, META_LANES), BF16), preferred_element_type=F32)

    @pl.when(phase == 1)
    def _():
        cnt = cnt_s[...]
        nblk = jnp.floor((cnt + (MOE_BLOCK - 1.0)) * (1.0 / MOE_BLOCK))
        rr = lax.broadcasted_iota(jnp.int32, (CLASS_ROWS, CLASS_ROWS), 0)
        cc = lax.broadcasted_iota(jnp.int32, (CLASS_ROWS, CLASS_ROWS), 1)
        lower = (cc < rr).astype(BF16)
        start_blk = jnp.dot(lower, nblk.astype(BF16), preferred_element_type=F32)
        cls = cls_s[i]
        start_t = jnp.concatenate([start_blk] * reps, axis=1)
        start = jnp.sum(jnp.where(cid == cls, start_t, 0.0), axis=0, keepdims=True)
        dest_ref[0] = (start * float(MOE_BLOCK)).astype(jnp.int32) + rank_s[i]

        @pl.when(i == 0)
        def _():
            end_blk = start_blk + nblk
            n_used = jnp.max(end_blk, axis=0, keepdims=True)
            blk = lax.broadcasted_iota(jnp.int32, (CLASS_ROWS, META_LANES), 1).astype(F32)
            blk = jnp.minimum(blk, n_used - 1.0)
            bcls = jnp.sum(jnp.where(end_blk <= blk, 1.0, 0.0), axis=0, keepdims=True)
            grp = jnp.zeros_like(bcls)
            for g in range(1, N_GROUPS):
                grp = grp + jnp.where(bcls >= float(g * N_PAIRS), 1.0, 0.0)
            pair = bcls - grp * float(N_PAIRS)
            slot_a = jnp.zeros_like(pair)
            slot_b = jnp.zeros_like(pair)
            for k in range(N_PAIRS):
                slot_a = jnp.where(pair == float(k), float(PAIR_SLOT_A[k]), slot_a)
                slot_b = jnp.where(pair == float(k), float(PAIR_SLOT_B[k]), slot_b)
            rows = [grp * float(EXPERTS_PER_GROUP) + slot_a, grp * float(EXPERTS_PER_GROUP) + slot_b, n_used]
            rows = rows + [jnp.zeros_like(bcls)] * (8 - len(rows))
            meta_ref[...] = jnp.concatenate(rows, axis=0).astype(jnp.int32)


def _router(x2d, rw_t, rb_t):
    t = x2d.shape[0]
    nt = t // ROUTE_BM
    tri = jnp.asarray(np.triu(np.ones((ROUTE_BM, ROUTE_BM), np.float32), 1), BF16)
    hold = lambda ph, i: ((1 - ph) * i + ph * (nt - 1), 0)
    const = lambda ph, i: (0, 0)
    return pl.pallas_call(
        _router_kernel,
        out_shape=(jax.ShapeDtypeStruct((nt, 1, ROUTE_BM), jnp.int32),
                   jax.ShapeDtypeStruct((t, META_LANES), F32),
                   jax.ShapeDtypeStruct((8, META_LANES), jnp.int32)),
        grid=(2, nt),
        in_specs=[
            pl.BlockSpec((ROUTE_BM, D_MODEL), hold),
            pl.BlockSpec((N_EXPERTS, D_MODEL), const),
            pl.BlockSpec((N_EXPERTS, 1), const),
            pl.BlockSpec((ROUTE_BM, ROUTE_BM), const),
        ],
        out_specs=(pl.BlockSpec((1, 1, ROUTE_BM), lambda ph, i: (ph * i, 0, 0)),
                   pl.BlockSpec((ROUTE_BM, META_LANES), hold),
                   pl.BlockSpec((8, META_LANES), const)),
        scratch_shapes=[
            pltpu.VMEM((nt, 1, ROUTE_BM), jnp.int32),
            pltpu.VMEM((nt, 1, ROUTE_BM), jnp.int32),
            pltpu.VMEM((CLASS_ROWS, META_LANES), F32),
        ],
        compiler_params=_cparams(("arbitrary", "arbitrary")),
        name="router",
    )(x2d, rw_t, rb_t, tri)


ROW_BM = 512


def _start_rows(make_copy):
    for r in range(ROW_BM):
        make_copy(r).start(priority=r % 2)


def _dispatch_kernel(dest_ref, x_ref, zeros_ref, xs_ref, sem):
    del zeros_ref
    base = pl.program_id(0) * ROW_BM
    _start_rows(lambda r: pltpu.make_async_copy(x_ref.at[pl.ds(r, 1), :],
                                                xs_ref.at[pl.ds(dest_ref[base + r], 1), :], sem))
    pltpu.make_async_copy(x_ref, xs_ref.at[pl.ds(0, ROW_BM), :], sem).wait()


def _dispatch(dest, x_rows, n_rows):
    t, width = x_rows.shape
    grid_spec = pltpu.PrefetchScalarGridSpec(
        num_scalar_prefetch=1,
        grid=(t // ROW_BM,),
        in_specs=[
            pl.BlockSpec((ROW_BM, width), lambda i, dest: (i, 0)),
            pl.BlockSpec(memory_space=pl.ANY),
        ],
        out_specs=pl.BlockSpec(memory_space=pl.ANY),
        scratch_shapes=[pltpu.SemaphoreType.DMA],
    )
    return pl.pallas_call(
        _dispatch_kernel,
        out_shape=jax.ShapeDtypeStruct((n_rows, width), x_rows.dtype),
        grid_spec=grid_spec,
        input_output_aliases={2: 0},
        compiler_params=_cparams(("arbitrary",)),
        name="dispatch",
    )(dest, x_rows, jnp.zeros((n_rows, width), x_rows.dtype))


def _experts_kernel(ea_ref, eb_ref, nb_ref, x_ref, wga_ref, wua_ref, wda_ref, wgb_ref, wub_ref, wdb_ref, o_ref):
    @pl.when(pl.program_id(0) < nb_ref[0])
    def _():
        x = jnp.concatenate(_unpack_bf16_pair(x_ref[...]), axis=1).astype(BF16)
        ys = []
        for wg_ref, wu_ref, wd_ref in ((wga_ref, wua_ref, wda_ref), (wgb_ref, wub_ref, wdb_ref)):
            hg = jnp.dot(x, wg_ref[0, 0].astype(BF16), preferred_element_type=F32)
            hu = jnp.dot(x, wu_ref[0, 0].astype(BF16), preferred_element_type=F32)
            act = (hg * jax.nn.sigmoid(hg) * hu).astype(BF16)
            ys.append(jnp.dot(act, wd_ref[0, 0].astype(BF16), preferred_element_type=F32))
        o_ref[...] = _pack_bf16_pair(ys[0], ys[1])

    @pl.when(pl.program_id(0) >= nb_ref[0])
    def _():
        o_ref[...] = jnp.zeros_like(o_ref)


def _experts(blk_a, blk_b, n_used, xs, wg, wu, wd, layer):
    n_rows = xs.shape[0]
    n_blocks = n_rows // MOE_BLOCK
    wa_map = lambda j, ea, eb, nb: (layer, ea[j], 0, 0)
    wb_map = lambda j, ea, eb, nb: (layer, eb[j], 0, 0)
    in_row_map = lambda j, ea, eb, nb: (jnp.maximum(jnp.minimum(j, nb[0] - 1), 0), 0)
    row_map = lambda j, ea, eb, nb: (j, 0)
    grid_spec = pltpu.PrefetchScalarGridSpec(
        num_scalar_prefetch=3,
        grid=(n_blocks,),
        in_specs=[
            pl.BlockSpec((MOE_BLOCK, D_MODEL // 2), in_row_map),
            pl.BlockSpec((1, 1, D_MODEL, D_EXPERT), wa_map),
            pl.BlockSpec((1, 1, D_MODEL, D_EXPERT), wa_map),
            pl.BlockSpec((1, 1, D_EXPERT, D_MODEL), wa_map),
            pl.BlockSpec((1, 1, D_MODEL, D_EXPERT), wb_map),
            pl.BlockSpec((1, 1, D_MODEL, D_EXPERT), wb_map),
            pl.BlockSpec((1, 1, D_EXPERT, D_MODEL), wb_map),
        ],
        out_specs=pl.BlockSpec((MOE_BLOCK, D_MODEL), row_map),
    )
    return pl.pallas_call(
        _experts_kernel,
        out_shape=jax.ShapeDtypeStruct((n_rows, D_MODEL), U32),
        grid_spec=grid_spec,
        compiler_params=_cparams(("arbitrary",)),
        name="experts",
    )(blk_a, blk_b, n_used, xs, wg, wu, wd, wg, wu, wd)


def _combine_ln_kernel(dest_ref, x_ref, wtok_ref, lg_ref, lb_ref, ys_ref, o_ref, obf_ref, ybuf, sem):
    i = pl.program_id(0)
    slot = i % 2

    def start_tile(tile, to_slot):
        base = tile * ROW_BM
        _start_rows(lambda r: pltpu.make_async_copy(ys_ref.at[pl.ds(dest_ref[base + r], 1), :],
                                                    ybuf.at[to_slot, pl.ds(r, 1), :], sem.at[to_slot]))

    @pl.when(i == 0)
    def _():
        start_tile(0, 0)

    @pl.when(i + 1 < pl.num_programs(0))
    def _():
        start_tile(i + 1, 1 - slot)

    pltpu.make_async_copy(ys_ref.at[pl.ds(0, ROW_BM), :], ybuf.at[slot], sem.at[slot]).wait()
    w = wtok_ref[...]
    y_a, y_b = _unpack_bf16_pair(ybuf[slot])
    y = w[:, 0:1] * y_a + w[:, 1:2] * y_b
    z = DEEPNORM_ALPHA * x_ref[...] + y
    out = _layernorm_rows(z, lg_ref[...], lb_ref[...])
    o_ref[...] = out
    obf_ref[...] = out.astype(BF16)


def _combine_ln(dest, x2d, wtok, ys, lg, lb):
    t = x2d.shape[0]
    const = lambda i, dest: (0, 0)
    grid_spec = pltpu.PrefetchScalarGridSpec(
        num_scalar_prefetch=1,
        grid=(t // ROW_BM,),
        in_specs=[
            pl.BlockSpec((ROW_BM, D_MODEL), lambda i, dest: (i, 0)),
            pl.BlockSpec((ROW_BM, META_LANES), lambda i, dest: (i, 0)),
            pl.BlockSpec((1, D_MODEL), const),
            pl.BlockSpec((1, D_MODEL), const),
            pl.BlockSpec(memory_space=pl.ANY),
        ],
        out_specs=(pl.BlockSpec((ROW_BM, D_MODEL), lambda i, dest: (i, 0)),
                   pl.BlockSpec((ROW_BM, D_MODEL), lambda i, dest: (i, 0))),
        scratch_shapes=[pltpu.VMEM((2, ROW_BM, D_MODEL), U32), pltpu.SemaphoreType.DMA((2,))],
    )
    return pl.pallas_call(
        _combine_ln_kernel,
        out_shape=(jax.ShapeDtypeStruct((t, D_MODEL), F32), jax.ShapeDtypeStruct((t, D_MODEL), BF16)),
        grid_spec=grid_spec,
        compiler_params=_cparams(("arbitrary",)),
        name="combine_ln",
    )(dest, x2d, wtok, lg, lb, ys)


def _moe(x2d, x_packed, rw_t, rb_t, wg, wu, wd, layer, lg, lb):
    t = x2d.shape[0]
    n_blocks = (t + N_CLASSES * (MOE_BLOCK - 1)) // MOE_BLOCK
    assert n_blocks <= META_LANES
    dest3, wtok, meta = _router(x2d, rw_t, rb_t)
    dest = dest3.reshape(t)
    xs = _dispatch(dest, x_packed, n_blocks * MOE_BLOCK)
    ys = _experts(meta[0, :n_blocks], meta[1, :n_blocks], meta[2, :1], xs, wg, wu, wd, layer)
    return _combine_ln(dest, x2d, wtok, ys, lg, lb)


def _rope_tables(seq):
    half = RET_DK // 2
    pos = jnp.arange(seq, dtype=F32)
    freqs = ROPE_BASE ** (-jnp.arange(half, dtype=F32) / half)
    ang = pos[:, None] * freqs[None, :]
    return jnp.cos(ang), jnp.sin(ang)


def kernel(x, w_in, ret_decay_logit, w_ret_o, na_rpb, w_na_o, w_out, ln_mix_g, ln_mix_b, router_w, router_bias,
           w_exp_gate, w_exp_up, w_exp_down, ln_ffn_g, ln_ffn_b):
    b, s, d = x.shape
    depth = w_in.shape[0]
    t = b * s
    rows = s // GRID_W
    cos, sin = _rope_tables(s)
    rw_t = router_w.astype(F32).T.reshape(N_GROUPS, EXPERTS_PER_GROUP, d).transpose(1, 0, 2).reshape(N_EXPERTS, d)
    rb_t = router_bias.astype(F32).reshape(N_GROUPS, EXPERTS_PER_GROUP).T.reshape(N_EXPERTS, 1)
    x2d = x.reshape(t, d)
    x_bf = x2d.astype(BF16)
    w_in = w_in.astype(F32)
    for l in range(depth):
        qk = _inproj(x_bf, w_in, l, cos, sin, s, "rotary").reshape(b, s, -1)
        vn = _inproj(x_bf, w_in, l, cos, sin, s, "scale").reshape(b, s, -1)
        gr = _inproj(x_bf, w_in, l, cos, sin, s, "silu").reshape(b, s, -1)
        gates = _inproj(x_bf, w_in, l, cos, sin, s, "sigmoid")
        ret = _retention(qk, vn, gr, ret_decay_logit[l].astype(F32))
        na = _na(vn, na_rpb[l].astype(F32).reshape(-1))
        x2d, x_packed = _mixout(ret.reshape(t, -1), na.reshape(t, -1), gates, x2d,
                                w_ret_o[l].astype(BF16), w_na_o[l].astype(BF16), w_out[l].astype(BF16),
                                ln_mix_g[l].reshape(1, d).astype(F32), ln_mix_b[l].reshape(1, d).astype(F32))
        x2d, x_bf = _moe(x2d, x_packed, rw_t, rb_t, w_exp_gate, w_exp_up, w_exp_down, l,
                         ln_ffn_g[l].reshape(1, d).astype(F32), ln_ffn_b[l].reshape(1, d).astype(F32))
    return x2d.reshape(b, s, d)
```

```python
import functools

import numpy as np
import jax
import jax.numpy as jnp
from jax import lax
from jax.experimental import pallas as pl
from jax.experimental.pallas import tpu as pltpu

F32 = jnp.float32
BF16 = jnp.bfloat16

D_MODEL = 1024
GRID_W = 64
RET_HEADS = 4
RET_DK = 256
RET_DV = 512
ROPE_BASE = 10000.0
NA_HEADS = 16
NA_HD = 64
NA_KH = 8
NA_KW = 16
N_EXPERTS = 32
N_GROUPS = 8
EXPERTS_PER_GROUP = 4
D_EXPERT = 512
LN_EPS = 1e-5
GN_EPS = 1e-5
MODEL_DEPTH = 4
DEEPNORM_ALPHA = (2 * MODEL_DEPTH) ** 0.25

D_IN = 11264

RET_CHUNK = 256
NA_QROWS = 4
NA_KROWS = 12
NA_NEG = -1e30
MOE_BLOCK = 256
N_PAIRS = 6
PAIR_SLOT_A = (0, 2, 2, 3, 3, 3)
PAIR_SLOT_B = (1, 1, 0, 0, 1, 2)
N_CLASSES = N_GROUPS * N_PAIRS
VMEM_LIMIT = 56 * 1024 * 1024


def _cparams(sem):
    return pltpu.CompilerParams(dimension_semantics=sem, vmem_limit_bytes=VMEM_LIMIT)


U32 = jnp.uint32
_HIGH_HALF = 0xFFFF0000


def _pack_bf16_pair(hi, lo):
    hi_bits = lax.bitcast_convert_type(hi.astype(jnp.bfloat16).astype(F32), U32)
    lo_bits = lax.bitcast_convert_type(lo.astype(jnp.bfloat16).astype(F32), U32)
    return (hi_bits & jnp.uint32(_HIGH_HALF)) | (lo_bits >> 16)


def _unpack_bf16_pair(packed):
    hi = lax.bitcast_convert_type(packed & jnp.uint32(_HIGH_HALF), F32)
    lo = lax.bitcast_convert_type(packed << 16, F32)
    return hi, lo


IN_BM = 1024
IN_CHUNK = 256
LOG2E = 1.4426950408889634

IN_GROUPS = {
    "rotary": (2048, (0,)),
    "scale": (1024, (2, 3, 6, 7, 8)),
    "silu": (2048, (2,)),
    "sigmoid": (1024, (9, 10)),
}
VN_Q = 2048
VN_K = 3072
VN_V = 4096


def _inproj_kernel(x_ref, w_ref, cos_ref, sin_ref, o_ref, wbf_ref, *, kind, bn):
    j = pl.program_id(0)

    @pl.when(pl.program_id(1) == 0)
    def _():
        wbf_ref[...] = w_ref[0].astype(BF16)

    x = x_ref[...]
    half = RET_DK // 2
    for c in range(bn // IN_CHUNK):
        lo = c * IN_CHUNK
        acc = jnp.dot(x, wbf_ref[:, lo:lo + IN_CHUNK], preferred_element_type=F32)
        if kind == "rotary":
            scale = 1.0 if lo < RET_HEADS * RET_DK else RET_DK ** -0.5
            cos = cos_ref[...] * scale
            sin = sin_ref[...] * scale
            x1 = acc[:, :half]
            x2 = acc[:, half:]
            o_ref[:, lo:lo + half] = (x1 * cos - x2 * sin).astype(o_ref.dtype)
            o_ref[:, lo + half:lo + IN_CHUNK] = (x1 * sin + x2 * cos).astype(o_ref.dtype)
        elif kind == "scale":
            scale = jnp.where(j == 2, NA_HD ** -0.5 * LOG2E, 1.0)
            o_ref[:, lo:lo + IN_CHUNK] = (acc * scale).astype(o_ref.dtype)
        elif kind == "silu":
            o_ref[:, lo:lo + IN_CHUNK] = (acc * jax.nn.sigmoid(acc)).astype(o_ref.dtype)
        else:
            o_ref[:, lo:lo + IN_CHUNK] = jax.nn.sigmoid(acc).astype(o_ref.dtype)


def _inproj(x_bf, w_in, layer, cos, sin, seq, kind):
    t = x_bf.shape[0]
    pos_blocks = seq // IN_BM
    bn, blocks = IN_GROUPS[kind]

    def wcol(j):
        col = blocks[0]
        for k in range(1, len(blocks)):
            col = jnp.where(j == k, blocks[k], col)
        return col

    pos_map = (lambda j, i: (i % pos_blocks, 0)) if kind == "rotary" else (lambda j, i: (0, 0))
    return pl.pallas_call(
        functools.partial(_inproj_kernel, kind=kind, bn=bn),
        out_shape=jax.ShapeDtypeStruct((t, len(blocks) * bn), BF16),
        grid=(len(blocks), t // IN_BM),
        in_specs=[
            pl.BlockSpec((IN_BM, D_MODEL), lambda j, i: (i, 0)),
            pl.BlockSpec((1, D_MODEL, bn), lambda j, i: (layer, 0, wcol(j))),
            pl.BlockSpec((IN_BM, RET_DK // 2), pos_map),
            pl.BlockSpec((IN_BM, RET_DK // 2), pos_map),
        ],
        out_specs=pl.BlockSpec((IN_BM, bn), lambda j, i: (i, j)),
        scratch_shapes=[pltpu.VMEM((D_MODEL, bn), BF16)],
        compiler_params=_cparams(("arbitrary", "arbitrary")),
        name="inproj_" + kind,
    )(x_bf, w_in, cos, sin)


def _log_sigmoid(x):
    return jnp.minimum(x, 0.0) - jnp.log1p(jnp.exp(-jnp.abs(x)))


def _retention_kernel(dl_ref, q_ref, k_ref, v_ref, g_ref, o_ref,
                      sf_ref, st_ref, dm_ref, qdf_ref, qdb_ref, kdf_ref, kdb_ref, *, nc):
    c_len = RET_CHUNK
    h = pl.program_id(1)
    lgf = _log_sigmoid(jnp.full((c_len, RET_DK), dl_ref[0, h], F32))
    lgb = _log_sigmoid(jnp.full((c_len, RET_DK), dl_ref[1, h], F32))
    ri = lax.broadcasted_iota(jnp.int32, (c_len, RET_DK), 0).astype(F32)
    qdf_ref[...] = jnp.exp(lgf * (ri + 1.0))
    qdb_ref[...] = jnp.exp(lgb * (c_len - ri))
    kdf_ref[...] = jnp.exp(lgf * (c_len - 1.0 - ri))
    kdb_ref[...] = jnp.exp(lgb * ri)
    rr = lax.broadcasted_iota(jnp.int32, (c_len, c_len), 0).astype(F32)
    cc = lax.broadcasted_iota(jnp.int32, (c_len, c_len), 1).astype(F32)
    diff = rr - cc
    lgf_cc = _log_sigmoid(jnp.full((c_len, c_len), dl_ref[0, h], F32))
    lgb_cc = _log_sigmoid(jnp.full((c_len, c_len), dl_ref[1, h], F32))
    dm_ref[...] = jnp.where(diff >= 0.0, jnp.exp(lgf_cc * jnp.maximum(diff, 0.0)),
                            jnp.exp(lgb_cc * jnp.maximum(-diff, 0.0)))
    chunk_f = jnp.exp(lgf[:1, :1] * float(c_len))
    chunk_b = jnp.exp(lgb[:1, :1] * float(c_len))

    tn_dims = (((0,), (0,)), ((), ()))
    nt_dims = (((1,), (1,)), ((), ()))

    st_ref[...] = jnp.zeros_like(st_ref)

    def fwd_body(c, carry):
        off = pl.multiple_of(c * c_len, c_len)
        sf_ref[c] = st_ref[...].astype(BF16)
        kc = k_ref[0, pl.ds(off, c_len), :].astype(F32)
        vc = v_ref[0, pl.ds(off, c_len), :]
        kd = (kc * kdf_ref[...]).astype(BF16)
        upd = lax.dot_general(kd, vc, tn_dims, preferred_element_type=F32)
        st_ref[...] = st_ref[...] * chunk_f + upd
        return carry

    lax.fori_loop(0, nc, fwd_body, 0, unroll=8)

    st_ref[...] = jnp.zeros_like(st_ref)

    def bwd_body(i, carry):
        c = nc - 1 - i
        off = pl.multiple_of(c * c_len, c_len)
        qb = q_ref[0, pl.ds(off, c_len), :]
        kb = k_ref[0, pl.ds(off, c_len), :]
        vc = v_ref[0, pl.ds(off, c_len), :]
        qc = qb.astype(F32)
        kc = kb.astype(F32)
        s = lax.dot_general(qb, kb, nt_dims, preferred_element_type=F32)
        p = (s * dm_ref[...]).astype(BF16)
        out = jnp.dot(p, vc, preferred_element_type=F32)
        out = out + jnp.dot((qc * qdf_ref[...]).astype(BF16), sf_ref[c], preferred_element_type=F32)
        out = out + jnp.dot((qc * qdb_ref[...]).astype(BF16), st_ref[...].astype(BF16),
                            preferred_element_type=F32)
        mu = jnp.mean(out, axis=-1, keepdims=True)
        cen = out - mu
        var = jnp.mean(cen * cen, axis=-1, keepdims=True)
        y = cen * lax.rsqrt(var + GN_EPS)
        gate = g_ref[0, pl.ds(off, c_len), :].astype(F32)
        o_ref[0, pl.ds(off, c_len), :] = (gate * y).astype(o_ref.dtype)
        kd = (kc * kdb_ref[...]).astype(BF16)
        upd = lax.dot_general(kd, vc, tn_dims, preferred_element_type=F32)
        st_ref[...] = st_ref[...] * chunk_b + upd
        return carry

    lax.fori_loop(0, nc, bwd_body, 0, unroll=8)


def _retention(qk3, vn3, gr3, decay_logit):
    b, s, _ = qk3.shape
    nc = s // RET_CHUNK
    return pl.pallas_call(
        functools.partial(_retention_kernel, nc=nc),
        out_shape=jax.ShapeDtypeStruct((b, s, RET_HEADS * RET_DV), BF16),
        grid=(b, RET_HEADS),
        in_specs=[
            pl.BlockSpec(memory_space=pltpu.SMEM),
            pl.BlockSpec((1, s, RET_DK), lambda bi, h: (bi, 0, h)),
            pl.BlockSpec((1, s, RET_DK), lambda bi, h: (bi, 0, RET_HEADS + h)),
            pl.BlockSpec((1, s, RET_DV), lambda bi, h: (bi, 0, h)),
            pl.BlockSpec((1, s, RET_DV), lambda bi, h: (bi, 0, h)),
        ],
        out_specs=pl.BlockSpec((1, s, RET_DV), lambda bi, h: (bi, 0, h)),
        scratch_shapes=[
            pltpu.VMEM((nc, RET_DK, RET_DV), BF16),
            pltpu.VMEM((RET_DK, RET_DV), F32),
            pltpu.VMEM((RET_CHUNK, RET_CHUNK), F32),
            pltpu.VMEM((RET_CHUNK, RET_DK), F32),
            pltpu.VMEM((RET_CHUNK, RET_DK), F32),
            pltpu.VMEM((RET_CHUNK, RET_DK), F32),
            pltpu.VMEM((RET_CHUNK, RET_DK), F32),
        ],
        compiler_params=_cparams(("arbitrary", "arbitrary")),
        name="retention",
    )(decay_logit, qk3, qk3, vn3, gr3)


NA_ROW_OFFS = 2 * NA_KH - 1
NA_COL_OFFS = 2 * NA_KW - 1
NA_PATTERNS = 3


def _na_row_offsets(rows):
    n_tiles = rows // NA_QROWS
    offs = np.full((NA_PATTERNS, NA_QROWS, NA_KROWS), NA_ROW_OFFS, np.int32)
    for p, t in enumerate((0, 1, n_tiles - 1)):
        kstart = int(np.clip(NA_QROWS * t - NA_KH // 2, 0, rows - NA_KROWS))
        for rr in range(NA_QROWS):
            r = NA_QROWS * t + rr
            rs = int(np.clip(r - NA_KH // 2, 0, rows - NA_KH))
            for i in range(NA_KROWS):
                krow = kstart + i
                if rs <= krow < rs + NA_KH:
                    offs[p, rr, i] = krow - r + (NA_KH - 1)
    return offs


def _na_build_bias(rpb_ref, e_ref, bias_ref, head0, rows):
    wide = 2 * GRID_W
    lane = lax.broadcasted_iota(jnp.int32, (GRID_W, wide), 1)
    col = lax.broadcasted_iota(jnp.int32, (GRID_W, wide), 0)
    kcol = jnp.where(lane < GRID_W, lane, lane - GRID_W)
    cstart = jnp.clip(col - NA_KW // 2, 0, GRID_W - NA_KW)
    col_ok = (kcol >= cstart) & (kcol < cstart + NA_KW)
    coff = kcol - col + (NA_KW - 1)
    neg = jnp.full((GRID_W, wide), NA_NEG, F32)
    left = lane < GRID_W
    offs = _na_row_offsets(rows)
    for hh in range(2):
        base = (head0 + hh) * (NA_ROW_OFFS * NA_COL_OFFS)
        for ro in range(NA_ROW_OFFS):
            acc = neg
            for d in range(NA_COL_OFFS):
                acc = jnp.where(coff == d, rpb_ref[base + ro * NA_COL_OFFS + d] * LOG2E, acc)
            e_ref[hh, ro] = jnp.where(col_ok, acc, neg)
        e_ref[hh, NA_ROW_OFFS] = neg
        for p in range(NA_PATTERNS):
            for rr in range(NA_QROWS):
                for ip in range(NA_KROWS // 2):
                    tile = jnp.where(left, e_ref[hh, int(offs[p, rr, 2 * ip])], e_ref[hh, int(offs[p, rr, 2 * ip + 1])])
                    bias_ref[hh, p, rr * GRID_W:(rr + 1) * GRID_W, ip * wide:(ip + 1) * wide] = tile


def _na_kernel(rpb_ref, q_ref, k_ref, v_ref, o_ref, bias_ref, e_ref, *, n_tiles, rows):
    nq = NA_QROWS * GRID_W
    nk = NA_KROWS * GRID_W
    nt_dims = (((1,), (1,)), ((), ()))
    first = lax.broadcasted_iota(jnp.int32, (nq, 2 * NA_HD), 1) < NA_HD
    kfirst = lax.broadcasted_iota(jnp.int32, (nk, 2 * NA_HD), 1) < NA_HD

    @pl.when(pl.program_id(1) == 0)
    def _():
        _na_build_bias(rpb_ref, e_ref, bias_ref, 2 * pl.program_id(0), rows)

    def body(t, carry):
        qoff = pl.multiple_of(t * nq, nq)
        krow0 = jnp.clip(NA_QROWS * t - NA_KH // 2, 0, rows - NA_KROWS)
        koff = pl.multiple_of(krow0 * GRID_W, nq)
        pat = jnp.where(t == 0, 0, jnp.where(t == n_tiles - 1, 2, 1))
        q2 = q_ref[0, pl.ds(qoff, nq), :]
        k2 = k_ref[0, pl.ds(koff, nk), :]
        v2 = v_ref[0, pl.ds(koff, nk), :]
        outs = []
        for hh in range(2):
            qm = jnp.where(first if hh == 0 else jnp.logical_not(first), q2, jnp.zeros_like(q2))
            s = lax.dot_general(qm, k2, nt_dims, preferred_element_type=F32)
            s = s + bias_ref[hh, pat]
            m = jnp.max(s, axis=-1, keepdims=True)
            e = jnp.exp2(s - m).astype(BF16)
            vv = jnp.where(kfirst if hh == 0 else jnp.logical_not(kfirst), v2, jnp.ones_like(v2))
            o = jnp.dot(e, vv, preferred_element_type=F32)
            outs.append(o / pltpu.roll(o, NA_HD, axis=1))
        o_ref[0, pl.ds(qoff, nq), :] = jnp.where(first, outs[0], outs[1]).astype(o_ref.dtype)
        return carry

    lax.fori_loop(0, n_tiles, body, 0, unroll=4)


def _na(vn3, rpb_flat):
    b, s, _ = vn3.shape
    rows = s // GRID_W
    n_tiles = rows // NA_QROWS
    w2 = 2 * NA_HD
    nq = NA_QROWS * GRID_W
    nk = NA_KROWS * GRID_W
    return pl.pallas_call(
        functools.partial(_na_kernel, n_tiles=n_tiles, rows=rows),
        out_shape=jax.ShapeDtypeStruct((b, s, NA_HEADS * NA_HD), BF16),
        grid=(NA_HEADS // 2, b),
        in_specs=[
            pl.BlockSpec(memory_space=pltpu.SMEM),
            pl.BlockSpec((1, s, w2), lambda hp, bi: (bi, 0, VN_Q // w2 + hp)),
            pl.BlockSpec((1, s, w2), lambda hp, bi: (bi, 0, VN_K // w2 + hp)),
            pl.BlockSpec((1, s, w2), lambda hp, bi: (bi, 0, VN_V // w2 + hp)),
        ],
        out_specs=pl.BlockSpec((1, s, w2), lambda hp, bi: (bi, 0, hp)),
        scratch_shapes=[
            pltpu.VMEM((2, NA_PATTERNS, nq, nk), F32),
            pltpu.VMEM((2, NA_ROW_OFFS + 1, GRID_W, 2 * GRID_W), F32),
        ],
        compiler_params=_cparams(("arbitrary", "arbitrary")),
        name="natten",
    )(rpb_flat, vn3, vn3, vn3)


MIX_BM = 512
MIX_SUB = 256


def _layernorm_rows(z, g, b):
    mu = jnp.mean(z, axis=-1, keepdims=True)
    cen = z - mu
    var = jnp.mean(cen * cen, axis=-1, keepdims=True)
    return cen * lax.rsqrt(var + LN_EPS) * g + b


def _mixout_kernel(ret_ref, na_ref, g1_ref, g2_ref, x_ref, wr_ref, wn_ref, wo_ref, lg_ref, lb_ref, o_ref, op_ref):
    for r in range(0, MIX_BM, MIX_SUB):
        rows = slice(r, r + MIX_SUB)
        y_ret = jnp.dot(ret_ref[rows, :], wr_ref[...], preferred_element_type=F32)
        y_na = jnp.dot(na_ref[rows, :], wn_ref[...], preferred_element_type=F32)
        merged = g1_ref[rows, :].astype(F32) * y_ret + g2_ref[rows, :].astype(F32) * y_na
        mix = jnp.dot(merged.astype(BF16), wo_ref[...], preferred_element_type=F32)
        z = DEEPNORM_ALPHA * x_ref[rows, :] + mix
        out = _layernorm_rows(z, lg_ref[...], lb_ref[...])
        o_ref[rows, :] = out
        op_ref[rows, :] = _pack_bf16_pair(out[:, :D_MODEL // 2], out[:, D_MODEL // 2:])


def _mixout(ret2d, na2d, gates2d, x2d, wr, wn, wo, lg, lb):
    t = x2d.shape[0]
    const = lambda i: (0, 0)
    return pl.pallas_call(
        _mixout_kernel,
        out_shape=(jax.ShapeDtypeStruct((t, D_MODEL), F32), jax.ShapeDtypeStruct((t, D_MODEL // 2), U32)),
        grid=(t // MIX_BM,),
        in_specs=[
            pl.BlockSpec((MIX_BM, RET_HEADS * RET_DV), lambda i: (i, 0)),
            pl.BlockSpec((MIX_BM, NA_HEADS * NA_HD), lambda i: (i, 0)),
            pl.BlockSpec((MIX_BM, D_MODEL), lambda i: (i, 0)),
            pl.BlockSpec((MIX_BM, D_MODEL), lambda i: (i, 1)),
            pl.BlockSpec((MIX_BM, D_MODEL), lambda i: (i, 0)),
            pl.BlockSpec((RET_HEADS * RET_DV, D_MODEL), const),
            pl.BlockSpec((NA_HEADS * NA_HD, D_MODEL), const),
            pl.BlockSpec((D_MODEL, D_MODEL), const),
            pl.BlockSpec((1, D_MODEL), const),
            pl.BlockSpec((1, D_MODEL), const),
        ],
        out_specs=(pl.BlockSpec((MIX_BM, D_MODEL), lambda i: (i, 0)),
                   pl.BlockSpec((MIX_BM, D_MODEL // 2), lambda i: (i, 0))),
        compiler_params=_cparams(("arbitrary",)),
        name="mixout",
    )(ret2d, na2d, gates2d, gates2d, x2d, wr, wn, wo, lg, lb)


ROUTE_BM = 512


def _route_tile(x, rw, rb):
    nt_dims = (((1,), (1,)), ((), ()))
    x_hi = x.astype(BF16)
    x_lo = (x - x_hi.astype(F32)).astype(BF16)
    rw_hi = rw.astype(BF16)
    rw_lo = (rw - rw_hi.astype(F32)).astype(BF16)
    both = lax.dot_general(jnp.concatenate([rw_hi, rw_lo], axis=0), x_hi, nt_dims, preferred_element_type=F32)
    logits = (both[:N_EXPERTS] + both[N_EXPERTS:]
              + lax.dot_general(rw_hi, x_lo, nt_dims, preferred_element_type=F32))
    scores = jax.nn.sigmoid(logits)
    sel = scores + rb
    p = [scores[m * N_GROUPS:(m + 1) * N_GROUPS] for m in range(EXPERTS_PER_GROUP)]
    s = [sel[m * N_GROUPS:(m + 1) * N_GROUPS] for m in range(EXPERTS_PER_GROUP)]
    one = jnp.ones_like(s[0])
    zero = jnp.zeros_like(s[0])
    chosen = []
    for m in range(EXPERTS_PER_GROUP):
        rank = zero
        for j in range(EXPERTS_PER_GROUP):
            if j == m:
                continue
            beats = (s[j] >= s[m]) if j < m else (s[j] > s[m])
            rank = rank + jnp.where(beats, one, zero)
        chosen.append(rank < 2.0)
    group_score = zero
    for m in range(EXPERTS_PER_GROUP):
        group_score = group_score + jnp.where(chosen[m], s[m], zero)
    gid = lax.broadcasted_iota(jnp.int32, group_score.shape, 0)
    gmax = jnp.max(group_score, axis=0, keepdims=True)
    gbest = jnp.min(jnp.where(group_score == gmax, gid, N_GROUPS), axis=0, keepdims=True)
    in_best = gid == gbest
    picked = [chosen[m] & in_best for m in range(EXPERTS_PER_GROUP)]
    before = zero
    wa = zero
    wb = zero
    ma = zero
    mb = zero
    for m in range(EXPERTS_PER_GROUP):
        is_a = picked[m] & (before == 0.0)
        is_b = picked[m] & (before == 1.0)
        wa = wa + jnp.where(is_a, p[m], zero)
        wb = wb + jnp.where(is_b, p[m], zero)
        ma = ma + jnp.where(is_a, float(m), 0.0)
        mb = mb + jnp.where(is_b, float(m), 0.0)
        before = before + jnp.where(chosen[m], one, zero)
    wa = jnp.sum(wa, axis=0, keepdims=True)
    wb = jnp.sum(wb, axis=0, keepdims=True)
    ma = jnp.sum(ma, axis=0, keepdims=True)
    mb = jnp.sum(mb, axis=0, keepdims=True)
    denom = wa + wb
    w_lo = wa / denom
    w_hi = wb / denom
    pair = jnp.where(ma == 0.0, jnp.where(mb == 1.0, 0.0, mb),
                     jnp.where(ma == 1.0, jnp.where(mb == 2.0, 1.0, 4.0), 5.0))
    keep = pair == 0.0
    return (gbest * N_PAIRS + pair.astype(jnp.int32),
            jnp.where(keep, w_lo, w_hi), jnp.where(keep, w_hi, w_lo))


CLASS_ROWS = 64
META_LANES = 128


def _router_kernel(x_ref, rw_ref, rb_ref, tri_ref, dest_ref, wtok_ref, meta_ref, cls_s, rank_s, cnt_s):
    phase = pl.program_id(0)
    i = pl.program_id(1)
    bm = ROUTE_BM
    reps = bm // META_LANES
    cid = lax.broadcasted_iota(jnp.int32, (CLASS_ROWS, bm), 0)

    @pl.when((phase == 0) & (i == 0))
    def _():
        cnt_s[...] = jnp.zeros_like(cnt_s)

    @pl.when(phase == 0)
    def _():
        cls, w_lo, w_hi = _route_tile(x_ref[...], rw_ref[...], rb_ref[...])
        cls_s[i] = cls
        w_rows = jnp.concatenate([w_lo, w_hi, jnp.zeros((META_LANES - 2, bm), F32)], axis=0)
        wtok_ref[...] = w_rows.T
        onehot = (cid == cls).astype(BF16)
        before = jnp.dot(onehot, tri_ref[...], preferred_element_type=F32)
        carry = jnp.concatenate([cnt_s[...]] * reps, axis=1)
        rank = jnp.sum(jnp.where(cid == cls, before + carry, 0.0), axis=0, keepdims=True)
        rank_s[i] = rank.astype(jnp.int32)
        cnt_s[...] = cnt_s[...] + jnp.dot(onehot, jnp.ones((bm, META_LANES), BF16), preferred_element_type=F32)

    @pl.when(phase == 1)
    def _():
        cnt = cnt_s[...]
        nblk = jnp.floor((cnt + (MOE_BLOCK - 1.0)) * (1.0 / MOE_BLOCK))
        rr = lax.broadcasted_iota(jnp.int32, (CLASS_ROWS, CLASS_ROWS), 0)
        cc = lax.broadcasted_iota(jnp.int32, (CLASS_ROWS, CLASS_ROWS), 1)
        lower = (cc < rr).astype(BF16)
        start_blk = jnp.dot(lower, nblk.astype(BF16), preferred_element_type=F32)
        cls = cls_s[i]
        start_t = jnp.concatenate([start_blk] * reps, axis=1)
        start = jnp.sum(jnp.where(cid == cls, start_t, 0.0), axis=0, keepdims=True)
        dest_ref[0] = (start * float(MOE_BLOCK)).astype(jnp.int32) + rank_s[i]

        @pl.when(i == 0)
        def _():
            end_blk = start_blk + nblk
            n_used = jnp.max(end_blk, axis=0, keepdims=True)
            blk = lax.broadcasted_iota(jnp.int32, (CLASS_ROWS, META_LANES), 1).astype(F32)
            blk = jnp.minimum(blk, n_used - 1.0)
            bcls = jnp.sum(jnp.where(end_blk <= blk, 1.0, 0.0), axis=0, keepdims=True)
            grp = jnp.zeros_like(bcls)
            for g in range(1, N_GROUPS):
                grp = grp + jnp.where(bcls >= float(g * N_PAIRS), 1.0, 0.0)
            pair = bcls - grp * float(N_PAIRS)
            slot_a = jnp.zeros_like(pair)
            slot_b = jnp.zeros_like(pair)
            for k in range(N_PAIRS):
                slot_a = jnp.where(pair == float(k), float(PAIR_SLOT_A[k]), slot_a)
                slot_b = jnp.where(pair == float(k), float(PAIR_SLOT_B[k]), slot_b)
            rows = [grp * float(EXPERTS_PER_GROUP) + slot_a, grp * float(EXPERTS_PER_GROUP) + slot_b, n_used]
            rows = rows + [jnp.zeros_like(bcls)] * (8 - len(rows))
            meta_ref[...] = jnp.concatenate(rows, axis=0).astype(jnp.int32)


def _router(x2d, rw_t, rb_t):
    t = x2d.shape[0]
    nt = t // ROUTE_BM
    tri = jnp.asarray(np.triu(np.ones((ROUTE_BM, ROUTE_BM), np.float32), 1), BF16)
    hold = lambda ph, i: ((1 - ph) * i + ph * (nt - 1), 0)
    const = lambda ph, i: (0, 0)
    return pl.pallas_call(
        _router_kernel,
        out_shape=(jax.ShapeDtypeStruct((nt, 1, ROUTE_BM), jnp.int32),
                   jax.ShapeDtypeStruct((t, META_LANES), F32),
                   jax.ShapeDtypeStruct((8, META_LANES), jnp.int32)),
        grid=(2, nt),
        in_specs=[
            pl.BlockSpec((ROUTE_BM, D_MODEL), hold),
            pl.BlockSpec((N_EXPERTS, D_MODEL), const),
            pl.BlockSpec((N_EXPERTS, 1), const),
            pl.BlockSpec((ROUTE_BM, ROUTE_BM), const),
        ],
        out_specs=(pl.BlockSpec((1, 1, ROUTE_BM), lambda ph, i: (ph * i, 0, 0)),
                   pl.BlockSpec((ROUTE_BM, META_LANES), hold),
                   pl.BlockSpec((8, META_LANES), const)),
        scratch_shapes=[
            pltpu.VMEM((nt, 1, ROUTE_BM), jnp.int32),
            pltpu.VMEM((nt, 1, ROUTE_BM), jnp.int32),
            pltpu.VMEM((CLASS_ROWS, META_LANES), F32),
        ],
        compiler_params=_cparams(("arbitrary", "arbitrary")),
        name="router",
    )(x2d, rw_t, rb_t, tri)


ROW_BM = 512


def _start_rows(make_copy):
    for r in range(ROW_BM):
        make_copy(r).start(priority=r % 2)


def _dispatch_kernel(dest_ref, x_ref, zeros_ref, xs_ref, sem):
    del zeros_ref
    base = pl.program_id(0) * ROW_BM
    _start_rows(lambda r: pltpu.make_async_copy(x_ref.at[pl.ds(r, 1), :],
                                                xs_ref.at[pl.ds(dest_ref[base + r], 1), :], sem))
    pltpu.make_async_copy(x_ref, xs_ref.at[pl.ds(0, ROW_BM), :], sem).wait()


def _dispatch(dest, x_rows, n_rows):
    t, width = x_rows.shape
    grid_spec = pltpu.PrefetchScalarGridSpec(
        num_scalar_prefetch=1,
        grid=(t // ROW_BM,),
        in_specs=[
            pl.BlockSpec((ROW_BM, width), lambda i, dest: (i, 0)),
            pl.BlockSpec(memory_space=pl.ANY),
        ],
        out_specs=pl.BlockSpec(memory_space=pl.ANY),
        scratch_shapes=[pltpu.SemaphoreType.DMA],
    )
    return pl.pallas_call(
        _dispatch_kernel,
        out_shape=jax.ShapeDtypeStruct((n_rows, width), x_rows.dtype),
        grid_spec=grid_spec,
        input_output_aliases={2: 0},
        compiler_params=_cparams(("arbitrary",)),
        name="dispatch",
    )(dest, x_rows, jnp.zeros((n_rows, width), x_rows.dtype))


EXPERT_RING = 3
EXPERT_AHEAD = EXPERT_RING - 1


def _experts_kernel(ea_ref, eb_ref, nb_ref, x_ref, wg_hbm, wu_hbm, wd_hbm, o_ref,
                    ring_g, ring_u, ring_d, sems, loads_ref, *, layer, n_blocks):
    j = pl.program_id(0)
    nb = nb_ref[0]
    experts_of = (ea_ref, eb_ref)

    def is_load(s, step):
        step = jnp.minimum(step, n_blocks - 1)
        return ((step == 0) | (experts_of[s][step] != experts_of[s][jnp.maximum(step - 1, 0)])).astype(jnp.int32)

    def weight_copies(s, step, buf):
        e = experts_of[s][jnp.minimum(step, n_blocks - 1)]
        return (pltpu.make_async_copy(wg_hbm.at[layer, e], ring_g.at[s, buf], sems.at[s, buf]),
                pltpu.make_async_copy(wu_hbm.at[layer, e], ring_u.at[s, buf], sems.at[s, buf]),
                pltpu.make_async_copy(wd_hbm.at[layer, e], ring_d.at[s, buf], sems.at[s, buf]))

    def start(s, step, load_number):
        for c in weight_copies(s, step, lax.rem(load_number - 1, EXPERT_RING)):
            c.start()

    @pl.when(j < nb)
    def _():
        bufs = []
        for s in range(2):
            loads_now = jnp.where(j == 0, 0, loads_ref[s]) + is_load(s, j)
            loads_ref[s] = loads_now
            buf = lax.rem(loads_now - 1, EXPERT_RING)
            bufs.append(buf)

            for first in range(EXPERT_AHEAD):
                ahead_loads = 1 + sum(is_load(s, k) for k in range(1, first + 1))

                @pl.when((j == 0) & (first < nb) & (is_load(s, first) == 1))
                def _(first=first, ahead_loads=ahead_loads):
                    start(s, first, ahead_loads)

            ahead = j + EXPERT_AHEAD
            ahead_loads = loads_now + sum(is_load(s, j + k) for k in range(1, EXPERT_AHEAD + 1))

            @pl.when((ahead < nb) & (is_load(s, ahead) == 1))
            def _(ahead=ahead, ahead_loads=ahead_loads):
                start(s, ahead, ahead_loads)

            @pl.when(is_load(s, j) == 1)
            def _(buf=buf):
                for c in weight_copies(s, j, buf):
                    c.wait()

        x = jnp.concatenate(_unpack_bf16_pair(x_ref[...]), axis=1).astype(BF16)
        ys = []
        for s in range(2):
            hg = jnp.dot(x, ring_g[s, bufs[s]].astype(BF16), preferred_element_type=F32)
            hu = jnp.dot(x, ring_u[s, bufs[s]].astype(BF16), preferred_element_type=F32)
            act = (hg * jax.nn.sigmoid(hg) * hu).astype(BF16)
            ys.append(jnp.dot(act, ring_d[s, bufs[s]].astype(BF16), preferred_element_type=F32))
        o_ref[...] = _pack_bf16_pair(ys[0], ys[1])

    @pl.when(j >= nb)
    def _():
        o_ref[...] = jnp.zeros_like(o_ref)


def _experts(blk_a, blk_b, n_used, xs, wg, wu, wd, layer):
    n_rows = xs.shape[0]
    n_blocks = n_rows // MOE_BLOCK
    in_row_map = lambda j, ea, eb, nb: (jnp.maximum(jnp.minimum(j, nb[0] - 1), 0), 0)
    row_map = lambda j, ea, eb, nb: (j, 0)
    grid_spec = pltpu.PrefetchScalarGridSpec(
        num_scalar_prefetch=3,
        grid=(n_blocks,),
        in_specs=[
            pl.BlockSpec((MOE_BLOCK, D_MODEL // 2), in_row_map),
            pl.BlockSpec(memory_space=pl.ANY),
            pl.BlockSpec(memory_space=pl.ANY),
            pl.BlockSpec(memory_space=pl.ANY),
        ],
        out_specs=pl.BlockSpec((MOE_BLOCK, D_MODEL), row_map),
        scratch_shapes=[
            pltpu.VMEM((2, EXPERT_RING, D_MODEL, D_EXPERT), F32),
            pltpu.VMEM((2, EXPERT_RING, D_MODEL, D_EXPERT), F32),
            pltpu.VMEM((2, EXPERT_RING, D_EXPERT, D_MODEL), F32),
            pltpu.SemaphoreType.DMA((2, EXPERT_RING)),
            pltpu.SMEM((2,), jnp.int32),
        ],
    )
    return pl.pallas_call(
        functools.partial(_experts_kernel, layer=layer, n_blocks=n_blocks),
        out_shape=jax.ShapeDtypeStruct((n_rows, D_MODEL), U32),
        grid_spec=grid_spec,
        compiler_params=_cparams(("arbitrary",)),
        name="experts",
    )(blk_a, blk_b, n_used, xs, wg, wu, wd)


def _combine_ln_kernel(dest_ref, x_ref, wtok_ref, lg_ref, lb_ref, ys_ref, o_ref, obf_ref, ybuf, sem):
    i = pl.program_id(0)
    slot = i % 2

    def start_tile(tile, to_slot):
        base = tile * ROW_BM
        _start_rows(lambda r: pltpu.make_async_copy(ys_ref.at[pl.ds(dest_ref[base + r], 1), :],
                                                    ybuf.at[to_slot, pl.ds(r, 1), :], sem.at[to_slot]))

    @pl.when(i == 0)
    def _():
        start_tile(0, 0)

    @pl.when(i + 1 < pl.num_programs(0))
    def _():
        start_tile(i + 1, 1 - slot)

    pltpu.make_async_copy(ys_ref.at[pl.ds(0, ROW_BM), :], ybuf.at[slot], sem.at[slot]).wait()
    w = wtok_ref[...]
    y_a, y_b = _unpack_bf16_pair(ybuf[slot])
    y = w[:, 0:1] * y_a + w[:, 1:2] * y_b
    z = DEEPNORM_ALPHA * x_ref[...] + y
    out = _layernorm_rows(z, lg_ref[...], lb_ref[...])
    o_ref[...] = out
    obf_ref[...] = out.astype(BF16)


def _combine_ln(dest, x2d, wtok, ys, lg, lb):
    t = x2d.shape[0]
    const = lambda i, dest: (0, 0)
    grid_spec = pltpu.PrefetchScalarGridSpec(
        num_scalar_prefetch=1,
        grid=(t // ROW_BM,),
        in_specs=[
            pl.BlockSpec((ROW_BM, D_MODEL), lambda i, dest: (i, 0)),
            pl.BlockSpec((ROW_BM, META_LANES), lambda i, dest: (i, 0)),
            pl.BlockSpec((1, D_MODEL), const),
            pl.BlockSpec((1, D_MODEL), const),
            pl.BlockSpec(memory_space=pl.ANY),
        ],
        out_specs=(pl.BlockSpec((ROW_BM, D_MODEL), lambda i, dest: (i, 0)),
                   pl.BlockSpec((ROW_BM, D_MODEL), lambda i, dest: (i, 0))),
        scratch_shapes=[pltpu.VMEM((2, ROW_BM, D_MODEL), U32), pltpu.SemaphoreType.DMA((2,))],
    )
    return pl.pallas_call(
        _combine_ln_kernel,
        out_shape=(jax.ShapeDtypeStruct((t, D_MODEL), F32), jax.ShapeDtypeStruct((t, D_MODEL), BF16)),
        grid_spec=grid_spec,
        compiler_params=_cparams(("arbitrary",)),
        name="combine_ln",
    )(dest, x2d, wtok, lg, lb, ys)


def _moe(x2d, x_packed, rw_t, rb_t, wg, wu, wd, layer, lg, lb):
    t = x2d.shape[0]
    n_blocks = (t + N_CLASSES * (MOE_BLOCK - 1)) // MOE_BLOCK
    assert n_blocks <= META_LANES
    dest3, wtok, meta = _router(x2d, rw_t, rb_t)
    dest = dest3.reshape(t)
    xs = _dispatch(dest, x_packed, n_blocks * MOE_BLOCK)
    ys = _experts(meta[0, :n_blocks], meta[1, :n_blocks], meta[2, :1], xs, wg, wu, wd, layer)
    return _combine_ln(dest, x2d, wtok, ys, lg, lb)


def _rope_tables(seq):
    half = RET_DK // 2
    pos = jnp.arange(seq, dtype=F32)
    freqs = ROPE_BASE ** (-jnp.arange(half, dtype=F32) / half)
    ang = pos[:, None] * freqs[None, :]
    return jnp.cos(ang), jnp.sin(ang)


def kernel(x, w_in, ret_decay_logit, w_ret_o, na_rpb, w_na_o, w_out, ln_mix_g, ln_mix_b, router_w, router_bias,
           w_exp_gate, w_exp_up, w_exp_down, ln_ffn_g, ln_ffn_b):
    b, s, d = x.shape
    depth = w_in.shape[0]
    t = b * s
    rows = s // GRID_W
    cos, sin = _rope_tables(s)
    rw_t = router_w.astype(F32).T.reshape(N_GROUPS, EXPERTS_PER_GROUP, d).transpose(1, 0, 2).reshape(N_EXPERTS, d)
    rb_t = router_bias.astype(F32).reshape(N_GROUPS, EXPERTS_PER_GROUP).T.reshape(N_EXPERTS, 1)
    x2d = x.reshape(t, d)
    x_bf = x2d.astype(BF16)
    w_in = w_in.astype(F32)
    for l in range(depth):
        qk = _inproj(x_bf, w_in, l, cos, sin, s, "rotary").reshape(b, s, -1)
        vn = _inproj(x_bf, w_in, l, cos, sin, s, "scale").reshape(b, s, -1)
        gr = _inproj(x_bf, w_in, l, cos, sin, s, "silu").reshape(b, s, -1)
        gates = _inproj(x_bf, w_in, l, cos, sin, s, "sigmoid")
        ret = _retention(qk, vn, gr, ret_decay_logit[l].astype(F32))
        na = _na(vn, na_rpb[l].astype(F32).reshape(-1))
        x2d, x_packed = _mixout(ret.reshape(t, -1), na.reshape(t, -1), gates, x2d,
                                w_ret_o[l].astype(BF16), w_na_o[l].astype(BF16), w_out[l].astype(BF16),
                                ln_mix_g[l].reshape(1, d).astype(F32), ln_mix_b[l].reshape(1, d).astype(F32))
        x2d, x_bf = _moe(x2d, x_packed, rw_t, rb_t, w_exp_gate, w_exp_up, w_exp_down, l,
                         ln_ffn_g[l].reshape(1, d).astype(F32), ln_ffn_b[l].reshape(1, d).astype(F32))
    return x2d.reshape(b, s, d)
```

```python
import functools

import numpy as np
import jax
import jax.numpy as jnp
from jax import lax
from jax.experimental import pallas as pl
from jax.experimental.pallas import tpu as pltpu

F32 = jnp.float32
BF16 = jnp.bfloat16

D_MODEL = 1024
GRID_W = 64
RET_HEADS = 4
RET_DK = 256
RET_DV = 512
ROPE_BASE = 10000.0
NA_HEADS = 16
NA_HD = 64
NA_KH = 8
NA_KW = 16
N_EXPERTS = 32
N_GROUPS = 8
EXPERTS_PER_GROUP = 4
D_EXPERT = 512
LN_EPS = 1e-5
GN_EPS = 1e-5
MODEL_DEPTH = 4
DEEPNORM_ALPHA = (2 * MODEL_DEPTH) ** 0.25

D_IN = 11264

RET_CHUNK = 256
NA_QROWS = 4
NA_KROWS = 12
NA_NEG = -1e30
MOE_BLOCK = 256
N_PAIRS = 6
PAIR_SLOT_A = (0, 2, 2, 3, 3, 3)
PAIR_SLOT_B = (1, 1, 0, 0, 1, 2)
N_CLASSES = N_GROUPS * N_PAIRS
VMEM_LIMIT = 56 * 1024 * 1024


def _cparams(sem):
    return pltpu.CompilerParams(dimension_semantics=sem, vmem_limit_bytes=VMEM_LIMIT)


U32 = jnp.uint32
_HIGH_HALF = 0xFFFF0000


def _pack_bf16_pair(hi, lo):
    hi_bits = lax.bitcast_convert_type(hi.astype(jnp.bfloat16).astype(F32), U32)
    lo_bits = lax.bitcast_convert_type(lo.astype(jnp.bfloat16).astype(F32), U32)
    return (hi_bits & jnp.uint32(_HIGH_HALF)) | (lo_bits >> 16)


def _unpack_bf16_pair(packed):
    hi = lax.bitcast_convert_type(packed & jnp.uint32(_HIGH_HALF), F32)
    lo = lax.bitcast_convert_type(packed << 16, F32)
    return hi, lo


IN_BM = 1024
IN_CHUNK = 256
LOG2E = 1.4426950408889634

IN_GROUPS = {
    "rotary": (2048, (0,)),
    "scale": (1024, (2, 3, 6, 7, 8)),
    "silu": (2048, (2,)),
    "sigmoid": (1024, (9, 10)),
}
VN_Q = 2048
VN_K = 3072
VN_V = 4096


def _inproj_kernel(x_ref, w_ref, cos_ref, sin_ref, o_ref, wbf_ref, *, kind, bn):
    j = pl.program_id(0)

    @pl.when(pl.program_id(1) == 0)
    def _():
        wbf_ref[...] = w_ref[0].astype(BF16)

    x = x_ref[...]
    half = RET_DK // 2
    for c in range(bn // IN_CHUNK):
        lo = c * IN_CHUNK
        acc = jnp.dot(x, wbf_ref[:, lo:lo + IN_CHUNK], preferred_element_type=F32)
        if kind == "rotary":
            scale = 1.0 if lo < RET_HEADS * RET_DK else RET_DK ** -0.5
            cos = cos_ref[...] * scale
            sin = sin_ref[...] * scale
            x1 = acc[:, :half]
            x2 = acc[:, half:]
            o_ref[:, lo:lo + half] = (x1 * cos - x2 * sin).astype(o_ref.dtype)
            o_ref[:, lo + half:lo + IN_CHUNK] = (x1 * sin + x2 * cos).astype(o_ref.dtype)
        elif kind == "scale":
            scale = jnp.where(j == 2, NA_HD ** -0.5 * LOG2E, 1.0)
            o_ref[:, lo:lo + IN_CHUNK] = (acc * scale).astype(o_ref.dtype)
        elif kind == "silu":
            o_ref[:, lo:lo + IN_CHUNK] = (acc * jax.nn.sigmoid(acc)).astype(o_ref.dtype)
        else:
            o_ref[:, lo:lo + IN_CHUNK] = jax.nn.sigmoid(acc).astype(o_ref.dtype)


def _inproj(x_bf, w_in, layer, cos, sin, seq, kind):
    t = x_bf.shape[0]
    pos_blocks = seq // IN_BM
    bn, blocks = IN_GROUPS[kind]

    def wcol(j):
        col = blocks[0]
        for k in range(1, len(blocks)):
            col = jnp.where(j == k, blocks[k], col)
        return col

    pos_map = (lambda j, i: (i % pos_blocks, 0)) if kind == "rotary" else (lambda j, i: (0, 0))
    return pl.pallas_call(
        functools.partial(_inproj_kernel, kind=kind, bn=bn),
        out_shape=jax.ShapeDtypeStruct((t, len(blocks) * bn), BF16),
        grid=(len(blocks), t // IN_BM),
        in_specs=[
            pl.BlockSpec((IN_BM, D_MODEL), lambda j, i: (i, 0)),
            pl.BlockSpec((1, D_MODEL, bn), lambda j, i: (layer, 0, wcol(j))),
            pl.BlockSpec((IN_BM, RET_DK // 2), pos_map),
            pl.BlockSpec((IN_BM, RET_DK // 2), pos_map),
        ],
        out_specs=pl.BlockSpec((IN_BM, bn), lambda j, i: (i, j)),
        scratch_shapes=[pltpu.VMEM((D_MODEL, bn), BF16)],
        compiler_params=_cparams(("arbitrary", "arbitrary")),
        name="inproj_" + kind,
    )(x_bf, w_in, cos, sin)


def _log_sigmoid(x):
    return jnp.minimum(x, 0.0) - jnp.log1p(jnp.exp(-jnp.abs(x)))


def _retention_kernel(dl_ref, q_ref, k_ref, v_ref, g_ref, o_ref,
                      sf_ref, st_ref, dm_ref, qdf_ref, qdb_ref, kdf_ref, kdb_ref, *, nc):
    c_len = RET_CHUNK
    h = pl.program_id(1)
    lgf = _log_sigmoid(jnp.full((c_len, RET_DK), dl_ref[0, h], F32))
    lgb = _log_sigmoid(jnp.full((c_len, RET_DK), dl_ref[1, h], F32))
    ri = lax.broadcasted_iota(jnp.int32, (c_len, RET_DK), 0).astype(F32)
    qdf_ref[...] = jnp.exp(lgf * (ri + 1.0))
    qdb_ref[...] = jnp.exp(lgb * (c_len - ri))
    kdf_ref[...] = jnp.exp(lgf * (c_len - 1.0 - ri))
    kdb_ref[...] = jnp.exp(lgb * ri)
    rr = lax.broadcasted_iota(jnp.int32, (c_len, c_len), 0).astype(F32)
    cc = lax.broadcasted_iota(jnp.int32, (c_len, c_len), 1).astype(F32)
    diff = rr - cc
    lgf_cc = _log_sigmoid(jnp.full((c_len, c_len), dl_ref[0, h], F32))
    lgb_cc = _log_sigmoid(jnp.full((c_len, c_len), dl_ref[1, h], F32))
    dm_ref[...] = jnp.where(diff >= 0.0, jnp.exp(lgf_cc * jnp.maximum(diff, 0.0)),
                            jnp.exp(lgb_cc * jnp.maximum(-diff, 0.0)))
    chunk_f = jnp.exp(lgf[:1, :1] * float(c_len))
    chunk_b = jnp.exp(lgb[:1, :1] * float(c_len))

    tn_dims = (((0,), (0,)), ((), ()))
    nt_dims = (((1,), (1,)), ((), ()))

    st_ref[...] = jnp.zeros_like(st_ref)

    def fwd_body(c, carry):
        off = pl.multiple_of(c * c_len, c_len)
        sf_ref[c] = st_ref[...].astype(BF16)
        kc = k_ref[0, pl.ds(off, c_len), :].astype(F32)
        vc = v_ref[0, pl.ds(off, c_len), :]
        kd = (kc * kdf_ref[...]).astype(BF16)
        upd = lax.dot_general(kd, vc, tn_dims, preferred_element_type=F32)
        st_ref[...] = st_ref[...] * chunk_f + upd
        return carry

    lax.fori_loop(0, nc, fwd_body, 0, unroll=8)

    st_ref[...] = jnp.zeros_like(st_ref)

    def bwd_body(i, carry):
        c = nc - 1 - i
        off = pl.multiple_of(c * c_len, c_len)
        qb = q_ref[0, pl.ds(off, c_len), :]
        kb = k_ref[0, pl.ds(off, c_len), :]
        vc = v_ref[0, pl.ds(off, c_len), :]
        qc = qb.astype(F32)
        kc = kb.astype(F32)
        s = lax.dot_general(qb, kb, nt_dims, preferred_element_type=F32)
        p = (s * dm_ref[...]).astype(BF16)
        out = jnp.dot(p, vc, preferred_element_type=F32)
        out = out + jnp.dot((qc * qdf_ref[...]).astype(BF16), sf_ref[c], preferred_element_type=F32)
        out = out + jnp.dot((qc * qdb_ref[...]).astype(BF16), st_ref[...].astype(BF16),
                            preferred_element_type=F32)
        mu = jnp.mean(out, axis=-1, keepdims=True)
        cen = out - mu
        var = jnp.mean(cen * cen, axis=-1, keepdims=True)
        y = cen * lax.rsqrt(var + GN_EPS)
        gate = g_ref[0, pl.ds(off, c_len), :].astype(F32)
        o_ref[0, pl.ds(off, c_len), :] = (gate * y).astype(o_ref.dtype)
        kd = (kc * kdb_ref[...]).astype(BF16)
        upd = lax.dot_general(kd, vc, tn_dims, preferred_element_type=F32)
        st_ref[...] = st_ref[...] * chunk_b + upd
        return carry

    lax.fori_loop(0, nc, bwd_body, 0, unroll=8)


def _retention(qk3, vn3, gr3, decay_logit):
    b, s, _ = qk3.shape
    nc = s // RET_CHUNK
    return pl.pallas_call(
        functools.partial(_retention_kernel, nc=nc),
        out_shape=jax.ShapeDtypeStruct((b, s, RET_HEADS * RET_DV), BF16),
        grid=(b, RET_HEADS),
        in_specs=[
            pl.BlockSpec(memory_space=pltpu.SMEM),
            pl.BlockSpec((1, s, RET_DK), lambda bi, h: (bi, 0, h)),
            pl.BlockSpec((1, s, RET_DK), lambda bi, h: (bi, 0, RET_HEADS + h)),
            pl.BlockSpec((1, s, RET_DV), lambda bi, h: (bi, 0, h)),
            pl.BlockSpec((1, s, RET_DV), lambda bi, h: (bi, 0, h)),
        ],
        out_specs=pl.BlockSpec((1, s, RET_DV), lambda bi, h: (bi, 0, h)),
        scratch_shapes=[
            pltpu.VMEM((nc, RET_DK, RET_DV), BF16),
            pltpu.VMEM((RET_DK, RET_DV), F32),
            pltpu.VMEM((RET_CHUNK, RET_CHUNK), F32),
            pltpu.VMEM((RET_CHUNK, RET_DK), F32),
            pltpu.VMEM((RET_CHUNK, RET_DK), F32),
            pltpu.VMEM((RET_CHUNK, RET_DK), F32),
            pltpu.VMEM((RET_CHUNK, RET_DK), F32),
        ],
        compiler_params=_cparams(("arbitrary", "arbitrary")),
        name="retention",
    )(decay_logit, qk3, qk3, vn3, gr3)


NA_ROW_OFFS = 2 * NA_KH - 1
NA_COL_OFFS = 2 * NA_KW - 1
NA_PATTERNS = 3


def _na_row_offsets(rows):
    n_tiles = rows // NA_QROWS
    offs = np.full((NA_PATTERNS, NA_QROWS, NA_KROWS), NA_ROW_OFFS, np.int32)
    for p, t in enumerate((0, 1, n_tiles - 1)):
        kstart = int(np.clip(NA_QROWS * t - NA_KH // 2, 0, rows - NA_KROWS))
        for rr in range(NA_QROWS):
            r = NA_QROWS * t + rr
            rs = int(np.clip(r - NA_KH // 2, 0, rows - NA_KH))
            for i in range(NA_KROWS):
                krow = kstart + i
                if rs <= krow < rs + NA_KH:
                    offs[p, rr, i] = krow - r + (NA_KH - 1)
    return offs


def _na_build_bias(rpb_ref, e_ref, bias_ref, head0, rows):
    wide = 2 * GRID_W
    lane = lax.broadcasted_iota(jnp.int32, (GRID_W, wide), 1)
    col = lax.broadcasted_iota(jnp.int32, (GRID_W, wide), 0)
    kcol = jnp.where(lane < GRID_W, lane, lane - GRID_W)
    cstart = jnp.clip(col - NA_KW // 2, 0, GRID_W - NA_KW)
    col_ok = (kcol >= cstart) & (kcol < cstart + NA_KW)
    coff = kcol - col + (NA_KW - 1)
    neg = jnp.full((GRID_W, wide), NA_NEG, F32)
    left = lane < GRID_W
    offs = _na_row_offsets(rows)
    for hh in range(2):
        base = (head0 + hh) * (NA_ROW_OFFS * NA_COL_OFFS)
        for ro in range(NA_ROW_OFFS):
            acc = neg
            for d in range(NA_COL_OFFS):
                acc = jnp.where(coff == d, rpb_ref[base + ro * NA_COL_OFFS + d] * LOG2E, acc)
            e_ref[hh, ro] = jnp.where(col_ok, acc, neg)
        e_ref[hh, NA_ROW_OFFS] = neg
        for p in range(NA_PATTERNS):
            for rr in range(NA_QROWS):
                for ip in range(NA_KROWS // 2):
                    tile = jnp.where(left, e_ref[hh, int(offs[p, rr, 2 * ip])], e_ref[hh, int(offs[p, rr, 2 * ip + 1])])
                    bias_ref[hh, p, rr * GRID_W:(rr + 1) * GRID_W, ip * wide:(ip + 1) * wide] = tile


NA_GROUP = 8


def _na_kernel(rpb_ref, q_ref, k_ref, v_ref, o_ref, bias_ref, e_ref, s_buf, p_buf, *, n_tiles, rows):
    nq = NA_QROWS * GRID_W
    nk = NA_KROWS * GRID_W
    nt_dims = (((1,), (1,)), ((), ()))
    first = lax.broadcasted_iota(jnp.int32, (nq, 2 * NA_HD), 1) < NA_HD
    kfirst = lax.broadcasted_iota(jnp.int32, (nk, 2 * NA_HD), 1) < NA_HD

    @pl.when(pl.program_id(1) == 0)
    def _():
        _na_build_bias(rpb_ref, e_ref, bias_ref, 2 * pl.program_id(0), rows)

    def body(g, carry):
        offs = []
        for u in range(NA_GROUP):
            t = g * NA_GROUP + u
            qoff = pl.multiple_of(t * nq, nq)
            krow0 = jnp.clip(NA_QROWS * t - NA_KH // 2, 0, rows - NA_KROWS)
            koff = pl.multiple_of(krow0 * GRID_W, nq)
            pat = jnp.where(t == 0, 0, jnp.where(t == n_tiles - 1, 2, 1))
            offs.append((qoff, koff, pat))
        for u, (qoff, koff, pat) in enumerate(offs):
            q2 = q_ref[0, pl.ds(qoff, nq), :]
            k2 = k_ref[0, pl.ds(koff, nk), :]
            for hh in range(2):
                qm = jnp.where(first if hh == 0 else jnp.logical_not(first), q2, jnp.zeros_like(q2))
                s_buf[2 * u + hh] = lax.dot_general(qm, k2, nt_dims, preferred_element_type=F32) + bias_ref[hh, pat]
        for n in range(2 * NA_GROUP):
            s = s_buf[n]
            p_buf[n] = jnp.exp2(s - jnp.max(s, axis=-1, keepdims=True)).astype(BF16)
        for u, (qoff, koff, pat) in enumerate(offs):
            v2 = v_ref[0, pl.ds(koff, nk), :]
            outs = []
            for hh in range(2):
                vv = jnp.where(kfirst if hh == 0 else jnp.logical_not(kfirst), v2, jnp.ones_like(v2))
                o = jnp.dot(p_buf[2 * u + hh], vv, preferred_element_type=F32)
                outs.append(o / pltpu.roll(o, NA_HD, axis=1))
            o_ref[0, pl.ds(qoff, nq), :] = jnp.where(first, outs[0], outs[1]).astype(o_ref.dtype)
        return carry

    lax.fori_loop(0, n_tiles // NA_GROUP, body, 0)


def _na(vn3, rpb_flat):
    b, s, _ = vn3.shape
    rows = s // GRID_W
    n_tiles = rows // NA_QROWS
    w2 = 2 * NA_HD
    nq = NA_QROWS * GRID_W
    nk = NA_KROWS * GRID_W
    return pl.pallas_call(
        functools.partial(_na_kernel, n_tiles=n_tiles, rows=rows),
        out_shape=jax.ShapeDtypeStruct((b, s, NA_HEADS * NA_HD), BF16),
        grid=(NA_HEADS // 2, b),
        in_specs=[
            pl.BlockSpec(memory_space=pltpu.SMEM),
            pl.BlockSpec((1, s, w2), lambda hp, bi: (bi, 0, VN_Q // w2 + hp)),
            pl.BlockSpec((1, s, w2), lambda hp, bi: (bi, 0, VN_K // w2 + hp)),
            pl.BlockSpec((1, s, w2), lambda hp, bi: (bi, 0, VN_V // w2 + hp)),
        ],
        out_specs=pl.BlockSpec((1, s, w2), lambda hp, bi: (bi, 0, hp)),
        scratch_shapes=[
            pltpu.VMEM((2, NA_PATTERNS, nq, nk), F32),
            pltpu.VMEM((2, NA_ROW_OFFS + 1, GRID_W, 2 * GRID_W), F32),
            pltpu.VMEM((2 * NA_GROUP, nq, nk), F32),
            pltpu.VMEM((2 * NA_GROUP, nq, nk), BF16),
        ],
        compiler_params=_cparams(("arbitrary", "arbitrary")),
        name="natten",
    )(rpb_flat, vn3, vn3, vn3)


MIX_BM = 512
MIX_SUB = 256


def _layernorm_rows(z, g, b):
    mu = jnp.mean(z, axis=-1, keepdims=True)
    cen = z - mu
    var = jnp.mean(cen * cen, axis=-1, keepdims=True)
    return cen * lax.rsqrt(var + LN_EPS) * g + b


def _mixout_kernel(ret_ref, na_ref, g1_ref, g2_ref, x_ref, wr_ref, wn_ref, wo_ref, lg_ref, lb_ref, o_ref, op_ref):
    for r in range(0, MIX_BM, MIX_SUB):
        rows = slice(r, r + MIX_SUB)
        y_ret = jnp.dot(ret_ref[rows, :], wr_ref[...], preferred_element_type=F32)
        y_na = jnp.dot(na_ref[rows, :], wn_ref[...], preferred_element_type=F32)
        merged = g1_ref[rows, :].astype(F32) * y_ret + g2_ref[rows, :].astype(F32) * y_na
        mix = jnp.dot(merged.astype(BF16), wo_ref[...], preferred_element_type=F32)
        z = DEEPNORM_ALPHA * x_ref[rows, :] + mix
        out = _layernorm_rows(z, lg_ref[...], lb_ref[...])
        o_ref[rows, :] = out
        op_ref[rows, :] = _pack_bf16_pair(out[:, :D_MODEL // 2], out[:, D_MODEL // 2:])


def _mixout(ret2d, na2d, gates2d, x2d, wr, wn, wo, lg, lb):
    t = x2d.shape[0]
    const = lambda i: (0, 0)
    return pl.pallas_call(
        _mixout_kernel,
        out_shape=(jax.ShapeDtypeStruct((t, D_MODEL), F32), jax.ShapeDtypeStruct((t, D_MODEL // 2), U32)),
        grid=(t // MIX_BM,),
        in_specs=[
            pl.BlockSpec((MIX_BM, RET_HEADS * RET_DV), lambda i: (i, 0)),
            pl.BlockSpec((MIX_BM, NA_HEADS * NA_HD), lambda i: (i, 0)),
            pl.BlockSpec((MIX_BM, D_MODEL), lambda i: (i, 0)),
            pl.BlockSpec((MIX_BM, D_MODEL), lambda i: (i, 1)),
            pl.BlockSpec((MIX_BM, D_MODEL), lambda i: (i, 0)),
            pl.BlockSpec((RET_HEADS * RET_DV, D_MODEL), const),
            pl.BlockSpec((NA_HEADS * NA_HD, D_MODEL), const),
            pl.BlockSpec((D_MODEL, D_MODEL), const),
            pl.BlockSpec((1, D_MODEL), const),
            pl.BlockSpec((1, D_MODEL), const),
        ],
        out_specs=(pl.BlockSpec((MIX_BM, D_MODEL), lambda i: (i, 0)),
                   pl.BlockSpec((MIX_BM, D_MODEL // 2), lambda i: (i, 0))),
        compiler_params=_cparams(("arbitrary",)),
        name="mixout",
    )(ret2d, na2d, gates2d, gates2d, x2d, wr, wn, wo, lg, lb)


ROUTE_BM = 512


def _route_tile(x, rw, rb):
    nt_dims = (((1,), (1,)), ((), ()))
    x_hi = x.astype(BF16)
    x_lo = (x - x_hi.astype(F32)).astype(BF16)
    rw_hi = rw.astype(BF16)
    rw_lo = (rw - rw_hi.astype(F32)).astype(BF16)
    both = lax.dot_general(jnp.concatenate([rw_hi, rw_lo], axis=0), x_hi, nt_dims, preferred_element_type=F32)
    logits = (both[:N_EXPERTS] + both[N_EXPERTS:]
              + lax.dot_general(rw_hi, x_lo, nt_dims, preferred_element_type=F32))
    scores = jax.nn.sigmoid(logits)
    sel = scores + rb
    p = [scores[m * N_GROUPS:(m + 1) * N_GROUPS] for m in range(EXPERTS_PER_GROUP)]
    s = [sel[m * N_GROUPS:(m + 1) * N_GROUPS] for m in range(EXPERTS_PER_GROUP)]
    one = jnp.ones_like(s[0])
    zero = jnp.zeros_like(s[0])
    chosen = []
    for m in range(EXPERTS_PER_GROUP):
        rank = zero
        for j in range(EXPERTS_PER_GROUP):
            if j == m:
                continue
            beats = (s[j] >= s[m]) if j < m else (s[j] > s[m])
            rank = rank + jnp.where(beats, one, zero)
        chosen.append(rank < 2.0)
    group_score = zero
    for m in range(EXPERTS_PER_GROUP):
        group_score = group_score + jnp.where(chosen[m], s[m], zero)
    gid = lax.broadcasted_iota(jnp.int32, group_score.shape, 0)
    gmax = jnp.max(group_score, axis=0, keepdims=True)
    gbest = jnp.min(jnp.where(group_score == gmax, gid, N_GROUPS), axis=0, keepdims=True)
    in_best = gid == gbest
    picked = [chosen[m] & in_best for m in range(EXPERTS_PER_GROUP)]
    before = zero
    wa = zero
    wb = zero
    ma = zero
    mb = zero
    for m in range(EXPERTS_PER_GROUP):
        is_a = picked[m] & (before == 0.0)
        is_b = picked[m] & (before == 1.0)
        wa = wa + jnp.where(is_a, p[m], zero)
        wb = wb + jnp.where(is_b, p[m], zero)
        ma = ma + jnp.where(is_a, float(m), 0.0)
        mb = mb + jnp.where(is_b, float(m), 0.0)
        before = before + jnp.where(chosen[m], one, zero)
    wa = jnp.sum(wa, axis=0, keepdims=True)
    wb = jnp.sum(wb, axis=0, keepdims=True)
    ma = jnp.sum(ma, axis=0, keepdims=True)
    mb = jnp.sum(mb, axis=0, keepdims=True)
    denom = wa + wb
    w_lo = wa / denom
    w_hi = wb / denom
    pair = jnp.where(ma == 0.0, jnp.where(mb == 1.0, 0.0, mb),
                     jnp.where(ma == 1.0, jnp.where(mb == 2.0, 1.0, 4.0), 5.0))
    keep = pair == 0.0
    return (gbest * N_PAIRS + pair.astype(jnp.int32),
            jnp.where(keep, w_lo, w_hi), jnp.where(keep, w_hi, w_lo))


CLASS_ROWS = 64
META_LANES = 128


def _router_kernel(x_ref, rw_ref, rb_ref, tri_ref, dest_ref, wtok_ref, meta_ref, cls_s, rank_s, cnt_s):
    phase = pl.program_id(0)
    i = pl.program_id(1)
    bm = ROUTE_BM
    reps = bm // META_LANES
    cid = lax.broadcasted_iota(jnp.int32, (CLASS_ROWS, bm), 0)

    @pl.when((phase == 0) & (i == 0))
    def _():
        cnt_s[...] = jnp.zeros_like(cnt_s)

    @pl.when(phase == 0)
    def _():
        cls, w_lo, w_hi = _route_tile(x_ref[...], rw_ref[...], rb_ref[...])
        cls_s[i] = cls
        w_rows = jnp.concatenate([w_lo, w_hi, jnp.zeros((META_LANES - 2, bm), F32)], axis=0)
        wtok_ref[...] = w_rows.T
        onehot = (cid == cls).astype(BF16)
        before = jnp.dot(onehot, tri_ref[...], preferred_element_type=F32)
        carry = jnp.concatenate([cnt_s[...]] * reps, axis=1)
        rank = jnp.sum(jnp.where(cid == cls, before + carry, 0.0), axis=0, keepdims=True)
        rank_s[i] = rank.astype(jnp.int32)
        cnt_s[...] = cnt_s[...] + jnp.dot(onehot, jnp.ones((bm, META_LANES), BF16), preferred_element_type=F32)

    @pl.when(phase == 1)
    def _():
        cnt = cnt_s[...]
        nblk = jnp.floor((cnt + (MOE_BLOCK - 1.0)) * (1.0 / MOE_BLOCK))
        rr = lax.broadcasted_iota(jnp.int32, (CLASS_ROWS, CLASS_ROWS), 0)
        cc = lax.broadcasted_iota(jnp.int32, (CLASS_ROWS, CLASS_ROWS), 1)
        lower = (cc < rr).astype(BF16)
        start_blk = jnp.dot(lower, nblk.astype(BF16), preferred_element_type=F32)
        cls = cls_s[i]
        start_t = jnp.concatenate([start_blk] * reps, axis=1)
        start = jnp.sum(jnp.where(cid == cls, start_t, 0.0), axis=0, keepdims=True)
        dest_ref[0] = (start * float(MOE_BLOCK)).astype(jnp.int32) + rank_s[i]

        @pl.when(i == 0)
        def _():
            end_blk = start_blk + nblk
            n_used = jnp.max(end_blk, axis=0, keepdims=True)
            blk = lax.broadcasted_iota(jnp.int32, (CLASS_ROWS, META_LANES), 1).astype(F32)
            blk = jnp.minimum(blk, n_used - 1.0)
            bcls = jnp.sum(jnp.where(end_blk <= blk, 1.0, 0.0), axis=0, keepdims=True)
            grp = jnp.zeros_like(bcls)
            for g in range(1, N_GROUPS):
                grp = grp + jnp.where(bcls >= float(g * N_PAIRS), 1.0, 0.0)
            pair = bcls - grp * float(N_PAIRS)
            slot_a = jnp.zeros_like(pair)
            slot_b = jnp.zeros_like(pair)
            for k in range(N_PAIRS):
                slot_a = jnp.where(pair == float(k), float(PAIR_SLOT_A[k]), slot_a)
                slot_b = jnp.where(pair == float(k), float(PAIR_SLOT_B[k]), slot_b)
            rows = [grp * float(EXPERTS_PER_GROUP) + slot_a, grp * float(EXPERTS_PER_GROUP) + slot_b, n_used]
            rows = rows + [jnp.zeros_like(bcls)] * (8 - len(rows))
            meta_ref[...] = jnp.concatenate(rows, axis=0).astype(jnp.int32)


def _router(x2d, rw_t, rb_t):
    t = x2d.shape[0]
    nt = t // ROUTE_BM
    tri = jnp.asarray(np.triu(np.ones((ROUTE_BM, ROUTE_BM), np.float32), 1), BF16)
    hold = lambda ph, i: ((1 - ph) * i + ph * (nt - 1), 0)
    const = lambda ph, i: (0, 0)
    return pl.pallas_call(
        _router_kernel,
        out_shape=(jax.ShapeDtypeStruct((nt, 1, ROUTE_BM), jnp.int32),
                   jax.ShapeDtypeStruct((t, META_LANES), F32),
                   jax.ShapeDtypeStruct((8, META_LANES), jnp.int32)),
        grid=(2, nt),
        in_specs=[
            pl.BlockSpec((ROUTE_BM, D_MODEL), hold),
            pl.BlockSpec((N_EXPERTS, D_MODEL), const),
            pl.BlockSpec((N_EXPERTS, 1), const),
            pl.BlockSpec((ROUTE_BM, ROUTE_BM), const),
        ],
        out_specs=(pl.BlockSpec((1, 1, ROUTE_BM), lambda ph, i: (ph * i, 0, 0)),
                   pl.BlockSpec((ROUTE_BM, META_LANES), hold),
                   pl.BlockSpec((8, META_LANES), const)),
        scratch_shapes=[
            pltpu.VMEM((nt, 1, ROUTE_BM), jnp.int32),
            pltpu.VMEM((nt, 1, ROUTE_BM), jnp.int32),
            pltpu.VMEM((CLASS_ROWS, META_LANES), F32),
        ],
        compiler_params=_cparams(("arbitrary", "arbitrary")),
        name="router",
    )(x2d, rw_t, rb_t, tri)


ROW_BM = 512


def _start_rows(make_copy):
    for r in range(ROW_BM):
        make_copy(r).start(priority=r % 2)


def _dispatch_kernel(dest_ref, x_ref, zeros_ref, xs_ref, sem):
    del zeros_ref
    base = pl.program_id(0) * ROW_BM
    _start_rows(lambda r: pltpu.make_async_copy(x_ref.at[pl.ds(r, 1), :],
                                                xs_ref.at[pl.ds(dest_ref[base + r], 1), :], sem))
    pltpu.make_async_copy(x_ref, xs_ref.at[pl.ds(0, ROW_BM), :], sem).wait()


def _dispatch(dest, x_rows, n_rows):
    t, width = x_rows.shape
    grid_spec = pltpu.PrefetchScalarGridSpec(
        num_scalar_prefetch=1,
        grid=(t // ROW_BM,),
        in_specs=[
            pl.BlockSpec((ROW_BM, width), lambda i, dest: (i, 0)),
            pl.BlockSpec(memory_space=pl.ANY),
        ],
        out_specs=pl.BlockSpec(memory_space=pl.ANY),
        scratch_shapes=[pltpu.SemaphoreType.DMA],
    )
    return pl.pallas_call(
        _dispatch_kernel,
        out_shape=jax.ShapeDtypeStruct((n_rows, width), x_rows.dtype),
        grid_spec=grid_spec,
        input_output_aliases={2: 0},
        compiler_params=_cparams(("arbitrary",)),
        name="dispatch",
    )(dest, x_rows, jnp.zeros((n_rows, width), x_rows.dtype))


EXPERT_RING = 3
EXPERT_AHEAD = EXPERT_RING - 1


def _experts_kernel(ea_ref, eb_ref, nb_ref, x_ref, wg_hbm, wu_hbm, wd_hbm, o_ref,
                    ring_g, ring_u, ring_d, sems, loads_ref, *, layer, n_blocks):
    j = pl.program_id(0)
    nb = nb_ref[0]
    experts_of = (ea_ref, eb_ref)

    def is_load(s, step):
        step = jnp.minimum(step, n_blocks - 1)
        return ((step == 0) | (experts_of[s][step] != experts_of[s][jnp.maximum(step - 1, 0)])).astype(jnp.int32)

    def weight_copies(s, step, buf):
        e = experts_of[s][jnp.minimum(step, n_blocks - 1)]
        return (pltpu.make_async_copy(wg_hbm.at[layer, e], ring_g.at[s, buf], sems.at[s, buf]),
                pltpu.make_async_copy(wu_hbm.at[layer, e], ring_u.at[s, buf], sems.at[s, buf]),
                pltpu.make_async_copy(wd_hbm.at[layer, e], ring_d.at[s, buf], sems.at[s, buf]))

    def start(s, step, load_number):
        for c in weight_copies(s, step, lax.rem(load_number - 1, EXPERT_RING)):
            c.start()

    @pl.when(j < nb)
    def _():
        bufs = []
        for s in range(2):
            loads_now = jnp.where(j == 0, 0, loads_ref[s]) + is_load(s, j)
            loads_ref[s] = loads_now
            buf = lax.rem(loads_now - 1, EXPERT_RING)
            bufs.append(buf)

            for first in range(EXPERT_AHEAD):
                ahead_loads = 1 + sum(is_load(s, k) for k in range(1, first + 1))

                @pl.when((j == 0) & (first < nb) & (is_load(s, first) == 1))
                def _(first=first, ahead_loads=ahead_loads):
                    start(s, first, ahead_loads)

            ahead = j + EXPERT_AHEAD
            ahead_loads = loads_now + sum(is_load(s, j + k) for k in range(1, EXPERT_AHEAD + 1))

            @pl.when((ahead < nb) & (is_load(s, ahead) == 1))
            def _(ahead=ahead, ahead_loads=ahead_loads):
                start(s, ahead, ahead_loads)

            @pl.when(is_load(s, j) == 1)
            def _(buf=buf):
                for c in weight_copies(s, j, buf):
                    c.wait()

        x = jnp.concatenate(_unpack_bf16_pair(x_ref[...]), axis=1).astype(BF16)
        ys = []
        for s in range(2):
            hg = jnp.dot(x, ring_g[s, bufs[s]].astype(BF16), preferred_element_type=F32)
            hu = jnp.dot(x, ring_u[s, bufs[s]].astype(BF16), preferred_element_type=F32)
            act = (hg * jax.nn.sigmoid(hg) * hu).astype(BF16)
            ys.append(jnp.dot(act, ring_d[s, bufs[s]].astype(BF16), preferred_element_type=F32))
        o_ref[...] = _pack_bf16_pair(ys[0], ys[1])

    @pl.when(j >= nb)
    def _():
        o_ref[...] = jnp.zeros_like(o_ref)


def _experts(blk_a, blk_b, n_used, xs, wg, wu, wd, layer):
    n_rows = xs.shape[0]
    n_blocks = n_rows // MOE_BLOCK
    in_row_map = lambda j, ea, eb, nb: (jnp.maximum(jnp.minimum(j, nb[0] - 1), 0), 0)
    row_map = lambda j, ea, eb, nb: (j, 0)
    grid_spec = pltpu.PrefetchScalarGridSpec(
        num_scalar_prefetch=3,
        grid=(n_blocks,),
        in_specs=[
            pl.BlockSpec((MOE_BLOCK, D_MODEL // 2), in_row_map),
            pl.BlockSpec(memory_space=pl.ANY),
            pl.BlockSpec(memory_space=pl.ANY),
            pl.BlockSpec(memory_space=pl.ANY),
        ],
        out_specs=pl.BlockSpec((MOE_BLOCK, D_MODEL), row_map),
        scratch_shapes=[
            pltpu.VMEM((2, EXPERT_RING, D_MODEL, D_EXPERT), F32),
            pltpu.VMEM((2, EXPERT_RING, D_MODEL, D_EXPERT), F32),
            pltpu.VMEM((2, EXPERT_RING, D_EXPERT, D_MODEL), F32),
            pltpu.SemaphoreType.DMA((2, EXPERT_RING)),
            pltpu.SMEM((2,), jnp.int32),
        ],
    )
    return pl.pallas_call(
        functools.partial(_experts_kernel, layer=layer, n_blocks=n_blocks),
        out_shape=jax.ShapeDtypeStruct((n_rows, D_MODEL), U32),
        grid_spec=grid_spec,
        compiler_params=_cparams(("arbitrary",)),
        name="experts",
    )(blk_a, blk_b, n_used, xs, wg, wu, wd)


def _combine_ln_kernel(dest_ref, x_ref, wtok_ref, lg_ref, lb_ref, ys_ref, o_ref, obf_ref, ybuf, sem):
    i = pl.program_id(0)
    slot = i % 2

    def start_tile(tile, to_slot):
        base = tile * ROW_BM
        _start_rows(lambda r: pltpu.make_async_copy(ys_ref.at[pl.ds(dest_ref[base + r], 1), :],
                                                    ybuf.at[to_slot, pl.ds(r, 1), :], sem.at[to_slot]))

    @pl.when(i == 0)
    def _():
        start_tile(0, 0)

    @pl.when(i + 1 < pl.num_programs(0))
    def _():
        start_tile(i + 1, 1 - slot)

    pltpu.make_async_copy(ys_ref.at[pl.ds(0, ROW_BM), :], ybuf.at[slot], sem.at[slot]).wait()
    w = wtok_ref[...]
    y_a, y_b = _unpack_bf16_pair(ybuf[slot])
    y = w[:, 0:1] * y_a + w[:, 1:2] * y_b
    z = DEEPNORM_ALPHA * x_ref[...] + y
    out = _layernorm_rows(z, lg_ref[...], lb_ref[...])
    o_ref[...] = out
    obf_ref[...] = out.astype(BF16)


def _combine_ln(dest, x2d, wtok, ys, lg, lb):
    t = x2d.shape[0]
    const = lambda i, dest: (0, 0)
    grid_spec = pltpu.PrefetchScalarGridSpec(
        num_scalar_prefetch=1,
        grid=(t // ROW_BM,),
        in_specs=[
            pl.BlockSpec((ROW_BM, D_MODEL), lambda i, dest: (i, 0)),
            pl.BlockSpec((ROW_BM, META_LANES), lambda i, dest: (i, 0)),
            pl.BlockSpec((1, D_MODEL), const),
            pl.BlockSpec((1, D_MODEL), const),
            pl.BlockSpec(memory_space=pl.ANY),
        ],
        out_specs=(pl.BlockSpec((ROW_BM, D_MODEL), lambda i, dest: (i, 0)),
                   pl.BlockSpec((ROW_BM, D_MODEL), lambda i, dest: (i, 0))),
        scratch_shapes=[pltpu.VMEM((2, ROW_BM, D_MODEL), U32), pltpu.SemaphoreType.DMA((2,))],
    )
    return pl.pallas_call(
        _combine_ln_kernel,
        out_shape=(jax.ShapeDtypeStruct((t, D_MODEL), F32), jax.ShapeDtypeStruct((t, D_MODEL), BF16)),
        grid_spec=grid_spec,
        compiler_params=_cparams(("arbitrary",)),
        name="combine_ln",
    )(dest, x2d, wtok, lg, lb, ys)


def _moe(x2d, x_packed, rw_t, rb_t, wg, wu, wd, layer, lg, lb):
    t = x2d.shape[0]
    n_blocks = (t + N_CLASSES * (MOE_BLOCK - 1)) // MOE_BLOCK
    assert n_blocks <= META_LANES
    dest3, wtok, meta = _router(x2d, rw_t, rb_t)
    dest = dest3.reshape(t)
    xs = _dispatch(dest, x_packed, n_blocks * MOE_BLOCK)
    ys = _experts(meta[0, :n_blocks], meta[1, :n_blocks], meta[2, :1], xs, wg, wu, wd, layer)
    return _combine_ln(dest, x2d, wtok, ys, lg, lb)


def _rope_tables(seq):
    half = RET_DK // 2
    pos = jnp.arange(seq, dtype=F32)
    freqs = ROPE_BASE ** (-jnp.arange(half, dtype=F32) / half)
    ang = pos[:, None] * freqs[None, :]
    return jnp.cos(ang), jnp.sin(ang)


def kernel(x, w_in, ret_decay_logit, w_ret_o, na_rpb, w_na_o, w_out, ln_mix_g, ln_mix_b, router_w, router_bias,
           w_exp_gate, w_exp_up, w_exp_down, ln_ffn_g, ln_ffn_b):
    b, s, d = x.shape
    depth = w_in.shape[0]
    t = b * s
    rows = s // GRID_W
    cos, sin = _rope_tables(s)
    rw_t = router_w.astype(F32).T.reshape(N_GROUPS, EXPERTS_PER_GROUP, d).transpose(1, 0, 2).reshape(N_EXPERTS, d)
    rb_t = router_bias.astype(F32).reshape(N_GROUPS, EXPERTS_PER_GROUP).T.reshape(N_EXPERTS, 1)
    x2d = x.reshape(t, d)
    x_bf = x2d.astype(BF16)
    w_in = w_in.astype(F32)
    for l in range(depth):
        qk = _inproj(x_bf, w_in, l, cos, sin, s, "rotary").reshape(b, s, -1)
        vn = _inproj(x_bf, w_in, l, cos, sin, s, "scale").reshape(b, s, -1)
        gr = _inproj(x_bf, w_in, l, cos, sin, s, "silu").reshape(b, s, -1)
        gates = _inproj(x_bf, w_in, l, cos, sin, s, "sigmoid")
        ret = _retention(qk, vn, gr, ret_decay_logit[l].astype(F32))
        na = _na(vn, na_rpb[l].astype(F32).reshape(-1))
        x2d, x_packed = _mixout(ret.reshape(t, -1), na.reshape(t, -1), gates, x2d,
                                w_ret_o[l].astype(BF16), w_na_o[l].astype(BF16), w_out[l].astype(BF16),
                                ln_mix_g[l].reshape(1, d).astype(F32), ln_mix_b[l].reshape(1, d).astype(F32))
        x2d, x_bf = _moe(x2d, x_packed, rw_t, rb_t, w_exp_gate, w_exp_up, w_exp_down, l,
                         ln_ffn_g[l].reshape(1, d).astype(F32), ln_ffn_b[l].reshape(1, d).astype(F32))
    return x2d.reshape(b, s, d)
```

```python
import functools

import numpy as np
import jax
import jax.numpy as jnp
from jax import lax
from jax.experimental import pallas as pl
from jax.experimental.pallas import tpu as pltpu

F32 = jnp.float32
BF16 = jnp.bfloat16

D_MODEL = 1024
GRID_W = 64
RET_HEADS = 4
RET_DK = 256
RET_DV = 512
ROPE_BASE = 10000.0
NA_HEADS = 16
NA_HD = 64
NA_KH = 8
NA_KW = 16
N_EXPERTS = 32
N_GROUPS = 8
EXPERTS_PER_GROUP = 4
D_EXPERT = 512
LN_EPS = 1e-5
GN_EPS = 1e-5
MODEL_DEPTH = 4
DEEPNORM_ALPHA = (2 * MODEL_DEPTH) ** 0.25

D_IN = 11264

RET_CHUNK = 256
NA_QROWS = 4
NA_KROWS = 12
NA_NEG = -1e30
MOE_BLOCK = 256
N_PAIRS = 6
PAIR_SLOT_A = (0, 2, 2, 3, 3, 3)
PAIR_SLOT_B = (1, 1, 0, 0, 1, 2)
N_CLASSES = N_GROUPS * N_PAIRS
VMEM_LIMIT = 56 * 1024 * 1024


def _cparams(sem):
    return pltpu.CompilerParams(dimension_semantics=sem, vmem_limit_bytes=VMEM_LIMIT)


U32 = jnp.uint32
_HIGH_HALF = 0xFFFF0000


def _pack_bf16_pair(hi, lo):
    hi_bits = lax.bitcast_convert_type(hi.astype(jnp.bfloat16).astype(F32), U32)
    lo_bits = lax.bitcast_convert_type(lo.astype(jnp.bfloat16).astype(F32), U32)
    return (hi_bits & jnp.uint32(_HIGH_HALF)) | (lo_bits >> 16)


def _unpack_bf16_pair(packed):
    hi = lax.bitcast_convert_type(packed & jnp.uint32(_HIGH_HALF), F32)
    lo = lax.bitcast_convert_type(packed << 16, F32)
    return hi, lo


IN_BM = 1024
IN_CHUNK = 256
LOG2E = 1.4426950408889634

IN_GROUPS = {
    "rotary": (2048, (0,)),
    "scale": (1024, (2, 3, 6, 7, 8)),
    "silu": (2048, (2,)),
    "sigmoid": (1024, (9, 10)),
}
VN_Q = 2048
VN_K = 3072
VN_V = 4096


def _inproj_kernel(x_ref, w_ref, cos_ref, sin_ref, o_ref, wbf_ref, *, kind, bn):
    j = pl.program_id(0)

    @pl.when(pl.program_id(1) == 0)
    def _():
        wbf_ref[...] = w_ref[0].astype(BF16)

    x = x_ref[...]
    half = RET_DK // 2
    for c in range(bn // IN_CHUNK):
        lo = c * IN_CHUNK
        acc = jnp.dot(x, wbf_ref[:, lo:lo + IN_CHUNK], preferred_element_type=F32)
        if kind == "rotary":
            scale = 1.0 if lo < RET_HEADS * RET_DK else RET_DK ** -0.5
            cos = cos_ref[...] * scale
            sin = sin_ref[...] * scale
            x1 = acc[:, :half]
            x2 = acc[:, half:]
            o_ref[:, lo:lo + half] = (x1 * cos - x2 * sin).astype(o_ref.dtype)
            o_ref[:, lo + half:lo + IN_CHUNK] = (x1 * sin + x2 * cos).astype(o_ref.dtype)
        elif kind == "scale":
            scale = jnp.where(j == 2, NA_HD ** -0.5 * LOG2E, 1.0)
            o_ref[:, lo:lo + IN_CHUNK] = (acc * scale).astype(o_ref.dtype)
        elif kind == "silu":
            o_ref[:, lo:lo + IN_CHUNK] = (acc * jax.nn.sigmoid(acc)).astype(o_ref.dtype)
        else:
            o_ref[:, lo:lo + IN_CHUNK] = jax.nn.sigmoid(acc).astype(o_ref.dtype)


def _inproj(x_bf, w_in, layer, cos, sin, seq, kind):
    t = x_bf.shape[0]
    pos_blocks = seq // IN_BM
    bn, blocks = IN_GROUPS[kind]

    def wcol(j):
        col = blocks[0]
        for k in range(1, len(blocks)):
            col = jnp.where(j == k, blocks[k], col)
        return col

    pos_map = (lambda j, i: (i % pos_blocks, 0)) if kind == "rotary" else (lambda j, i: (0, 0))
    return pl.pallas_call(
        functools.partial(_inproj_kernel, kind=kind, bn=bn),
        out_shape=jax.ShapeDtypeStruct((t, len(blocks) * bn), BF16),
        grid=(len(blocks), t // IN_BM),
        in_specs=[
            pl.BlockSpec((IN_BM, D_MODEL), lambda j, i: (i, 0)),
            pl.BlockSpec((1, D_MODEL, bn), lambda j, i: (layer, 0, wcol(j))),
            pl.BlockSpec((IN_BM, RET_DK // 2), pos_map),
            pl.BlockSpec((IN_BM, RET_DK // 2), pos_map),
        ],
        out_specs=pl.BlockSpec((IN_BM, bn), lambda j, i: (i, j)),
        scratch_shapes=[pltpu.VMEM((D_MODEL, bn), BF16)],
        compiler_params=_cparams(("arbitrary", "arbitrary")),
        name="inproj_" + kind,
    )(x_bf, w_in, cos, sin)


def _log_sigmoid(x):
    return jnp.minimum(x, 0.0) - jnp.log1p(jnp.exp(-jnp.abs(x)))


RET_GROUP = 8


def _retention_kernel(dl_ref, q_ref, k_ref, v_ref, g_ref, o_ref,
                      sf_ref, sb_ref, st_ref, stb_ref, dm_ref, qdf_ref, qdb_ref, kdf_ref, kdb_ref,
                      p_buf, qf_buf, qb_buf, out_buf, *, nc):
    c_len = RET_CHUNK
    h = pl.program_id(1)
    lgf = _log_sigmoid(jnp.full((c_len, RET_DK), dl_ref[0, h], F32))
    lgb = _log_sigmoid(jnp.full((c_len, RET_DK), dl_ref[1, h], F32))
    ri = lax.broadcasted_iota(jnp.int32, (c_len, RET_DK), 0).astype(F32)
    qdf_ref[...] = jnp.exp(lgf * (ri + 1.0))
    qdb_ref[...] = jnp.exp(lgb * (c_len - ri))
    kdf_ref[...] = jnp.exp(lgf * (c_len - 1.0 - ri))
    kdb_ref[...] = jnp.exp(lgb * ri)
    rr = lax.broadcasted_iota(jnp.int32, (c_len, c_len), 0).astype(F32)
    cc = lax.broadcasted_iota(jnp.int32, (c_len, c_len), 1).astype(F32)
    diff = rr - cc
    lgf_cc = _log_sigmoid(jnp.full((c_len, c_len), dl_ref[0, h], F32))
    lgb_cc = _log_sigmoid(jnp.full((c_len, c_len), dl_ref[1, h], F32))
    dm_ref[...] = jnp.where(diff >= 0.0, jnp.exp(lgf_cc * jnp.maximum(diff, 0.0)),
                            jnp.exp(lgb_cc * jnp.maximum(-diff, 0.0)))
    chunk_f = jnp.exp(lgf[:1, :1] * float(c_len))
    chunk_b = jnp.exp(lgb[:1, :1] * float(c_len))

    tn_dims = (((0,), (0,)), ((), ()))
    nt_dims = (((1,), (1,)), ((), ()))

    st_ref[...] = jnp.zeros_like(st_ref)
    stb_ref[...] = jnp.zeros_like(stb_ref)

    def state_body(i, carry):
        for c, all_ref, cur_ref, kdec_ref, chunk_dec in ((i, sf_ref, st_ref, kdf_ref, chunk_f),
                                                         (nc - 1 - i, sb_ref, stb_ref, kdb_ref, chunk_b)):
            off = pl.multiple_of(c * c_len, c_len)
            all_ref[c] = cur_ref[...].astype(BF16)
            kc = k_ref[0, pl.ds(off, c_len), :].astype(F32)
            vc = v_ref[0, pl.ds(off, c_len), :]
            kd = (kc * kdec_ref[...]).astype(BF16)
            upd = lax.dot_general(kd, vc, tn_dims, preferred_element_type=F32)
            cur_ref[...] = cur_ref[...] * chunk_dec + upd
        return carry

    lax.fori_loop(0, nc, state_body, 0, unroll=4)

    def out_body(g, carry):
        offs = [pl.multiple_of((g * RET_GROUP + u) * c_len, c_len) for u in range(RET_GROUP)]
        for u, off in enumerate(offs):
            qb = q_ref[0, pl.ds(off, c_len), :]
            kb = k_ref[0, pl.ds(off, c_len), :]
            qc = qb.astype(F32)
            s = lax.dot_general(qb, kb, nt_dims, preferred_element_type=F32)
            p_buf[u] = (s * dm_ref[...]).astype(BF16)
            qf_buf[u] = (qc * qdf_ref[...]).astype(BF16)
            qb_buf[u] = (qc * qdb_ref[...]).astype(BF16)
        for u, off in enumerate(offs):
            c = g * RET_GROUP + u
            out = jnp.dot(p_buf[u], v_ref[0, pl.ds(off, c_len), :], preferred_element_type=F32)
            out = out + jnp.dot(qf_buf[u], sf_ref[c], preferred_element_type=F32)
            out_buf[u] = out + jnp.dot(qb_buf[u], sb_ref[c], preferred_element_type=F32)
        for u, off in enumerate(offs):
            out = out_buf[u]
            mu = jnp.mean(out, axis=-1, keepdims=True)
            cen = out - mu
            var = jnp.mean(cen * cen, axis=-1, keepdims=True)
            y = cen * lax.rsqrt(var + GN_EPS)
            gate = g_ref[0, pl.ds(off, c_len), :].astype(F32)
            o_ref[0, pl.ds(off, c_len), :] = (gate * y).astype(o_ref.dtype)
        return carry

    lax.fori_loop(0, nc // RET_GROUP, out_body, 0)


def _retention(qk3, vn3, gr3, decay_logit):
    b, s, _ = qk3.shape
    nc = s // RET_CHUNK
    return pl.pallas_call(
        functools.partial(_retention_kernel, nc=nc),
        out_shape=jax.ShapeDtypeStruct((b, s, RET_HEADS * RET_DV), BF16),
        grid=(b, RET_HEADS),
        in_specs=[
            pl.BlockSpec(memory_space=pltpu.SMEM),
            pl.BlockSpec((1, s, RET_DK), lambda bi, h: (bi, 0, h)),
            pl.BlockSpec((1, s, RET_DK), lambda bi, h: (bi, 0, RET_HEADS + h)),
            pl.BlockSpec((1, s, RET_DV), lambda bi, h: (bi, 0, h)),
            pl.BlockSpec((1, s, RET_DV), lambda bi, h: (bi, 0, h)),
        ],
        out_specs=pl.BlockSpec((1, s, RET_DV), lambda bi, h: (bi, 0, h)),
        scratch_shapes=[
            pltpu.VMEM((nc, RET_DK, RET_DV), BF16),
            pltpu.VMEM((nc, RET_DK, RET_DV), BF16),
            pltpu.VMEM((RET_DK, RET_DV), F32),
            pltpu.VMEM((RET_DK, RET_DV), F32),
            pltpu.VMEM((RET_CHUNK, RET_CHUNK), F32),
            pltpu.VMEM((RET_CHUNK, RET_DK), F32),
            pltpu.VMEM((RET_CHUNK, RET_DK), F32),
            pltpu.VMEM((RET_CHUNK, RET_DK), F32),
            pltpu.VMEM((RET_CHUNK, RET_DK), F32),
            pltpu.VMEM((RET_GROUP, RET_CHUNK, RET_CHUNK), BF16),
            pltpu.VMEM((RET_GROUP, RET_CHUNK, RET_DK), BF16),
            pltpu.VMEM((RET_GROUP, RET_CHUNK, RET_DK), BF16),
            pltpu.VMEM((RET_GROUP, RET_CHUNK, RET_DV), F32),
        ],
        compiler_params=_cparams(("arbitrary", "arbitrary")),
        name="retention",
    )(decay_logit, qk3, qk3, vn3, gr3)


NA_ROW_OFFS = 2 * NA_KH - 1
NA_COL_OFFS = 2 * NA_KW - 1
NA_PATTERNS = 3


def _na_row_offsets(rows):
    n_tiles = rows // NA_QROWS
    offs = np.full((NA_PATTERNS, NA_QROWS, NA_KROWS), NA_ROW_OFFS, np.int32)
    for p, t in enumerate((0, 1, n_tiles - 1)):
        kstart = int(np.clip(NA_QROWS * t - NA_KH // 2, 0, rows - NA_KROWS))
        for rr in range(NA_QROWS):
            r = NA_QROWS * t + rr
            rs = int(np.clip(r - NA_KH // 2, 0, rows - NA_KH))
            for i in range(NA_KROWS):
                krow = kstart + i
                if rs <= krow < rs + NA_KH:
                    offs[p, rr, i] = krow - r + (NA_KH - 1)
    return offs


def _na_build_bias(rpb_ref, e_ref, bias_ref, head0, rows):
    wide = 2 * GRID_W
    lane = lax.broadcasted_iota(jnp.int32, (GRID_W, wide), 1)
    col = lax.broadcasted_iota(jnp.int32, (GRID_W, wide), 0)
    kcol = jnp.where(lane < GRID_W, lane, lane - GRID_W)
    cstart = jnp.clip(col - NA_KW // 2, 0, GRID_W - NA_KW)
    col_ok = (kcol >= cstart) & (kcol < cstart + NA_KW)
    coff = kcol - col + (NA_KW - 1)
    neg = jnp.full((GRID_W, wide), NA_NEG, F32)
    left = lane < GRID_W
    offs = _na_row_offsets(rows)
    for hh in range(2):
        base = (head0 + hh) * (NA_ROW_OFFS * NA_COL_OFFS)
        for ro in range(NA_ROW_OFFS):
            acc = neg
            for d in range(NA_COL_OFFS):
                acc = jnp.where(coff == d, rpb_ref[base + ro * NA_COL_OFFS + d] * LOG2E, acc)
            e_ref[hh, ro] = jnp.where(col_ok, acc, neg)
        e_ref[hh, NA_ROW_OFFS] = neg
        for p in range(NA_PATTERNS):
            for rr in range(NA_QROWS):
                for ip in range(NA_KROWS // 2):
                    tile = jnp.where(left, e_ref[hh, int(offs[p, rr, 2 * ip])], e_ref[hh, int(offs[p, rr, 2 * ip + 1])])
                    bias_ref[hh, p, rr * GRID_W:(rr + 1) * GRID_W, ip * wide:(ip + 1) * wide] = tile


NA_GROUP = 8


def _na_kernel(rpb_ref, q_ref, k_ref, v_ref, o_ref, bias_ref, e_ref, s_buf, p_buf, *, n_tiles, rows):
    nq = NA_QROWS * GRID_W
    nk = NA_KROWS * GRID_W
    nt_dims = (((1,), (1,)), ((), ()))
    first = lax.broadcasted_iota(jnp.int32, (nq, 2 * NA_HD), 1) < NA_HD
    kfirst = lax.broadcasted_iota(jnp.int32, (nk, 2 * NA_HD), 1) < NA_HD

    @pl.when(pl.program_id(1) == 0)
    def _():
        _na_build_bias(rpb_ref, e_ref, bias_ref, 2 * pl.program_id(0), rows)

    def body(g, carry):
        offs = []
        for u in range(NA_GROUP):
            t = g * NA_GROUP + u
            qoff = pl.multiple_of(t * nq, nq)
            krow0 = jnp.clip(NA_QROWS * t - NA_KH // 2, 0, rows - NA_KROWS)
            koff = pl.multiple_of(krow0 * GRID_W, nq)
            pat = jnp.where(t == 0, 0, jnp.where(t == n_tiles - 1, 2, 1))
            offs.append((qoff, koff, pat))
        for u, (qoff, koff, pat) in enumerate(offs):
            q2 = q_ref[0, pl.ds(qoff, nq), :]
            k2 = k_ref[0, pl.ds(koff, nk), :]
            for hh in range(2):
                qm = jnp.where(first if hh == 0 else jnp.logical_not(first), q2, jnp.zeros_like(q2))
                s_buf[2 * u + hh] = lax.dot_general(qm, k2, nt_dims, preferred_element_type=F32) + bias_ref[hh, pat]
        for n in range(2 * NA_GROUP):
            s = s_buf[n]
            p_buf[n] = jnp.exp2(s - jnp.max(s, axis=-1, keepdims=True)).astype(BF16)
        for u, (qoff, koff, pat) in enumerate(offs):
            v2 = v_ref[0, pl.ds(koff, nk), :]
            outs = []
            for hh in range(2):
                vv = jnp.where(kfirst if hh == 0 else jnp.logical_not(kfirst), v2, jnp.ones_like(v2))
                o = jnp.dot(p_buf[2 * u + hh], vv, preferred_element_type=F32)
                outs.append(o / pltpu.roll(o, NA_HD, axis=1))
            o_ref[0, pl.ds(qoff, nq), :] = jnp.where(first, outs[0], outs[1]).astype(o_ref.dtype)
        return carry

    lax.fori_loop(0, n_tiles // NA_GROUP, body, 0)


def _na(vn3, rpb_flat):
    b, s, _ = vn3.shape
    rows = s // GRID_W
    n_tiles = rows // NA_QROWS
    w2 = 2 * NA_HD
    nq = NA_QROWS * GRID_W
    nk = NA_KROWS * GRID_W
    return pl.pallas_call(
        functools.partial(_na_kernel, n_tiles=n_tiles, rows=rows),
        out_shape=jax.ShapeDtypeStruct((b, s, NA_HEADS * NA_HD), BF16),
        grid=(NA_HEADS // 2, b),
        in_specs=[
            pl.BlockSpec(memory_space=pltpu.SMEM),
            pl.BlockSpec((1, s, w2), lambda hp, bi: (bi, 0, VN_Q // w2 + hp)),
            pl.BlockSpec((1, s, w2), lambda hp, bi: (bi, 0, VN_K // w2 + hp)),
            pl.BlockSpec((1, s, w2), lambda hp, bi: (bi, 0, VN_V // w2 + hp)),
        ],
        out_specs=pl.BlockSpec((1, s, w2), lambda hp, bi: (bi, 0, hp)),
        scratch_shapes=[
            pltpu.VMEM((2, NA_PATTERNS, nq, nk), F32),
            pltpu.VMEM((2, NA_ROW_OFFS + 1, GRID_W, 2 * GRID_W), F32),
            pltpu.VMEM((2 * NA_GROUP, nq, nk), F32),
            pltpu.VMEM((2 * NA_GROUP, nq, nk), BF16),
        ],
        compiler_params=_cparams(("arbitrary", "arbitrary")),
        name="natten",
    )(rpb_flat, vn3, vn3, vn3)


MIX_BM = 512
MIX_SUB = 256


def _layernorm_rows(z, g, b):
    mu = jnp.mean(z, axis=-1, keepdims=True)
    cen = z - mu
    var = jnp.mean(cen * cen, axis=-1, keepdims=True)
    return cen * lax.rsqrt(var + LN_EPS) * g + b


def _mixout_kernel(ret_ref, na_ref, g1_ref, g2_ref, x_ref, wr_ref, wn_ref, wo_ref, lg_ref, lb_ref, o_ref, op_ref):
    for r in range(0, MIX_BM, MIX_SUB):
        rows = slice(r, r + MIX_SUB)
        y_ret = jnp.dot(ret_ref[rows, :], wr_ref[...], preferred_element_type=F32)
        y_na = jnp.dot(na_ref[rows, :], wn_ref[...], preferred_element_type=F32)
        merged = g1_ref[rows, :].astype(F32) * y_ret + g2_ref[rows, :].astype(F32) * y_na
        mix = jnp.dot(merged.astype(BF16), wo_ref[...], preferred_element_type=F32)
        z = DEEPNORM_ALPHA * x_ref[rows, :] + mix
        out = _layernorm_rows(z, lg_ref[...], lb_ref[...])
        o_ref[rows, :] = out
        op_ref[rows, :] = _pack_bf16_pair(out[:, :D_MODEL // 2], out[:, D_MODEL // 2:])


def _mixout(ret2d, na2d, gates2d, x2d, wr, wn, wo, lg, lb):
    t = x2d.shape[0]
    const = lambda i: (0, 0)
    return pl.pallas_call(
        _mixout_kernel,
        out_shape=(jax.ShapeDtypeStruct((t, D_MODEL), F32), jax.ShapeDtypeStruct((t, D_MODEL // 2), U32)),
        grid=(t // MIX_BM,),
        in_specs=[
            pl.BlockSpec((MIX_BM, RET_HEADS * RET_DV), lambda i: (i, 0)),
            pl.BlockSpec((MIX_BM, NA_HEADS * NA_HD), lambda i: (i, 0)),
            pl.BlockSpec((MIX_BM, D_MODEL), lambda i: (i, 0)),
            pl.BlockSpec((MIX_BM, D_MODEL), lambda i: (i, 1)),
            pl.BlockSpec((MIX_BM, D_MODEL), lambda i: (i, 0)),
            pl.BlockSpec((RET_HEADS * RET_DV, D_MODEL), const),
            pl.BlockSpec((NA_HEADS * NA_HD, D_MODEL), const),
            pl.BlockSpec((D_MODEL, D_MODEL), const),
            pl.BlockSpec((1, D_MODEL), const),
            pl.BlockSpec((1, D_MODEL), const),
        ],
        out_specs=(pl.BlockSpec((MIX_BM, D_MODEL), lambda i: (i, 0)),
                   pl.BlockSpec((MIX_BM, D_MODEL // 2), lambda i: (i, 0))),
        compiler_params=_cparams(("arbitrary",)),
        name="mixout",
    )(ret2d, na2d, gates2d, gates2d, x2d, wr, wn, wo, lg, lb)


ROUTE_BM = 512


def _route_tile(x, rw, rb):
    nt_dims = (((1,), (1,)), ((), ()))
    x_hi = x.astype(BF16)
    x_lo = (x - x_hi.astype(F32)).astype(BF16)
    rw_hi = rw.astype(BF16)
    rw_lo = (rw - rw_hi.astype(F32)).astype(BF16)
    both = lax.dot_general(jnp.concatenate([rw_hi, rw_lo], axis=0), x_hi, nt_dims, preferred_element_type=F32)
    logits = (both[:N_EXPERTS] + both[N_EXPERTS:]
              + lax.dot_general(rw_hi, x_lo, nt_dims, preferred_element_type=F32))
    scores = jax.nn.sigmoid(logits)
    sel = scores + rb
    p = [scores[m * N_GROUPS:(m + 1) * N_GROUPS] for m in range(EXPERTS_PER_GROUP)]
    s = [sel[m * N_GROUPS:(m + 1) * N_GROUPS] for m in range(EXPERTS_PER_GROUP)]
    one = jnp.ones_like(s[0])
    zero = jnp.zeros_like(s[0])
    chosen = []
    for m in range(EXPERTS_PER_GROUP):
        rank = zero
        for j in range(EXPERTS_PER_GROUP):
            if j == m:
                continue
            beats = (s[j] >= s[m]) if j < m else (s[j] > s[m])
            rank = rank + jnp.where(beats, one, zero)
        chosen.append(rank < 2.0)
    group_score = zero
    for m in range(EXPERTS_PER_GROUP):
        group_score = group_score + jnp.where(chosen[m], s[m], zero)
    gid = lax.broadcasted_iota(jnp.int32, group_score.shape, 0)
    gmax = jnp.max(group_score, axis=0, keepdims=True)
    gbest = jnp.min(jnp.where(group_score == gmax, gid, N_GROUPS), axis=0, keepdims=True)
    in_best = gid == gbest
    picked = [chosen[m] & in_best for m in range(EXPERTS_PER_GROUP)]
    before = zero
    wa = zero
    wb = zero
    ma = zero
    mb = zero
    for m in range(EXPERTS_PER_GROUP):
        is_a = picked[m] & (before == 0.0)
        is_b = picked[m] & (before == 1.0)
        wa = wa + jnp.where(is_a, p[m], zero)
        wb = wb + jnp.where(is_b, p[m], zero)
        ma = ma + jnp.where(is_a, float(m), 0.0)
        mb = mb + jnp.where(is_b, float(m), 0.0)
        before = before + jnp.where(chosen[m], one, zero)
    wa = jnp.sum(wa, axis=0, keepdims=True)
    wb = jnp.sum(wb, axis=0, keepdims=True)
    ma = jnp.sum(ma, axis=0, keepdims=True)
    mb = jnp.sum(mb, axis=0, keepdims=True)
    denom = wa + wb
    w_lo = wa / denom
    w_hi = wb / denom
    pair = jnp.where(ma == 0.0, jnp.where(mb == 1.0, 0.0, mb),
                     jnp.where(ma == 1.0, jnp.where(mb == 2.0, 1.0, 4.0), 5.0))
    keep = pair == 0.0
    return (gbest * N_PAIRS + pair.astype(jnp.int32),
            jnp.where(keep, w_lo, w_hi), jnp.where(keep, w_hi, w_lo))


CLASS_ROWS = 64
META_LANES = 128


def _router_kernel(x_ref, rw_ref, rb_ref, tri_ref, dest_ref, wtok_ref, meta_ref, cls_s, rank_s, cnt_s):
    phase = pl.program_id(0)
    i = pl.program_id(1)
    bm = ROUTE_BM
    reps = bm // META_LANES
    cid = lax.broadcasted_iota(jnp.int32, (CLASS_ROWS, bm), 0)

    @pl.when((phase == 0) & (i == 0))
    def _():
        cnt_s[...] = jnp.zeros_like(cnt_s)

    @pl.when(phase == 0)
    def _():
        cls, w_lo, w_hi = _route_tile(x_ref[...], rw_ref[...], rb_ref[...])
        cls_s[i] = cls
        w_rows = jnp.concatenate([w_lo, w_hi, jnp.zeros((META_LANES - 2, bm), F32)], axis=0)
        wtok_ref[...] = w_rows.T
        onehot = (cid == cls).astype(BF16)
        before = jnp.dot(onehot, tri_ref[...], preferred_element_type=F32)
        carry = jnp.concatenate([cnt_s[...]] * reps, axis=1)
        rank = jnp.sum(jnp.where(cid == cls, before + carry, 0.0), axis=0, keepdims=True)
        rank_s[i] = rank.astype(jnp.int32)
        cnt_s[...] = cnt_s[...] + jnp.dot(onehot, jnp.ones((bm, META_LANES), BF16), preferred_element_type=F32)

    @pl.when(phase == 1)
    def _():
        cnt = cnt_s[...]
        nblk = jnp.floor((cnt + (MOE_BLOCK - 1.0)) * (1.0 / MOE_BLOCK))
        rr = lax.broadcasted_iota(jnp.int32, (CLASS_ROWS, CLASS_ROWS), 0)
        cc = lax.broadcasted_iota(jnp.int32, (CLASS_ROWS, CLASS_ROWS), 1)
        lower = (cc < rr).astype(BF16)
        start_blk = jnp.dot(lower, nblk.astype(BF16), preferred_element_type=F32)
        cls = cls_s[i]
        start_t = jnp.concatenate([start_blk] * reps, axis=1)
        start = jnp.sum(jnp.where(cid == cls, start_t, 0.0), axis=0, keepdims=True)
        dest_ref[0] = (start * float(MOE_BLOCK)).astype(jnp.int32) + rank_s[i]

        @pl.when(i == 0)
        def _():
            end_blk = start_blk + nblk
            n_used = jnp.max(end_blk, axis=0, keepdims=True)
            blk = lax.broadcasted_iota(jnp.int32, (CLASS_ROWS, META_LANES), 1).astype(F32)
            blk = jnp.minimum(blk, n_used - 1.0)
            bcls = jnp.sum(jnp.where(end_blk <= blk, 1.0, 0.0), axis=0, keepdims=True)
            grp = jnp.zeros_like(bcls)
            for g in range(1, N_GROUPS):
                grp = grp + jnp.where(bcls >= float(g * N_PAIRS), 1.0, 0.0)
            pair = bcls - grp * float(N_PAIRS)
            slot_a = jnp.zeros_like(pair)
            slot_b = jnp.zeros_like(pair)
            for k in range(N_PAIRS):
                slot_a = jnp.where(pair == float(k), float(PAIR_SLOT_A[k]), slot_a)
                slot_b = jnp.where(pair == float(k), float(PAIR_SLOT_B[k]), slot_b)
            rows = [grp * float(EXPERTS_PER_GROUP) + slot_a, grp * float(EXPERTS_PER_GROUP) + slot_b, n_used]
            rows = rows + [jnp.zeros_like(bcls)] * (8 - len(rows))
            meta_ref[...] = jnp.concatenate(rows, axis=0).astype(jnp.int32)


def _router(x2d, rw_t, rb_t):
    t = x2d.shape[0]
    nt = t // ROUTE_BM
    tri = jnp.asarray(np.triu(np.ones((ROUTE_BM, ROUTE_BM), np.float32), 1), BF16)
    hold = lambda ph, i: ((1 - ph) * i + ph * (nt - 1), 0)
    const = lambda ph, i: (0, 0)
    return pl.pallas_call(
        _router_kernel,
        out_shape=(jax.ShapeDtypeStruct((nt, 1, ROUTE_BM), jnp.int32),
                   jax.ShapeDtypeStruct((t, META_LANES), F32),
                   jax.ShapeDtypeStruct((8, META_LANES), jnp.int32)),
        grid=(2, nt),
        in_specs=[
            pl.BlockSpec((ROUTE_BM, D_MODEL), hold),
            pl.BlockSpec((N_EXPERTS, D_MODEL), const),
            pl.BlockSpec((N_EXPERTS, 1), const),
            pl.BlockSpec((ROUTE_BM, ROUTE_BM), const),
        ],
        out_specs=(pl.BlockSpec((1, 1, ROUTE_BM), lambda ph, i: (ph * i, 0, 0)),
                   pl.BlockSpec((ROUTE_BM, META_LANES), hold),
                   pl.BlockSpec((8, META_LANES), const)),
        scratch_shapes=[
            pltpu.VMEM((nt, 1, ROUTE_BM), jnp.int32),
            pltpu.VMEM((nt, 1, ROUTE_BM), jnp.int32),
            pltpu.VMEM((CLASS_ROWS, META_LANES), F32),
        ],
        compiler_params=_cparams(("arbitrary", "arbitrary")),
        name="router",
    )(x2d, rw_t, rb_t, tri)


ROW_BM = 512


def _start_rows(make_copy):
    for r in range(ROW_BM):
        make_copy(r).start(priority=r % 2)


def _dispatch_kernel(dest_ref, x_ref, zeros_ref, xs_ref, sem):
    del zeros_ref
    base = pl.program_id(0) * ROW_BM
    _start_rows(lambda r: pltpu.make_async_copy(x_ref.at[pl.ds(r, 1), :],
                                                xs_ref.at[pl.ds(dest_ref[base + r], 1), :], sem))
    pltpu.make_async_copy(x_ref, xs_ref.at[pl.ds(0, ROW_BM), :], sem).wait()


def _dispatch(dest, x_rows, n_rows):
    t, width = x_rows.shape
    grid_spec = pltpu.PrefetchScalarGridSpec(
        num_scalar_prefetch=1,
        grid=(t // ROW_BM,),
        in_specs=[
            pl.BlockSpec((ROW_BM, width), lambda i, dest: (i, 0)),
            pl.BlockSpec(memory_space=pl.ANY),
        ],
        out_specs=pl.BlockSpec(memory_space=pl.ANY),
        scratch_shapes=[pltpu.SemaphoreType.DMA],
    )
    return pl.pallas_call(
        _dispatch_kernel,
        out_shape=jax.ShapeDtypeStruct((n_rows, width), x_rows.dtype),
        grid_spec=grid_spec,
        input_output_aliases={2: 0},
        compiler_params=_cparams(("arbitrary",)),
        name="dispatch",
    )(dest, x_rows, jnp.zeros((n_rows, width), x_rows.dtype))


EXPERT_RING = 3
EXPERT_AHEAD = EXPERT_RING - 1


def _experts_kernel(ea_ref, eb_ref, nb_ref, x_ref, wg_hbm, wu_hbm, wd_hbm, o_ref,
                    ring_g, ring_u, ring_d, sems, loads_ref, hg_buf, hu_buf, act_buf, *, layer, n_blocks):
    j = pl.program_id(0)
    nb = nb_ref[0]
    experts_of = (ea_ref, eb_ref)

    def is_load(s, step):
        step = jnp.minimum(step, n_blocks - 1)
        return ((step == 0) | (experts_of[s][step] != experts_of[s][jnp.maximum(step - 1, 0)])).astype(jnp.int32)

    def weight_copies(s, step, buf):
        e = experts_of[s][jnp.minimum(step, n_blocks - 1)]
        return (pltpu.make_async_copy(wg_hbm.at[layer, e], ring_g.at[s, buf], sems.at[s, buf]),
                pltpu.make_async_copy(wu_hbm.at[layer, e], ring_u.at[s, buf], sems.at[s, buf]),
                pltpu.make_async_copy(wd_hbm.at[layer, e], ring_d.at[s, buf], sems.at[s, buf]))

    def start(s, step, load_number):
        for c in weight_copies(s, step, lax.rem(load_number - 1, EXPERT_RING)):
            c.start()

    @pl.when(j < nb)
    def _():
        bufs = []
        for s in range(2):
            loads_now = jnp.where(j == 0, 0, loads_ref[s]) + is_load(s, j)
            loads_ref[s] = loads_now
            buf = lax.rem(loads_now - 1, EXPERT_RING)
            bufs.append(buf)

            for first in range(EXPERT_AHEAD):
                ahead_loads = 1 + sum(is_load(s, k) for k in range(1, first + 1))

                @pl.when((j == 0) & (first < nb) & (is_load(s, first) == 1))
                def _(first=first, ahead_loads=ahead_loads):
                    start(s, first, ahead_loads)

            ahead = j + EXPERT_AHEAD
            ahead_loads = loads_now + sum(is_load(s, j + k) for k in range(1, EXPERT_AHEAD + 1))

            @pl.when((ahead < nb) & (is_load(s, ahead) == 1))
            def _(ahead=ahead, ahead_loads=ahead_loads):
                start(s, ahead, ahead_loads)

            @pl.when(is_load(s, j) == 1)
            def _(buf=buf):
                for c in weight_copies(s, j, buf):
                    c.wait()

        x = jnp.concatenate(_unpack_bf16_pair(x_ref[...]), axis=1).astype(BF16)
        for s in range(2):
            hg_buf[s] = jnp.dot(x, ring_g[s, bufs[s]].astype(BF16), preferred_element_type=F32)
            hu_buf[s] = jnp.dot(x, ring_u[s, bufs[s]].astype(BF16), preferred_element_type=F32)
        for s in range(2):
            hg = hg_buf[s]
            act_buf[s] = (hg * jax.nn.sigmoid(hg) * hu_buf[s]).astype(BF16)
        ys = [jnp.dot(act_buf[s], ring_d[s, bufs[s]].astype(BF16), preferred_element_type=F32) for s in range(2)]
        o_ref[...] = _pack_bf16_pair(ys[0], ys[1])

    @pl.when(j >= nb)
    def _():
        o_ref[...] = jnp.zeros_like(o_ref)


def _experts(blk_a, blk_b, n_used, xs, wg, wu, wd, layer):
    n_rows = xs.shape[0]
    n_blocks = n_rows // MOE_BLOCK
    in_row_map = lambda j, ea, eb, nb: (jnp.maximum(jnp.minimum(j, nb[0] - 1), 0), 0)
    row_map = lambda j, ea, eb, nb: (j, 0)
    grid_spec = pltpu.PrefetchScalarGridSpec(
        num_scalar_prefetch=3,
        grid=(n_blocks,),
        in_specs=[
            pl.BlockSpec((MOE_BLOCK, D_MODEL // 2), in_row_map),
            pl.BlockSpec(memory_space=pl.ANY),
            pl.BlockSpec(memory_space=pl.ANY),
            pl.BlockSpec(memory_space=pl.ANY),
        ],
        out_specs=pl.BlockSpec((MOE_BLOCK, D_MODEL), row_map),
        scratch_shapes=[
            pltpu.VMEM((2, EXPERT_RING, D_MODEL, D_EXPERT), F32),
            pltpu.VMEM((2, EXPERT_RING, D_MODEL, D_EXPERT), F32),
            pltpu.VMEM((2, EXPERT_RING, D_EXPERT, D_MODEL), F32),
            pltpu.SemaphoreType.DMA((2, EXPERT_RING)),
            pltpu.SMEM((2,), jnp.int32),
            pltpu.VMEM((2, MOE_BLOCK, D_EXPERT), F32),
            pltpu.VMEM((2, MOE_BLOCK, D_EXPERT), F32),
            pltpu.VMEM((2, MOE_BLOCK, D_EXPERT), BF16),
        ],
    )
    return pl.pallas_call(
        functools.partial(_experts_kernel, layer=layer, n_blocks=n_blocks),
        out_shape=jax.ShapeDtypeStruct((n_rows, D_MODEL), U32),
        grid_spec=grid_spec,
        compiler_params=_cparams(("arbitrary",)),
        name="experts",
    )(blk_a, blk_b, n_used, xs, wg, wu, wd)


COMBINE_CHUNK = 64


def _combine_ln_kernel(dest_ref, x_ref, wtok_ref, lg_ref, lb_ref, ys_ref, o_ref, obf_ref, ybuf, sem):
    i = pl.program_id(0)
    n = pl.num_programs(0)
    slot = i % 2

    def row_copy(tile, to_slot, r):
        return pltpu.make_async_copy(ys_ref.at[pl.ds(dest_ref[tile * ROW_BM + r], 1), :],
                                     ybuf.at[to_slot, pl.ds(r, 1), :], sem.at[to_slot])

    def wait_tile(of_slot):
        pltpu.make_async_copy(ys_ref.at[pl.ds(0, ROW_BM), :], ybuf.at[of_slot], sem.at[of_slot]).wait()

    @pl.when(i == 0)
    def _():
        _start_rows(lambda r: row_copy(0, 0, r))

    wait_tile(slot)
    nxt = lax.rem(i + 1, n)
    for c in range(0, ROW_BM, COMBINE_CHUNK):
        rows = slice(c, c + COMBINE_CHUNK)
        w = wtok_ref[rows, :]
        y_a, y_b = _unpack_bf16_pair(ybuf[slot, rows, :])
        y = w[:, 0:1] * y_a + w[:, 1:2] * y_b
        z = DEEPNORM_ALPHA * x_ref[rows, :] + y
        out = _layernorm_rows(z, lg_ref[...], lb_ref[...])
        o_ref[rows, :] = out
        obf_ref[rows, :] = out.astype(BF16)
        for r in range(c, c + COMBINE_CHUNK):
            row_copy(nxt, 1 - slot, r).start(priority=r % 2)

    @pl.when(i == n - 1)
    def _():
        wait_tile(1 - slot)


def _combine_ln(dest, x2d, wtok, ys, lg, lb):
    t = x2d.shape[0]
    const = lambda i, dest: (0, 0)
    grid_spec = pltpu.PrefetchScalarGridSpec(
        num_scalar_prefetch=1,
        grid=(t // ROW_BM,),
        in_specs=[
            pl.BlockSpec((ROW_BM, D_MODEL), lambda i, dest: (i, 0)),
            pl.BlockSpec((ROW_BM, META_LANES), lambda i, dest: (i, 0)),
            pl.BlockSpec((1, D_MODEL), const),
            pl.BlockSpec((1, D_MODEL), const),
            pl.BlockSpec(memory_space=pl.ANY),
        ],
        out_specs=(pl.BlockSpec((ROW_BM, D_MODEL), lambda i, dest: (i, 0)),
                   pl.BlockSpec((ROW_BM, D_MODEL), lambda i, dest: (i, 0))),
        scratch_shapes=[pltpu.VMEM((2, ROW_BM, D_MODEL), U32), pltpu.SemaphoreType.DMA((2,))],
    )
    return pl.pallas_call(
        _combine_ln_kernel,
        out_shape=(jax.ShapeDtypeStruct((t, D_MODEL), F32), jax.ShapeDtypeStruct((t, D_MODEL), BF16)),
        grid_spec=grid_spec,
        compiler_params=_cparams(("arbitrary",)),
        name="combine_ln",
    )(dest, x2d, wtok, lg, lb, ys)


def _moe(x2d, x_packed, rw_t, rb_t, wg, wu, wd, layer, lg, lb):
    t = x2d.shape[0]
    n_blocks = (t + N_CLASSES * (MOE_BLOCK - 1)) // MOE_BLOCK
    assert n_blocks <= META_LANES
    dest3, wtok, meta = _router(x2d, rw_t, rb_t)
    dest = dest3.reshape(t)
    xs = _dispatch(dest, x_packed, n_blocks * MOE_BLOCK)
    ys = _experts(meta[0, :n_blocks], meta[1, :n_blocks], meta[2, :1], xs, wg, wu, wd, layer)
    return _combine_ln(dest, x2d, wtok, ys, lg, lb)


def _rope_tables(seq):
    half = RET_DK // 2
    pos = jnp.arange(seq, dtype=F32)
    freqs = ROPE_BASE ** (-jnp.arange(half, dtype=F32) / half)
    ang = pos[:, None] * freqs[None, :]
    return jnp.cos(ang), jnp.sin(ang)


def kernel(x, w_in, ret_decay_logit, w_ret_o, na_rpb, w_na_o, w_out, ln_mix_g, ln_mix_b, router_w, router_bias,
           w_exp_gate, w_exp_up, w_exp_down, ln_ffn_g, ln_ffn_b):
    b, s, d = x.shape
    depth = w_in.shape[0]
    t = b * s
    rows = s // GRID_W
    cos, sin = _rope_tables(s)
    rw_t = router_w.astype(F32).T.reshape(N_GROUPS, EXPERTS_PER_GROUP, d).transpose(1, 0, 2).reshape(N_EXPERTS, d)
    rb_t = router_bias.astype(F32).reshape(N_GROUPS, EXPERTS_PER_GROUP).T.reshape(N_EXPERTS, 1)
    x2d = x.reshape(t, d)
    x_bf = x2d.astype(BF16)
    w_in = w_in.astype(F32)
    for l in range(depth):
        qk = _inproj(x_bf, w_in, l, cos, sin, s, "rotary").reshape(b, s, -1)
        vn = _inproj(x_bf, w_in, l, cos, sin, s, "scale").reshape(b, s, -1)
        gr = _inproj(x_bf, w_in, l, cos, sin, s, "silu").reshape(b, s, -1)
        gates = _inproj(x_bf, w_in, l, cos, sin, s, "sigmoid")
        ret = _retention(qk, vn, gr, ret_decay_logit[l].astype(F32))
        na = _na(vn, na_rpb[l].astype(F32).reshape(-1))
        x2d, x_packed = _mixout(ret.reshape(t, -1), na.reshape(t, -1), gates, x2d,
                                w_ret_o[l].astype(BF16), w_na_o[l].astype(BF16), w_out[l].astype(BF16),
                                ln_mix_g[l].reshape(1, d).astype(F32), ln_mix_b[l].reshape(1, d).astype(F32))
        x2d, x_bf = _moe(x2d, x_packed, rw_t, rb_t, w_exp_gate, w_exp_up, w_exp_down, l,
                         ln_ffn_g[l].reshape(1, d).astype(F32), ln_ffn_b[l].reshape(1, d).astype(F32))
    return x2d.reshape(b, s, d)
```

```python
import functools

import numpy as np
import jax
import jax.numpy as jnp
from jax import lax
from jax.experimental import pallas as pl
from jax.experimental.pallas import tpu as pltpu

F32 = jnp.float32
BF16 = jnp.bfloat16

D_MODEL = 1024
GRID_W = 64
RET_HEADS = 4
RET_DK = 256
RET_DV = 512
ROPE_BASE = 10000.0
NA_HEADS = 16
NA_HD = 64
NA_KH = 8
NA_KW = 16
N_EXPERTS = 32
N_GROUPS = 8
EXPERTS_PER_GROUP = 4
D_EXPERT = 512
LN_EPS = 1e-5
GN_EPS = 1e-5
MODEL_DEPTH = 4
DEEPNORM_ALPHA = (2 * MODEL_DEPTH) ** 0.25

D_IN = 11264

RET_CHUNK = 256
NA_QROWS = 4
NA_KROWS = 12
NA_NEG = -1e30
MOE_BLOCK = 256
N_PAIRS = 6
PAIR_SLOT_A = (0, 2, 2, 3, 3, 3)
PAIR_SLOT_B = (1, 1, 0, 0, 1, 2)
N_CLASSES = N_GROUPS * N_PAIRS
VMEM_LIMIT = 56 * 1024 * 1024


def _cparams(sem):
    return pltpu.CompilerParams(dimension_semantics=sem, vmem_limit_bytes=VMEM_LIMIT)


U32 = jnp.uint32
_HIGH_HALF = 0xFFFF0000


def _pack_bf16_pair(hi, lo):
    hi_bits = lax.bitcast_convert_type(hi.astype(jnp.bfloat16).astype(F32), U32)
    lo_bits = lax.bitcast_convert_type(lo.astype(jnp.bfloat16).astype(F32), U32)
    return (hi_bits & jnp.uint32(_HIGH_HALF)) | (lo_bits >> 16)


def _unpack_bf16_pair(packed):
    hi = lax.bitcast_convert_type(packed & jnp.uint32(_HIGH_HALF), F32)
    lo = lax.bitcast_convert_type(packed << 16, F32)
    return hi, lo


IN_CHUNK = 256
LOG2E = 1.4426950408889634

IN_GROUPS = {
    "rotary": (1024, 2048, (0,)),
    "scale": (2048, 1024, (2, 3, 6, 7, 8)),
    "silu": (1024, 2048, (2,)),
    "sigmoid": (2048, 1024, (9, 10)),
}
VN_Q = 2048
VN_K = 3072
VN_V = 4096


def _inproj_kernel(x_ref, w_ref, cos_ref, sin_ref, o_ref, wbf_ref, *, kind, bn):
    j = pl.program_id(0)

    @pl.when(pl.program_id(1) == 0)
    def _():
        wbf_ref[...] = w_ref[0].astype(BF16)

    x = x_ref[...]
    half = RET_DK // 2
    for c in range(bn // IN_CHUNK):
        lo = c * IN_CHUNK
        acc = jnp.dot(x, wbf_ref[:, lo:lo + IN_CHUNK], preferred_element_type=F32)
        if kind == "rotary":
            scale = 1.0 if lo < RET_HEADS * RET_DK else RET_DK ** -0.5
            cos = cos_ref[...] * scale
            sin = sin_ref[...] * scale
            x1 = acc[:, :half]
            x2 = acc[:, half:]
            o_ref[:, lo:lo + half] = (x1 * cos - x2 * sin).astype(o_ref.dtype)
            o_ref[:, lo + half:lo + IN_CHUNK] = (x1 * sin + x2 * cos).astype(o_ref.dtype)
        elif kind == "scale":
            scale = jnp.where(j == 2, NA_HD ** -0.5 * LOG2E, 1.0)
            o_ref[:, lo:lo + IN_CHUNK] = (acc * scale).astype(o_ref.dtype)
        elif kind == "silu":
            o_ref[:, lo:lo + IN_CHUNK] = (acc * jax.nn.sigmoid(acc)).astype(o_ref.dtype)
        else:
            o_ref[:, lo:lo + IN_CHUNK] = jax.nn.sigmoid(acc).astype(o_ref.dtype)


def _inproj(x_bf, w_in, layer, cos, sin, seq, kind):
    t = x_bf.shape[0]
    bm, bn, blocks = IN_GROUPS[kind]
    pos_blocks = seq // bm

    def wcol(j):
        col = blocks[0]
        for k in range(1, len(blocks)):
            col = jnp.where(j == k, blocks[k], col)
        return col

    pos_map = (lambda j, i: (i % pos_blocks, 0)) if kind == "rotary" else (lambda j, i: (0, 0))
    return pl.pallas_call(
        functools.partial(_inproj_kernel, kind=kind, bn=bn),
        out_shape=jax.ShapeDtypeStruct((t, len(blocks) * bn), BF16),
        grid=(len(blocks), t // bm),
        in_specs=[
            pl.BlockSpec((bm, D_MODEL), lambda j, i: (i, 0)),
            pl.BlockSpec((1, D_MODEL, bn), lambda j, i: (layer, 0, wcol(j))),
            pl.BlockSpec((bm, RET_DK // 2), pos_map),
            pl.BlockSpec((bm, RET_DK // 2), pos_map),
        ],
        out_specs=pl.BlockSpec((bm, bn), lambda j, i: (i, j)),
        scratch_shapes=[pltpu.VMEM((D_MODEL, bn), BF16)],
        compiler_params=_cparams(("arbitrary", "arbitrary")),
        name="inproj_" + kind,
    )(x_bf, w_in, cos, sin)


def _log_sigmoid(x):
    return jnp.minimum(x, 0.0) - jnp.log1p(jnp.exp(-jnp.abs(x)))


RET_GROUP = 8


def _retention_kernel(dl_ref, q_ref, k_ref, v_ref, g_ref, o_ref,
                      sf_ref, sb_ref, st_ref, stb_ref, dm_ref, qdf_ref, qdb_ref, kdf_ref, kdb_ref,
                      p_buf, qf_buf, qb_buf, out_buf, *, nc):
    c_len = RET_CHUNK
    h = pl.program_id(1)
    lgf = _log_sigmoid(jnp.full((c_len, RET_DK), dl_ref[0, h], F32))
    lgb = _log_sigmoid(jnp.full((c_len, RET_DK), dl_ref[1, h], F32))
    ri = lax.broadcasted_iota(jnp.int32, (c_len, RET_DK), 0).astype(F32)
    qdf_ref[...] = jnp.exp(lgf * (ri + 1.0))
    qdb_ref[...] = jnp.exp(lgb * (c_len - ri))
    kdf_ref[...] = jnp.exp(lgf * (c_len - 1.0 - ri))
    kdb_ref[...] = jnp.exp(lgb * ri)
    rr = lax.broadcasted_iota(jnp.int32, (c_len, c_len), 0).astype(F32)
    cc = lax.broadcasted_iota(jnp.int32, (c_len, c_len), 1).astype(F32)
    diff = rr - cc
    lgf_cc = _log_sigmoid(jnp.full((c_len, c_len), dl_ref[0, h], F32))
    lgb_cc = _log_sigmoid(jnp.full((c_len, c_len), dl_ref[1, h], F32))
    dm_ref[...] = jnp.where(diff >= 0.0, jnp.exp(lgf_cc * jnp.maximum(diff, 0.0)),
                            jnp.exp(lgb_cc * jnp.maximum(-diff, 0.0)))
    chunk_f = jnp.exp(lgf[:1, :1] * float(c_len))
    chunk_b = jnp.exp(lgb[:1, :1] * float(c_len))

    tn_dims = (((0,), (0,)), ((), ()))
    nt_dims = (((1,), (1,)), ((), ()))

    st_ref[...] = jnp.zeros_like(st_ref)
    stb_ref[...] = jnp.zeros_like(stb_ref)

    def state_body(i, carry):
        for c, all_ref, cur_ref, kdec_ref, chunk_dec in ((i, sf_ref, st_ref, kdf_ref, chunk_f),
                                                         (nc - 1 - i, sb_ref, stb_ref, kdb_ref, chunk_b)):
            off = pl.multiple_of(c * c_len, c_len)
            all_ref[c] = cur_ref[...].astype(BF16)
            kc = k_ref[0, pl.ds(off, c_len), :].astype(F32)
            vc = v_ref[0, pl.ds(off, c_len), :]
            kd = (kc * kdec_ref[...]).astype(BF16)
            upd = lax.dot_general(kd, vc, tn_dims, preferred_element_type=F32)
            cur_ref[...] = cur_ref[...] * chunk_dec + upd
        return carry

    lax.fori_loop(0, nc, state_body, 0, unroll=4)

    def out_body(g, carry):
        offs = [pl.multiple_of((g * RET_GROUP + u) * c_len, c_len) for u in range(RET_GROUP)]
        for u, off in enumerate(offs):
            qb = q_ref[0, pl.ds(off, c_len), :]
            kb = k_ref[0, pl.ds(off, c_len), :]
            qc = qb.astype(F32)
            s = lax.dot_general(qb, kb, nt_dims, preferred_element_type=F32)
            p_buf[u] = (s * dm_ref[...]).astype(BF16)
            qf_buf[u] = (qc * qdf_ref[...]).astype(BF16)
            qb_buf[u] = (qc * qdb_ref[...]).astype(BF16)
        for u, off in enumerate(offs):
            c = g * RET_GROUP + u
            out = jnp.dot(p_buf[u], v_ref[0, pl.ds(off, c_len), :], preferred_element_type=F32)
            out = out + jnp.dot(qf_buf[u], sf_ref[c], preferred_element_type=F32)
            out_buf[u] = out + jnp.dot(qb_buf[u], sb_ref[c], preferred_element_type=F32)
        for u, off in enumerate(offs):
            out = out_buf[u]
            mu = jnp.mean(out, axis=-1, keepdims=True)
            cen = out - mu
            var = jnp.mean(cen * cen, axis=-1, keepdims=True)
            y = cen * lax.rsqrt(var + GN_EPS)
            gate = g_ref[0, pl.ds(off, c_len), :].astype(F32)
            o_ref[0, pl.ds(off, c_len), :] = (gate * y).astype(o_ref.dtype)
        return carry

    lax.fori_loop(0, nc // RET_GROUP, out_body, 0)


def _retention(qk3, vn3, gr3, decay_logit):
    b, s, _ = qk3.shape
    nc = s // RET_CHUNK
    return pl.pallas_call(
        functools.partial(_retention_kernel, nc=nc),
        out_shape=jax.ShapeDtypeStruct((b, s, RET_HEADS * RET_DV), BF16),
        grid=(b, RET_HEADS),
        in_specs=[
            pl.BlockSpec(memory_space=pltpu.SMEM),
            pl.BlockSpec((1, s, RET_DK), lambda bi, h: (bi, 0, h)),
            pl.BlockSpec((1, s, RET_DK), lambda bi, h: (bi, 0, RET_HEADS + h)),
            pl.BlockSpec((1, s, RET_DV), lambda bi, h: (bi, 0, h)),
            pl.BlockSpec((1, s, RET_DV), lambda bi, h: (bi, 0, h)),
        ],
        out_specs=pl.BlockSpec((1, s, RET_DV), lambda bi, h: (bi, 0, h)),
        scratch_shapes=[
            pltpu.VMEM((nc, RET_DK, RET_DV), BF16),
            pltpu.VMEM((nc, RET_DK, RET_DV), BF16),
            pltpu.VMEM((RET_DK, RET_DV), F32),
            pltpu.VMEM((RET_DK, RET_DV), F32),
            pltpu.VMEM((RET_CHUNK, RET_CHUNK), F32),
            pltpu.VMEM((RET_CHUNK, RET_DK), F32),
            pltpu.VMEM((RET_CHUNK, RET_DK), F32),
            pltpu.VMEM((RET_CHUNK, RET_DK), F32),
            pltpu.VMEM((RET_CHUNK, RET_DK), F32),
            pltpu.VMEM((RET_GROUP, RET_CHUNK, RET_CHUNK), BF16),
            pltpu.VMEM((RET_GROUP, RET_CHUNK, RET_DK), BF16),
            pltpu.VMEM((RET_GROUP, RET_CHUNK, RET_DK), BF16),
            pltpu.VMEM((RET_GROUP, RET_CHUNK, RET_DV), F32),
        ],
        compiler_params=_cparams(("arbitrary", "arbitrary")),
        name="retention",
    )(decay_logit, qk3, qk3, vn3, gr3)


NA_ROW_OFFS = 2 * NA_KH - 1
NA_COL_OFFS = 2 * NA_KW - 1
NA_PATTERNS = 3


def _na_row_offsets(rows):
    n_tiles = rows // NA_QROWS
    offs = np.full((NA_PATTERNS, NA_QROWS, NA_KROWS), NA_ROW_OFFS, np.int32)
    for p, t in enumerate((0, 1, n_tiles - 1)):
        kstart = int(np.clip(NA_QROWS * t - NA_KH // 2, 0, rows - NA_KROWS))
        for rr in range(NA_QROWS):
            r = NA_QROWS * t + rr
            rs = int(np.clip(r - NA_KH // 2, 0, rows - NA_KH))
            for i in range(NA_KROWS):
                krow = kstart + i
                if rs <= krow < rs + NA_KH:
                    offs[p, rr, i] = krow - r + (NA_KH - 1)
    return offs


def _na_build_bias(rpb_ref, e_ref, bias_ref, head0, rows):
    wide = 2 * GRID_W
    lane = lax.broadcasted_iota(jnp.int32, (GRID_W, wide), 1)
    col = lax.broadcasted_iota(jnp.int32, (GRID_W, wide), 0)
    kcol = jnp.where(lane < GRID_W, lane, lane - GRID_W)
    cstart = jnp.clip(col - NA_KW // 2, 0, GRID_W - NA_KW)
    col_ok = (kcol >= cstart) & (kcol < cstart + NA_KW)
    coff = kcol - col + (NA_KW - 1)
    neg = jnp.full((GRID_W, wide), NA_NEG, F32)
    left = lane < GRID_W
    offs = _na_row_offsets(rows)
    for hh in range(2):
        base = (head0 + hh) * (NA_ROW_OFFS * NA_COL_OFFS)
        for ro in range(NA_ROW_OFFS):
            acc = neg
            for d in range(NA_COL_OFFS):
                acc = jnp.where(coff == d, rpb_ref[base + ro * NA_COL_OFFS + d] * LOG2E, acc)
            e_ref[hh, ro] = jnp.where(col_ok, acc, neg)
        e_ref[hh, NA_ROW_OFFS] = neg
        for p in range(NA_PATTERNS):
            for rr in range(NA_QROWS):
                for ip in range(NA_KROWS // 2):
                    tile = jnp.where(left, e_ref[hh, int(offs[p, rr, 2 * ip])], e_ref[hh, int(offs[p, rr, 2 * ip + 1])])
                    bias_ref[hh, p, rr * GRID_W:(rr + 1) * GRID_W, ip * wide:(ip + 1) * wide] = tile


NA_GROUP = 8


def _na_kernel(rpb_ref, q_ref, k_ref, v_ref, o_ref, bias_ref, e_ref, s_buf, p_buf, *, n_tiles, rows):
    nq = NA_QROWS * GRID_W
    nk = NA_KROWS * GRID_W
    nt_dims = (((1,), (1,)), ((), ()))
    first = lax.broadcasted_iota(jnp.int32, (nq, 2 * NA_HD), 1) < NA_HD
    kfirst = lax.broadcasted_iota(jnp.int32, (nk, 2 * NA_HD), 1) < NA_HD

    @pl.when(pl.program_id(1) == 0)
    def _():
        _na_build_bias(rpb_ref, e_ref, bias_ref, 2 * pl.program_id(0), rows)

    def body(g, carry):
        offs = []
        for u in range(NA_GROUP):
            t = g * NA_GROUP + u
            qoff = pl.multiple_of(t * nq, nq)
            krow0 = jnp.clip(NA_QROWS * t - NA_KH // 2, 0, rows - NA_KROWS)
            koff = pl.multiple_of(krow0 * GRID_W, nq)
            pat = jnp.where(t == 0, 0, jnp.where(t == n_tiles - 1, 2, 1))
            offs.append((qoff, koff, pat))
        for u, (qoff, koff, pat) in enumerate(offs):
            q2 = q_ref[0, pl.ds(qoff, nq), :]
            k2 = k_ref[0, pl.ds(koff, nk), :]
            for hh in range(2):
                qm = jnp.where(first if hh == 0 else jnp.logical_not(first), q2, jnp.zeros_like(q2))
                s_buf[2 * u + hh] = lax.dot_general(qm, k2, nt_dims, preferred_element_type=F32) + bias_ref[hh, pat]
        for n in range(2 * NA_GROUP):
            s = s_buf[n]
            p_buf[n] = jnp.exp2(s - jnp.max(s, axis=-1, keepdims=True)).astype(BF16)
        for u, (qoff, koff, pat) in enumerate(offs):
            v2 = v_ref[0, pl.ds(koff, nk), :]
            outs = []
            for hh in range(2):
                vv = jnp.where(kfirst if hh == 0 else jnp.logical_not(kfirst), v2, jnp.ones_like(v2))
                o = jnp.dot(p_buf[2 * u + hh], vv, preferred_element_type=F32)
                outs.append(o / pltpu.roll(o, NA_HD, axis=1))
            o_ref[0, pl.ds(qoff, nq), :] = jnp.where(first, outs[0], outs[1]).astype(o_ref.dtype)
        return carry

    lax.fori_loop(0, n_tiles // NA_GROUP, body, 0)


def _na(vn3, rpb_flat):
    b, s, _ = vn3.shape
    rows = s // GRID_W
    n_tiles = rows // NA_QROWS
    w2 = 2 * NA_HD
    nq = NA_QROWS * GRID_W
    nk = NA_KROWS * GRID_W
    return pl.pallas_call(
        functools.partial(_na_kernel, n_tiles=n_tiles, rows=rows),
        out_shape=jax.ShapeDtypeStruct((b, s, NA_HEADS * NA_HD), BF16),
        grid=(NA_HEADS // 2, b),
        in_specs=[
            pl.BlockSpec(memory_space=pltpu.SMEM),
            pl.BlockSpec((1, s, w2), lambda hp, bi: (bi, 0, VN_Q // w2 + hp)),
            pl.BlockSpec((1, s, w2), lambda hp, bi: (bi, 0, VN_K // w2 + hp)),
            pl.BlockSpec((1, s, w2), lambda hp, bi: (bi, 0, VN_V // w2 + hp)),
        ],
        out_specs=pl.BlockSpec((1, s, w2), lambda hp, bi: (bi, 0, hp)),
        scratch_shapes=[
            pltpu.VMEM((2, NA_PATTERNS, nq, nk), F32),
            pltpu.VMEM((2, NA_ROW_OFFS + 1, GRID_W, 2 * GRID_W), F32),
            pltpu.VMEM((2 * NA_GROUP, nq, nk), F32),
            pltpu.VMEM((2 * NA_GROUP, nq, nk), BF16),
        ],
        compiler_params=_cparams(("arbitrary", "arbitrary")),
        name="natten",
    )(rpb_flat, vn3, vn3, vn3)


MIX_BM = 512
MIX_SUB = 256


def _layernorm_rows(z, g, b):
    mu = jnp.mean(z, axis=-1, keepdims=True)
    cen = z - mu
    var = jnp.mean(cen * cen, axis=-1, keepdims=True)
    return cen * lax.rsqrt(var + LN_EPS) * g + b


def _mixout_kernel(ret_ref, na_ref, g1_ref, g2_ref, x_ref, wr_ref, wn_ref, wo_ref, lg_ref, lb_ref, o_ref, op_ref):
    for r in range(0, MIX_BM, MIX_SUB):
        rows = slice(r, r + MIX_SUB)
        y_ret = jnp.dot(ret_ref[rows, :], wr_ref[...], preferred_element_type=F32)
        y_na = jnp.dot(na_ref[rows, :], wn_ref[...], preferred_element_type=F32)
        merged = g1_ref[rows, :].astype(F32) * y_ret + g2_ref[rows, :].astype(F32) * y_na
        mix = jnp.dot(merged.astype(BF16), wo_ref[...], preferred_element_type=F32)
        z = DEEPNORM_ALPHA * x_ref[rows, :] + mix
        out = _layernorm_rows(z, lg_ref[...], lb_ref[...])
        o_ref[rows, :] = out
        op_ref[rows, :] = _pack_bf16_pair(out[:, :D_MODEL // 2], out[:, D_MODEL // 2:])


def _mixout(ret2d, na2d, gates2d, x2d, wr, wn, wo, lg, lb):
    t = x2d.shape[0]
    const = lambda i: (0, 0)
    return pl.pallas_call(
        _mixout_kernel,
        out_shape=(jax.ShapeDtypeStruct((t, D_MODEL), F32), jax.ShapeDtypeStruct((t, D_MODEL // 2), U32)),
        grid=(t // MIX_BM,),
        in_specs=[
            pl.BlockSpec((MIX_BM, RET_HEADS * RET_DV), lambda i: (i, 0)),
            pl.BlockSpec((MIX_BM, NA_HEADS * NA_HD), lambda i: (i, 0)),
            pl.BlockSpec((MIX_BM, D_MODEL), lambda i: (i, 0)),
            pl.BlockSpec((MIX_BM, D_MODEL), lambda i: (i, 1)),
            pl.BlockSpec((MIX_BM, D_MODEL), lambda i: (i, 0)),
            pl.BlockSpec((RET_HEADS * RET_DV, D_MODEL), const),
            pl.BlockSpec((NA_HEADS * NA_HD, D_MODEL), const),
            pl.BlockSpec((D_MODEL, D_MODEL), const),
            pl.BlockSpec((1, D_MODEL), const),
            pl.BlockSpec((1, D_MODEL), const),
        ],
        out_specs=(pl.BlockSpec((MIX_BM, D_MODEL), lambda i: (i, 0)),
                   pl.BlockSpec((MIX_BM, D_MODEL // 2), lambda i: (i, 0))),
        compiler_params=_cparams(("arbitrary",)),
        name="mixout",
    )(ret2d, na2d, gates2d, gates2d, x2d, wr, wn, wo, lg, lb)


ROUTE_BM = 512


def _route_tile(x, rw, rb):
    nt_dims = (((1,), (1,)), ((), ()))
    x_hi = x.astype(BF16)
    x_lo = (x - x_hi.astype(F32)).astype(BF16)
    rw_hi = rw.astype(BF16)
    rw_lo = (rw - rw_hi.astype(F32)).astype(BF16)
    both = lax.dot_general(jnp.concatenate([rw_hi, rw_lo], axis=0), x_hi, nt_dims, preferred_element_type=F32)
    logits = (both[:N_EXPERTS] + both[N_EXPERTS:]
              + lax.dot_general(rw_hi, x_lo, nt_dims, preferred_element_type=F32))
    scores = jax.nn.sigmoid(logits)
    sel = scores + rb
    p = [scores[m * N_GROUPS:(m + 1) * N_GROUPS] for m in range(EXPERTS_PER_GROUP)]
    s = [sel[m * N_GROUPS:(m + 1) * N_GROUPS] for m in range(EXPERTS_PER_GROUP)]
    one = jnp.ones_like(s[0])
    zero = jnp.zeros_like(s[0])
    chosen = []
    for m in range(EXPERTS_PER_GROUP):
        rank = zero
        for j in range(EXPERTS_PER_GROUP):
            if j == m:
                continue
            beats = (s[j] >= s[m]) if j < m else (s[j] > s[m])
            rank = rank + jnp.where(beats, one, zero)
        chosen.append(rank < 2.0)
    group_score = zero
    for m in range(EXPERTS_PER_GROUP):
        group_score = group_score + jnp.where(chosen[m], s[m], zero)
    gid = lax.broadcasted_iota(jnp.int32, group_score.shape, 0)
    gmax = jnp.max(group_score, axis=0, keepdims=True)
    gbest = jnp.min(jnp.where(group_score == gmax, gid, N_GROUPS), axis=0, keepdims=True)
    in_best = gid == gbest
    picked = [chosen[m] & in_best for m in range(EXPERTS_PER_GROUP)]
    before = zero
    wa = zero
    wb = zero
    ma = zero
    mb = zero
    for m in range(EXPERTS_PER_GROUP):
        is_a = picked[m] & (before == 0.0)
        is_b = picked[m] & (before == 1.0)
        wa = wa + jnp.where(is_a, p[m], zero)
        wb = wb + jnp.where(is_b, p[m], zero)
        ma = ma + jnp.where(is_a, float(m), 0.0)
        mb = mb + jnp.where(is_b, float(m), 0.0)
        before = before + jnp.where(chosen[m], one, zero)
    wa = jnp.sum(wa, axis=0, keepdims=True)
    wb = jnp.sum(wb, axis=0, keepdims=True)
    ma = jnp.sum(ma, axis=0, keepdims=True)
    mb = jnp.sum(mb, axis=0, keepdims=True)
    denom = wa + wb
    w_lo = wa / denom
    w_hi = wb / denom
    pair = jnp.where(ma == 0.0, jnp.where(mb == 1.0, 0.0, mb),
                     jnp.where(ma == 1.0, jnp.where(mb == 2.0, 1.0, 4.0), 5.0))
    keep = pair == 0.0
    return (gbest * N_PAIRS + pair.astype(jnp.int32),
            jnp.where(keep, w_lo, w_hi), jnp.where(keep, w_hi, w_lo))


CLASS_ROWS = 64
META_LANES = 128


def _router_kernel(x_ref, rw_ref, rb_ref, tri_ref, dest_ref, wtok_ref, meta_ref, cls_s, rank_s, cnt_s):
    phase = pl.program_id(0)
    i = pl.program_id(1)
    bm = ROUTE_BM
    reps = bm // META_LANES
    cid = lax.broadcasted_iota(jnp.int32, (CLASS_ROWS, bm), 0)

    @pl.when((phase == 0) & (i == 0))
    def _():
        cnt_s[...] = jnp.zeros_like(cnt_s)

    @pl.when(phase == 0)
    def _():
        cls, w_lo, w_hi = _route_tile(x_ref[...], rw_ref[...], rb_ref[...])
        cls_s[i] = cls
        w_rows = jnp.concatenate([w_lo, w_hi, jnp.zeros((META_LANES - 2, bm), F32)], axis=0)
        wtok_ref[...] = w_rows.T
        onehot = (cid == cls).astype(BF16)
        before = jnp.dot(onehot, tri_ref[...], preferred_element_type=F32)
        carry = jnp.concatenate([cnt_s[...]] * reps, axis=1)
        rank = jnp.sum(jnp.where(cid == cls, before + carry, 0.0), axis=0, keepdims=True)
        rank_s[i] = rank.astype(jnp.int32)
        cnt_s[...] = cnt_s[...] + jnp.dot(onehot, jnp.ones((bm, META_LANES), BF16), preferred_element_type=F32)

    @pl.when(phase == 1)
    def _():
        cnt = cnt_s[...]
        nblk = jnp.floor((cnt + (MOE_BLOCK - 1.0)) * (1.0 / MOE_BLOCK))
        rr = lax.broadcasted_iota(jnp.int32, (CLASS_ROWS, CLASS_ROWS), 0)
        cc = lax.broadcasted_iota(jnp.int32, (CLASS_ROWS, CLASS_ROWS), 1)
        lower = (cc < rr).astype(BF16)
        start_blk = jnp.dot(lower, nblk.astype(BF16), preferred_element_type=F32)
        cls = cls_s[i]
        start_t = jnp.concatenate([start_blk] * reps, axis=1)
        start = jnp.sum(jnp.where(cid == cls, start_t, 0.0), axis=0, keepdims=True)
        dest_ref[0] = (start * float(MOE_BLOCK)).astype(jnp.int32) + rank_s[i]

        @pl.when(i == 0)
        def _():
            end_blk = start_blk + nblk
            n_used = jnp.max(end_blk, axis=0, keepdims=True)
            blk = lax.broadcasted_iota(jnp.int32, (CLASS_ROWS, META_LANES), 1).astype(F32)
            blk = jnp.minimum(blk, n_used - 1.0)
            bcls = jnp.sum(jnp.where(end_blk <= blk, 1.0, 0.0), axis=0, keepdims=True)
            grp = jnp.zeros_like(bcls)
            for g in range(1, N_GROUPS):
                grp = grp + jnp.where(bcls >= float(g * N_PAIRS), 1.0, 0.0)
            pair = bcls - grp * float(N_PAIRS)
            slot_a = jnp.zeros_like(pair)
            slot_b = jnp.zeros_like(pair)
            for k in range(N_PAIRS):
                slot_a = jnp.where(pair == float(k), float(PAIR_SLOT_A[k]), slot_a)
                slot_b = jnp.where(pair == float(k), float(PAIR_SLOT_B[k]), slot_b)
            rows = [grp * float(EXPERTS_PER_GROUP) + slot_a, grp * float(EXPERTS_PER_GROUP) + slot_b, n_used]
            rows = rows + [jnp.zeros_like(bcls)] * (8 - len(rows))
            meta_ref[...] = jnp.concatenate(rows, axis=0).astype(jnp.int32)


def _router(x2d, rw_t, rb_t):
    t = x2d.shape[0]
    nt = t // ROUTE_BM
    tri = jnp.asarray(np.triu(np.ones((ROUTE_BM, ROUTE_BM), np.float32), 1), BF16)
    hold = lambda ph, i: ((1 - ph) * i + ph * (nt - 1), 0)
    const = lambda ph, i: (0, 0)
    return pl.pallas_call(
        _router_kernel,
        out_shape=(jax.ShapeDtypeStruct((nt, 1, ROUTE_BM), jnp.int32),
                   jax.ShapeDtypeStruct((t, META_LANES), F32),
                   jax.ShapeDtypeStruct((8, META_LANES), jnp.int32)),
        grid=(2, nt),
        in_specs=[
            pl.BlockSpec((ROUTE_BM, D_MODEL), hold),
            pl.BlockSpec((N_EXPERTS, D_MODEL), const),
            pl.BlockSpec((N_EXPERTS, 1), const),
            pl.BlockSpec((ROUTE_BM, ROUTE_BM), const),
        ],
        out_specs=(pl.BlockSpec((1, 1, ROUTE_BM), lambda ph, i: (ph * i, 0, 0)),
                   pl.BlockSpec((ROUTE_BM, META_LANES), hold),
                   pl.BlockSpec((8, META_LANES), const)),
        scratch_shapes=[
            pltpu.VMEM((nt, 1, ROUTE_BM), jnp.int32),
            pltpu.VMEM((nt, 1, ROUTE_BM), jnp.int32),
            pltpu.VMEM((CLASS_ROWS, META_LANES), F32),
        ],
        compiler_params=_cparams(("arbitrary", "arbitrary")),
        name="router",
    )(x2d, rw_t, rb_t, tri)


ROW_BM = 512


def _start_rows(make_copy):
    for r in range(ROW_BM):
        make_copy(r).start(priority=r % 2)


def _dispatch_kernel(dest_ref, x_ref, zeros_ref, xs_ref, sem):
    del zeros_ref
    base = pl.program_id(0) * ROW_BM
    _start_rows(lambda r: pltpu.make_async_copy(x_ref.at[pl.ds(r, 1), :],
                                                xs_ref.at[pl.ds(dest_ref[base + r], 1), :], sem))
    pltpu.make_async_copy(x_ref, xs_ref.at[pl.ds(0, ROW_BM), :], sem).wait()


def _dispatch(dest, x_rows, n_rows):
    t, width = x_rows.shape
    grid_spec = pltpu.PrefetchScalarGridSpec(
        num_scalar_prefetch=1,
        grid=(t // ROW_BM,),
        in_specs=[
            pl.BlockSpec((ROW_BM, width), lambda i, dest: (i, 0)),
            pl.BlockSpec(memory_space=pl.ANY),
        ],
        out_specs=pl.BlockSpec(memory_space=pl.ANY),
        scratch_shapes=[pltpu.SemaphoreType.DMA],
    )
    return pl.pallas_call(
        _dispatch_kernel,
        out_shape=jax.ShapeDtypeStruct((n_rows, width), x_rows.dtype),
        grid_spec=grid_spec,
        input_output_aliases={2: 0},
        compiler_params=_cparams(("arbitrary",)),
        name="dispatch",
    )(dest, x_rows, jnp.zeros((n_rows, width), x_rows.dtype))


EXPERT_RING = 3
EXPERT_AHEAD = EXPERT_RING - 1


def _experts_kernel(ea_ref, eb_ref, nb_ref, x_ref, wg_hbm, wu_hbm, wd_hbm, o_ref,
                    ring_g, ring_u, ring_d, sems, loads_ref, hg_buf, hu_buf, act_buf, *, layer, n_blocks):
    j = pl.program_id(0)
    nb = nb_ref[0]
    experts_of = (ea_ref, eb_ref)

    def is_load(s, step):
        step = jnp.minimum(step, n_blocks - 1)
        return ((step == 0) | (experts_of[s][step] != experts_of[s][jnp.maximum(step - 1, 0)])).astype(jnp.int32)

    def weight_copies(s, step, buf):
        e = experts_of[s][jnp.minimum(step, n_blocks - 1)]
        return (pltpu.make_async_copy(wg_hbm.at[layer, e], ring_g.at[s, buf], sems.at[s, buf]),
                pltpu.make_async_copy(wu_hbm.at[layer, e], ring_u.at[s, buf], sems.at[s, buf]),
                pltpu.make_async_copy(wd_hbm.at[layer, e], ring_d.at[s, buf], sems.at[s, buf]))

    def start(s, step, load_number):
        for c in weight_copies(s, step, lax.rem(load_number - 1, EXPERT_RING)):
            c.start()

    @pl.when(j < nb)
    def _():
        bufs = []
        for s in range(2):
            loads_now = jnp.where(j == 0, 0, loads_ref[s]) + is_load(s, j)
            loads_ref[s] = loads_now
            buf = lax.rem(loads_now - 1, EXPERT_RING)
            bufs.append(buf)

            for first in range(EXPERT_AHEAD):
                ahead_loads = 1 + sum(is_load(s, k) for k in range(1, first + 1))

                @pl.when((j == 0) & (first < nb) & (is_load(s, first) == 1))
                def _(first=first, ahead_loads=ahead_loads):
                    start(s, first, ahead_loads)

            ahead = j + EXPERT_AHEAD
            ahead_loads = loads_now + sum(is_load(s, j + k) for k in range(1, EXPERT_AHEAD + 1))

            @pl.when((ahead < nb) & (is_load(s, ahead) == 1))
            def _(ahead=ahead, ahead_loads=ahead_loads):
                start(s, ahead, ahead_loads)

            @pl.when(is_load(s, j) == 1)
            def _(buf=buf):
                for c in weight_copies(s, j, buf):
                    c.wait()

        x = jnp.concatenate(_unpack_bf16_pair(x_ref[...]), axis=1).astype(BF16)
        for s in range(2):
            hg_buf[s] = jnp.dot(x, ring_g[s, bufs[s]].astype(BF16), preferred_element_type=F32)
            hu_buf[s] = jnp.dot(x, ring_u[s, bufs[s]].astype(BF16), preferred_element_type=F32)
        for s in range(2):
            hg = hg_buf[s]
            act_buf[s] = (hg * jax.nn.sigmoid(hg) * hu_buf[s]).astype(BF16)
        ys = [jnp.dot(act_buf[s], ring_d[s, bufs[s]].astype(BF16), preferred_element_type=F32) for s in range(2)]
        o_ref[...] = _pack_bf16_pair(ys[0], ys[1])

    @pl.when(j >= nb)
    def _():
        o_ref[...] = jnp.zeros_like(o_ref)


def _experts(blk_a, blk_b, n_used, xs, wg, wu, wd, layer):
    n_rows = xs.shape[0]
    n_blocks = n_rows // MOE_BLOCK
    in_row_map = lambda j, ea, eb, nb: (jnp.maximum(jnp.minimum(j, nb[0] - 1), 0), 0)
    row_map = lambda j, ea, eb, nb: (j, 0)
    grid_spec = pltpu.PrefetchScalarGridSpec(
        num_scalar_prefetch=3,
        grid=(n_blocks,),
        in_specs=[
            pl.BlockSpec((MOE_BLOCK, D_MODEL // 2), in_row_map),
            pl.BlockSpec(memory_space=pl.ANY),
            pl.BlockSpec(memory_space=pl.ANY),
            pl.BlockSpec(memory_space=pl.ANY),
        ],
        out_specs=pl.BlockSpec((MOE_BLOCK, D_MODEL), row_map),
        scratch_shapes=[
            pltpu.VMEM((2, EXPERT_RING, D_MODEL, D_EXPERT), F32),
            pltpu.VMEM((2, EXPERT_RING, D_MODEL, D_EXPERT), F32),
            pltpu.VMEM((2, EXPERT_RING, D_EXPERT, D_MODEL), F32),
            pltpu.SemaphoreType.DMA((2, EXPERT_RING)),
            pltpu.SMEM((2,), jnp.int32),
            pltpu.VMEM((2, MOE_BLOCK, D_EXPERT), F32),
            pltpu.VMEM((2, MOE_BLOCK, D_EXPERT), F32),
            pltpu.VMEM((2, MOE_BLOCK, D_EXPERT), BF16),
        ],
    )
    return pl.pallas_call(
        functools.partial(_experts_kernel, layer=layer, n_blocks=n_blocks),
        out_shape=jax.ShapeDtypeStruct((n_rows, D_MODEL), U32),
        grid_spec=grid_spec,
        compiler_params=_cparams(("arbitrary",)),
        name="experts",
    )(blk_a, blk_b, n_used, xs, wg, wu, wd)


COMBINE_CHUNK = 64


def _combine_ln_kernel(dest_ref, x_ref, wtok_ref, lg_ref, lb_ref, ys_ref, o_ref, obf_ref, ybuf, sem):
    i = pl.program_id(0)
    n = pl.num_programs(0)
    slot = i % 2

    def row_copy(tile, to_slot, r):
        return pltpu.make_async_copy(ys_ref.at[pl.ds(dest_ref[tile * ROW_BM + r], 1), :],
                                     ybuf.at[to_slot, pl.ds(r, 1), :], sem.at[to_slot])

    def wait_tile(of_slot):
        pltpu.make_async_copy(ys_ref.at[pl.ds(0, ROW_BM), :], ybuf.at[of_slot], sem.at[of_slot]).wait()

    @pl.when(i == 0)
    def _():
        _start_rows(lambda r: row_copy(0, 0, r))

    wait_tile(slot)
    nxt = lax.rem(i + 1, n)
    for c in range(0, ROW_BM, COMBINE_CHUNK):
        rows = slice(c, c + COMBINE_CHUNK)
        w = wtok_ref[rows, :]
        y_a, y_b = _unpack_bf16_pair(ybuf[slot, rows, :])
        y = w[:, 0:1] * y_a + w[:, 1:2] * y_b
        z = DEEPNORM_ALPHA * x_ref[rows, :] + y
        out = _layernorm_rows(z, lg_ref[...], lb_ref[...])
        o_ref[rows, :] = out
        obf_ref[rows, :] = out.astype(BF16)
        for r in range(c, c + COMBINE_CHUNK):
            row_copy(nxt, 1 - slot, r).start(priority=r % 2)

    @pl.when(i == n - 1)
    def _():
        wait_tile(1 - slot)


def _combine_ln(dest, x2d, wtok, ys, lg, lb):
    t = x2d.shape[0]
    const = lambda i, dest: (0, 0)
    grid_spec = pltpu.PrefetchScalarGridSpec(
        num_scalar_prefetch=1,
        grid=(t // ROW_BM,),
        in_specs=[
            pl.BlockSpec((ROW_BM, D_MODEL), lambda i, dest: (i, 0)),
            pl.BlockSpec((ROW_BM, META_LANES), lambda i, dest: (i, 0)),
            pl.BlockSpec((1, D_MODEL), const),
            pl.BlockSpec((1, D_MODEL), const),
            pl.BlockSpec(memory_space=pl.ANY),
        ],
        out_specs=(pl.BlockSpec((ROW_BM, D_MODEL), lambda i, dest: (i, 0)),
                   pl.BlockSpec((ROW_BM, D_MODEL), lambda i, dest: (i, 0))),
        scratch_shapes=[pltpu.VMEM((2, ROW_BM, D_MODEL), U32), pltpu.SemaphoreType.DMA((2,))],
    )
    return pl.pallas_call(
        _combine_ln_kernel,
        out_shape=(jax.ShapeDtypeStruct((t, D_MODEL), F32), jax.ShapeDtypeStruct((t, D_MODEL), BF16)),
        grid_spec=grid_spec,
        compiler_params=_cparams(("arbitrary",)),
        name="combine_ln",
    )(dest, x2d, wtok, lg, lb, ys)


def _moe(x2d, x_packed, rw_t, rb_t, wg, wu, wd, layer, lg, lb):
    t = x2d.shape[0]
    n_blocks = (t + N_CLASSES * (MOE_BLOCK - 1)) // MOE_BLOCK
    assert n_blocks <= META_LANES
    dest3, wtok, meta = _router(x2d, rw_t, rb_t)
    dest = dest3.reshape(t)
    xs = _dispatch(dest, x_packed, n_blocks * MOE_BLOCK)
    ys = _experts(meta[0, :n_blocks], meta[1, :n_blocks], meta[2, :1], xs, wg, wu, wd, layer)
    return _combine_ln(dest, x2d, wtok, ys, lg, lb)


def _rope_tables(seq):
    half = RET_DK // 2
    pos = jnp.arange(seq, dtype=F32)
    freqs = ROPE_BASE ** (-jnp.arange(half, dtype=F32) / half)
    ang = pos[:, None] * freqs[None, :]
    return jnp.cos(ang), jnp.sin(ang)


def kernel(x, w_in, ret_decay_logit, w_ret_o, na_rpb, w_na_o, w_out, ln_mix_g, ln_mix_b, router_w, router_bias,
           w_exp_gate, w_exp_up, w_exp_down, ln_ffn_g, ln_ffn_b):
    b, s, d = x.shape
    depth = w_in.shape[0]
    t = b * s
    rows = s // GRID_W
    cos, sin = _rope_tables(s)
    rw_t = router_w.astype(F32).T.reshape(N_GROUPS, EXPERTS_PER_GROUP, d).transpose(1, 0, 2).reshape(N_EXPERTS, d)
    rb_t = router_bias.astype(F32).reshape(N_GROUPS, EXPERTS_PER_GROUP).T.reshape(N_EXPERTS, 1)
    x2d = x.reshape(t, d)
    x_bf = x2d.astype(BF16)
    w_in = w_in.astype(F32)
    for l in range(depth):
        qk = _inproj(x_bf, w_in, l, cos, sin, s, "rotary").reshape(b, s, -1)
        vn = _inproj(x_bf, w_in, l, cos, sin, s, "scale").reshape(b, s, -1)
        gr = _inproj(x_bf, w_in, l, cos, sin, s, "silu").reshape(b, s, -1)
        gates = _inproj(x_bf, w_in, l, cos, sin, s, "sigmoid")
        ret = _retention(qk, vn, gr, ret_decay_logit[l].astype(F32))
        na = _na(vn, na_rpb[l].astype(F32).reshape(-1))
        x2d, x_packed = _mixout(ret.reshape(t, -1), na.reshape(t, -1), gates, x2d,
                                w_ret_o[l].astype(BF16), w_na_o[l].astype(BF16), w_out[l].astype(BF16),
                                ln_mix_g[l].reshape(1, d).astype(F32), ln_mix_b[l].reshape(1, d).astype(F32))
        x2d, x_bf = _moe(x2d, x_packed, rw_t, rb_t, w_exp_gate, w_exp_up, w_exp_down, l,
                         ln_ffn_g[l].reshape(1, d).astype(F32), ln_ffn_b[l].reshape(1, d).astype(F32))
    return x2d.reshape(b, s, d)
```

```python
import functools

import numpy as np
import jax
import jax.numpy as jnp
from jax import lax
from jax.experimental import pallas as pl
from jax.experimental.pallas import tpu as pltpu

F32 = jnp.float32
BF16 = jnp.bfloat16

D_MODEL = 1024
GRID_W = 64
RET_HEADS = 4
RET_DK = 256
RET_DV = 512
ROPE_BASE = 10000.0
NA_HEADS = 16
NA_HD = 64
NA_KH = 8
NA_KW = 16
N_EXPERTS = 32
N_GROUPS = 8
EXPERTS_PER_GROUP = 4
D_EXPERT = 512
LN_EPS = 1e-5
GN_EPS = 1e-5
MODEL_DEPTH = 4
DEEPNORM_ALPHA = (2 * MODEL_DEPTH) ** 0.25

D_IN = 11264

RET_CHUNK = 256
NA_QROWS = 4
NA_KROWS = 12
NA_NEG = -1e30
MOE_BLOCK = 256
N_PAIRS = 6
PAIR_SLOT_A = (0, 2, 2, 3, 3, 3)
PAIR_SLOT_B = (1, 1, 0, 0, 1, 2)
N_CLASSES = N_GROUPS * N_PAIRS
VMEM_LIMIT = 56 * 1024 * 1024


def _cparams(sem):
    return pltpu.CompilerParams(dimension_semantics=sem, vmem_limit_bytes=VMEM_LIMIT)


U32 = jnp.uint32
_HIGH_HALF = 0xFFFF0000


def _pack_bf16_pair(hi, lo):
    hi_bits = lax.bitcast_convert_type(hi.astype(jnp.bfloat16).astype(F32), U32)
    lo_bits = lax.bitcast_convert_type(lo.astype(jnp.bfloat16).astype(F32), U32)
    return (hi_bits & jnp.uint32(_HIGH_HALF)) | (lo_bits >> 16)


def _unpack_bf16_pair(packed):
    hi = lax.bitcast_convert_type(packed & jnp.uint32(_HIGH_HALF), F32)
    lo = lax.bitcast_convert_type(packed << 16, F32)
    return hi, lo


IN_CHUNK = 256
LOG2E = 1.4426950408889634

IN_GROUPS = {
    "rotary": (1024, 2048, (0,)),
    "scale": (2048, 1024, (2, 3, 6, 7, 8)),
    "silu": (1024, 2048, (2,)),
    "sigmoid": (2048, 1024, (9, 10)),
}
VN_Q = 2048
VN_K = 3072
VN_V = 4096


def _inproj_kernel(x_ref, w_ref, cos_ref, sin_ref, o_ref, wbf_ref, *, kind, bn):
    j = pl.program_id(0)

    @pl.when(pl.program_id(1) == 0)
    def _():
        wbf_ref[...] = w_ref[0].astype(BF16)

    x = x_ref[...]
    half = RET_DK // 2
    for c in range(bn // IN_CHUNK):
        lo = c * IN_CHUNK
        acc = jnp.dot(x, wbf_ref[:, lo:lo + IN_CHUNK], preferred_element_type=F32)
        if kind == "rotary":
            scale = 1.0 if lo < RET_HEADS * RET_DK else RET_DK ** -0.5
            cos = cos_ref[...] * scale
            sin = sin_ref[...] * scale
            x1 = acc[:, :half]
            x2 = acc[:, half:]
            o_ref[:, lo:lo + half] = (x1 * cos - x2 * sin).astype(o_ref.dtype)
            o_ref[:, lo + half:lo + IN_CHUNK] = (x1 * sin + x2 * cos).astype(o_ref.dtype)
        elif kind == "scale":
            scale = jnp.where(j == 2, NA_HD ** -0.5 * LOG2E, 1.0)
            o_ref[:, lo:lo + IN_CHUNK] = (acc * scale).astype(o_ref.dtype)
        elif kind == "silu":
            o_ref[:, lo:lo + IN_CHUNK] = (acc * jax.nn.sigmoid(acc)).astype(o_ref.dtype)
        else:
            o_ref[:, lo:lo + IN_CHUNK] = jax.nn.sigmoid(acc).astype(o_ref.dtype)


def _inproj(x_bf, w_in, layer, cos, sin, seq, kind):
    t = x_bf.shape[0]
    bm, bn, blocks = IN_GROUPS[kind]
    pos_blocks = seq // bm

    def wcol(j):
        col = blocks[0]
        for k in range(1, len(blocks)):
            col = jnp.where(j == k, blocks[k], col)
        return col

    pos_map = (lambda j, i: (i % pos_blocks, 0)) if kind == "rotary" else (lambda j, i: (0, 0))
    return pl.pallas_call(
        functools.partial(_inproj_kernel, kind=kind, bn=bn),
        out_shape=jax.ShapeDtypeStruct((t, len(blocks) * bn), BF16),
        grid=(len(blocks), t // bm),
        in_specs=[
            pl.BlockSpec((bm, D_MODEL), lambda j, i: (i, 0)),
            pl.BlockSpec((1, D_MODEL, bn), lambda j, i: (layer, 0, wcol(j))),
            pl.BlockSpec((bm, RET_DK // 2), pos_map),
            pl.BlockSpec((bm, RET_DK // 2), pos_map),
        ],
        out_specs=pl.BlockSpec((bm, bn), lambda j, i: (i, j)),
        scratch_shapes=[pltpu.VMEM((D_MODEL, bn), BF16)],
        compiler_params=_cparams(("arbitrary", "arbitrary")),
        name="inproj_" + kind,
    )(x_bf, w_in, cos, sin)


def _log_sigmoid(x):
    return jnp.minimum(x, 0.0) - jnp.log1p(jnp.exp(-jnp.abs(x)))


RET_GROUP = 8


def _retention_kernel(dl_ref, q_ref, k_ref, v_ref, g_ref, o_ref,
                      sf_ref, sb_ref, st_ref, stb_ref, dm_ref, qdf_ref, qdb_ref, kdf_ref, kdb_ref,
                      p_buf, qf_buf, qb_buf, out_buf, *, nc):
    c_len = RET_CHUNK
    h = pl.program_id(1)
    lgf = _log_sigmoid(jnp.full((c_len, RET_DK), dl_ref[0, h], F32))
    lgb = _log_sigmoid(jnp.full((c_len, RET_DK), dl_ref[1, h], F32))
    ri = lax.broadcasted_iota(jnp.int32, (c_len, RET_DK), 0).astype(F32)
    qdf_ref[...] = jnp.exp(lgf * (ri + 1.0))
    qdb_ref[...] = jnp.exp(lgb * (c_len - ri))
    kdf_ref[...] = jnp.exp(lgf * (c_len - 1.0 - ri))
    kdb_ref[...] = jnp.exp(lgb * ri)
    rr = lax.broadcasted_iota(jnp.int32, (c_len, c_len), 0).astype(F32)
    cc = lax.broadcasted_iota(jnp.int32, (c_len, c_len), 1).astype(F32)
    diff = rr - cc
    lgf_cc = _log_sigmoid(jnp.full((c_len, c_len), dl_ref[0, h], F32))
    lgb_cc = _log_sigmoid(jnp.full((c_len, c_len), dl_ref[1, h], F32))
    dm_ref[...] = jnp.where(diff >= 0.0, jnp.exp(lgf_cc * jnp.maximum(diff, 0.0)),
                            jnp.exp(lgb_cc * jnp.maximum(-diff, 0.0)))
    chunk_f = jnp.exp(lgf[:1, :1] * float(c_len))
    chunk_b = jnp.exp(lgb[:1, :1] * float(c_len))

    tn_dims = (((0,), (0,)), ((), ()))
    nt_dims = (((1,), (1,)), ((), ()))

    st_ref[...] = jnp.zeros_like(st_ref)
    stb_ref[...] = jnp.zeros_like(stb_ref)

    def state_body(i, carry):
        for c, all_ref, cur_ref, kdec_ref, chunk_dec in ((i, sf_ref, st_ref, kdf_ref, chunk_f),
                                                         (nc - 1 - i, sb_ref, stb_ref, kdb_ref, chunk_b)):
            off = pl.multiple_of(c * c_len, c_len)
            all_ref[c] = cur_ref[...].astype(BF16)
            kc = k_ref[0, pl.ds(off, c_len), :].astype(F32)
            vc = v_ref[0, pl.ds(off, c_len), :]
            kd = (kc * kdec_ref[...]).astype(BF16)
            upd = lax.dot_general(kd, vc, tn_dims, preferred_element_type=F32)
            cur_ref[...] = cur_ref[...] * chunk_dec + upd
        return carry

    lax.fori_loop(0, nc, state_body, 0, unroll=4)

    def out_body(g, carry):
        offs = [pl.multiple_of((g * RET_GROUP + u) * c_len, c_len) for u in range(RET_GROUP)]
        for u, off in enumerate(offs):
            qb = q_ref[0, pl.ds(off, c_len), :]
            kb = k_ref[0, pl.ds(off, c_len), :]
            qc = qb.astype(F32)
            s = lax.dot_general(qb, kb, nt_dims, preferred_element_type=F32)
            p_buf[u] = (s * dm_ref[...]).astype(BF16)
            qf_buf[u] = (qc * qdf_ref[...]).astype(BF16)
            qb_buf[u] = (qc * qdb_ref[...]).astype(BF16)
        for u, off in enumerate(offs):
            c = g * RET_GROUP + u
            out = jnp.dot(p_buf[u], v_ref[0, pl.ds(off, c_len), :], preferred_element_type=F32)
            out = out + jnp.dot(qf_buf[u], sf_ref[c], preferred_element_type=F32)
            out_buf[u] = out + jnp.dot(qb_buf[u], sb_ref[c], preferred_element_type=F32)
        for u, off in enumerate(offs):
            out = out_buf[u]
            mu = jnp.mean(out, axis=-1, keepdims=True)
            cen = out - mu
            var = jnp.mean(cen * cen, axis=-1, keepdims=True)
            y = cen * lax.rsqrt(var + GN_EPS)
            gate = g_ref[0, pl.ds(off, c_len), :].astype(F32)
            o_ref[0, pl.ds(off, c_len), :] = (gate * y).astype(o_ref.dtype)
        return carry

    lax.fori_loop(0, nc // RET_GROUP, out_body, 0)


def _retention(qk3, vn3, gr3, decay_logit):
    b, s, _ = qk3.shape
    nc = s // RET_CHUNK
    return pl.pallas_call(
        functools.partial(_retention_kernel, nc=nc),
        out_shape=jax.ShapeDtypeStruct((b, s, RET_HEADS * RET_DV), BF16),
        grid=(b, RET_HEADS),
        in_specs=[
            pl.BlockSpec(memory_space=pltpu.SMEM),
            pl.BlockSpec((1, s, RET_DK), lambda bi, h: (bi, 0, h)),
            pl.BlockSpec((1, s, RET_DK), lambda bi, h: (bi, 0, RET_HEADS + h)),
            pl.BlockSpec((1, s, RET_DV), lambda bi, h: (bi, 0, h)),
            pl.BlockSpec((1, s, RET_DV), lambda bi, h: (bi, 0, h)),
        ],
        out_specs=pl.BlockSpec((1, s, RET_DV), lambda bi, h: (bi, 0, h)),
        scratch_shapes=[
            pltpu.VMEM((nc, RET_DK, RET_DV), BF16),
            pltpu.VMEM((nc, RET_DK, RET_DV), BF16),
            pltpu.VMEM((RET_DK, RET_DV), F32),
            pltpu.VMEM((RET_DK, RET_DV), F32),
            pltpu.VMEM((RET_CHUNK, RET_CHUNK), F32),
            pltpu.VMEM((RET_CHUNK, RET_DK), F32),
            pltpu.VMEM((RET_CHUNK, RET_DK), F32),
            pltpu.VMEM((RET_CHUNK, RET_DK), F32),
            pltpu.VMEM((RET_CHUNK, RET_DK), F32),
            pltpu.VMEM((RET_GROUP, RET_CHUNK, RET_CHUNK), BF16),
            pltpu.VMEM((RET_GROUP, RET_CHUNK, RET_DK), BF16),
            pltpu.VMEM((RET_GROUP, RET_CHUNK, RET_DK), BF16),
            pltpu.VMEM((RET_GROUP, RET_CHUNK, RET_DV), F32),
        ],
        compiler_params=_cparams(("arbitrary", "arbitrary")),
        name="retention",
    )(decay_logit, qk3, qk3, vn3, gr3)


NA_ROW_OFFS = 2 * NA_KH - 1
NA_COL_OFFS = 2 * NA_KW - 1
NA_PATTERNS = 3


def _na_row_offsets(rows):
    n_tiles = rows // NA_QROWS
    offs = np.full((NA_PATTERNS, NA_QROWS, NA_KROWS), NA_ROW_OFFS, np.int32)
    for p, t in enumerate((0, 1, n_tiles - 1)):
        kstart = int(np.clip(NA_QROWS * t - NA_KH // 2, 0, rows - NA_KROWS))
        for rr in range(NA_QROWS):
            r = NA_QROWS * t + rr
            rs = int(np.clip(r - NA_KH // 2, 0, rows - NA_KH))
            for i in range(NA_KROWS):
                krow = kstart + i
                if rs <= krow < rs + NA_KH:
                    offs[p, rr, i] = krow - r + (NA_KH - 1)
    return offs


def _na_build_bias(rpb_ref, e_ref, bias_ref, head0, rows):
    wide = 2 * GRID_W
    lane = lax.broadcasted_iota(jnp.int32, (GRID_W, wide), 1)
    col = lax.broadcasted_iota(jnp.int32, (GRID_W, wide), 0)
    kcol = jnp.where(lane < GRID_W, lane, lane - GRID_W)
    cstart = jnp.clip(col - NA_KW // 2, 0, GRID_W - NA_KW)
    col_ok = (kcol >= cstart) & (kcol < cstart + NA_KW)
    coff = kcol - col + (NA_KW - 1)
    neg = jnp.full((GRID_W, wide), NA_NEG, F32)
    left = lane < GRID_W
    offs = _na_row_offsets(rows)
    for hh in range(2):
        base = (head0 + hh) * (NA_ROW_OFFS * NA_COL_OFFS)
        for ro in range(NA_ROW_OFFS):
            acc = neg
            for d in range(NA_COL_OFFS):
                acc = jnp.where(coff == d, rpb_ref[base + ro * NA_COL_OFFS + d] * LOG2E, acc)
            e_ref[hh, ro] = jnp.where(col_ok, acc, neg)
        e_ref[hh, NA_ROW_OFFS] = neg
        for p in range(NA_PATTERNS):
            for rr in range(NA_QROWS):
                for ip in range(NA_KROWS // 2):
                    tile = jnp.where(left, e_ref[hh, int(offs[p, rr, 2 * ip])], e_ref[hh, int(offs[p, rr, 2 * ip + 1])])
                    bias_ref[hh, p, rr * GRID_W:(rr + 1) * GRID_W, ip * wide:(ip + 1) * wide] = tile


NA_GROUP = 8


def _na_kernel(rpb_ref, q_ref, k_ref, v_ref, o_ref, bias_ref, e_ref, s_buf, p_buf, *, n_tiles, rows):
    nq = NA_QROWS * GRID_W
    nk = NA_KROWS * GRID_W
    nt_dims = (((1,), (1,)), ((), ()))
    first = lax.broadcasted_iota(jnp.int32, (nq, 2 * NA_HD), 1) < NA_HD
    kfirst = lax.broadcasted_iota(jnp.int32, (nk, 2 * NA_HD), 1) < NA_HD

    @pl.when(pl.program_id(1) == 0)
    def _():
        _na_build_bias(rpb_ref, e_ref, bias_ref, 2 * pl.program_id(0), rows)

    def body(g, carry):
        offs = []
        for u in range(NA_GROUP):
            t = g * NA_GROUP + u
            qoff = pl.multiple_of(t * nq, nq)
            krow0 = jnp.clip(NA_QROWS * t - NA_KH // 2, 0, rows - NA_KROWS)
            koff = pl.multiple_of(krow0 * GRID_W, nq)
            pat = jnp.where(t == 0, 0, jnp.where(t == n_tiles - 1, 2, 1))
            offs.append((qoff, koff, pat))
        for u, (qoff, koff, pat) in enumerate(offs):
            q2 = q_ref[0, pl.ds(qoff, nq), :]
            k2 = k_ref[0, pl.ds(koff, nk), :]
            for hh in range(2):
                qm = jnp.where(first if hh == 0 else jnp.logical_not(first), q2, jnp.zeros_like(q2))
                s_buf[2 * u + hh] = lax.dot_general(qm, k2, nt_dims, preferred_element_type=F32) + bias_ref[hh, pat]
        for n in range(2 * NA_GROUP):
            s = s_buf[n]
            p_buf[n] = jnp.exp2(s - jnp.max(s, axis=-1, keepdims=True)).astype(BF16)
        for u, (qoff, koff, pat) in enumerate(offs):
            v2 = v_ref[0, pl.ds(koff, nk), :]
            outs = []
            for hh in range(2):
                vv = jnp.where(kfirst if hh == 0 else jnp.logical_not(kfirst), v2, jnp.ones_like(v2))
                o = jnp.dot(p_buf[2 * u + hh], vv, preferred_element_type=F32)
                outs.append(o / pltpu.roll(o, NA_HD, axis=1))
            o_ref[0, pl.ds(qoff, nq), :] = jnp.where(first, outs[0], outs[1]).astype(o_ref.dtype)
        return carry

    lax.fori_loop(0, n_tiles // NA_GROUP, body, 0)


def _na(vn3, rpb_flat):
    b, s, _ = vn3.shape
    rows = s // GRID_W
    n_tiles = rows // NA_QROWS
    w2 = 2 * NA_HD
    nq = NA_QROWS * GRID_W
    nk = NA_KROWS * GRID_W
    return pl.pallas_call(
        functools.partial(_na_kernel, n_tiles=n_tiles, rows=rows),
        out_shape=jax.ShapeDtypeStruct((b, s, NA_HEADS * NA_HD), BF16),
        grid=(NA_HEADS // 2, b),
        in_specs=[
            pl.BlockSpec(memory_space=pltpu.SMEM),
            pl.BlockSpec((1, s, w2), lambda hp, bi: (bi, 0, VN_Q // w2 + hp)),
            pl.BlockSpec((1, s, w2), lambda hp, bi: (bi, 0, VN_K // w2 + hp)),
            pl.BlockSpec((1, s, w2), lambda hp, bi: (bi, 0, VN_V // w2 + hp)),
        ],
        out_specs=pl.BlockSpec((1, s, w2), lambda hp, bi: (bi, 0, hp)),
        scratch_shapes=[
            pltpu.VMEM((2, NA_PATTERNS, nq, nk), F32),
            pltpu.VMEM((2, NA_ROW_OFFS + 1, GRID_W, 2 * GRID_W), F32),
            pltpu.VMEM((2 * NA_GROUP, nq, nk), F32),
            pltpu.VMEM((2 * NA_GROUP, nq, nk), BF16),
        ],
        compiler_params=_cparams(("arbitrary", "arbitrary")),
        name="natten",
    )(rpb_flat, vn3, vn3, vn3)


MIX_BM = 512
MIX_SUB = 256


def _layernorm_rows(z, g, b):
    mu = jnp.mean(z, axis=-1, keepdims=True)
    cen = z - mu
    var = jnp.mean(cen * cen, axis=-1, keepdims=True)
    return cen * lax.rsqrt(var + LN_EPS) * g + b


def _mixout_kernel(ret_ref, na_ref, g1_ref, g2_ref, x_ref, wr_ref, wn_ref, wo_ref, lg_ref, lb_ref, o_ref):
    for r in range(0, MIX_BM, MIX_SUB):
        rows = slice(r, r + MIX_SUB)
        y_ret = jnp.dot(ret_ref[rows, :], wr_ref[...], preferred_element_type=F32)
        y_na = jnp.dot(na_ref[rows, :], wn_ref[...], preferred_element_type=F32)
        merged = g1_ref[rows, :].astype(F32) * y_ret + g2_ref[rows, :].astype(F32) * y_na
        mix = jnp.dot(merged.astype(BF16), wo_ref[...], preferred_element_type=F32)
        z = DEEPNORM_ALPHA * x_ref[rows, :] + mix
        o_ref[rows, :] = _layernorm_rows(z, lg_ref[...], lb_ref[...])


def _mixout(ret2d, na2d, gates2d, x2d, wr, wn, wo, lg, lb):
    t = x2d.shape[0]
    const = lambda i: (0, 0)
    return pl.pallas_call(
        _mixout_kernel,
        out_shape=jax.ShapeDtypeStruct((t, D_MODEL), F32),
        grid=(t // MIX_BM,),
        in_specs=[
            pl.BlockSpec((MIX_BM, RET_HEADS * RET_DV), lambda i: (i, 0)),
            pl.BlockSpec((MIX_BM, NA_HEADS * NA_HD), lambda i: (i, 0)),
            pl.BlockSpec((MIX_BM, D_MODEL), lambda i: (i, 0)),
            pl.BlockSpec((MIX_BM, D_MODEL), lambda i: (i, 1)),
            pl.BlockSpec((MIX_BM, D_MODEL), lambda i: (i, 0)),
            pl.BlockSpec((RET_HEADS * RET_DV, D_MODEL), const),
            pl.BlockSpec((NA_HEADS * NA_HD, D_MODEL), const),
            pl.BlockSpec((D_MODEL, D_MODEL), const),
            pl.BlockSpec((1, D_MODEL), const),
            pl.BlockSpec((1, D_MODEL), const),
        ],
        out_specs=pl.BlockSpec((MIX_BM, D_MODEL), lambda i: (i, 0)),
        compiler_params=_cparams(("arbitrary",)),
        name="mixout",
    )(ret2d, na2d, gates2d, gates2d, x2d, wr, wn, wo, lg, lb)


ROUTE_BM = 512


def _route_tile(x, rw, rb):
    nt_dims = (((1,), (1,)), ((), ()))
    x_hi = x.astype(BF16)
    x_lo = (x - x_hi.astype(F32)).astype(BF16)
    rw_hi = rw.astype(BF16)
    rw_lo = (rw - rw_hi.astype(F32)).astype(BF16)
    both = lax.dot_general(jnp.concatenate([rw_hi, rw_lo], axis=0), x_hi, nt_dims, preferred_element_type=F32)
    logits = (both[:N_EXPERTS] + both[N_EXPERTS:]
              + lax.dot_general(rw_hi, x_lo, nt_dims, preferred_element_type=F32))
    scores = jax.nn.sigmoid(logits)
    sel = scores + rb
    p = [scores[m * N_GROUPS:(m + 1) * N_GROUPS] for m in range(EXPERTS_PER_GROUP)]
    s = [sel[m * N_GROUPS:(m + 1) * N_GROUPS] for m in range(EXPERTS_PER_GROUP)]
    one = jnp.ones_like(s[0])
    zero = jnp.zeros_like(s[0])
    chosen = []
    for m in range(EXPERTS_PER_GROUP):
        rank = zero
        for j in range(EXPERTS_PER_GROUP):
            if j == m:
                continue
            beats = (s[j] >= s[m]) if j < m else (s[j] > s[m])
            rank = rank + jnp.where(beats, one, zero)
        chosen.append(rank < 2.0)
    group_score = zero
    for m in range(EXPERTS_PER_GROUP):
        group_score = group_score + jnp.where(chosen[m], s[m], zero)
    gid = lax.broadcasted_iota(jnp.int32, group_score.shape, 0)
    gmax = jnp.max(group_score, axis=0, keepdims=True)
    gbest = jnp.min(jnp.where(group_score == gmax, gid, N_GROUPS), axis=0, keepdims=True)
    in_best = gid == gbest
    picked = [chosen[m] & in_best for m in range(EXPERTS_PER_GROUP)]
    before = zero
    wa = zero
    wb = zero
    ma = zero
    mb = zero
    for m in range(EXPERTS_PER_GROUP):
        is_a = picked[m] & (before == 0.0)
        is_b = picked[m] & (before == 1.0)
        wa = wa + jnp.where(is_a, p[m], zero)
        wb = wb + jnp.where(is_b, p[m], zero)
        ma = ma + jnp.where(is_a, float(m), 0.0)
        mb = mb + jnp.where(is_b, float(m), 0.0)
        before = before + jnp.where(chosen[m], one, zero)
    wa = jnp.sum(wa, axis=0, keepdims=True)
    wb = jnp.sum(wb, axis=0, keepdims=True)
    ma = jnp.sum(ma, axis=0, keepdims=True)
    mb = jnp.sum(mb, axis=0, keepdims=True)
    denom = wa + wb
    w_lo = wa / denom
    w_hi = wb / denom
    pair = jnp.where(ma == 0.0, jnp.where(mb == 1.0, 0.0, mb),
                     jnp.where(ma == 1.0, jnp.where(mb == 2.0, 1.0, 4.0), 5.0))
    keep = pair == 0.0
    return (gbest * N_PAIRS + pair.astype(jnp.int32),
            jnp.where(keep, w_lo, w_hi), jnp.where(keep, w_hi, w_lo))


CLASS_ROWS = 64
META_LANES = 128


DISPATCH_X = D_MODEL // 2
DISPATCH_W = DISPATCH_X + META_LANES


def _router_kernel(x_ref, rw_ref, rb_ref, tri_ref, dest_ref, xw_ref, meta_ref, cls_s, rank_s, cnt_s):
    phase = pl.program_id(0)
    i = pl.program_id(1)
    bm = ROUTE_BM
    reps = bm // META_LANES
    cid = lax.broadcasted_iota(jnp.int32, (CLASS_ROWS, bm), 0)

    @pl.when((phase == 0) & (i == 0))
    def _():
        cnt_s[...] = jnp.zeros_like(cnt_s)

    @pl.when(phase == 0)
    def _():
        x = x_ref[...]
        cls, w_a, w_b = _route_tile(x, rw_ref[...], rb_ref[...])
        cls_s[i] = cls
        w_rows = jnp.concatenate([w_a, w_b, jnp.zeros((META_LANES - 2, bm), F32)], axis=0)
        xw_ref[:, :DISPATCH_X] = _pack_bf16_pair(x[:, :DISPATCH_X], x[:, DISPATCH_X:])
        xw_ref[:, DISPATCH_X:] = lax.bitcast_convert_type(w_rows.T, U32)
        onehot = (cid == cls).astype(BF16)
        before = jnp.dot(onehot, tri_ref[...], preferred_element_type=F32)
        carry = jnp.concatenate([cnt_s[...]] * reps, axis=1)
        rank = jnp.sum(jnp.where(cid == cls, before + carry, 0.0), axis=0, keepdims=True)
        rank_s[i] = rank.astype(jnp.int32)
        cnt_s[...] = cnt_s[...] + jnp.dot(onehot, jnp.ones((bm, META_LANES), BF16), preferred_element_type=F32)

    @pl.when(phase == 1)
    def _():
        cnt = cnt_s[...]
        nblk = jnp.floor((cnt + (MOE_BLOCK - 1.0)) * (1.0 / MOE_BLOCK))
        rr = lax.broadcasted_iota(jnp.int32, (CLASS_ROWS, CLASS_ROWS), 0)
        cc = lax.broadcasted_iota(jnp.int32, (CLASS_ROWS, CLASS_ROWS), 1)
        lower = (cc < rr).astype(BF16)
        start_blk = jnp.dot(lower, nblk.astype(BF16), preferred_element_type=F32)
        cls = cls_s[i]
        start_t = jnp.concatenate([start_blk] * reps, axis=1)
        start = jnp.sum(jnp.where(cid == cls, start_t, 0.0), axis=0, keepdims=True)
        dest_ref[0] = (start * float(MOE_BLOCK)).astype(jnp.int32) + rank_s[i]

        @pl.when(i == 0)
        def _():
            end_blk = start_blk + nblk
            n_used = jnp.max(end_blk, axis=0, keepdims=True)
            blk = lax.broadcasted_iota(jnp.int32, (CLASS_ROWS, META_LANES), 1).astype(F32)
            blk = jnp.minimum(blk, n_used - 1.0)
            bcls = jnp.sum(jnp.where(end_blk <= blk, 1.0, 0.0), axis=0, keepdims=True)
            grp = jnp.zeros_like(bcls)
            for g in range(1, N_GROUPS):
                grp = grp + jnp.where(bcls >= float(g * N_PAIRS), 1.0, 0.0)
            pair = bcls - grp * float(N_PAIRS)
            slot_a = jnp.zeros_like(pair)
            slot_b = jnp.zeros_like(pair)
            for k in range(N_PAIRS):
                slot_a = jnp.where(pair == float(k), float(PAIR_SLOT_A[k]), slot_a)
                slot_b = jnp.where(pair == float(k), float(PAIR_SLOT_B[k]), slot_b)
            rows = [grp * float(EXPERTS_PER_GROUP) + slot_a, grp * float(EXPERTS_PER_GROUP) + slot_b, n_used]
            rows = rows + [jnp.zeros_like(bcls)] * (8 - len(rows))
            meta_ref[...] = jnp.concatenate(rows, axis=0).astype(jnp.int32)


def _router(x2d, rw_t, rb_t):
    t = x2d.shape[0]
    nt = t // ROUTE_BM
    tri = jnp.asarray(np.triu(np.ones((ROUTE_BM, ROUTE_BM), np.float32), 1), BF16)
    hold = lambda ph, i: ((1 - ph) * i + ph * (nt - 1), 0)
    const = lambda ph, i: (0, 0)
    return pl.pallas_call(
        _router_kernel,
        out_shape=(jax.ShapeDtypeStruct((nt, 1, ROUTE_BM), jnp.int32),
                   jax.ShapeDtypeStruct((t, DISPATCH_W), U32),
                   jax.ShapeDtypeStruct((8, META_LANES), jnp.int32)),
        grid=(2, nt),
        in_specs=[
            pl.BlockSpec((ROUTE_BM, D_MODEL), hold),
            pl.BlockSpec((N_EXPERTS, D_MODEL), const),
            pl.BlockSpec((N_EXPERTS, 1), const),
            pl.BlockSpec((ROUTE_BM, ROUTE_BM), const),
        ],
        out_specs=(pl.BlockSpec((1, 1, ROUTE_BM), lambda ph, i: (ph * i, 0, 0)),
                   pl.BlockSpec((ROUTE_BM, DISPATCH_W), hold),
                   pl.BlockSpec((8, META_LANES), const)),
        scratch_shapes=[
            pltpu.VMEM((nt, 1, ROUTE_BM), jnp.int32),
            pltpu.VMEM((nt, 1, ROUTE_BM), jnp.int32),
            pltpu.VMEM((CLASS_ROWS, META_LANES), F32),
        ],
        compiler_params=_cparams(("arbitrary", "arbitrary")),
        name="router",
    )(x2d, rw_t, rb_t, tri)


ROW_BM = 512


def _start_rows(make_copy):
    for r in range(ROW_BM):
        make_copy(r).start(priority=r % 2)


def _dispatch_kernel(dest_ref, x_ref, zeros_ref, xs_ref, sem):
    del zeros_ref
    base = pl.program_id(0) * ROW_BM
    _start_rows(lambda r: pltpu.make_async_copy(x_ref.at[pl.ds(r, 1), :],
                                                xs_ref.at[pl.ds(dest_ref[base + r], 1), :], sem))
    pltpu.make_async_copy(x_ref, xs_ref.at[pl.ds(0, ROW_BM), :], sem).wait()


def _dispatch(dest, x_rows, n_rows):
    t, width = x_rows.shape
    grid_spec = pltpu.PrefetchScalarGridSpec(
        num_scalar_prefetch=1,
        grid=(t // ROW_BM,),
        in_specs=[
            pl.BlockSpec((ROW_BM, width), lambda i, dest: (i, 0)),
            pl.BlockSpec(memory_space=pl.ANY),
        ],
        out_specs=pl.BlockSpec(memory_space=pl.ANY),
        scratch_shapes=[pltpu.SemaphoreType.DMA],
    )
    return pl.pallas_call(
        _dispatch_kernel,
        out_shape=jax.ShapeDtypeStruct((n_rows, width), x_rows.dtype),
        grid_spec=grid_spec,
        input_output_aliases={2: 0},
        compiler_params=_cparams(("arbitrary",)),
        name="dispatch",
    )(dest, x_rows, jnp.zeros((n_rows, width), x_rows.dtype))


EXPERT_RING = 3
EXPERT_AHEAD = EXPERT_RING - 1


def _experts_kernel(ea_ref, eb_ref, nb_ref, x_ref, wg_hbm, wu_hbm, wd_hbm, o_ref,
                    ring_g, ring_u, ring_d, sems, loads_ref, hg_buf, hu_buf, act_buf, *, layer, n_blocks):
    j = pl.program_id(0)
    nb = nb_ref[0]
    experts_of = (ea_ref, eb_ref)

    def is_load(s, step):
        step = jnp.minimum(step, n_blocks - 1)
        return ((step == 0) | (experts_of[s][step] != experts_of[s][jnp.maximum(step - 1, 0)])).astype(jnp.int32)

    def weight_copies(s, step, buf):
        e = experts_of[s][jnp.minimum(step, n_blocks - 1)]
        return (pltpu.make_async_copy(wg_hbm.at[layer, e], ring_g.at[s, buf], sems.at[s, buf]),
                pltpu.make_async_copy(wu_hbm.at[layer, e], ring_u.at[s, buf], sems.at[s, buf]),
                pltpu.make_async_copy(wd_hbm.at[layer, e], ring_d.at[s, buf], sems.at[s, buf]))

    def start(s, step, load_number):
        for c in weight_copies(s, step, lax.rem(load_number - 1, EXPERT_RING)):
            c.start()

    @pl.when(j < nb)
    def _():
        bufs = []
        for s in range(2):
            loads_now = jnp.where(j == 0, 0, loads_ref[s]) + is_load(s, j)
            loads_ref[s] = loads_now
            buf = lax.rem(loads_now - 1, EXPERT_RING)
            bufs.append(buf)

            for first in range(EXPERT_AHEAD):
                ahead_loads = 1 + sum(is_load(s, k) for k in range(1, first + 1))

                @pl.when((j == 0) & (first < nb) & (is_load(s, first) == 1))
                def _(first=first, ahead_loads=ahead_loads):
                    start(s, first, ahead_loads)

            ahead = j + EXPERT_AHEAD
            ahead_loads = loads_now + sum(is_load(s, j + k) for k in range(1, EXPERT_AHEAD + 1))

            @pl.when((ahead < nb) & (is_load(s, ahead) == 1))
            def _(ahead=ahead, ahead_loads=ahead_loads):
                start(s, ahead, ahead_loads)

            @pl.when(is_load(s, j) == 1)
            def _(buf=buf):
                for c in weight_copies(s, j, buf):
                    c.wait()

        x = jnp.concatenate(_unpack_bf16_pair(x_ref[:, :DISPATCH_X]), axis=1).astype(BF16)
        gate_w = lax.bitcast_convert_type(x_ref[:, DISPATCH_X:], F32)
        for s in range(2):
            hg_buf[s] = jnp.dot(x, ring_g[s, bufs[s]].astype(BF16), preferred_element_type=F32)
            hu_buf[s] = jnp.dot(x, ring_u[s, bufs[s]].astype(BF16), preferred_element_type=F32)
        for s in range(2):
            hg = hg_buf[s]
            act_buf[s] = (hg * jax.nn.sigmoid(hg) * hu_buf[s]).astype(BF16)
        ys = [jnp.dot(act_buf[s], ring_d[s, bufs[s]].astype(BF16), preferred_element_type=F32) for s in range(2)]
        y = gate_w[:, 0:1] * ys[0] + gate_w[:, 1:2] * ys[1]
        o_ref[...] = _pack_bf16_pair(y[:, :DISPATCH_X], y[:, DISPATCH_X:])

    @pl.when(j >= nb)
    def _():
        o_ref[...] = jnp.zeros_like(o_ref)


def _experts(blk_a, blk_b, n_used, xs, wg, wu, wd, layer):
    n_rows = xs.shape[0]
    n_blocks = n_rows // MOE_BLOCK
    in_row_map = lambda j, ea, eb, nb: (jnp.maximum(jnp.minimum(j, nb[0] - 1), 0), 0)
    row_map = lambda j, ea, eb, nb: (j, 0)
    grid_spec = pltpu.PrefetchScalarGridSpec(
        num_scalar_prefetch=3,
        grid=(n_blocks,),
        in_specs=[
            pl.BlockSpec((MOE_BLOCK, DISPATCH_W), in_row_map),
            pl.BlockSpec(memory_space=pl.ANY),
            pl.BlockSpec(memory_space=pl.ANY),
            pl.BlockSpec(memory_space=pl.ANY),
        ],
        out_specs=pl.BlockSpec((MOE_BLOCK, DISPATCH_X), row_map),
        scratch_shapes=[
            pltpu.VMEM((2, EXPERT_RING, D_MODEL, D_EXPERT), F32),
            pltpu.VMEM((2, EXPERT_RING, D_MODEL, D_EXPERT), F32),
            pltpu.VMEM((2, EXPERT_RING, D_EXPERT, D_MODEL), F32),
            pltpu.SemaphoreType.DMA((2, EXPERT_RING)),
            pltpu.SMEM((2,), jnp.int32),
            pltpu.VMEM((2, MOE_BLOCK, D_EXPERT), F32),
            pltpu.VMEM((2, MOE_BLOCK, D_EXPERT), F32),
            pltpu.VMEM((2, MOE_BLOCK, D_EXPERT), BF16),
        ],
    )
    return pl.pallas_call(
        functools.partial(_experts_kernel, layer=layer, n_blocks=n_blocks),
        out_shape=jax.ShapeDtypeStruct((n_rows, DISPATCH_X), U32),
        grid_spec=grid_spec,
        compiler_params=_cparams(("arbitrary",)),
        name="experts",
    )(blk_a, blk_b, n_used, xs, wg, wu, wd)


COMBINE_CHUNK = 64


def _combine_ln_kernel(dest_ref, x_ref, lg_ref, lb_ref, ys_ref, o_ref, obf_ref, ybuf, sem):
    i = pl.program_id(0)
    n = pl.num_programs(0)
    slot = i % 2

    def row_copy(tile, to_slot, r):
        return pltpu.make_async_copy(ys_ref.at[pl.ds(dest_ref[tile * ROW_BM + r], 1), :],
                                     ybuf.at[to_slot, pl.ds(r, 1), :], sem.at[to_slot])

    def wait_tile(of_slot):
        pltpu.make_async_copy(ys_ref.at[pl.ds(0, ROW_BM), :], ybuf.at[of_slot], sem.at[of_slot]).wait()

    @pl.when(i == 0)
    def _():
        _start_rows(lambda r: row_copy(0, 0, r))

    wait_tile(slot)
    nxt = lax.rem(i + 1, n)
    for c in range(0, ROW_BM, COMBINE_CHUNK):
        rows = slice(c, c + COMBINE_CHUNK)
        y = jnp.concatenate(_unpack_bf16_pair(ybuf[slot, rows, :]), axis=1)
        z = DEEPNORM_ALPHA * x_ref[rows, :] + y
        out = _layernorm_rows(z, lg_ref[...], lb_ref[...])
        o_ref[rows, :] = out
        obf_ref[rows, :] = out.astype(BF16)
        for r in range(c, c + COMBINE_CHUNK):
            row_copy(nxt, 1 - slot, r).start(priority=r % 2)

    @pl.when(i == n - 1)
    def _():
        wait_tile(1 - slot)


def _combine_ln(dest, x2d, ys, lg, lb):
    t = x2d.shape[0]
    const = lambda i, dest: (0, 0)
    grid_spec = pltpu.PrefetchScalarGridSpec(
        num_scalar_prefetch=1,
        grid=(t // ROW_BM,),
        in_specs=[
            pl.BlockSpec((ROW_BM, D_MODEL), lambda i, dest: (i, 0)),
            pl.BlockSpec((1, D_MODEL), const),
            pl.BlockSpec((1, D_MODEL), const),
            pl.BlockSpec(memory_space=pl.ANY),
        ],
        out_specs=(pl.BlockSpec((ROW_BM, D_MODEL), lambda i, dest: (i, 0)),
                   pl.BlockSpec((ROW_BM, D_MODEL), lambda i, dest: (i, 0))),
        scratch_shapes=[pltpu.VMEM((2, ROW_BM, DISPATCH_X), U32), pltpu.SemaphoreType.DMA((2,))],
    )
    return pl.pallas_call(
        _combine_ln_kernel,
        out_shape=(jax.ShapeDtypeStruct((t, D_MODEL), F32), jax.ShapeDtypeStruct((t, D_MODEL), BF16)),
        grid_spec=grid_spec,
        compiler_params=_cparams(("arbitrary",)),
        name="combine_ln",
    )(dest, x2d, lg, lb, ys)


def _moe(x2d, rw_t, rb_t, wg, wu, wd, layer, lg, lb):
    t = x2d.shape[0]
    n_blocks = (t + N_CLASSES * (MOE_BLOCK - 1)) // MOE_BLOCK
    assert n_blocks <= META_LANES
    dest3, xw, meta = _router(x2d, rw_t, rb_t)
    dest = dest3.reshape(t)
    xs = _dispatch(dest, xw, n_blocks * MOE_BLOCK)
    ys = _experts(meta[0, :n_blocks], meta[1, :n_blocks], meta[2, :1], xs, wg, wu, wd, layer)
    return _combine_ln(dest, x2d, ys, lg, lb)


def _rope_tables(seq):
    half = RET_DK // 2
    pos = jnp.arange(seq, dtype=F32)
    freqs = ROPE_BASE ** (-jnp.arange(half, dtype=F32) / half)
    ang = pos[:, None] * freqs[None, :]
    return jnp.cos(ang), jnp.sin(ang)


def kernel(x, w_in, ret_decay_logit, w_ret_o, na_rpb, w_na_o, w_out, ln_mix_g, ln_mix_b, router_w, router_bias,
           w_exp_gate, w_exp_up, w_exp_down, ln_ffn_g, ln_ffn_b):
    b, s, d = x.shape
    depth = w_in.shape[0]
    t = b * s
    rows = s // GRID_W
    cos, sin = _rope_tables(s)
    rw_t = router_w.astype(F32).T.reshape(N_GROUPS, EXPERTS_PER_GROUP, d).transpose(1, 0, 2).reshape(N_EXPERTS, d)
    rb_t = router_bias.astype(F32).reshape(N_GROUPS, EXPERTS_PER_GROUP).T.reshape(N_EXPERTS, 1)
    x2d = x.reshape(t, d)
    x_bf = x2d.astype(BF16)
    w_in = w_in.astype(F32)
    for l in range(depth):
        qk = _inproj(x_bf, w_in, l, cos, sin, s, "rotary").reshape(b, s, -1)
        vn = _inproj(x_bf, w_in, l, cos, sin, s, "scale").reshape(b, s, -1)
        gr = _inproj(x_bf, w_in, l, cos, sin, s, "silu").reshape(b, s, -1)
        gates = _inproj(x_bf, w_in, l, cos, sin, s, "sigmoid")
        ret = _retention(qk, vn, gr, ret_decay_logit[l].astype(F32))
        na = _na(vn, na_rpb[l].astype(F32).reshape(-1))
        x2d = _mixout(ret.reshape(t, -1), na.reshape(t, -1), gates, x2d,
                      w_ret_o[l].astype(BF16), w_na_o[l].astype(BF16), w_out[l].astype(BF16),
                      ln_mix_g[l].reshape(1, d).astype(F32), ln_mix_b[l].reshape(1, d).astype(F32))
        x2d, x_bf = _moe(x2d, rw_t, rb_t, w_exp_gate, w_exp_up, w_exp_down, l,
                         ln_ffn_g[l].reshape(1, d).astype(F32), ln_ffn_b[l].reshape(1, d).astype(F32))
    return x2d.reshape(b, s, d)
```

```python
import functools

import numpy as np
import jax
import jax.numpy as jnp
from jax import lax
from jax.experimental import pallas as pl
from jax.experimental.pallas import tpu as pltpu

F32 = jnp.float32
BF16 = jnp.bfloat16

D_MODEL = 1024
GRID_W = 64
RET_HEADS = 4
RET_DK = 256
RET_DV = 512
ROPE_BASE = 10000.0
NA_HEADS = 16
NA_HD = 64
NA_KH = 8
NA_KW = 16
N_EXPERTS = 32
N_GROUPS = 8
EXPERTS_PER_GROUP = 4
D_EXPERT = 512
LN_EPS = 1e-5
GN_EPS = 1e-5
MODEL_DEPTH = 4
DEEPNORM_ALPHA = (2 * MODEL_DEPTH) ** 0.25

D_IN = 11264

RET_CHUNK = 256
NA_QROWS = 4
NA_KROWS = 12
NA_NEG = -1e30
MOE_BLOCK = 256
N_PAIRS = 6
PAIR_SLOT_A = (0, 2, 2, 3, 3, 3)
PAIR_SLOT_B = (1, 1, 0, 0, 1, 2)
N_CLASSES = N_GROUPS * N_PAIRS
VMEM_LIMIT = 56 * 1024 * 1024


def _cparams(sem):
    return pltpu.CompilerParams(dimension_semantics=sem, vmem_limit_bytes=VMEM_LIMIT)


U32 = jnp.uint32
_HIGH_HALF = 0xFFFF0000


def _pack_bf16_pair(hi, lo):
    hi_bits = lax.bitcast_convert_type(hi.astype(jnp.bfloat16).astype(F32), U32)
    lo_bits = lax.bitcast_convert_type(lo.astype(jnp.bfloat16).astype(F32), U32)
    return (hi_bits & jnp.uint32(_HIGH_HALF)) | (lo_bits >> 16)


def _unpack_bf16_pair(packed):
    hi = lax.bitcast_convert_type(packed & jnp.uint32(_HIGH_HALF), F32)
    lo = lax.bitcast_convert_type(packed << 16, F32)
    return hi, lo


IN_CHUNK = 256
LOG2E = 1.4426950408889634

IN_GROUPS = {
    "rotary": (1024, 2048, (0,)),
    "scale": (2048, 1024, (2, 3, 6, 7, 8)),
    "silu": (1024, 2048, (2,)),
    "sigmoid": (2048, 1024, (9, 10)),
}
VN_Q = 2048
VN_K = 3072
VN_V = 4096


def _sigmoid(x):
    return 0.5 * jnp.tanh(0.5 * x) + 0.5


def _inproj_kernel(x_ref, w_ref, cos_ref, sin_ref, o_ref, wbf_ref, *, kind, bn):
    j = pl.program_id(0)

    @pl.when(pl.program_id(1) == 0)
    def _():
        wbf_ref[...] = w_ref[0].astype(BF16)

    x = x_ref[...]
    half = RET_DK // 2
    for c in range(bn // IN_CHUNK):
        lo = c * IN_CHUNK
        acc = jnp.dot(x, wbf_ref[:, lo:lo + IN_CHUNK], preferred_element_type=F32)
        if kind == "rotary":
            scale = 1.0 if lo < RET_HEADS * RET_DK else RET_DK ** -0.5
            cos = cos_ref[...] * scale
            sin = sin_ref[...] * scale
            x1 = acc[:, :half]
            x2 = acc[:, half:]
            o_ref[:, lo:lo + half] = (x1 * cos - x2 * sin).astype(o_ref.dtype)
            o_ref[:, lo + half:lo + IN_CHUNK] = (x1 * sin + x2 * cos).astype(o_ref.dtype)
        elif kind == "scale":
            scale = jnp.where(j == 2, NA_HD ** -0.5 * LOG2E, 1.0)
            o_ref[:, lo:lo + IN_CHUNK] = (acc * scale).astype(o_ref.dtype)
        elif kind == "silu":
            o_ref[:, lo:lo + IN_CHUNK] = (acc * _sigmoid(acc)).astype(o_ref.dtype)
        else:
            o_ref[:, lo:lo + IN_CHUNK] = _sigmoid(acc).astype(o_ref.dtype)


def _inproj(x_bf, w_in, layer, cos, sin, seq, kind):
    t = x_bf.shape[0]
    bm, bn, blocks = IN_GROUPS[kind]
    pos_blocks = seq // bm

    def wcol(j):
        col = blocks[0]
        for k in range(1, len(blocks)):
            col = jnp.where(j == k, blocks[k], col)
        return col

    pos_map = (lambda j, i: (i % pos_blocks, 0)) if kind == "rotary" else (lambda j, i: (0, 0))
    return pl.pallas_call(
        functools.partial(_inproj_kernel, kind=kind, bn=bn),
        out_shape=jax.ShapeDtypeStruct((t, len(blocks) * bn), BF16),
        grid=(len(blocks), t // bm),
        in_specs=[
            pl.BlockSpec((bm, D_MODEL), lambda j, i: (i, 0)),
            pl.BlockSpec((1, D_MODEL, bn), lambda j, i: (layer, 0, wcol(j))),
            pl.BlockSpec((bm, RET_DK // 2), pos_map),
            pl.BlockSpec((bm, RET_DK // 2), pos_map),
        ],
        out_specs=pl.BlockSpec((bm, bn), lambda j, i: (i, j)),
        scratch_shapes=[pltpu.VMEM((D_MODEL, bn), BF16)],
        compiler_params=_cparams(("arbitrary", "arbitrary")),
        name="inproj_" + kind,
    )(x_bf, w_in, cos, sin)


def _log_sigmoid(x):
    return jnp.minimum(x, 0.0) - jnp.log1p(jnp.exp(-jnp.abs(x)))


RET_GROUP = 8


def _retention_kernel(dl_ref, q_ref, k_ref, v_ref, g_ref, o_ref,
                      sf_ref, sb_ref, st_ref, stb_ref, dm_ref, qdf_ref, qdb_ref, kdf_ref, kdb_ref,
                      p_buf, qf_buf, qb_buf, out_buf, *, nc):
    c_len = RET_CHUNK
    h = pl.program_id(1)
    lgf = _log_sigmoid(jnp.full((c_len, RET_DK), dl_ref[0, h], F32))
    lgb = _log_sigmoid(jnp.full((c_len, RET_DK), dl_ref[1, h], F32))
    ri = lax.broadcasted_iota(jnp.int32, (c_len, RET_DK), 0).astype(F32)
    qdf_ref[...] = jnp.exp(lgf * (ri + 1.0))
    qdb_ref[...] = jnp.exp(lgb * (c_len - ri))
    kdf_ref[...] = jnp.exp(lgf * (c_len - 1.0 - ri))
    kdb_ref[...] = jnp.exp(lgb * ri)
    rr = lax.broadcasted_iota(jnp.int32, (c_len, c_len), 0).astype(F32)
    cc = lax.broadcasted_iota(jnp.int32, (c_len, c_len), 1).astype(F32)
    diff = rr - cc
    lgf_cc = _log_sigmoid(jnp.full((c_len, c_len), dl_ref[0, h], F32))
    lgb_cc = _log_sigmoid(jnp.full((c_len, c_len), dl_ref[1, h], F32))
    dm_ref[...] = jnp.where(diff >= 0.0, jnp.exp(lgf_cc * jnp.maximum(diff, 0.0)),
                            jnp.exp(lgb_cc * jnp.maximum(-diff, 0.0)))
    chunk_f = jnp.exp(lgf[:1, :1] * float(c_len))
    chunk_b = jnp.exp(lgb[:1, :1] * float(c_len))

    tn_dims = (((0,), (0,)), ((), ()))
    nt_dims = (((1,), (1,)), ((), ()))

    st_ref[...] = jnp.zeros_like(st_ref)
    stb_ref[...] = jnp.zeros_like(stb_ref)

    def state_body(i, carry):
        for c, all_ref, cur_ref, kdec_ref, chunk_dec in ((i, sf_ref, st_ref, kdf_ref, chunk_f),
                                                         (nc - 1 - i, sb_ref, stb_ref, kdb_ref, chunk_b)):
            off = pl.multiple_of(c * c_len, c_len)
            all_ref[c] = cur_ref[...].astype(BF16)
            kc = k_ref[0, pl.ds(off, c_len), :].astype(F32)
            vc = v_ref[0, pl.ds(off, c_len), :]
            kd = (kc * kdec_ref[...]).astype(BF16)
            upd = lax.dot_general(kd, vc, tn_dims, preferred_element_type=F32)
            cur_ref[...] = cur_ref[...] * chunk_dec + upd
        return carry

    lax.fori_loop(0, nc, state_body, 0, unroll=4)

    def out_body(g, carry):
        offs = [pl.multiple_of((g * RET_GROUP + u) * c_len, c_len) for u in range(RET_GROUP)]
        for u, off in enumerate(offs):
            qb = q_ref[0, pl.ds(off, c_len), :]
            kb = k_ref[0, pl.ds(off, c_len), :]
            qc = qb.astype(F32)
            s = lax.dot_general(qb, kb, nt_dims, preferred_element_type=F32)
            p_buf[u] = (s * dm_ref[...]).astype(BF16)
            qf_buf[u] = (qc * qdf_ref[...]).astype(BF16)
            qb_buf[u] = (qc * qdb_ref[...]).astype(BF16)
        for u, off in enumerate(offs):
            c = g * RET_GROUP + u
            out = jnp.dot(p_buf[u], v_ref[0, pl.ds(off, c_len), :], preferred_element_type=F32)
            out = out + jnp.dot(qf_buf[u], sf_ref[c], preferred_element_type=F32)
            out_buf[u] = out + jnp.dot(qb_buf[u], sb_ref[c], preferred_element_type=F32)
        for u, off in enumerate(offs):
            out = out_buf[u]
            mu = jnp.mean(out, axis=-1, keepdims=True)
            cen = out - mu
            var = jnp.mean(cen * cen, axis=-1, keepdims=True)
            y = cen * lax.rsqrt(var + GN_EPS)
            gate = g_ref[0, pl.ds(off, c_len), :].astype(F32)
            o_ref[0, pl.ds(off, c_len), :] = (gate * y).astype(o_ref.dtype)
        return carry

    lax.fori_loop(0, nc // RET_GROUP, out_body, 0)


def _retention(qk3, vn3, gr3, decay_logit):
    b, s, _ = qk3.shape
    nc = s // RET_CHUNK
    return pl.pallas_call(
        functools.partial(_retention_kernel, nc=nc),
        out_shape=jax.ShapeDtypeStruct((b, s, RET_HEADS * RET_DV), BF16),
        grid=(b, RET_HEADS),
        in_specs=[
            pl.BlockSpec(memory_space=pltpu.SMEM),
            pl.BlockSpec((1, s, RET_DK), lambda bi, h: (bi, 0, h)),
            pl.BlockSpec((1, s, RET_DK), lambda bi, h: (bi, 0, RET_HEADS + h)),
            pl.BlockSpec((1, s, RET_DV), lambda bi, h: (bi, 0, h)),
            pl.BlockSpec((1, s, RET_DV), lambda bi, h: (bi, 0, h)),
        ],
        out_specs=pl.BlockSpec((1, s, RET_DV), lambda bi, h: (bi, 0, h)),
        scratch_shapes=[
            pltpu.VMEM((nc, RET_DK, RET_DV), BF16),
            pltpu.VMEM((nc, RET_DK, RET_DV), BF16),
            pltpu.VMEM((RET_DK, RET_DV), F32),
            pltpu.VMEM((RET_DK, RET_DV), F32),
            pltpu.VMEM((RET_CHUNK, RET_CHUNK), F32),
            pltpu.VMEM((RET_CHUNK, RET_DK), F32),
            pltpu.VMEM((RET_CHUNK, RET_DK), F32),
            pltpu.VMEM((RET_CHUNK, RET_DK), F32),
            pltpu.VMEM((RET_CHUNK, RET_DK), F32),
            pltpu.VMEM((RET_GROUP, RET_CHUNK, RET_CHUNK), BF16),
            pltpu.VMEM((RET_GROUP, RET_CHUNK, RET_DK), BF16),
            pltpu.VMEM((RET_GROUP, RET_CHUNK, RET_DK), BF16),
            pltpu.VMEM((RET_GROUP, RET_CHUNK, RET_DV), F32),
        ],
        compiler_params=_cparams(("arbitrary", "arbitrary")),
        name="retention",
    )(decay_logit, qk3, qk3, vn3, gr3)


NA_ROW_OFFS = 2 * NA_KH - 1
NA_COL_OFFS = 2 * NA_KW - 1
NA_PATTERNS = 3


def _na_row_offsets(rows):
    n_tiles = rows // NA_QROWS
    offs = np.full((NA_PATTERNS, NA_QROWS, NA_KROWS), NA_ROW_OFFS, np.int32)
    for p, t in enumerate((0, 1, n_tiles - 1)):
        kstart = int(np.clip(NA_QROWS * t - NA_KH // 2, 0, rows - NA_KROWS))
        for rr in range(NA_QROWS):
            r = NA_QROWS * t + rr
            rs = int(np.clip(r - NA_KH // 2, 0, rows - NA_KH))
            for i in range(NA_KROWS):
                krow = kstart + i
                if rs <= krow < rs + NA_KH:
                    offs[p, rr, i] = krow - r + (NA_KH - 1)
    return offs


def _na_build_bias(rpb_ref, e_ref, bias_ref, head0, rows):
    wide = 2 * GRID_W
    lane = lax.broadcasted_iota(jnp.int32, (GRID_W, wide), 1)
    col = lax.broadcasted_iota(jnp.int32, (GRID_W, wide), 0)
    kcol = jnp.where(lane < GRID_W, lane, lane - GRID_W)
    cstart = jnp.clip(col - NA_KW // 2, 0, GRID_W - NA_KW)
    col_ok = (kcol >= cstart) & (kcol < cstart + NA_KW)
    coff = kcol - col + (NA_KW - 1)
    neg = jnp.full((GRID_W, wide), NA_NEG, F32)
    left = lane < GRID_W
    offs = _na_row_offsets(rows)
    for hh in range(2):
        base = (head0 + hh) * (NA_ROW_OFFS * NA_COL_OFFS)
        for ro in range(NA_ROW_OFFS):
            acc = neg
            for d in range(NA_COL_OFFS):
                acc = jnp.where(coff == d, rpb_ref[base + ro * NA_COL_OFFS + d] * LOG2E, acc)
            e_ref[hh, ro] = jnp.where(col_ok, acc, neg)
        e_ref[hh, NA_ROW_OFFS] = neg
        for p in range(NA_PATTERNS):
            for rr in range(NA_QROWS):
                for ip in range(NA_KROWS // 2):
                    tile = jnp.where(left, e_ref[hh, int(offs[p, rr, 2 * ip])], e_ref[hh, int(offs[p, rr, 2 * ip + 1])])
                    bias_ref[hh, p, rr * GRID_W:(rr + 1) * GRID_W, ip * wide:(ip + 1) * wide] = tile


NA_GROUP = 8


def _na_kernel(rpb_ref, q_ref, k_ref, v_ref, o_ref, bias_ref, e_ref, s_buf, p_buf, *, n_tiles, rows):
    nq = NA_QROWS * GRID_W
    nk = NA_KROWS * GRID_W
    nt_dims = (((1,), (1,)), ((), ()))
    first = lax.broadcasted_iota(jnp.int32, (nq, 2 * NA_HD), 1) < NA_HD
    kfirst = lax.broadcasted_iota(jnp.int32, (nk, 2 * NA_HD), 1) < NA_HD

    @pl.when(pl.program_id(1) == 0)
    def _():
        _na_build_bias(rpb_ref, e_ref, bias_ref, 2 * pl.program_id(0), rows)

    def body(g, carry):
        offs = []
        for u in range(NA_GROUP):
            t = g * NA_GROUP + u
            qoff = pl.multiple_of(t * nq, nq)
            krow0 = jnp.clip(NA_QROWS * t - NA_KH // 2, 0, rows - NA_KROWS)
            koff = pl.multiple_of(krow0 * GRID_W, nq)
            pat = jnp.where(t == 0, 0, jnp.where(t == n_tiles - 1, 2, 1))
            offs.append((qoff, koff, pat))
        for u, (qoff, koff, pat) in enumerate(offs):
            q2 = q_ref[0, pl.ds(qoff, nq), :]
            k2 = k_ref[0, pl.ds(koff, nk), :]
            for hh in range(2):
                qm = jnp.where(first if hh == 0 else jnp.logical_not(first), q2, jnp.zeros_like(q2))
                s_buf[2 * u + hh] = lax.dot_general(qm, k2, nt_dims, preferred_element_type=F32) + bias_ref[hh, pat]
        for n in range(2 * NA_GROUP):
            s = s_buf[n]
            p_buf[n] = jnp.exp2(s - jnp.max(s, axis=-1, keepdims=True)).astype(BF16)
        for u, (qoff, koff, pat) in enumerate(offs):
            v2 = v_ref[0, pl.ds(koff, nk), :]
            outs = []
            for hh in range(2):
                vv = jnp.where(kfirst if hh == 0 else jnp.logical_not(kfirst), v2, jnp.ones_like(v2))
                o = jnp.dot(p_buf[2 * u + hh], vv, preferred_element_type=F32)
                outs.append(o / pltpu.roll(o, NA_HD, axis=1))
            o_ref[0, pl.ds(qoff, nq), :] = jnp.where(first, outs[0], outs[1]).astype(o_ref.dtype)
        return carry

    lax.fori_loop(0, n_tiles // NA_GROUP, body, 0)


def _na(vn3, rpb_flat):
    b, s, _ = vn3.shape
    rows = s // GRID_W
    n_tiles = rows // NA_QROWS
    w2 = 2 * NA_HD
    nq = NA_QROWS * GRID_W
    nk = NA_KROWS * GRID_W
    return pl.pallas_call(
        functools.partial(_na_kernel, n_tiles=n_tiles, rows=rows),
        out_shape=jax.ShapeDtypeStruct((b, s, NA_HEADS * NA_HD), BF16),
        grid=(NA_HEADS // 2, b),
        in_specs=[
            pl.BlockSpec(memory_space=pltpu.SMEM),
            pl.BlockSpec((1, s, w2), lambda hp, bi: (bi, 0, VN_Q // w2 + hp)),
            pl.BlockSpec((1, s, w2), lambda hp, bi: (bi, 0, VN_K // w2 + hp)),
            pl.BlockSpec((1, s, w2), lambda hp, bi: (bi, 0, VN_V // w2 + hp)),
        ],
        out_specs=pl.BlockSpec((1, s, w2), lambda hp, bi: (bi, 0, hp)),
        scratch_shapes=[
            pltpu.VMEM((2, NA_PATTERNS, nq, nk), F32),
            pltpu.VMEM((2, NA_ROW_OFFS + 1, GRID_W, 2 * GRID_W), F32),
            pltpu.VMEM((2 * NA_GROUP, nq, nk), F32),
            pltpu.VMEM((2 * NA_GROUP, nq, nk), BF16),
        ],
        compiler_params=_cparams(("arbitrary", "arbitrary")),
        name="natten",
    )(rpb_flat, vn3, vn3, vn3)


MIX_BM = 512
MIX_SUB = 256


def _layernorm_rows(z, g, b):
    mu = jnp.mean(z, axis=-1, keepdims=True)
    cen = z - mu
    var = jnp.mean(cen * cen, axis=-1, keepdims=True)
    return cen * lax.rsqrt(var + LN_EPS) * g + b


def _mixout_kernel(ret_ref, na_ref, g1_ref, g2_ref, x_ref, wr_ref, wn_ref, wo_ref, lg_ref, lb_ref, o_ref):
    for r in range(0, MIX_BM, MIX_SUB):
        rows = slice(r, r + MIX_SUB)
        y_ret = jnp.dot(ret_ref[rows, :], wr_ref[...], preferred_element_type=F32)
        y_na = jnp.dot(na_ref[rows, :], wn_ref[...], preferred_element_type=F32)
        merged = g1_ref[rows, :].astype(F32) * y_ret + g2_ref[rows, :].astype(F32) * y_na
        mix = jnp.dot(merged.astype(BF16), wo_ref[...], preferred_element_type=F32)
        z = DEEPNORM_ALPHA * x_ref[rows, :] + mix
        o_ref[rows, :] = _layernorm_rows(z, lg_ref[...], lb_ref[...])


def _mixout(ret2d, na2d, gates2d, x2d, wr, wn, wo, lg, lb):
    t = x2d.shape[0]
    const = lambda i: (0, 0)
    return pl.pallas_call(
        _mixout_kernel,
        out_shape=jax.ShapeDtypeStruct((t, D_MODEL), F32),
        grid=(t // MIX_BM,),
        in_specs=[
            pl.BlockSpec((MIX_BM, RET_HEADS * RET_DV), lambda i: (i, 0)),
            pl.BlockSpec((MIX_BM, NA_HEADS * NA_HD), lambda i: (i, 0)),
            pl.BlockSpec((MIX_BM, D_MODEL), lambda i: (i, 0)),
            pl.BlockSpec((MIX_BM, D_MODEL), lambda i: (i, 1)),
            pl.BlockSpec((MIX_BM, D_MODEL), lambda i: (i, 0)),
            pl.BlockSpec((RET_HEADS * RET_DV, D_MODEL), const),
            pl.BlockSpec((NA_HEADS * NA_HD, D_MODEL), const),
            pl.BlockSpec((D_MODEL, D_MODEL), const),
            pl.BlockSpec((1, D_MODEL), const),
            pl.BlockSpec((1, D_MODEL), const),
        ],
        out_specs=pl.BlockSpec((MIX_BM, D_MODEL), lambda i: (i, 0)),
        compiler_params=_cparams(("arbitrary",)),
        name="mixout",
    )(ret2d, na2d, gates2d, gates2d, x2d, wr, wn, wo, lg, lb)


ROUTE_BM = 512


def _route_tile(x, rw, rb):
    nt_dims = (((1,), (1,)), ((), ()))
    x_hi = x.astype(BF16)
    x_lo = (x - x_hi.astype(F32)).astype(BF16)
    rw_hi = rw.astype(BF16)
    rw_lo = (rw - rw_hi.astype(F32)).astype(BF16)
    both = lax.dot_general(jnp.concatenate([rw_hi, rw_lo], axis=0), x_hi, nt_dims, preferred_element_type=F32)
    logits = (both[:N_EXPERTS] + both[N_EXPERTS:]
              + lax.dot_general(rw_hi, x_lo, nt_dims, preferred_element_type=F32))
    scores = jax.nn.sigmoid(logits)
    sel = scores + rb
    p = [scores[m * N_GROUPS:(m + 1) * N_GROUPS] for m in range(EXPERTS_PER_GROUP)]
    s = [sel[m * N_GROUPS:(m + 1) * N_GROUPS] for m in range(EXPERTS_PER_GROUP)]
    one = jnp.ones_like(s[0])
    zero = jnp.zeros_like(s[0])
    chosen = []
    for m in range(EXPERTS_PER_GROUP):
        rank = zero
        for j in range(EXPERTS_PER_GROUP):
            if j == m:
                continue
            beats = (s[j] >= s[m]) if j < m else (s[j] > s[m])
            rank = rank + jnp.where(beats, one, zero)
        chosen.append(rank < 2.0)
    group_score = zero
    for m in range(EXPERTS_PER_GROUP):
        group_score = group_score + jnp.where(chosen[m], s[m], zero)
    gid = lax.broadcasted_iota(jnp.int32, group_score.shape, 0)
    gmax = jnp.max(group_score, axis=0, keepdims=True)
    gbest = jnp.min(jnp.where(group_score == gmax, gid, N_GROUPS), axis=0, keepdims=True)
    in_best = gid == gbest
    picked = [chosen[m] & in_best for m in range(EXPERTS_PER_GROUP)]
    before = zero
    wa = zero
    wb = zero
    ma = zero
    mb = zero
    for m in range(EXPERTS_PER_GROUP):
        is_a = picked[m] & (before == 0.0)
        is_b = picked[m] & (before == 1.0)
        wa = wa + jnp.where(is_a, p[m], zero)
        wb = wb + jnp.where(is_b, p[m], zero)
        ma = ma + jnp.where(is_a, float(m), 0.0)
        mb = mb + jnp.where(is_b, float(m), 0.0)
        before = before + jnp.where(chosen[m], one, zero)
    wa = jnp.sum(wa, axis=0, keepdims=True)
    wb = jnp.sum(wb, axis=0, keepdims=True)
    ma = jnp.sum(ma, axis=0, keepdims=True)
    mb = jnp.sum(mb, axis=0, keepdims=True)
    denom = wa + wb
    w_lo = wa / denom
    w_hi = wb / denom
    pair = jnp.where(ma == 0.0, jnp.where(mb == 1.0, 0.0, mb),
                     jnp.where(ma == 1.0, jnp.where(mb == 2.0, 1.0, 4.0), 5.0))
    keep = pair == 0.0
    return (gbest * N_PAIRS + pair.astype(jnp.int32),
            jnp.where(keep, w_lo, w_hi), jnp.where(keep, w_hi, w_lo))


CLASS_ROWS = 64
META_LANES = 128


DISPATCH_X = D_MODEL // 2
DISPATCH_W = DISPATCH_X + META_LANES


def _router_kernel(x_ref, rw_ref, rb_ref, tri_ref, dest_ref, xw_ref, meta_ref, cls_s, rank_s, cnt_s):
    phase = pl.program_id(0)
    i = pl.program_id(1)
    bm = ROUTE_BM
    reps = bm // META_LANES
    cid = lax.broadcasted_iota(jnp.int32, (CLASS_ROWS, bm), 0)

    @pl.when((phase == 0) & (i == 0))
    def _():
        cnt_s[...] = jnp.zeros_like(cnt_s)

    @pl.when(phase == 0)
    def _():
        x = x_ref[...]
        cls, w_a, w_b = _route_tile(x, rw_ref[...], rb_ref[...])
        cls_s[i] = cls
        w_rows = jnp.concatenate([w_a, w_b, jnp.zeros((META_LANES - 2, bm), F32)], axis=0)
        xw_ref[:, :DISPATCH_X] = _pack_bf16_pair(x[:, :DISPATCH_X], x[:, DISPATCH_X:])
        xw_ref[:, DISPATCH_X:] = lax.bitcast_convert_type(w_rows.T, U32)
        onehot = (cid == cls).astype(BF16)
        before = jnp.dot(onehot, tri_ref[...], preferred_element_type=F32)
        carry = jnp.concatenate([cnt_s[...]] * reps, axis=1)
        rank = jnp.sum(jnp.where(cid == cls, before + carry, 0.0), axis=0, keepdims=True)
        rank_s[i] = rank.astype(jnp.int32)
        cnt_s[...] = cnt_s[...] + jnp.dot(onehot, jnp.ones((bm, META_LANES), BF16), preferred_element_type=F32)

    @pl.when(phase == 1)
    def _():
        cnt = cnt_s[...]
        nblk = jnp.floor((cnt + (MOE_BLOCK - 1.0)) * (1.0 / MOE_BLOCK))
        rr = lax.broadcasted_iota(jnp.int32, (CLASS_ROWS, CLASS_ROWS), 0)
        cc = lax.broadcasted_iota(jnp.int32, (CLASS_ROWS, CLASS_ROWS), 1)
        lower = (cc < rr).astype(BF16)
        start_blk = jnp.dot(lower, nblk.astype(BF16), preferred_element_type=F32)
        cls = cls_s[i]
        start_t = jnp.concatenate([start_blk] * reps, axis=1)
        start = jnp.sum(jnp.where(cid == cls, start_t, 0.0), axis=0, keepdims=True)
        dest_ref[0] = (start * float(MOE_BLOCK)).astype(jnp.int32) + rank_s[i]

        @pl.when(i == 0)
        def _():
            end_blk = start_blk + nblk
            n_used = jnp.max(end_blk, axis=0, keepdims=True)
            blk = lax.broadcasted_iota(jnp.int32, (CLASS_ROWS, META_LANES), 1).astype(F32)
            blk = jnp.minimum(blk, n_used - 1.0)
            bcls = jnp.sum(jnp.where(end_blk <= blk, 1.0, 0.0), axis=0, keepdims=True)
            grp = jnp.zeros_like(bcls)
            for g in range(1, N_GROUPS):
                grp = grp + jnp.where(bcls >= float(g * N_PAIRS), 1.0, 0.0)
            pair = bcls - grp * float(N_PAIRS)
            slot_a = jnp.zeros_like(pair)
            slot_b = jnp.zeros_like(pair)
            for k in range(N_PAIRS):
                slot_a = jnp.where(pair == float(k), float(PAIR_SLOT_A[k]), slot_a)
                slot_b = jnp.where(pair == float(k), float(PAIR_SLOT_B[k]), slot_b)
            rows = [grp * float(EXPERTS_PER_GROUP) + slot_a, grp * float(EXPERTS_PER_GROUP) + slot_b, n_used]
            rows = rows + [jnp.zeros_like(bcls)] * (8 - len(rows))
            meta_ref[...] = jnp.concatenate(rows, axis=0).astype(jnp.int32)


def _router(x2d, rw_t, rb_t):
    t = x2d.shape[0]
    nt = t // ROUTE_BM
    tri = jnp.asarray(np.triu(np.ones((ROUTE_BM, ROUTE_BM), np.float32), 1), BF16)
    hold = lambda ph, i: ((1 - ph) * i + ph * (nt - 1), 0)
    const = lambda ph, i: (0, 0)
    return pl.pallas_call(
        _router_kernel,
        out_shape=(jax.ShapeDtypeStruct((nt, 1, ROUTE_BM), jnp.int32),
                   jax.ShapeDtypeStruct((t, DISPATCH_W), U32),
                   jax.ShapeDtypeStruct((8, META_LANES), jnp.int32)),
        grid=(2, nt),
        in_specs=[
            pl.BlockSpec((ROUTE_BM, D_MODEL), hold),
            pl.BlockSpec((N_EXPERTS, D_MODEL), const),
            pl.BlockSpec((N_EXPERTS, 1), const),
            pl.BlockSpec((ROUTE_BM, ROUTE_BM), const),
        ],
        out_specs=(pl.BlockSpec((1, 1, ROUTE_BM), lambda ph, i: (ph * i, 0, 0)),
                   pl.BlockSpec((ROUTE_BM, DISPATCH_W), hold),
                   pl.BlockSpec((8, META_LANES), const)),
        scratch_shapes=[
            pltpu.VMEM((nt, 1, ROUTE_BM), jnp.int32),
            pltpu.VMEM((nt, 1, ROUTE_BM), jnp.int32),
            pltpu.VMEM((CLASS_ROWS, META_LANES), F32),
        ],
        compiler_params=_cparams(("arbitrary", "arbitrary")),
        name="router",
    )(x2d, rw_t, rb_t, tri)


ROW_BM = 512


def _start_rows(make_copy):
    for r in range(ROW_BM):
        make_copy(r).start(priority=r % 2)


def _dispatch_kernel(dest_ref, x_ref, zeros_ref, xs_ref, sem):
    del zeros_ref
    base = pl.program_id(0) * ROW_BM
    _start_rows(lambda r: pltpu.make_async_copy(x_ref.at[pl.ds(r, 1), :],
                                                xs_ref.at[pl.ds(dest_ref[base + r], 1), :], sem))
    pltpu.make_async_copy(x_ref, xs_ref.at[pl.ds(0, ROW_BM), :], sem).wait()


def _dispatch(dest, x_rows, n_rows):
    t, width = x_rows.shape
    grid_spec = pltpu.PrefetchScalarGridSpec(
        num_scalar_prefetch=1,
        grid=(t // ROW_BM,),
        in_specs=[
            pl.BlockSpec((ROW_BM, width), lambda i, dest: (i, 0)),
            pl.BlockSpec(memory_space=pl.ANY),
        ],
        out_specs=pl.BlockSpec(memory_space=pl.ANY),
        scratch_shapes=[pltpu.SemaphoreType.DMA],
    )
    return pl.pallas_call(
        _dispatch_kernel,
        out_shape=jax.ShapeDtypeStruct((n_rows, width), x_rows.dtype),
        grid_spec=grid_spec,
        input_output_aliases={2: 0},
        compiler_params=_cparams(("arbitrary",)),
        name="dispatch",
    )(dest, x_rows, jnp.zeros((n_rows, width), x_rows.dtype))


EXPERT_RING = 3
EXPERT_AHEAD = EXPERT_RING - 1


def _experts_kernel(ea_ref, eb_ref, nb_ref, x_ref, wg_hbm, wu_hbm, wd_hbm, o_ref,
                    ring_g, ring_u, ring_d, sems, loads_ref, hg_buf, hu_buf, act_buf, *, layer, n_blocks):
    j = pl.program_id(0)
    nb = nb_ref[0]
    experts_of = (ea_ref, eb_ref)

    def is_load(s, step):
        step = jnp.minimum(step, n_blocks - 1)
        return ((step == 0) | (experts_of[s][step] != experts_of[s][jnp.maximum(step - 1, 0)])).astype(jnp.int32)

    def weight_copies(s, step, buf):
        e = experts_of[s][jnp.minimum(step, n_blocks - 1)]
        return (pltpu.make_async_copy(wg_hbm.at[layer, e], ring_g.at[s, buf], sems.at[s, buf]),
                pltpu.make_async_copy(wu_hbm.at[layer, e], ring_u.at[s, buf], sems.at[s, buf]),
                pltpu.make_async_copy(wd_hbm.at[layer, e], ring_d.at[s, buf], sems.at[s, buf]))

    def start(s, step, load_number):
        for c in weight_copies(s, step, lax.rem(load_number - 1, EXPERT_RING)):
            c.start()

    @pl.when(j < nb)
    def _():
        bufs = []
        for s in range(2):
            loads_now = jnp.where(j == 0, 0, loads_ref[s]) + is_load(s, j)
            loads_ref[s] = loads_now
            buf = lax.rem(loads_now - 1, EXPERT_RING)
            bufs.append(buf)

            for first in range(EXPERT_AHEAD):
                ahead_loads = 1 + sum(is_load(s, k) for k in range(1, first + 1))

                @pl.when((j == 0) & (first < nb) & (is_load(s, first) == 1))
                def _(first=first, ahead_loads=ahead_loads):
                    start(s, first, ahead_loads)

            ahead = j + EXPERT_AHEAD
            ahead_loads = loads_now + sum(is_load(s, j + k) for k in range(1, EXPERT_AHEAD + 1))

            @pl.when((ahead < nb) & (is_load(s, ahead) == 1))
            def _(ahead=ahead, ahead_loads=ahead_loads):
                start(s, ahead, ahead_loads)

            @pl.when(is_load(s, j) == 1)
            def _(buf=buf):
                for c in weight_copies(s, j, buf):
                    c.wait()

        x = jnp.concatenate(_unpack_bf16_pair(x_ref[:, :DISPATCH_X]), axis=1).astype(BF16)
        gate_w = lax.bitcast_convert_type(x_ref[:, DISPATCH_X:], F32)
        for s in range(2):
            hg_buf[s] = jnp.dot(x, ring_g[s, bufs[s]].astype(BF16), preferred_element_type=F32)
            hu_buf[s] = jnp.dot(x, ring_u[s, bufs[s]].astype(BF16), preferred_element_type=F32)
        for s in range(2):
            hg = hg_buf[s]
            act_buf[s] = (hg * _sigmoid(hg) * hu_buf[s]).astype(BF16)
        ys = [jnp.dot(act_buf[s], ring_d[s, bufs[s]].astype(BF16), preferred_element_type=F32) for s in range(2)]
        y = gate_w[:, 0:1] * ys[0] + gate_w[:, 1:2] * ys[1]
        o_ref[...] = _pack_bf16_pair(y[:, :DISPATCH_X], y[:, DISPATCH_X:])

    @pl.when(j >= nb)
    def _():
        o_ref[...] = jnp.zeros_like(o_ref)


def _experts(blk_a, blk_b, n_used, xs, wg, wu, wd, layer):
    n_rows = xs.shape[0]
    n_blocks = n_rows // MOE_BLOCK
    in_row_map = lambda j, ea, eb, nb: (jnp.maximum(jnp.minimum(j, nb[0] - 1), 0), 0)
    row_map = lambda j, ea, eb, nb: (j, 0)
    grid_spec = pltpu.PrefetchScalarGridSpec(
        num_scalar_prefetch=3,
        grid=(n_blocks,),
        in_specs=[
            pl.BlockSpec((MOE_BLOCK, DISPATCH_W), in_row_map),
            pl.BlockSpec(memory_space=pl.ANY),
            pl.BlockSpec(memory_space=pl.ANY),
            pl.BlockSpec(memory_space=pl.ANY),
        ],
        out_specs=pl.BlockSpec((MOE_BLOCK, DISPATCH_X), row_map),
        scratch_shapes=[
            pltpu.VMEM((2, EXPERT_RING, D_MODEL, D_EXPERT), F32),
            pltpu.VMEM((2, EXPERT_RING, D_MODEL, D_EXPERT), F32),
            pltpu.VMEM((2, EXPERT_RING, D_EXPERT, D_MODEL), F32),
            pltpu.SemaphoreType.DMA((2, EXPERT_RING)),
            pltpu.SMEM((2,), jnp.int32),
            pltpu.VMEM((2, MOE_BLOCK, D_EXPERT), F32),
            pltpu.VMEM((2, MOE_BLOCK, D_EXPERT), F32),
            pltpu.VMEM((2, MOE_BLOCK, D_EXPERT), BF16),
        ],
    )
    return pl.pallas_call(
        functools.partial(_experts_kernel, layer=layer, n_blocks=n_blocks),
        out_shape=jax.ShapeDtypeStruct((n_rows, DISPATCH_X), U32),
        grid_spec=grid_spec,
        compiler_params=_cparams(("arbitrary",)),
        name="experts",
    )(blk_a, blk_b, n_used, xs, wg, wu, wd)


COMBINE_CHUNK = 64


def _combine_ln_kernel(dest_ref, x_ref, lg_ref, lb_ref, ys_ref, o_ref, obf_ref, ybuf, sem):
    i = pl.program_id(0)
    n = pl.num_programs(0)
    slot = i % 2

    def row_copy(tile, to_slot, r):
        return pltpu.make_async_copy(ys_ref.at[pl.ds(dest_ref[tile * ROW_BM + r], 1), :],
                                     ybuf.at[to_slot, pl.ds(r, 1), :], sem.at[to_slot])

    def wait_tile(of_slot):
        pltpu.make_async_copy(ys_ref.at[pl.ds(0, ROW_BM), :], ybuf.at[of_slot], sem.at[of_slot]).wait()

    @pl.when(i == 0)
    def _():
        _start_rows(lambda r: row_copy(0, 0, r))

    wait_tile(slot)
    nxt = lax.rem(i + 1, n)
    for c in range(0, ROW_BM, COMBINE_CHUNK):
        rows = slice(c, c + COMBINE_CHUNK)
        y = jnp.concatenate(_unpack_bf16_pair(ybuf[slot, rows, :]), axis=1)
        z = DEEPNORM_ALPHA * x_ref[rows, :] + y
        out = _layernorm_rows(z, lg_ref[...], lb_ref[...])
        o_ref[rows, :] = out
        obf_ref[rows, :] = out.astype(BF16)
        for r in range(c, c + COMBINE_CHUNK):
            row_copy(nxt, 1 - slot, r).start(priority=r % 2)

    @pl.when(i == n - 1)
    def _():
        wait_tile(1 - slot)


def _combine_ln(dest, x2d, ys, lg, lb):
    t = x2d.shape[0]
    const = lambda i, dest: (0, 0)
    grid_spec = pltpu.PrefetchScalarGridSpec(
        num_scalar_prefetch=1,
        grid=(t // ROW_BM,),
        in_specs=[
            pl.BlockSpec((ROW_BM, D_MODEL), lambda i, dest: (i, 0)),
            pl.BlockSpec((1, D_MODEL), const),
            pl.BlockSpec((1, D_MODEL), const),
            pl.BlockSpec(memory_space=pl.ANY),
        ],
        out_specs=(pl.BlockSpec((ROW_BM, D_MODEL), lambda i, dest: (i, 0)),
                   pl.BlockSpec((ROW_BM, D_MODEL), lambda i, dest: (i, 0))),
        scratch_shapes=[pltpu.VMEM((2, ROW_BM, DISPATCH_X), U32), pltpu.SemaphoreType.DMA((2,))],
    )
    return pl.pallas_call(
        _combine_ln_kernel,
        out_shape=(jax.ShapeDtypeStruct((t, D_MODEL), F32), jax.ShapeDtypeStruct((t, D_MODEL), BF16)),
        grid_spec=grid_spec,
        compiler_params=_cparams(("arbitrary",)),
        name="combine_ln",
    )(dest, x2d, lg, lb, ys)


def _moe(x2d, rw_t, rb_t, wg, wu, wd, layer, lg, lb):
    t = x2d.shape[0]
    n_blocks = (t + N_CLASSES * (MOE_BLOCK - 1)) // MOE_BLOCK
    assert n_blocks <= META_LANES
    dest3, xw, meta = _router(x2d, rw_t, rb_t)
    dest = dest3.reshape(t)
    xs = _dispatch(dest, xw, n_blocks * MOE_BLOCK)
    ys = _experts(meta[0, :n_blocks], meta[1, :n_blocks], meta[2, :1], xs, wg, wu, wd, layer)
    return _combine_ln(dest, x2d, ys, lg, lb)


def _rope_tables(seq):
    half = RET_DK // 2
    pos = jnp.arange(seq, dtype=F32)
    freqs = ROPE_BASE ** (-jnp.arange(half, dtype=F32) / half)
    ang = pos[:, None] * freqs[None, :]
    return jnp.cos(ang), jnp.sin(ang)


def kernel(x, w_in, ret_decay_logit, w_ret_o, na_rpb, w_na_o, w_out, ln_mix_g, ln_mix_b, router_w, router_bias,
           w_exp_gate, w_exp_up, w_exp_down, ln_ffn_g, ln_ffn_b):
    b, s, d = x.shape
    depth = w_in.shape[0]
    t = b * s
    rows = s // GRID_W
    cos, sin = _rope_tables(s)
    rw_t = router_w.astype(F32).T.reshape(N_GROUPS, EXPERTS_PER_GROUP, d).transpose(1, 0, 2).reshape(N_EXPERTS, d)
    rb_t = router_bias.astype(F32).reshape(N_GROUPS, EXPERTS_PER_GROUP).T.reshape(N_EXPERTS, 1)
    x2d = x.reshape(t, d)
    x_bf = x2d.astype(BF16)
    w_in = w_in.astype(F32)
    for l in range(depth):
        qk = _inproj(x_bf, w_in, l, cos, sin, s, "rotary").reshape(b, s, -1)
        vn = _inproj(x_bf, w_in, l, cos, sin, s, "scale").reshape(b, s, -1)
        gr = _inproj(x_bf, w_in, l, cos, sin, s, "silu").reshape(b, s, -1)
        gates = _inproj(x_bf, w_in, l, cos, sin, s, "sigmoid")
        ret = _retention(qk, vn, gr, ret_decay_logit[l].astype(F32))
        na = _na(vn, na_rpb[l].astype(F32).reshape(-1))
        x2d = _mixout(ret.reshape(t, -1), na.reshape(t, -1), gates, x2d,
                      w_ret_o[l].astype(BF16), w_na_o[l].astype(BF16), w_out[l].astype(BF16),
                      ln_mix_g[l].reshape(1, d).astype(F32), ln_mix_b[l].reshape(1, d).astype(F32))
        x2d, x_bf = _moe(x2d, rw_t, rb_t, w_exp_gate, w_exp_up, w_exp_down, l,
                         ln_ffn_g[l].reshape(1, d).astype(F32), ln_ffn_b[l].reshape(1, d).astype(F32))
    return x2d.reshape(b, s, d)
```

```python
import functools

import numpy as np
import jax
import jax.numpy as jnp
from jax import lax
from jax.experimental import pallas as pl
from jax.experimental.pallas import tpu as pltpu

F32 = jnp.float32
BF16 = jnp.bfloat16

D_MODEL = 1024
GRID_W = 64
RET_HEADS = 4
RET_DK = 256
RET_DV = 512
ROPE_BASE = 10000.0
NA_HEADS = 16
NA_HD = 64
NA_KH = 8
NA_KW = 16
N_EXPERTS = 32
N_GROUPS = 8
EXPERTS_PER_GROUP = 4
D_EXPERT = 512
LN_EPS = 1e-5
GN_EPS = 1e-5
MODEL_DEPTH = 4
DEEPNORM_ALPHA = (2 * MODEL_DEPTH) ** 0.25

D_IN = 11264

RET_CHUNK = 256
NA_QROWS = 4
NA_KROWS = 12
NA_NEG = -1e30
MOE_BLOCK = 256
N_PAIRS = 6
PAIR_SLOT_A = (0, 2, 2, 3, 3, 3)
PAIR_SLOT_B = (1, 1, 0, 0, 1, 2)
N_CLASSES = N_GROUPS * N_PAIRS
VMEM_LIMIT = 56 * 1024 * 1024


def _cparams(sem):
    return pltpu.CompilerParams(dimension_semantics=sem, vmem_limit_bytes=VMEM_LIMIT)


U32 = jnp.uint32
_HIGH_HALF = 0xFFFF0000


def _pack_bf16_pair(hi, lo):
    hi_bits = lax.bitcast_convert_type(hi.astype(jnp.bfloat16).astype(F32), U32)
    lo_bits = lax.bitcast_convert_type(lo.astype(jnp.bfloat16).astype(F32), U32)
    return (hi_bits & jnp.uint32(_HIGH_HALF)) | (lo_bits >> 16)


def _unpack_bf16_pair(packed):
    hi = lax.bitcast_convert_type(packed & jnp.uint32(_HIGH_HALF), F32)
    lo = lax.bitcast_convert_type(packed << 16, F32)
    return hi, lo


IN_CHUNK = 256
LOG2E = 1.4426950408889634

IN_GROUPS = {
    "rotary": (1024, 2048, (0,)),
    "scale": (2048, 1024, (2, 3, 6, 7, 8)),
    "silu": (1024, 2048, (2,)),
    "sigmoid": (2048, 1024, (9, 10)),
}
VN_Q = 2048
VN_K = 3072
VN_V = 4096


def _sigmoid(x):
    return 0.5 * jnp.tanh(0.5 * x) + 0.5


def _inproj_kernel(x_ref, w_ref, cos_ref, sin_ref, o_ref, wbf_ref, *, kind, bn):
    j = pl.program_id(0)

    @pl.when(pl.program_id(1) == 0)
    def _():
        wbf_ref[...] = w_ref[0].astype(BF16)

    x = x_ref[...]
    half = RET_DK // 2
    for c in range(bn // IN_CHUNK):
        lo = c * IN_CHUNK
        acc = jnp.dot(x, wbf_ref[:, lo:lo + IN_CHUNK], preferred_element_type=F32)
        if kind == "rotary":
            scale = 1.0 if lo < RET_HEADS * RET_DK else RET_DK ** -0.5
            cos = cos_ref[...] * scale
            sin = sin_ref[...] * scale
            x1 = acc[:, :half]
            x2 = acc[:, half:]
            o_ref[:, lo:lo + half] = (x1 * cos - x2 * sin).astype(o_ref.dtype)
            o_ref[:, lo + half:lo + IN_CHUNK] = (x1 * sin + x2 * cos).astype(o_ref.dtype)
        elif kind == "scale":
            scale = jnp.where(j == 2, NA_HD ** -0.5 * LOG2E, 1.0)
            o_ref[:, lo:lo + IN_CHUNK] = (acc * scale).astype(o_ref.dtype)
        elif kind == "silu":
            o_ref[:, lo:lo + IN_CHUNK] = (acc * _sigmoid(acc)).astype(o_ref.dtype)
        else:
            o_ref[:, lo:lo + IN_CHUNK] = _sigmoid(acc).astype(o_ref.dtype)


def _inproj(x_bf, w_in, layer, cos, sin, seq, kind):
    t = x_bf.shape[0]
    bm, bn, blocks = IN_GROUPS[kind]
    pos_blocks = seq // bm

    def wcol(j):
        col = blocks[0]
        for k in range(1, len(blocks)):
            col = jnp.where(j == k, blocks[k], col)
        return col

    pos_map = (lambda j, i: (i % pos_blocks, 0)) if kind == "rotary" else (lambda j, i: (0, 0))
    return pl.pallas_call(
        functools.partial(_inproj_kernel, kind=kind, bn=bn),
        out_shape=jax.ShapeDtypeStruct((t, len(blocks) * bn), BF16),
        grid=(len(blocks), t // bm),
        in_specs=[
            pl.BlockSpec((bm, D_MODEL), lambda j, i: (i, 0)),
            pl.BlockSpec((1, D_MODEL, bn), lambda j, i: (layer, 0, wcol(j))),
            pl.BlockSpec((bm, RET_DK // 2), pos_map),
            pl.BlockSpec((bm, RET_DK // 2), pos_map),
        ],
        out_specs=pl.BlockSpec((bm, bn), lambda j, i: (i, j)),
        scratch_shapes=[pltpu.VMEM((D_MODEL, bn), BF16)],
        compiler_params=_cparams(("arbitrary", "arbitrary")),
        name="inproj_" + kind,
    )(x_bf, w_in, cos, sin)


def _log_sigmoid(x):
    return jnp.minimum(x, 0.0) - jnp.log1p(jnp.exp(-jnp.abs(x)))


RET_GROUP = 8


def _retention_kernel(dl_ref, q_ref, k_ref, v_ref, g_ref, o_ref,
                      sf_ref, sb_ref, st_ref, stb_ref, dm_ref, qdf_ref, qdb_ref, kdf_ref, kdb_ref,
                      p_buf, qf_buf, qb_buf, out_buf, *, nc):
    c_len = RET_CHUNK
    h = pl.program_id(1)
    lgf = _log_sigmoid(jnp.full((c_len, RET_DK), dl_ref[0, h], F32))
    lgb = _log_sigmoid(jnp.full((c_len, RET_DK), dl_ref[1, h], F32))
    ri = lax.broadcasted_iota(jnp.int32, (c_len, RET_DK), 0).astype(F32)
    qdf_ref[...] = jnp.exp(lgf * (ri + 1.0))
    qdb_ref[...] = jnp.exp(lgb * (c_len - ri))
    kdf_ref[...] = jnp.exp(lgf * (c_len - 1.0 - ri))
    kdb_ref[...] = jnp.exp(lgb * ri)
    rr = lax.broadcasted_iota(jnp.int32, (c_len, c_len), 0).astype(F32)
    cc = lax.broadcasted_iota(jnp.int32, (c_len, c_len), 1).astype(F32)
    diff = rr - cc
    lgf_cc = _log_sigmoid(jnp.full((c_len, c_len), dl_ref[0, h], F32))
    lgb_cc = _log_sigmoid(jnp.full((c_len, c_len), dl_ref[1, h], F32))
    dm_ref[...] = jnp.where(diff >= 0.0, jnp.exp(lgf_cc * jnp.maximum(diff, 0.0)),
                            jnp.exp(lgb_cc * jnp.maximum(-diff, 0.0)))
    chunk_f = jnp.exp(lgf[:1, :1] * float(c_len))
    chunk_b = jnp.exp(lgb[:1, :1] * float(c_len))

    tn_dims = (((0,), (0,)), ((), ()))
    nt_dims = (((1,), (1,)), ((), ()))

    st_ref[...] = jnp.zeros_like(st_ref)
    stb_ref[...] = jnp.zeros_like(stb_ref)

    def state_body(i, carry):
        for c, all_ref, cur_ref, kdec_ref, chunk_dec in ((i, sf_ref, st_ref, kdf_ref, chunk_f),
                                                         (nc - 1 - i, sb_ref, stb_ref, kdb_ref, chunk_b)):
            off = pl.multiple_of(c * c_len, c_len)
            all_ref[c] = cur_ref[...].astype(BF16)
            kc = k_ref[0, pl.ds(off, c_len), :].astype(F32)
            vc = v_ref[0, pl.ds(off, c_len), :]
            kd = (kc * kdec_ref[...]).astype(BF16)
            upd = lax.dot_general(kd, vc, tn_dims, preferred_element_type=F32)
            cur_ref[...] = cur_ref[...] * chunk_dec + upd
        return carry

    lax.fori_loop(0, nc, state_body, 0, unroll=4)

    def out_body(g, carry):
        offs = [pl.multiple_of((g * RET_GROUP + u) * c_len, c_len) for u in range(RET_GROUP)]
        for u, off in enumerate(offs):
            qb = q_ref[0, pl.ds(off, c_len), :]
            kb = k_ref[0, pl.ds(off, c_len), :]
            qc = qb.astype(F32)
            s = lax.dot_general(qb, kb, nt_dims, preferred_element_type=F32)
            p_buf[u] = (s * dm_ref[...]).astype(BF16)
            qf_buf[u] = (qc * qdf_ref[...]).astype(BF16)
            qb_buf[u] = (qc * qdb_ref[...]).astype(BF16)
        for u, off in enumerate(offs):
            c = g * RET_GROUP + u
            out = jnp.dot(p_buf[u], v_ref[0, pl.ds(off, c_len), :], preferred_element_type=F32)
            out = out + jnp.dot(qf_buf[u], sf_ref[c], preferred_element_type=F32)
            out_buf[u] = out + jnp.dot(qb_buf[u], sb_ref[c], preferred_element_type=F32)
        for u, off in enumerate(offs):
            out = out_buf[u]
            mu = jnp.mean(out, axis=-1, keepdims=True)
            cen = out - mu
            var = jnp.mean(cen * cen, axis=-1, keepdims=True)
            y = cen * lax.rsqrt(var + GN_EPS)
            gate = g_ref[0, pl.ds(off, c_len), :].astype(F32)
            o_ref[0, pl.ds(off, c_len), :] = (gate * y).astype(o_ref.dtype)
        return carry

    lax.fori_loop(0, nc // RET_GROUP, out_body, 0)


def _retention(qk3, vn3, gr3, decay_logit):
    b, s, _ = qk3.shape
    nc = s // RET_CHUNK
    return pl.pallas_call(
        functools.partial(_retention_kernel, nc=nc),
        out_shape=jax.ShapeDtypeStruct((b, s, RET_HEADS * RET_DV), BF16),
        grid=(b, RET_HEADS),
        in_specs=[
            pl.BlockSpec(memory_space=pltpu.SMEM),
            pl.BlockSpec((1, s, RET_DK), lambda bi, h: (bi, 0, h)),
            pl.BlockSpec((1, s, RET_DK), lambda bi, h: (bi, 0, RET_HEADS + h)),
            pl.BlockSpec((1, s, RET_DV), lambda bi, h: (bi, 0, h)),
            pl.BlockSpec((1, s, RET_DV), lambda bi, h: (bi, 0, h)),
        ],
        out_specs=pl.BlockSpec((1, s, RET_DV), lambda bi, h: (bi, 0, h)),
        scratch_shapes=[
            pltpu.VMEM((nc, RET_DK, RET_DV), BF16),
            pltpu.VMEM((nc, RET_DK, RET_DV), BF16),
            pltpu.VMEM((RET_DK, RET_DV), F32),
            pltpu.VMEM((RET_DK, RET_DV), F32),
            pltpu.VMEM((RET_CHUNK, RET_CHUNK), F32),
            pltpu.VMEM((RET_CHUNK, RET_DK), F32),
            pltpu.VMEM((RET_CHUNK, RET_DK), F32),
            pltpu.VMEM((RET_CHUNK, RET_DK), F32),
            pltpu.VMEM((RET_CHUNK, RET_DK), F32),
            pltpu.VMEM((RET_GROUP, RET_CHUNK, RET_CHUNK), BF16),
            pltpu.VMEM((RET_GROUP, RET_CHUNK, RET_DK), BF16),
            pltpu.VMEM((RET_GROUP, RET_CHUNK, RET_DK), BF16),
            pltpu.VMEM((RET_GROUP, RET_CHUNK, RET_DV), F32),
        ],
        compiler_params=_cparams(("arbitrary", "arbitrary")),
        name="retention",
    )(decay_logit, qk3, qk3, vn3, gr3)


NA_ROW_OFFS = 2 * NA_KH - 1
NA_COL_OFFS = 2 * NA_KW - 1
NA_PATTERNS = 3


def _na_row_offsets(rows):
    n_tiles = rows // NA_QROWS
    offs = np.full((NA_PATTERNS, NA_QROWS, NA_KROWS), NA_ROW_OFFS, np.int32)
    for p, t in enumerate((0, 1, n_tiles - 1)):
        kstart = int(np.clip(NA_QROWS * t - NA_KH // 2, 0, rows - NA_KROWS))
        for rr in range(NA_QROWS):
            r = NA_QROWS * t + rr
            rs = int(np.clip(r - NA_KH // 2, 0, rows - NA_KH))
            for i in range(NA_KROWS):
                krow = kstart + i
                if rs <= krow < rs + NA_KH:
                    offs[p, rr, i] = krow - r + (NA_KH - 1)
    return offs


def _na_build_bias(rpb_ref, e_ref, bias_ref, head0, rows):
    wide = 2 * GRID_W
    lane = lax.broadcasted_iota(jnp.int32, (GRID_W, wide), 1)
    col = lax.broadcasted_iota(jnp.int32, (GRID_W, wide), 0)
    kcol = jnp.where(lane < GRID_W, lane, lane - GRID_W)
    cstart = jnp.clip(col - NA_KW // 2, 0, GRID_W - NA_KW)
    col_ok = (kcol >= cstart) & (kcol < cstart + NA_KW)
    coff = kcol - col + (NA_KW - 1)
    neg = jnp.full((GRID_W, wide), NA_NEG, F32)
    left = lane < GRID_W
    offs = _na_row_offsets(rows)
    for hh in range(2):
        base = (head0 + hh) * (NA_ROW_OFFS * NA_COL_OFFS)
        for ro in range(NA_ROW_OFFS):
            acc = neg
            for d in range(NA_COL_OFFS):
                acc = jnp.where(coff == d, rpb_ref[base + ro * NA_COL_OFFS + d] * LOG2E, acc)
            e_ref[hh, ro] = jnp.where(col_ok, acc, neg)
        e_ref[hh, NA_ROW_OFFS] = neg
        for p in range(NA_PATTERNS):
            for rr in range(NA_QROWS):
                for ip in range(NA_KROWS // 2):
                    tile = jnp.where(left, e_ref[hh, int(offs[p, rr, 2 * ip])], e_ref[hh, int(offs[p, rr, 2 * ip + 1])])
                    bias_ref[hh, p, rr * GRID_W:(rr + 1) * GRID_W, ip * wide:(ip + 1) * wide] = tile


NA_GROUP = 8


def _na_kernel(rpb_ref, q_ref, k_ref, v_ref, o_ref, bias_ref, e_ref, s_buf, p_buf, *, n_tiles, rows):
    nq = NA_QROWS * GRID_W
    nk = NA_KROWS * GRID_W
    nt_dims = (((1,), (1,)), ((), ()))
    first = lax.broadcasted_iota(jnp.int32, (nq, 2 * NA_HD), 1) < NA_HD
    kfirst = lax.broadcasted_iota(jnp.int32, (nk, 2 * NA_HD), 1) < NA_HD

    @pl.when(pl.program_id(1) == 0)
    def _():
        _na_build_bias(rpb_ref, e_ref, bias_ref, 2 * pl.program_id(0), rows)

    def body(g, carry):
        offs = []
        for u in range(NA_GROUP):
            t = g * NA_GROUP + u
            qoff = pl.multiple_of(t * nq, nq)
            krow0 = jnp.clip(NA_QROWS * t - NA_KH // 2, 0, rows - NA_KROWS)
            koff = pl.multiple_of(krow0 * GRID_W, nq)
            pat = jnp.where(t == 0, 0, jnp.where(t == n_tiles - 1, 2, 1))
            offs.append((qoff, koff, pat))
        for u, (qoff, koff, pat) in enumerate(offs):
            q2 = q_ref[0, pl.ds(qoff, nq), :]
            k2 = k_ref[0, pl.ds(koff, nk), :]
            for hh in range(2):
                qm = jnp.where(first if hh == 0 else jnp.logical_not(first), q2, jnp.zeros_like(q2))
                s_buf[2 * u + hh] = (lax.dot_general(qm, k2, nt_dims, preferred_element_type=F32)
                                     + bias_ref[hh, pat]).astype(BF16)
        for n in range(2 * NA_GROUP):
            s = s_buf[n]
            p_buf[n] = jnp.exp2(s - jnp.max(s, axis=-1, keepdims=True))
        for u, (qoff, koff, pat) in enumerate(offs):
            v2 = v_ref[0, pl.ds(koff, nk), :]
            outs = []
            for hh in range(2):
                vv = jnp.where(kfirst if hh == 0 else jnp.logical_not(kfirst), v2, jnp.ones_like(v2))
                o = jnp.dot(p_buf[2 * u + hh], vv, preferred_element_type=F32)
                outs.append(o / pltpu.roll(o, NA_HD, axis=1))
            o_ref[0, pl.ds(qoff, nq), :] = jnp.where(first, outs[0], outs[1]).astype(o_ref.dtype)
        return carry

    lax.fori_loop(0, n_tiles // NA_GROUP, body, 0)


def _na(vn3, rpb_flat):
    b, s, _ = vn3.shape
    rows = s // GRID_W
    n_tiles = rows // NA_QROWS
    w2 = 2 * NA_HD
    nq = NA_QROWS * GRID_W
    nk = NA_KROWS * GRID_W
    return pl.pallas_call(
        functools.partial(_na_kernel, n_tiles=n_tiles, rows=rows),
        out_shape=jax.ShapeDtypeStruct((b, s, NA_HEADS * NA_HD), BF16),
        grid=(NA_HEADS // 2, b),
        in_specs=[
            pl.BlockSpec(memory_space=pltpu.SMEM),
            pl.BlockSpec((1, s, w2), lambda hp, bi: (bi, 0, VN_Q // w2 + hp)),
            pl.BlockSpec((1, s, w2), lambda hp, bi: (bi, 0, VN_K // w2 + hp)),
            pl.BlockSpec((1, s, w2), lambda hp, bi: (bi, 0, VN_V // w2 + hp)),
        ],
        out_specs=pl.BlockSpec((1, s, w2), lambda hp, bi: (bi, 0, hp)),
        scratch_shapes=[
            pltpu.VMEM((2, NA_PATTERNS, nq, nk), F32),
            pltpu.VMEM((2, NA_ROW_OFFS + 1, GRID_W, 2 * GRID_W), F32),
            pltpu.VMEM((2 * NA_GROUP, nq, nk), BF16),
            pltpu.VMEM((2 * NA_GROUP, nq, nk), BF16),
        ],
        compiler_params=_cparams(("arbitrary", "arbitrary")),
        name="natten",
    )(rpb_flat, vn3, vn3, vn3)


MIX_BM = 512
MIX_SUB = 256


def _layernorm_rows(z, g, b):
    mu = jnp.mean(z, axis=-1, keepdims=True)
    cen = z - mu
    var = jnp.mean(cen * cen, axis=-1, keepdims=True)
    return cen * lax.rsqrt(var + LN_EPS) * g + b


def _mixout_kernel(ret_ref, na_ref, g1_ref, g2_ref, x_ref, wr_ref, wn_ref, wo_ref, lg_ref, lb_ref, o_ref):
    for r in range(0, MIX_BM, MIX_SUB):
        rows = slice(r, r + MIX_SUB)
        y_ret = jnp.dot(ret_ref[rows, :], wr_ref[...], preferred_element_type=F32)
        y_na = jnp.dot(na_ref[rows, :], wn_ref[...], preferred_element_type=F32)
        merged = g1_ref[rows, :].astype(F32) * y_ret + g2_ref[rows, :].astype(F32) * y_na
        mix = jnp.dot(merged.astype(BF16), wo_ref[...], preferred_element_type=F32)
        z = DEEPNORM_ALPHA * x_ref[rows, :] + mix
        o_ref[rows, :] = _layernorm_rows(z, lg_ref[...], lb_ref[...])


def _mixout(ret2d, na2d, gates2d, x2d, wr, wn, wo, lg, lb):
    t = x2d.shape[0]
    const = lambda i: (0, 0)
    return pl.pallas_call(
        _mixout_kernel,
        out_shape=jax.ShapeDtypeStruct((t, D_MODEL), F32),
        grid=(t // MIX_BM,),
        in_specs=[
            pl.BlockSpec((MIX_BM, RET_HEADS * RET_DV), lambda i: (i, 0)),
            pl.BlockSpec((MIX_BM, NA_HEADS * NA_HD), lambda i: (i, 0)),
            pl.BlockSpec((MIX_BM, D_MODEL), lambda i: (i, 0)),
            pl.BlockSpec((MIX_BM, D_MODEL), lambda i: (i, 1)),
            pl.BlockSpec((MIX_BM, D_MODEL), lambda i: (i, 0)),
            pl.BlockSpec((RET_HEADS * RET_DV, D_MODEL), const),
            pl.BlockSpec((NA_HEADS * NA_HD, D_MODEL), const),
            pl.BlockSpec((D_MODEL, D_MODEL), const),
            pl.BlockSpec((1, D_MODEL), const),
            pl.BlockSpec((1, D_MODEL), const),
        ],
        out_specs=pl.BlockSpec((MIX_BM, D_MODEL), lambda i: (i, 0)),
        compiler_params=_cparams(("arbitrary",)),
        name="mixout",
    )(ret2d, na2d, gates2d, gates2d, x2d, wr, wn, wo, lg, lb)


ROUTE_BM = 512


def _route_tile(x, rw, rb):
    nt_dims = (((1,), (1,)), ((), ()))
    x_hi = x.astype(BF16)
    x_lo = (x - x_hi.astype(F32)).astype(BF16)
    rw_hi = rw.astype(BF16)
    rw_lo = (rw - rw_hi.astype(F32)).astype(BF16)
    both = lax.dot_general(jnp.concatenate([rw_hi, rw_lo], axis=0), x_hi, nt_dims, preferred_element_type=F32)
    logits = (both[:N_EXPERTS] + both[N_EXPERTS:]
              + lax.dot_general(rw_hi, x_lo, nt_dims, preferred_element_type=F32))
    scores = jax.nn.sigmoid(logits)
    sel = scores + rb
    p = [scores[m * N_GROUPS:(m + 1) * N_GROUPS] for m in range(EXPERTS_PER_GROUP)]
    s = [sel[m * N_GROUPS:(m + 1) * N_GROUPS] for m in range(EXPERTS_PER_GROUP)]
    one = jnp.ones_like(s[0])
    zero = jnp.zeros_like(s[0])
    chosen = []
    for m in range(EXPERTS_PER_GROUP):
        rank = zero
        for j in range(EXPERTS_PER_GROUP):
            if j == m:
                continue
            beats = (s[j] >= s[m]) if j < m else (s[j] > s[m])
            rank = rank + jnp.where(beats, one, zero)
        chosen.append(rank < 2.0)
    group_score = zero
    for m in range(EXPERTS_PER_GROUP):
        group_score = group_score + jnp.where(chosen[m], s[m], zero)
    gid = lax.broadcasted_iota(jnp.int32, group_score.shape, 0)
    gmax = jnp.max(group_score, axis=0, keepdims=True)
    gbest = jnp.min(jnp.where(group_score == gmax, gid, N_GROUPS), axis=0, keepdims=True)
    in_best = gid == gbest
    picked = [chosen[m] & in_best for m in range(EXPERTS_PER_GROUP)]
    before = zero
    wa = zero
    wb = zero
    ma = zero
    mb = zero
    for m in range(EXPERTS_PER_GROUP):
        is_a = picked[m] & (before == 0.0)
        is_b = picked[m] & (before == 1.0)
        wa = wa + jnp.where(is_a, p[m], zero)
        wb = wb + jnp.where(is_b, p[m], zero)
        ma = ma + jnp.where(is_a, float(m), 0.0)
        mb = mb + jnp.where(is_b, float(m), 0.0)
        before = before + jnp.where(chosen[m], one, zero)
    wa = jnp.sum(wa, axis=0, keepdims=True)
    wb = jnp.sum(wb, axis=0, keepdims=True)
    ma = jnp.sum(ma, axis=0, keepdims=True)
    mb = jnp.sum(mb, axis=0, keepdims=True)
    denom = wa + wb
    w_lo = wa / denom
    w_hi = wb / denom
    pair = jnp.where(ma == 0.0, jnp.where(mb == 1.0, 0.0, mb),
                     jnp.where(ma == 1.0, jnp.where(mb == 2.0, 1.0, 4.0), 5.0))
    keep = pair == 0.0
    return (gbest * N_PAIRS + pair.astype(jnp.int32),
            jnp.where(keep, w_lo, w_hi), jnp.where(keep, w_hi, w_lo))


CLASS_ROWS = 64
META_LANES = 128


DISPATCH_X = D_MODEL // 2
DISPATCH_W = DISPATCH_X + META_LANES


def _router_kernel(x_ref, rw_ref, rb_ref, tri_ref, dest_ref, xw_ref, meta_ref, cls_s, rank_s, cnt_s):
    i = pl.program_id(0)
    nt = pl.num_programs(0) - 1
    bm = ROUTE_BM
    reps = bm // META_LANES
    cid = lax.broadcasted_iota(jnp.int32, (CLASS_ROWS, bm), 0)

    @pl.when(i == 0)
    def _():
        cnt_s[...] = jnp.zeros_like(cnt_s)

    @pl.when(i < nt)
    def _():
        x = x_ref[...]
        cls, w_a, w_b = _route_tile(x, rw_ref[...], rb_ref[...])
        cls_s[i] = cls
        w_rows = jnp.concatenate([w_a, w_b, jnp.zeros((META_LANES - 2, bm), F32)], axis=0)
        xw_ref[:, :DISPATCH_X] = _pack_bf16_pair(x[:, :DISPATCH_X], x[:, DISPATCH_X:])
        xw_ref[:, DISPATCH_X:] = lax.bitcast_convert_type(w_rows.T, U32)
        onehot = (cid == cls).astype(BF16)
        before = jnp.dot(onehot, tri_ref[...], preferred_element_type=F32)
        carry = jnp.concatenate([cnt_s[...]] * reps, axis=1)
        rank = jnp.sum(jnp.where(cid == cls, before + carry, 0.0), axis=0, keepdims=True)
        rank_s[i] = rank.astype(jnp.int32)
        cnt_s[...] = cnt_s[...] + jnp.dot(onehot, jnp.ones((bm, META_LANES), BF16), preferred_element_type=F32)

    @pl.when(i == nt)
    def _():
        cnt = cnt_s[...]
        nblk = jnp.floor((cnt + (MOE_BLOCK - 1.0)) * (1.0 / MOE_BLOCK))
        rr = lax.broadcasted_iota(jnp.int32, (CLASS_ROWS, CLASS_ROWS), 0)
        cc = lax.broadcasted_iota(jnp.int32, (CLASS_ROWS, CLASS_ROWS), 1)
        lower = (cc < rr).astype(BF16)
        start_blk = jnp.dot(lower, nblk.astype(BF16), preferred_element_type=F32)
        start_t = jnp.concatenate([start_blk] * reps, axis=1)

        def dest_tile(t, carry):
            start = jnp.sum(jnp.where(cid == cls_s[t], start_t, 0.0), axis=0, keepdims=True)
            dest_ref[t] = (start * float(MOE_BLOCK)).astype(jnp.int32) + rank_s[t]
            return carry

        lax.fori_loop(0, nt, dest_tile, 0)

        end_blk = start_blk + nblk
        n_used = jnp.max(end_blk, axis=0, keepdims=True)
        blk = lax.broadcasted_iota(jnp.int32, (CLASS_ROWS, META_LANES), 1).astype(F32)
        blk = jnp.minimum(blk, n_used - 1.0)
        bcls = jnp.sum(jnp.where(end_blk <= blk, 1.0, 0.0), axis=0, keepdims=True)
        grp = jnp.zeros_like(bcls)
        for g in range(1, N_GROUPS):
            grp = grp + jnp.where(bcls >= float(g * N_PAIRS), 1.0, 0.0)
        pair = bcls - grp * float(N_PAIRS)
        slot_a = jnp.zeros_like(pair)
        slot_b = jnp.zeros_like(pair)
        for k in range(N_PAIRS):
            slot_a = jnp.where(pair == float(k), float(PAIR_SLOT_A[k]), slot_a)
            slot_b = jnp.where(pair == float(k), float(PAIR_SLOT_B[k]), slot_b)
        rows = [grp * float(EXPERTS_PER_GROUP) + slot_a, grp * float(EXPERTS_PER_GROUP) + slot_b, n_used]
        rows = rows + [jnp.zeros_like(bcls)] * (8 - len(rows))
        meta_ref[...] = jnp.concatenate(rows, axis=0).astype(jnp.int32)


def _router(x2d, rw_t, rb_t):
    t = x2d.shape[0]
    nt = t // ROUTE_BM
    tri = jnp.asarray(np.triu(np.ones((ROUTE_BM, ROUTE_BM), np.float32), 1), BF16)
    tile = lambda i: (jnp.minimum(i, nt - 1), 0)
    const = lambda i: (0, 0)
    return pl.pallas_call(
        _router_kernel,
        out_shape=(jax.ShapeDtypeStruct((nt, 1, ROUTE_BM), jnp.int32),
                   jax.ShapeDtypeStruct((t, DISPATCH_W), U32),
                   jax.ShapeDtypeStruct((8, META_LANES), jnp.int32)),
        grid=(nt + 1,),
        in_specs=[
            pl.BlockSpec((ROUTE_BM, D_MODEL), tile),
            pl.BlockSpec((N_EXPERTS, D_MODEL), const),
            pl.BlockSpec((N_EXPERTS, 1), const),
            pl.BlockSpec((ROUTE_BM, ROUTE_BM), const),
        ],
        out_specs=(pl.BlockSpec((nt, 1, ROUTE_BM), lambda i: (0, 0, 0)),
                   pl.BlockSpec((ROUTE_BM, DISPATCH_W), tile),
                   pl.BlockSpec((8, META_LANES), const)),
        scratch_shapes=[
            pltpu.VMEM((nt, 1, ROUTE_BM), jnp.int32),
            pltpu.VMEM((nt, 1, ROUTE_BM), jnp.int32),
            pltpu.VMEM((CLASS_ROWS, META_LANES), F32),
        ],
        compiler_params=_cparams(("arbitrary",)),
        name="router",
    )(x2d, rw_t, rb_t, tri)


ROW_BM = 512


def _start_rows(make_copy):
    for r in range(ROW_BM):
        make_copy(r).start(priority=r % 2)


def _dispatch_kernel(dest_ref, x_ref, zeros_ref, xs_ref, sem):
    del zeros_ref
    base = pl.program_id(0) * ROW_BM
    _start_rows(lambda r: pltpu.make_async_copy(x_ref.at[pl.ds(r, 1), :],
                                                xs_ref.at[pl.ds(dest_ref[base + r], 1), :], sem))
    pltpu.make_async_copy(x_ref, xs_ref.at[pl.ds(0, ROW_BM), :], sem).wait()


def _dispatch(dest, x_rows, n_rows):
    t, width = x_rows.shape
    grid_spec = pltpu.PrefetchScalarGridSpec(
        num_scalar_prefetch=1,
        grid=(t // ROW_BM,),
        in_specs=[
            pl.BlockSpec((ROW_BM, width), lambda i, dest: (i, 0)),
            pl.BlockSpec(memory_space=pl.ANY),
        ],
        out_specs=pl.BlockSpec(memory_space=pl.ANY),
        scratch_shapes=[pltpu.SemaphoreType.DMA],
    )
    return pl.pallas_call(
        _dispatch_kernel,
        out_shape=jax.ShapeDtypeStruct((n_rows, width), x_rows.dtype),
        grid_spec=grid_spec,
        input_output_aliases={2: 0},
        compiler_params=_cparams(("arbitrary",)),
        name="dispatch",
    )(dest, x_rows, jnp.zeros((n_rows, width), x_rows.dtype))


EXPERT_RING = 3
EXPERT_AHEAD = EXPERT_RING - 1


def _experts_kernel(ea_ref, eb_ref, nb_ref, x_ref, wg_hbm, wu_hbm, wd_hbm, o_ref,
                    ring_g, ring_u, ring_d, sems, loads_ref, hg_buf, hu_buf, act_buf, *, layer, n_blocks):
    j = pl.program_id(0)
    nb = nb_ref[0]
    experts_of = (ea_ref, eb_ref)

    def is_load(s, step):
        step = jnp.minimum(step, n_blocks - 1)
        return ((step == 0) | (experts_of[s][step] != experts_of[s][jnp.maximum(step - 1, 0)])).astype(jnp.int32)

    def weight_copies(s, step, buf):
        e = experts_of[s][jnp.minimum(step, n_blocks - 1)]
        return (pltpu.make_async_copy(wg_hbm.at[layer, e], ring_g.at[s, buf], sems.at[s, buf]),
                pltpu.make_async_copy(wu_hbm.at[layer, e], ring_u.at[s, buf], sems.at[s, buf]),
                pltpu.make_async_copy(wd_hbm.at[layer, e], ring_d.at[s, buf], sems.at[s, buf]))

    def start(s, step, load_number):
        for c in weight_copies(s, step, lax.rem(load_number - 1, EXPERT_RING)):
            c.start()

    @pl.when(j < nb)
    def _():
        bufs = []
        for s in range(2):
            loads_now = jnp.where(j == 0, 0, loads_ref[s]) + is_load(s, j)
            loads_ref[s] = loads_now
            buf = lax.rem(loads_now - 1, EXPERT_RING)
            bufs.append(buf)

            for first in range(EXPERT_AHEAD):
                ahead_loads = 1 + sum(is_load(s, k) for k in range(1, first + 1))

                @pl.when((j == 0) & (first < nb) & (is_load(s, first) == 1))
                def _(first=first, ahead_loads=ahead_loads):
                    start(s, first, ahead_loads)

            ahead = j + EXPERT_AHEAD
            ahead_loads = loads_now + sum(is_load(s, j + k) for k in range(1, EXPERT_AHEAD + 1))

            @pl.when((ahead < nb) & (is_load(s, ahead) == 1))
            def _(ahead=ahead, ahead_loads=ahead_loads):
                start(s, ahead, ahead_loads)

            @pl.when(is_load(s, j) == 1)
            def _(buf=buf):
                for c in weight_copies(s, j, buf):
                    c.wait()

        x = jnp.concatenate(_unpack_bf16_pair(x_ref[:, :DISPATCH_X]), axis=1).astype(BF16)
        gate_w = lax.bitcast_convert_type(x_ref[:, DISPATCH_X:], F32)
        for s in range(2):
            hg_buf[s] = jnp.dot(x, ring_g[s, bufs[s]].astype(BF16), preferred_element_type=F32)
            hu_buf[s] = jnp.dot(x, ring_u[s, bufs[s]].astype(BF16), preferred_element_type=F32)
        for s in range(2):
            hg = hg_buf[s]
            act_buf[s] = (hg * _sigmoid(hg) * hu_buf[s]).astype(BF16)
        ys = [jnp.dot(act_buf[s], ring_d[s, bufs[s]].astype(BF16), preferred_element_type=F32) for s in range(2)]
        y = gate_w[:, 0:1] * ys[0] + gate_w[:, 1:2] * ys[1]
        o_ref[...] = _pack_bf16_pair(y[:, :DISPATCH_X], y[:, DISPATCH_X:])

    @pl.when(j >= nb)
    def _():
        o_ref[...] = jnp.zeros_like(o_ref)


def _experts(blk_a, blk_b, n_used, xs, wg, wu, wd, layer):
    n_rows = xs.shape[0]
    n_blocks = n_rows // MOE_BLOCK
    in_row_map = lambda j, ea, eb, nb: (jnp.maximum(jnp.minimum(j, nb[0] - 1), 0), 0)
    row_map = lambda j, ea, eb, nb: (j, 0)
    grid_spec = pltpu.PrefetchScalarGridSpec(
        num_scalar_prefetch=3,
        grid=(n_blocks,),
        in_specs=[
            pl.BlockSpec((MOE_BLOCK, DISPATCH_W), in_row_map),
            pl.BlockSpec(memory_space=pl.ANY),
            pl.BlockSpec(memory_space=pl.ANY),
            pl.BlockSpec(memory_space=pl.ANY),
        ],
        out_specs=pl.BlockSpec((MOE_BLOCK, DISPATCH_X), row_map),
        scratch_shapes=[
            pltpu.VMEM((2, EXPERT_RING, D_MODEL, D_EXPERT), F32),
            pltpu.VMEM((2, EXPERT_RING, D_MODEL, D_EXPERT), F32),
            pltpu.VMEM((2, EXPERT_RING, D_EXPERT, D_MODEL), F32),
            pltpu.SemaphoreType.DMA((2, EXPERT_RING)),
            pltpu.SMEM((2,), jnp.int32),
            pltpu.VMEM((2, MOE_BLOCK, D_EXPERT), F32),
            pltpu.VMEM((2, MOE_BLOCK, D_EXPERT), F32),
            pltpu.VMEM((2, MOE_BLOCK, D_EXPERT), BF16),
        ],
    )
    return pl.pallas_call(
        functools.partial(_experts_kernel, layer=layer, n_blocks=n_blocks),
        out_shape=jax.ShapeDtypeStruct((n_rows, DISPATCH_X), U32),
        grid_spec=grid_spec,
        compiler_params=_cparams(("arbitrary",)),
        name="experts",
    )(blk_a, blk_b, n_used, xs, wg, wu, wd)


COMBINE_CHUNK = 64


def _combine_ln_kernel(dest_ref, x_ref, lg_ref, lb_ref, ys_ref, o_ref, obf_ref, ybuf, sem):
    i = pl.program_id(0)
    n = pl.num_programs(0)
    slot = i % 2

    def row_copy(tile, to_slot, r):
        return pltpu.make_async_copy(ys_ref.at[pl.ds(dest_ref[tile * ROW_BM + r], 1), :],
                                     ybuf.at[to_slot, pl.ds(r, 1), :], sem.at[to_slot])

    def wait_tile(of_slot):
        pltpu.make_async_copy(ys_ref.at[pl.ds(0, ROW_BM), :], ybuf.at[of_slot], sem.at[of_slot]).wait()

    @pl.when(i == 0)
    def _():
        _start_rows(lambda r: row_copy(0, 0, r))

    wait_tile(slot)
    nxt = lax.rem(i + 1, n)
    for c in range(0, ROW_BM, COMBINE_CHUNK):
        rows = slice(c, c + COMBINE_CHUNK)
        y = jnp.concatenate(_unpack_bf16_pair(ybuf[slot, rows, :]), axis=1)
        z = DEEPNORM_ALPHA * x_ref[rows, :] + y
        out = _layernorm_rows(z, lg_ref[...], lb_ref[...])
        o_ref[rows, :] = out
        obf_ref[rows, :] = out.astype(BF16)
        for r in range(c, c + COMBINE_CHUNK):
            row_copy(nxt, 1 - slot, r).start(priority=r % 2)

    @pl.when(i == n - 1)
    def _():
        wait_tile(1 - slot)


def _combine_ln(dest, x2d, ys, lg, lb):
    t = x2d.shape[0]
    const = lambda i, dest: (0, 0)
    grid_spec = pltpu.PrefetchScalarGridSpec(
        num_scalar_prefetch=1,
        grid=(t // ROW_BM,),
        in_specs=[
            pl.BlockSpec((ROW_BM, D_MODEL), lambda i, dest: (i, 0)),
            pl.BlockSpec((1, D_MODEL), const),
            pl.BlockSpec((1, D_MODEL), const),
            pl.BlockSpec(memory_space=pl.ANY),
        ],
        out_specs=(pl.BlockSpec((ROW_BM, D_MODEL), lambda i, dest: (i, 0)),
                   pl.BlockSpec((ROW_BM, D_MODEL), lambda i, dest: (i, 0))),
        scratch_shapes=[pltpu.VMEM((2, ROW_BM, DISPATCH_X), U32), pltpu.SemaphoreType.DMA((2,))],
    )
    return pl.pallas_call(
        _combine_ln_kernel,
        out_shape=(jax.ShapeDtypeStruct((t, D_MODEL), F32), jax.ShapeDtypeStruct((t, D_MODEL), BF16)),
        grid_spec=grid_spec,
        compiler_params=_cparams(("arbitrary",)),
        name="combine_ln",
    )(dest, x2d, lg, lb, ys)


def _moe(x2d, rw_t, rb_t, wg, wu, wd, layer, lg, lb):
    t = x2d.shape[0]
    n_blocks = (t + N_CLASSES * (MOE_BLOCK - 1)) // MOE_BLOCK
    assert n_blocks <= META_LANES
    dest3, xw, meta = _router(x2d, rw_t, rb_t)
    dest = dest3.reshape(t)
    xs = _dispatch(dest, xw, n_blocks * MOE_BLOCK)
    ys = _experts(meta[0, :n_blocks], meta[1, :n_blocks], meta[2, :1], xs, wg, wu, wd, layer)
    return _combine_ln(dest, x2d, ys, lg, lb)


def _rope_tables(seq):
    half = RET_DK // 2
    pos = jnp.arange(seq, dtype=F32)
    freqs = ROPE_BASE ** (-jnp.arange(half, dtype=F32) / half)
    ang = pos[:, None] * freqs[None, :]
    return jnp.cos(ang), jnp.sin(ang)


def kernel(x, w_in, ret_decay_logit, w_ret_o, na_rpb, w_na_o, w_out, ln_mix_g, ln_mix_b, router_w, router_bias,
           w_exp_gate, w_exp_up, w_exp_down, ln_ffn_g, ln_ffn_b):
    b, s, d = x.shape
    depth = w_in.shape[0]
    t = b * s
    rows = s // GRID_W
    cos, sin = _rope_tables(s)
    rw_t = router_w.astype(F32).T.reshape(N_GROUPS, EXPERTS_PER_GROUP, d).transpose(1, 0, 2).reshape(N_EXPERTS, d)
    rb_t = router_bias.astype(F32).reshape(N_GROUPS, EXPERTS_PER_GROUP).T.reshape(N_EXPERTS, 1)
    x2d = x.reshape(t, d)
    x_bf = x2d.astype(BF16)
    w_in = w_in.astype(F32)
    for l in range(depth):
        qk = _inproj(x_bf, w_in, l, cos, sin, s, "rotary").reshape(b, s, -1)
        vn = _inproj(x_bf, w_in, l, cos, sin, s, "scale").reshape(b, s, -1)
        gr = _inproj(x_bf, w_in, l, cos, sin, s, "silu").reshape(b, s, -1)
        gates = _inproj(x_bf, w_in, l, cos, sin, s, "sigmoid")
        ret = _retention(qk, vn, gr, ret_decay_logit[l].astype(F32))
        na = _na(vn, na_rpb[l].astype(F32).reshape(-1))
        x2d = _mixout(ret.reshape(t, -1), na.reshape(t, -1), gates, x2d,
                      w_ret_o[l].astype(BF16), w_na_o[l].astype(BF16), w_out[l].astype(BF16),
                      ln_mix_g[l].reshape(1, d).astype(F32), ln_mix_b[l].reshape(1, d).astype(F32))
        x2d, x_bf = _moe(x2d, rw_t, rb_t, w_exp_gate, w_exp_up, w_exp_down, l,
                         ln_ffn_g[l].reshape(1, d).astype(F32), ln_ffn_b[l].reshape(1, d).astype(F32))
    return x2d.reshape(b, s, d)
```

```python
import functools

import numpy as np
import jax
import jax.numpy as jnp
from jax import lax
from jax.experimental import pallas as pl
from jax.experimental.pallas import tpu as pltpu

F32 = jnp.float32
BF16 = jnp.bfloat16

D_MODEL = 1024
GRID_W = 64
RET_HEADS = 4
RET_DK = 256
RET_DV = 512
ROPE_BASE = 10000.0
NA_HEADS = 16
NA_HD = 64
NA_KH = 8
NA_KW = 16
N_EXPERTS = 32
N_GROUPS = 8
EXPERTS_PER_GROUP = 4
D_EXPERT = 512
LN_EPS = 1e-5
GN_EPS = 1e-5
MODEL_DEPTH = 4
DEEPNORM_ALPHA = (2 * MODEL_DEPTH) ** 0.25

D_IN = 11264

RET_CHUNK = 256
NA_QROWS = 4
NA_KROWS = 12
NA_NEG = -1e30
MOE_BLOCK = 256
N_PAIRS = 6
PAIR_SLOT_A = (0, 2, 2, 3, 3, 3)
PAIR_SLOT_B = (1, 1, 0, 0, 1, 2)
N_CLASSES = N_GROUPS * N_PAIRS
VMEM_LIMIT = 56 * 1024 * 1024


def _cparams(sem):
    return pltpu.CompilerParams(dimension_semantics=sem, vmem_limit_bytes=VMEM_LIMIT)


U32 = jnp.uint32
_HIGH_HALF = 0xFFFF0000


def _pack_bf16_pair(hi, lo):
    hi_bits = lax.bitcast_convert_type(hi.astype(jnp.bfloat16).astype(F32), U32)
    lo_bits = lax.bitcast_convert_type(lo.astype(jnp.bfloat16).astype(F32), U32)
    return (hi_bits & jnp.uint32(_HIGH_HALF)) | (lo_bits >> 16)


def _unpack_bf16_pair(packed):
    hi = lax.bitcast_convert_type(packed & jnp.uint32(_HIGH_HALF), F32)
    lo = lax.bitcast_convert_type(packed << 16, F32)
    return hi, lo


IN_CHUNK = 256
LOG2E = 1.4426950408889634

IN_GROUPS = {
    "rotary": (1024, 2048, (0,)),
    "scale": (2048, 1024, (2, 3, 6, 7, 8)),
    "silu": (1024, 2048, (2,)),
    "sigmoid": (2048, 1024, (9, 10)),
}
VN_Q = 2048
VN_K = 3072
VN_V = 4096


def _sigmoid(x):
    return 0.5 * jnp.tanh(0.5 * x) + 0.5


def _inproj_kernel(x_ref, w_ref, cos_ref, sin_ref, o_ref, wbf_ref, *, kind, bn):
    j = pl.program_id(0)

    @pl.when(pl.program_id(1) == 0)
    def _():
        wbf_ref[...] = w_ref[0].astype(BF16)

    x = x_ref[...]
    half = RET_DK // 2
    for c in range(bn // IN_CHUNK):
        lo = c * IN_CHUNK
        acc = jnp.dot(x, wbf_ref[:, lo:lo + IN_CHUNK], preferred_element_type=F32)
        if kind == "rotary":
            scale = 1.0 if lo < RET_HEADS * RET_DK else RET_DK ** -0.5
            cos = cos_ref[...] * scale
            sin = sin_ref[...] * scale
            x1 = acc[:, :half]
            x2 = acc[:, half:]
            o_ref[:, lo:lo + half] = (x1 * cos - x2 * sin).astype(o_ref.dtype)
            o_ref[:, lo + half:lo + IN_CHUNK] = (x1 * sin + x2 * cos).astype(o_ref.dtype)
        elif kind == "scale":
            scale = jnp.where(j == 2, NA_HD ** -0.5 * LOG2E, 1.0)
            o_ref[:, lo:lo + IN_CHUNK] = (acc * scale).astype(o_ref.dtype)
        elif kind == "silu":
            o_ref[:, lo:lo + IN_CHUNK] = (acc * _sigmoid(acc)).astype(o_ref.dtype)
        else:
            o_ref[:, lo:lo + IN_CHUNK] = _sigmoid(acc).astype(o_ref.dtype)


def _inproj(x_bf, w_in, layer, cos, sin, seq, kind):
    t = x_bf.shape[0]
    bm, bn, blocks = IN_GROUPS[kind]
    pos_blocks = seq // bm

    def wcol(j):
        col = blocks[0]
        for k in range(1, len(blocks)):
            col = jnp.where(j == k, blocks[k], col)
        return col

    pos_map = (lambda j, i: (i % pos_blocks, 0)) if kind == "rotary" else (lambda j, i: (0, 0))
    return pl.pallas_call(
        functools.partial(_inproj_kernel, kind=kind, bn=bn),
        out_shape=jax.ShapeDtypeStruct((t, len(blocks) * bn), BF16),
        grid=(len(blocks), t // bm),
        in_specs=[
            pl.BlockSpec((bm, D_MODEL), lambda j, i: (i, 0)),
            pl.BlockSpec((1, D_MODEL, bn), lambda j, i: (layer, 0, wcol(j))),
            pl.BlockSpec((bm, RET_DK // 2), pos_map),
            pl.BlockSpec((bm, RET_DK // 2), pos_map),
        ],
        out_specs=pl.BlockSpec((bm, bn), lambda j, i: (i, j)),
        scratch_shapes=[pltpu.VMEM((D_MODEL, bn), BF16)],
        compiler_params=_cparams(("arbitrary", "arbitrary")),
        name="inproj_" + kind,
    )(x_bf, w_in, cos, sin)


def _log_sigmoid(x):
    return jnp.minimum(x, 0.0) - jnp.log1p(jnp.exp(-jnp.abs(x)))


RET_GROUP = 8


def _retention_kernel(dl_ref, q_ref, k_ref, v_ref, g_ref, o_ref,
                      sf_ref, sb_ref, st_ref, stb_ref, dm_ref, qdf_ref, qdb_ref, kdf_ref, kdb_ref,
                      p_buf, qf_buf, qb_buf, out_buf, *, nc):
    c_len = RET_CHUNK
    h = pl.program_id(1)
    lgf = _log_sigmoid(jnp.full((c_len, RET_DK), dl_ref[0, h], F32))
    lgb = _log_sigmoid(jnp.full((c_len, RET_DK), dl_ref[1, h], F32))
    ri = lax.broadcasted_iota(jnp.int32, (c_len, RET_DK), 0).astype(F32)
    qdf_ref[...] = jnp.exp(lgf * (ri + 1.0))
    qdb_ref[...] = jnp.exp(lgb * (c_len - ri))
    kdf_ref[...] = jnp.exp(lgf * (c_len - 1.0 - ri))
    kdb_ref[...] = jnp.exp(lgb * ri)
    rr = lax.broadcasted_iota(jnp.int32, (c_len, c_len), 0).astype(F32)
    cc = lax.broadcasted_iota(jnp.int32, (c_len, c_len), 1).astype(F32)
    diff = rr - cc
    lgf_cc = _log_sigmoid(jnp.full((c_len, c_len), dl_ref[0, h], F32))
    lgb_cc = _log_sigmoid(jnp.full((c_len, c_len), dl_ref[1, h], F32))
    dm_ref[...] = jnp.where(diff >= 0.0, jnp.exp(lgf_cc * jnp.maximum(diff, 0.0)),
                            jnp.exp(lgb_cc * jnp.maximum(-diff, 0.0)))
    chunk_f = jnp.exp(lgf[:1, :1] * float(c_len))
    chunk_b = jnp.exp(lgb[:1, :1] * float(c_len))

    tn_dims = (((0,), (0,)), ((), ()))
    nt_dims = (((1,), (1,)), ((), ()))

    st_ref[...] = jnp.zeros_like(st_ref)
    stb_ref[...] = jnp.zeros_like(stb_ref)

    def state_body(i, carry):
        for c, all_ref, cur_ref, kdec_ref, chunk_dec in ((i, sf_ref, st_ref, kdf_ref, chunk_f),
                                                         (nc - 1 - i, sb_ref, stb_ref, kdb_ref, chunk_b)):
            off = pl.multiple_of(c * c_len, c_len)
            all_ref[c] = cur_ref[...].astype(BF16)
            kc = k_ref[0, pl.ds(off, c_len), :].astype(F32)
            vc = v_ref[0, pl.ds(off, c_len), :]
            kd = (kc * kdec_ref[...]).astype(BF16)
            upd = lax.dot_general(kd, vc, tn_dims, preferred_element_type=F32)
            cur_ref[...] = cur_ref[...] * chunk_dec + upd
        return carry

    lax.fori_loop(0, nc, state_body, 0, unroll=4)

    def out_body(g, carry):
        offs = [pl.multiple_of((g * RET_GROUP + u) * c_len, c_len) for u in range(RET_GROUP)]
        for u, off in enumerate(offs):
            qb = q_ref[0, pl.ds(off, c_len), :]
            kb = k_ref[0, pl.ds(off, c_len), :]
            qc = qb.astype(F32)
            s = lax.dot_general(qb, kb, nt_dims, preferred_element_type=F32)
            p_buf[u] = (s * dm_ref[...]).astype(BF16)
            qf_buf[u] = (qc * qdf_ref[...]).astype(BF16)
            qb_buf[u] = (qc * qdb_ref[...]).astype(BF16)
        for u, off in enumerate(offs):
            c = g * RET_GROUP + u
            out = jnp.dot(p_buf[u], v_ref[0, pl.ds(off, c_len), :], preferred_element_type=F32)
            out = out + jnp.dot(qf_buf[u], sf_ref[c], preferred_element_type=F32)
            out_buf[u] = out + jnp.dot(qb_buf[u], sb_ref[c], preferred_element_type=F32)
        for u, off in enumerate(offs):
            out = out_buf[u]
            mu = jnp.mean(out, axis=-1, keepdims=True)
            cen = out - mu
            var = jnp.mean(cen * cen, axis=-1, keepdims=True)
            y = cen * lax.rsqrt(var + GN_EPS)
            gate = g_ref[0, pl.ds(off, c_len), :].astype(F32)
            o_ref[0, pl.ds(off, c_len), :] = (gate * y).astype(o_ref.dtype)
        return carry

    lax.fori_loop(0, nc // RET_GROUP, out_body, 0)


def _retention(qk3, vn3, gr3, decay_logit):
    b, s, _ = qk3.shape
    nc = s // RET_CHUNK
    return pl.pallas_call(
        functools.partial(_retention_kernel, nc=nc),
        out_shape=jax.ShapeDtypeStruct((b, s, RET_HEADS * RET_DV), BF16),
        grid=(b, RET_HEADS),
        in_specs=[
            pl.BlockSpec(memory_space=pltpu.SMEM),
            pl.BlockSpec((1, s, RET_DK), lambda bi, h: (bi, 0, h)),
            pl.BlockSpec((1, s, RET_DK), lambda bi, h: (bi, 0, RET_HEADS + h)),
            pl.BlockSpec((1, s, RET_DV), lambda bi, h: (bi, 0, h)),
            pl.BlockSpec((1, s, RET_DV), lambda bi, h: (bi, 0, h)),
        ],
        out_specs=pl.BlockSpec((1, s, RET_DV), lambda bi, h: (bi, 0, h)),
        scratch_shapes=[
            pltpu.VMEM((nc, RET_DK, RET_DV), BF16),
            pltpu.VMEM((nc, RET_DK, RET_DV), BF16),
            pltpu.VMEM((RET_DK, RET_DV), F32),
            pltpu.VMEM((RET_DK, RET_DV), F32),
            pltpu.VMEM((RET_CHUNK, RET_CHUNK), F32),
            pltpu.VMEM((RET_CHUNK, RET_DK), F32),
            pltpu.VMEM((RET_CHUNK, RET_DK), F32),
            pltpu.VMEM((RET_CHUNK, RET_DK), F32),
            pltpu.VMEM((RET_CHUNK, RET_DK), F32),
            pltpu.VMEM((RET_GROUP, RET_CHUNK, RET_CHUNK), BF16),
            pltpu.VMEM((RET_GROUP, RET_CHUNK, RET_DK), BF16),
            pltpu.VMEM((RET_GROUP, RET_CHUNK, RET_DK), BF16),
            pltpu.VMEM((RET_GROUP, RET_CHUNK, RET_DV), F32),
        ],
        compiler_params=_cparams(("arbitrary", "arbitrary")),
        name="retention",
    )(decay_logit, qk3, qk3, vn3, gr3)


NA_ROW_OFFS = 2 * NA_KH - 1
NA_COL_OFFS = 2 * NA_KW - 1
NA_PATTERNS = 3


def _na_row_offsets(rows):
    n_tiles = rows // NA_QROWS
    offs = np.full((NA_PATTERNS, NA_QROWS, NA_KROWS), NA_ROW_OFFS, np.int32)
    for p, t in enumerate((0, 1, n_tiles - 1)):
        kstart = int(np.clip(NA_QROWS * t - NA_KH // 2, 0, rows - NA_KROWS))
        for rr in range(NA_QROWS):
            r = NA_QROWS * t + rr
            rs = int(np.clip(r - NA_KH // 2, 0, rows - NA_KH))
            for i in range(NA_KROWS):
                krow = kstart + i
                if rs <= krow < rs + NA_KH:
                    offs[p, rr, i] = krow - r + (NA_KH - 1)
    return offs


def _na_build_bias(rpb_ref, e_ref, bias_ref, head0, rows):
    wide = 2 * GRID_W
    lane = lax.broadcasted_iota(jnp.int32, (GRID_W, wide), 1)
    col = lax.broadcasted_iota(jnp.int32, (GRID_W, wide), 0)
    kcol = jnp.where(lane < GRID_W, lane, lane - GRID_W)
    cstart = jnp.clip(col - NA_KW // 2, 0, GRID_W - NA_KW)
    col_ok = (kcol >= cstart) & (kcol < cstart + NA_KW)
    coff = kcol - col + (NA_KW - 1)
    neg = jnp.full((GRID_W, wide), NA_NEG, F32)
    left = lane < GRID_W
    offs = _na_row_offsets(rows)
    for hh in range(2):
        base = (head0 + hh) * (NA_ROW_OFFS * NA_COL_OFFS)
        for ro in range(NA_ROW_OFFS):
            acc = neg
            for d in range(NA_COL_OFFS):
                acc = jnp.where(coff == d, rpb_ref[base + ro * NA_COL_OFFS + d] * LOG2E, acc)
            e_ref[hh, ro] = jnp.where(col_ok, acc, neg)
        e_ref[hh, NA_ROW_OFFS] = neg
        for p in range(NA_PATTERNS):
            for rr in range(NA_QROWS):
                for ip in range(NA_KROWS // 2):
                    tile = jnp.where(left, e_ref[hh, int(offs[p, rr, 2 * ip])], e_ref[hh, int(offs[p, rr, 2 * ip + 1])])
                    bias_ref[hh, p, rr * GRID_W:(rr + 1) * GRID_W, ip * wide:(ip + 1) * wide] = tile


NA_GROUP = 8


def _na_kernel(rpb_ref, q_ref, k_ref, v_ref, o_ref, bias_ref, e_ref, s_buf, p_buf, *, n_tiles, rows):
    nq = NA_QROWS * GRID_W
    nk = NA_KROWS * GRID_W
    nt_dims = (((1,), (1,)), ((), ()))
    first = lax.broadcasted_iota(jnp.int32, (nq, 2 * NA_HD), 1) < NA_HD
    kfirst = lax.broadcasted_iota(jnp.int32, (nk, 2 * NA_HD), 1) < NA_HD

    @pl.when(pl.program_id(1) == 0)
    def _():
        _na_build_bias(rpb_ref, e_ref, bias_ref, 2 * pl.program_id(0), rows)

    def body(g, carry):
        offs = []
        for u in range(NA_GROUP):
            t = g * NA_GROUP + u
            qoff = pl.multiple_of(t * nq, nq)
            krow0 = jnp.clip(NA_QROWS * t - NA_KH // 2, 0, rows - NA_KROWS)
            koff = pl.multiple_of(krow0 * GRID_W, nq)
            pat = jnp.where(t == 0, 0, jnp.where(t == n_tiles - 1, 2, 1))
            offs.append((qoff, koff, pat))
        for u, (qoff, koff, pat) in enumerate(offs):
            q2 = q_ref[0, pl.ds(qoff, nq), :]
            k2 = k_ref[0, pl.ds(koff, nk), :]
            for hh in range(2):
                qm = jnp.where(first if hh == 0 else jnp.logical_not(first), q2, jnp.zeros_like(q2))
                s_buf[2 * u + hh] = lax.dot_general(qm, k2, nt_dims, preferred_element_type=F32) + bias_ref[hh, pat]
        for n in range(2 * NA_GROUP):
            s = s_buf[n]
            p_buf[n] = jnp.exp2(s - jnp.max(s, axis=-1, keepdims=True)).astype(BF16)
        for u, (qoff, koff, pat) in enumerate(offs):
            v2 = v_ref[0, pl.ds(koff, nk), :]
            outs = []
            for hh in range(2):
                vv = jnp.where(kfirst if hh == 0 else jnp.logical_not(kfirst), v2, jnp.ones_like(v2))
                o = jnp.dot(p_buf[2 * u + hh], vv, preferred_element_type=F32)
                outs.append(o / pltpu.roll(o, NA_HD, axis=1))
            o_ref[0, pl.ds(qoff, nq), :] = jnp.where(first, outs[0], outs[1]).astype(o_ref.dtype)
        return carry

    lax.fori_loop(0, n_tiles // NA_GROUP, body, 0)


def _na(vn3, rpb_flat):
    b, s, _ = vn3.shape
    rows = s // GRID_W
    n_tiles = rows // NA_QROWS
    w2 = 2 * NA_HD
    nq = NA_QROWS * GRID_W
    nk = NA_KROWS * GRID_W
    return pl.pallas_call(
        functools.partial(_na_kernel, n_tiles=n_tiles, rows=rows),
        out_shape=jax.ShapeDtypeStruct((b, s, NA_HEADS * NA_HD), BF16),
        grid=(NA_HEADS // 2, b),
        in_specs=[
            pl.BlockSpec(memory_space=pltpu.SMEM),
            pl.BlockSpec((1, s, w2), lambda hp, bi: (bi, 0, VN_Q // w2 + hp)),
            pl.BlockSpec((1, s, w2), lambda hp, bi: (bi, 0, VN_K // w2 + hp)),
            pl.BlockSpec((1, s, w2), lambda hp, bi: (bi, 0, VN_V // w2 + hp)),
        ],
        out_specs=pl.BlockSpec((1, s, w2), lambda hp, bi: (bi, 0, hp)),
        scratch_shapes=[
            pltpu.VMEM((2, NA_PATTERNS, nq, nk), F32),
            pltpu.VMEM((2, NA_ROW_OFFS + 1, GRID_W, 2 * GRID_W), F32),
            pltpu.VMEM((2 * NA_GROUP, nq, nk), F32),
            pltpu.VMEM((2 * NA_GROUP, nq, nk), BF16),
        ],
        compiler_params=_cparams(("arbitrary", "arbitrary")),
        name="natten",
    )(rpb_flat, vn3, vn3, vn3)


MIX_BM = 512
MIX_SUB = 256


def _layernorm_rows(z, g, b):
    mu = jnp.mean(z, axis=-1, keepdims=True)
    cen = z - mu
    var = jnp.mean(cen * cen, axis=-1, keepdims=True)
    return cen * lax.rsqrt(var + LN_EPS) * g + b


def _mixout_kernel(ret_ref, na_ref, g1_ref, g2_ref, x_ref, wr_ref, wn_ref, wo_ref, lg_ref, lb_ref, o_ref):
    for r in range(0, MIX_BM, MIX_SUB):
        rows = slice(r, r + MIX_SUB)
        y_ret = jnp.dot(ret_ref[rows, :], wr_ref[...], preferred_element_type=F32)
        y_na = jnp.dot(na_ref[rows, :], wn_ref[...], preferred_element_type=F32)
        merged = g1_ref[rows, :].astype(F32) * y_ret + g2_ref[rows, :].astype(F32) * y_na
        mix = jnp.dot(merged.astype(BF16), wo_ref[...], preferred_element_type=F32)
        z = DEEPNORM_ALPHA * x_ref[rows, :] + mix
        o_ref[rows, :] = _layernorm_rows(z, lg_ref[...], lb_ref[...])


def _mixout(ret2d, na2d, gates2d, x2d, wr, wn, wo, lg, lb):
    t = x2d.shape[0]
    const = lambda i: (0, 0)
    return pl.pallas_call(
        _mixout_kernel,
        out_shape=jax.ShapeDtypeStruct((t, D_MODEL), F32),
        grid=(t // MIX_BM,),
        in_specs=[
            pl.BlockSpec((MIX_BM, RET_HEADS * RET_DV), lambda i: (i, 0)),
            pl.BlockSpec((MIX_BM, NA_HEADS * NA_HD), lambda i: (i, 0)),
            pl.BlockSpec((MIX_BM, D_MODEL), lambda i: (i, 0)),
            pl.BlockSpec((MIX_BM, D_MODEL), lambda i: (i, 1)),
            pl.BlockSpec((MIX_BM, D_MODEL), lambda i: (i, 0)),
            pl.BlockSpec((RET_HEADS * RET_DV, D_MODEL), const),
            pl.BlockSpec((NA_HEADS * NA_HD, D_MODEL), const),
            pl.BlockSpec((D_MODEL, D_MODEL), const),
            pl.BlockSpec((1, D_MODEL), const),
            pl.BlockSpec((1, D_MODEL), const),
        ],
        out_specs=pl.BlockSpec((MIX_BM, D_MODEL), lambda i: (i, 0)),
        compiler_params=_cparams(("arbitrary",)),
        name="mixout",
    )(ret2d, na2d, gates2d, gates2d, x2d, wr, wn, wo, lg, lb)


ROUTE_BM = 512


def _route_tile(x, rw, rb):
    nt_dims = (((1,), (1,)), ((), ()))
    x_hi = x.astype(BF16)
    x_lo = (x - x_hi.astype(F32)).astype(BF16)
    rw_hi = rw.astype(BF16)
    rw_lo = (rw - rw_hi.astype(F32)).astype(BF16)
    both = lax.dot_general(jnp.concatenate([rw_hi, rw_lo], axis=0), x_hi, nt_dims, preferred_element_type=F32)
    logits = (both[:N_EXPERTS] + both[N_EXPERTS:]
              + lax.dot_general(rw_hi, x_lo, nt_dims, preferred_element_type=F32))
    scores = jax.nn.sigmoid(logits)
    sel = scores + rb
    p = [scores[m * N_GROUPS:(m + 1) * N_GROUPS] for m in range(EXPERTS_PER_GROUP)]
    s = [sel[m * N_GROUPS:(m + 1) * N_GROUPS] for m in range(EXPERTS_PER_GROUP)]
    one = jnp.ones_like(s[0])
    zero = jnp.zeros_like(s[0])
    chosen = []
    for m in range(EXPERTS_PER_GROUP):
        rank = zero
        for j in range(EXPERTS_PER_GROUP):
            if j == m:
                continue
            beats = (s[j] >= s[m]) if j < m else (s[j] > s[m])
            rank = rank + jnp.where(beats, one, zero)
        chosen.append(rank < 2.0)
    group_score = zero
    for m in range(EXPERTS_PER_GROUP):
        group_score = group_score + jnp.where(chosen[m], s[m], zero)
    gid = lax.broadcasted_iota(jnp.int32, group_score.shape, 0)
    gmax = jnp.max(group_score, axis=0, keepdims=True)
    gbest = jnp.min(jnp.where(group_score == gmax, gid, N_GROUPS), axis=0, keepdims=True)
    in_best = gid == gbest
    picked = [chosen[m] & in_best for m in range(EXPERTS_PER_GROUP)]
    before = zero
    wa = zero
    wb = zero
    ma = zero
    mb = zero
    for m in range(EXPERTS_PER_GROUP):
        is_a = picked[m] & (before == 0.0)
        is_b = picked[m] & (before == 1.0)
        wa = wa + jnp.where(is_a, p[m], zero)
        wb = wb + jnp.where(is_b, p[m], zero)
        ma = ma + jnp.where(is_a, float(m), 0.0)
        mb = mb + jnp.where(is_b, float(m), 0.0)
        before = before + jnp.where(chosen[m], one, zero)
    wa = jnp.sum(wa, axis=0, keepdims=True)
    wb = jnp.sum(wb, axis=0, keepdims=True)
    ma = jnp.sum(ma, axis=0, keepdims=True)
    mb = jnp.sum(mb, axis=0, keepdims=True)
    denom = wa + wb
    w_lo = wa / denom
    w_hi = wb / denom
    pair = jnp.where(ma == 0.0, jnp.where(mb == 1.0, 0.0, mb),
                     jnp.where(ma == 1.0, jnp.where(mb == 2.0, 1.0, 4.0), 5.0))
    keep = pair == 0.0
    return (gbest * N_PAIRS + pair.astype(jnp.int32),
            jnp.where(keep, w_lo, w_hi), jnp.where(keep, w_hi, w_lo))


CLASS_ROWS = 64
META_LANES = 128


DISPATCH_X = D_MODEL // 2
DISPATCH_W = DISPATCH_X + META_LANES


def _router_kernel(x_ref, rw_ref, rb_ref, tri_ref, dest_ref, xw_ref, meta_ref, cls_s, rank_s, cnt_s):
    i = pl.program_id(0)
    nt = pl.num_programs(0) - 1
    bm = ROUTE_BM
    reps = bm // META_LANES
    cid = lax.broadcasted_iota(jnp.int32, (CLASS_ROWS, bm), 0)

    @pl.when(i == 0)
    def _():
        cnt_s[...] = jnp.zeros_like(cnt_s)

    @pl.when(i < nt)
    def _():
        x = x_ref[...]
        cls, w_a, w_b = _route_tile(x, rw_ref[...], rb_ref[...])
        cls_s[i] = cls
        w_rows = jnp.concatenate([w_a, w_b, jnp.zeros((META_LANES - 2, bm), F32)], axis=0)
        xw_ref[:, :DISPATCH_X] = _pack_bf16_pair(x[:, :DISPATCH_X], x[:, DISPATCH_X:])
        xw_ref[:, DISPATCH_X:] = lax.bitcast_convert_type(w_rows.T, U32)
        onehot = (cid == cls).astype(BF16)
        before = jnp.dot(onehot, tri_ref[...], preferred_element_type=F32)
        carry = jnp.concatenate([cnt_s[...]] * reps, axis=1)
        rank = jnp.sum(jnp.where(cid == cls, before + carry, 0.0), axis=0, keepdims=True)
        rank_s[i] = rank.astype(jnp.int32)
        cnt_s[...] = cnt_s[...] + jnp.dot(onehot, jnp.ones((bm, META_LANES), BF16), preferred_element_type=F32)

    @pl.when(i == nt)
    def _():
        cnt = cnt_s[...]
        nblk = jnp.floor((cnt + (MOE_BLOCK - 1.0)) * (1.0 / MOE_BLOCK))
        rr = lax.broadcasted_iota(jnp.int32, (CLASS_ROWS, CLASS_ROWS), 0)
        cc = lax.broadcasted_iota(jnp.int32, (CLASS_ROWS, CLASS_ROWS), 1)
        lower = (cc < rr).astype(BF16)
        start_blk = jnp.dot(lower, nblk.astype(BF16), preferred_element_type=F32)
        start_t = jnp.concatenate([start_blk] * reps, axis=1)

        def dest_tile(t, carry):
            start = jnp.sum(jnp.where(cid == cls_s[t], start_t, 0.0), axis=0, keepdims=True)
            dest_ref[t] = (start * float(MOE_BLOCK)).astype(jnp.int32) + rank_s[t]
            return carry

        lax.fori_loop(0, nt, dest_tile, 0)

        end_blk = start_blk + nblk
        n_used = jnp.max(end_blk, axis=0, keepdims=True)
        blk = lax.broadcasted_iota(jnp.int32, (CLASS_ROWS, META_LANES), 1).astype(F32)
        blk = jnp.minimum(blk, n_used - 1.0)
        bcls = jnp.sum(jnp.where(end_blk <= blk, 1.0, 0.0), axis=0, keepdims=True)
        grp = jnp.zeros_like(bcls)
        for g in range(1, N_GROUPS):
            grp = grp + jnp.where(bcls >= float(g * N_PAIRS), 1.0, 0.0)
        pair = bcls - grp * float(N_PAIRS)
        slot_a = jnp.zeros_like(pair)
        slot_b = jnp.zeros_like(pair)
        for k in range(N_PAIRS):
            slot_a = jnp.where(pair == float(k), float(PAIR_SLOT_A[k]), slot_a)
            slot_b = jnp.where(pair == float(k), float(PAIR_SLOT_B[k]), slot_b)
        rows = [grp * float(EXPERTS_PER_GROUP) + slot_a, grp * float(EXPERTS_PER_GROUP) + slot_b, n_used]
        rows = rows + [jnp.zeros_like(bcls)] * (8 - len(rows))
        meta_ref[...] = jnp.concatenate(rows, axis=0).astype(jnp.int32)


def _router(x2d, rw_t, rb_t):
    t = x2d.shape[0]
    nt = t // ROUTE_BM
    tri = jnp.asarray(np.triu(np.ones((ROUTE_BM, ROUTE_BM), np.float32), 1), BF16)
    tile = lambda i: (jnp.minimum(i, nt - 1), 0)
    const = lambda i: (0, 0)
    return pl.pallas_call(
        _router_kernel,
        out_shape=(jax.ShapeDtypeStruct((nt, 1, ROUTE_BM), jnp.int32),
                   jax.ShapeDtypeStruct((t, DISPATCH_W), U32),
                   jax.ShapeDtypeStruct((8, META_LANES), jnp.int32)),
        grid=(nt + 1,),
        in_specs=[
            pl.BlockSpec((ROUTE_BM, D_MODEL), tile),
            pl.BlockSpec((N_EXPERTS, D_MODEL), const),
            pl.BlockSpec((N_EXPERTS, 1), const),
            pl.BlockSpec((ROUTE_BM, ROUTE_BM), const),
        ],
        out_specs=(pl.BlockSpec((nt, 1, ROUTE_BM), lambda i: (0, 0, 0)),
                   pl.BlockSpec((ROUTE_BM, DISPATCH_W), tile),
                   pl.BlockSpec((8, META_LANES), const)),
        scratch_shapes=[
            pltpu.VMEM((nt, 1, ROUTE_BM), jnp.int32),
            pltpu.VMEM((nt, 1, ROUTE_BM), jnp.int32),
            pltpu.VMEM((CLASS_ROWS, META_LANES), F32),
        ],
        compiler_params=_cparams(("arbitrary",)),
        name="router",
    )(x2d, rw_t, rb_t, tri)


ROW_BM = 512


def _start_rows(make_copy):
    for r in range(ROW_BM):
        make_copy(r).start(priority=r % 2)


def _dispatch_kernel(dest_ref, x_ref, zeros_ref, xs_ref, sem):
    del zeros_ref
    base = pl.program_id(0) * ROW_BM
    _start_rows(lambda r: pltpu.make_async_copy(x_ref.at[pl.ds(r, 1), :],
                                                xs_ref.at[pl.ds(dest_ref[base + r], 1), :], sem))
    pltpu.make_async_copy(x_ref, xs_ref.at[pl.ds(0, ROW_BM), :], sem).wait()


def _dispatch(dest, x_rows, n_rows):
    t, width = x_rows.shape
    grid_spec = pltpu.PrefetchScalarGridSpec(
        num_scalar_prefetch=1,
        grid=(t // ROW_BM,),
        in_specs=[
            pl.BlockSpec((ROW_BM, width), lambda i, dest: (i, 0)),
            pl.BlockSpec(memory_space=pl.ANY),
        ],
        out_specs=pl.BlockSpec(memory_space=pl.ANY),
        scratch_shapes=[pltpu.SemaphoreType.DMA],
    )
    return pl.pallas_call(
        _dispatch_kernel,
        out_shape=jax.ShapeDtypeStruct((n_rows, width), x_rows.dtype),
        grid_spec=grid_spec,
        input_output_aliases={2: 0},
        compiler_params=_cparams(("arbitrary",)),
        name="dispatch",
    )(dest, x_rows, jnp.zeros((n_rows, width), x_rows.dtype))


EXPERT_RING = 3
EXPERT_AHEAD = EXPERT_RING - 1


def _experts_kernel(ea_ref, eb_ref, nb_ref, x_ref, wg_hbm, wu_hbm, wd_hbm, o_ref,
                    ring_g, ring_u, ring_d, sems, loads_ref, hg_buf, hu_buf, act_buf, *, layer, n_blocks):
    j = pl.program_id(0)
    nb = nb_ref[0]
    experts_of = (ea_ref, eb_ref)

    def is_load(s, step):
        step = jnp.minimum(step, n_blocks - 1)
        return ((step == 0) | (experts_of[s][step] != experts_of[s][jnp.maximum(step - 1, 0)])).astype(jnp.int32)

    def weight_copies(s, step, buf):
        e = experts_of[s][jnp.minimum(step, n_blocks - 1)]
        return (pltpu.make_async_copy(wg_hbm.at[layer, e], ring_g.at[s, buf], sems.at[s, buf]),
                pltpu.make_async_copy(wu_hbm.at[layer, e], ring_u.at[s, buf], sems.at[s, buf]),
                pltpu.make_async_copy(wd_hbm.at[layer, e], ring_d.at[s, buf], sems.at[s, buf]))

    def start(s, step, load_number):
        for c in weight_copies(s, step, lax.rem(load_number - 1, EXPERT_RING)):
            c.start()

    @pl.when(j < nb)
    def _():
        bufs = []
        for s in range(2):
            loads_now = jnp.where(j == 0, 0, loads_ref[s]) + is_load(s, j)
            loads_ref[s] = loads_now
            buf = lax.rem(loads_now - 1, EXPERT_RING)
            bufs.append(buf)

            for first in range(EXPERT_AHEAD):
                ahead_loads = 1 + sum(is_load(s, k) for k in range(1, first + 1))

                @pl.when((j == 0) & (first < nb) & (is_load(s, first) == 1))
                def _(first=first, ahead_loads=ahead_loads):
                    start(s, first, ahead_loads)

            ahead = j + EXPERT_AHEAD
            ahead_loads = loads_now + sum(is_load(s, j + k) for k in range(1, EXPERT_AHEAD + 1))

            @pl.when((ahead < nb) & (is_load(s, ahead) == 1))
            def _(ahead=ahead, ahead_loads=ahead_loads):
                start(s, ahead, ahead_loads)

            @pl.when(is_load(s, j) == 1)
            def _(buf=buf):
                for c in weight_copies(s, j, buf):
                    c.wait()

        x = jnp.concatenate(_unpack_bf16_pair(x_ref[:, :DISPATCH_X]), axis=1).astype(BF16)
        gate_w = lax.bitcast_convert_type(x_ref[:, DISPATCH_X:], F32)
        for s in range(2):
            hg_buf[s] = jnp.dot(x, ring_g[s, bufs[s]].astype(BF16), preferred_element_type=F32)
            hu_buf[s] = jnp.dot(x, ring_u[s, bufs[s]].astype(BF16), preferred_element_type=F32)
        for s in range(2):
            hg = hg_buf[s]
            act_buf[s] = (hg * _sigmoid(hg) * hu_buf[s]).astype(BF16)
        ys = [jnp.dot(act_buf[s], ring_d[s, bufs[s]].astype(BF16), preferred_element_type=F32) for s in range(2)]
        y = gate_w[:, 0:1] * ys[0] + gate_w[:, 1:2] * ys[1]
        o_ref[...] = _pack_bf16_pair(y[:, :DISPATCH_X], y[:, DISPATCH_X:])

    @pl.when(j >= nb)
    def _():
        o_ref[...] = jnp.zeros_like(o_ref)


def _experts(blk_a, blk_b, n_used, xs, wg, wu, wd, layer):
    n_rows = xs.shape[0]
    n_blocks = n_rows // MOE_BLOCK
    in_row_map = lambda j, ea, eb, nb: (jnp.maximum(jnp.minimum(j, nb[0] - 1), 0), 0)
    row_map = lambda j, ea, eb, nb: (j, 0)
    grid_spec = pltpu.PrefetchScalarGridSpec(
        num_scalar_prefetch=3,
        grid=(n_blocks,),
        in_specs=[
            pl.BlockSpec((MOE_BLOCK, DISPATCH_W), in_row_map),
            pl.BlockSpec(memory_space=pl.ANY),
            pl.BlockSpec(memory_space=pl.ANY),
            pl.BlockSpec(memory_space=pl.ANY),
        ],
        out_specs=pl.BlockSpec((MOE_BLOCK, DISPATCH_X), row_map),
        scratch_shapes=[
            pltpu.VMEM((2, EXPERT_RING, D_MODEL, D_EXPERT), F32),
            pltpu.VMEM((2, EXPERT_RING, D_MODEL, D_EXPERT), F32),
            pltpu.VMEM((2, EXPERT_RING, D_EXPERT, D_MODEL), F32),
            pltpu.SemaphoreType.DMA((2, EXPERT_RING)),
            pltpu.SMEM((2,), jnp.int32),
            pltpu.VMEM((2, MOE_BLOCK, D_EXPERT), F32),
            pltpu.VMEM((2, MOE_BLOCK, D_EXPERT), F32),
            pltpu.VMEM((2, MOE_BLOCK, D_EXPERT), BF16),
        ],
    )
    return pl.pallas_call(
        functools.partial(_experts_kernel, layer=layer, n_blocks=n_blocks),
        out_shape=jax.ShapeDtypeStruct((n_rows, DISPATCH_X), U32),
        grid_spec=grid_spec,
        compiler_params=_cparams(("arbitrary",)),
        name="experts",
    )(blk_a, blk_b, n_used, xs, wg, wu, wd)


COMBINE_CHUNK = 64


def _combine_ln_kernel(dest_ref, x_ref, lg_ref, lb_ref, ys_ref, o_ref, obf_ref, ybuf, sem):
    i = pl.program_id(0)
    n = pl.num_programs(0)
    slot = i % 2

    def row_copy(tile, to_slot, r):
        return pltpu.make_async_copy(ys_ref.at[pl.ds(dest_ref[tile * ROW_BM + r], 1), :],
                                     ybuf.at[to_slot, pl.ds(r, 1), :], sem.at[to_slot])

    def wait_tile(of_slot):
        pltpu.make_async_copy(ys_ref.at[pl.ds(0, ROW_BM), :], ybuf.at[of_slot], sem.at[of_slot]).wait()

    @pl.when(i == 0)
    def _():
        _start_rows(lambda r: row_copy(0, 0, r))

    wait_tile(slot)
    nxt = lax.rem(i + 1, n)
    for c in range(0, ROW_BM, COMBINE_CHUNK):
        rows = slice(c, c + COMBINE_CHUNK)
        y = jnp.concatenate(_unpack_bf16_pair(ybuf[slot, rows, :]), axis=1)
        z = DEEPNORM_ALPHA * x_ref[rows, :] + y
        out = _layernorm_rows(z, lg_ref[...], lb_ref[...])
        o_ref[rows, :] = out
        obf_ref[rows, :] = out.astype(BF16)
        for r in range(c, c + COMBINE_CHUNK):
            row_copy(nxt, 1 - slot, r).start(priority=r % 2)

    @pl.when(i == n - 1)
    def _():
        wait_tile(1 - slot)


def _combine_ln(dest, x2d, ys, lg, lb):
    t = x2d.shape[0]
    const = lambda i, dest: (0, 0)
    grid_spec = pltpu.PrefetchScalarGridSpec(
        num_scalar_prefetch=1,
        grid=(t // ROW_BM,),
        in_specs=[
            pl.BlockSpec((ROW_BM, D_MODEL), lambda i, dest: (i, 0)),
            pl.BlockSpec((1, D_MODEL), const),
            pl.BlockSpec((1, D_MODEL), const),
            pl.BlockSpec(memory_space=pl.ANY),
        ],
        out_specs=(pl.BlockSpec((ROW_BM, D_MODEL), lambda i, dest: (i, 0)),
                   pl.BlockSpec((ROW_BM, D_MODEL), lambda i, dest: (i, 0))),
        scratch_shapes=[pltpu.VMEM((2, ROW_BM, DISPATCH_X), U32), pltpu.SemaphoreType.DMA((2,))],
    )
    return pl.pallas_call(
        _combine_ln_kernel,
        out_shape=(jax.ShapeDtypeStruct((t, D_MODEL), F32), jax.ShapeDtypeStruct((t, D_MODEL), BF16)),
        grid_spec=grid_spec,
        compiler_params=_cparams(("arbitrary",)),
        name="combine_ln",
    )(dest, x2d, lg, lb, ys)


def _moe(x2d, rw_t, rb_t, wg, wu, wd, layer, lg, lb):
    t = x2d.shape[0]
    n_blocks = (t + N_CLASSES * (MOE_BLOCK - 1)) // MOE_BLOCK
    assert n_blocks <= META_LANES
    dest3, xw, meta = _router(x2d, rw_t, rb_t)
    dest = dest3.reshape(t)
    xs = _dispatch(dest, xw, n_blocks * MOE_BLOCK)
    ys = _experts(meta[0, :n_blocks], meta[1, :n_blocks], meta[2, :1], xs, wg, wu, wd, layer)
    return _combine_ln(dest, x2d, ys, lg, lb)


def _rope_tables(seq):
    half = RET_DK // 2
    pos = jnp.arange(seq, dtype=F32)
    freqs = ROPE_BASE ** (-jnp.arange(half, dtype=F32) / half)
    ang = pos[:, None] * freqs[None, :]
    return jnp.cos(ang), jnp.sin(ang)


def kernel(x, w_in, ret_decay_logit, w_ret_o, na_rpb, w_na_o, w_out, ln_mix_g, ln_mix_b, router_w, router_bias,
           w_exp_gate, w_exp_up, w_exp_down, ln_ffn_g, ln_ffn_b):
    b, s, d = x.shape
    depth = w_in.shape[0]
    t = b * s
    rows = s // GRID_W
    cos, sin = _rope_tables(s)
    rw_t = router_w.astype(F32).T.reshape(N_GROUPS, EXPERTS_PER_GROUP, d).transpose(1, 0, 2).reshape(N_EXPERTS, d)
    rb_t = router_bias.astype(F32).reshape(N_GROUPS, EXPERTS_PER_GROUP).T.reshape(N_EXPERTS, 1)
    x2d = x.reshape(t, d)
    x_bf = x2d.astype(BF16)
    w_in = w_in.astype(F32)
    for l in range(depth):
        qk = _inproj(x_bf, w_in, l, cos, sin, s, "rotary").reshape(b, s, -1)
        vn = _inproj(x_bf, w_in, l, cos, sin, s, "scale").reshape(b, s, -1)
        gr = _inproj(x_bf, w_in, l, cos, sin, s, "silu").reshape(b, s, -1)
        gates = _inproj(x_bf, w_in, l, cos, sin, s, "sigmoid")
        ret = _retention(qk, vn, gr, ret_decay_logit[l].astype(F32))
        na = _na(vn, na_rpb[l].astype(F32).reshape(-1))
        x2d = _mixout(ret.reshape(t, -1), na.reshape(t, -1), gates, x2d,
                      w_ret_o[l].astype(BF16), w_na_o[l].astype(BF16), w_out[l].astype(BF16),
                      ln_mix_g[l].reshape(1, d).astype(F32), ln_mix_b[l].reshape(1, d).astype(F32))
        x2d, x_bf = _moe(x2d, rw_t, rb_t, w_exp_gate, w_exp_up, w_exp_down, l,
                         ln_ffn_g[l].reshape(1, d).astype(F32), ln_ffn_b[l].reshape(1, d).astype(F32))
    return x2d.reshape(b, s, d)
```

```python
import functools

import numpy as np
import jax
import jax.numpy as jnp
from jax import lax
from jax.experimental import pallas as pl
from jax.experimental.pallas import tpu as pltpu

F32 = jnp.float32
BF16 = jnp.bfloat16

D_MODEL = 1024
GRID_W = 64
RET_HEADS = 4
RET_DK = 256
RET_DV = 512
ROPE_BASE = 10000.0
NA_HEADS = 16
NA_HD = 64
NA_KH = 8
NA_KW = 16
N_EXPERTS = 32
N_GROUPS = 8
EXPERTS_PER_GROUP = 4
D_EXPERT = 512
LN_EPS = 1e-5
GN_EPS = 1e-5
MODEL_DEPTH = 4
DEEPNORM_ALPHA = (2 * MODEL_DEPTH) ** 0.25

D_IN = 11264

RET_CHUNK = 256
NA_QROWS = 4
NA_KROWS = 12
NA_NEG = -1e30
MOE_BLOCK = 256
N_PAIRS = 6
PAIR_SLOT_A = (0, 2, 2, 3, 3, 3)
PAIR_SLOT_B = (1, 1, 0, 0, 1, 2)
N_CLASSES = N_GROUPS * N_PAIRS
VMEM_LIMIT = 56 * 1024 * 1024


def _cparams(sem):
    return pltpu.CompilerParams(dimension_semantics=sem, vmem_limit_bytes=VMEM_LIMIT)


U32 = jnp.uint32
_HIGH_HALF = 0xFFFF0000


def _pack_bf16_pair(hi, lo):
    hi_bits = lax.bitcast_convert_type(hi.astype(jnp.bfloat16).astype(F32), U32)
    lo_bits = lax.bitcast_convert_type(lo.astype(jnp.bfloat16).astype(F32), U32)
    return (hi_bits & jnp.uint32(_HIGH_HALF)) | (lo_bits >> 16)


def _unpack_bf16_pair(packed):
    hi = lax.bitcast_convert_type(packed & jnp.uint32(_HIGH_HALF), F32)
    lo = lax.bitcast_convert_type(packed << 16, F32)
    return hi, lo


IN_CHUNK = 256
LOG2E = 1.4426950408889634

IN_GROUPS = {
    "rotary": (1024, 2048, (0,)),
    "scale": (2048, 1024, (2, 3, 6, 7, 8)),
    "silu": (1024, 2048, (2,)),
    "sigmoid": (2048, 1024, (9, 10)),
}
VN_Q = 2048
VN_K = 3072
VN_V = 4096


def _sigmoid(x):
    return 0.5 * jnp.tanh(0.5 * x) + 0.5


def _inproj_kernel(x_ref, w_ref, cos_ref, sin_ref, o_ref, wbf_ref, *, kind, bn):
    j = pl.program_id(0)

    @pl.when(pl.program_id(1) == 0)
    def _():
        wbf_ref[...] = w_ref[0].astype(BF16)

    x = x_ref[...]
    half = RET_DK // 2
    for c in range(bn // IN_CHUNK):
        lo = c * IN_CHUNK
        acc = jnp.dot(x, wbf_ref[:, lo:lo + IN_CHUNK], preferred_element_type=F32)
        if kind == "rotary":
            scale = 1.0 if lo < RET_HEADS * RET_DK else RET_DK ** -0.5
            cos = cos_ref[...] * scale
            sin = sin_ref[...] * scale
            x1 = acc[:, :half]
            x2 = acc[:, half:]
            o_ref[:, lo:lo + half] = (x1 * cos - x2 * sin).astype(o_ref.dtype)
            o_ref[:, lo + half:lo + IN_CHUNK] = (x1 * sin + x2 * cos).astype(o_ref.dtype)
        elif kind == "scale":
            scale = jnp.where(j == 2, NA_HD ** -0.5 * LOG2E, 1.0)
            o_ref[:, lo:lo + IN_CHUNK] = (acc * scale).astype(o_ref.dtype)
        elif kind == "silu":
            o_ref[:, lo:lo + IN_CHUNK] = (acc * _sigmoid(acc)).astype(o_ref.dtype)
        else:
            o_ref[:, lo:lo + IN_CHUNK] = _sigmoid(acc).astype(o_ref.dtype)


def _inproj(x_bf, w_in, layer, cos, sin, seq, kind):
    t = x_bf.shape[0]
    bm, bn, blocks = IN_GROUPS[kind]
    pos_blocks = seq // bm

    def wcol(j):
        col = blocks[0]
        for k in range(1, len(blocks)):
            col = jnp.where(j == k, blocks[k], col)
        return col

    pos_map = (lambda j, i: (i % pos_blocks, 0)) if kind == "rotary" else (lambda j, i: (0, 0))
    return pl.pallas_call(
        functools.partial(_inproj_kernel, kind=kind, bn=bn),
        out_shape=jax.ShapeDtypeStruct((t, len(blocks) * bn), BF16),
        grid=(len(blocks), t // bm),
        in_specs=[
            pl.BlockSpec((bm, D_MODEL), lambda j, i: (i, 0)),
            pl.BlockSpec((1, D_MODEL, bn), lambda j, i: (layer, 0, wcol(j))),
            pl.BlockSpec((bm, RET_DK // 2), pos_map),
            pl.BlockSpec((bm, RET_DK // 2), pos_map),
        ],
        out_specs=pl.BlockSpec((bm, bn), lambda j, i: (i, j)),
        scratch_shapes=[pltpu.VMEM((D_MODEL, bn), BF16)],
        compiler_params=_cparams(("arbitrary", "arbitrary")),
        name="inproj_" + kind,
    )(x_bf, w_in, cos, sin)


def _log_sigmoid(x):
    return jnp.minimum(x, 0.0) - jnp.log1p(jnp.exp(-jnp.abs(x)))


RET_GROUP = 8


def _retention_kernel(dl_ref, q_ref, k_ref, v_ref, g_ref, o_ref,
                      sf_ref, sb_ref, st_ref, stb_ref, dm_ref, qdf_ref, qdb_ref, kdf_ref, kdb_ref,
                      p_buf, qf_buf, qb_buf, out_buf, *, nc):
    c_len = RET_CHUNK
    h = pl.program_id(1)
    lgf = _log_sigmoid(jnp.full((c_len, RET_DK), dl_ref[0, h], F32))
    lgb = _log_sigmoid(jnp.full((c_len, RET_DK), dl_ref[1, h], F32))
    ri = lax.broadcasted_iota(jnp.int32, (c_len, RET_DK), 0).astype(F32)
    qdf_ref[...] = jnp.exp(lgf * (ri + 1.0))
    qdb_ref[...] = jnp.exp(lgb * (c_len - ri))
    kdf_ref[...] = jnp.exp(lgf * (c_len - 1.0 - ri))
    kdb_ref[...] = jnp.exp(lgb * ri)
    rr = lax.broadcasted_iota(jnp.int32, (c_len, c_len), 0).astype(F32)
    cc = lax.broadcasted_iota(jnp.int32, (c_len, c_len), 1).astype(F32)
    diff = rr - cc
    lgf_cc = _log_sigmoid(jnp.full((c_len, c_len), dl_ref[0, h], F32))
    lgb_cc = _log_sigmoid(jnp.full((c_len, c_len), dl_ref[1, h], F32))
    dm_ref[...] = jnp.where(diff >= 0.0, jnp.exp(lgf_cc * jnp.maximum(diff, 0.0)),
                            jnp.exp(lgb_cc * jnp.maximum(-diff, 0.0)))
    chunk_f = jnp.exp(lgf[:1, :1] * float(c_len))
    chunk_b = jnp.exp(lgb[:1, :1] * float(c_len))

    tn_dims = (((0,), (0,)), ((), ()))
    nt_dims = (((1,), (1,)), ((), ()))

    st_ref[...] = jnp.zeros_like(st_ref)
    stb_ref[...] = jnp.zeros_like(stb_ref)

    def state_body(i, carry):
        for c, all_ref, cur_ref, kdec_ref, chunk_dec in ((i, sf_ref, st_ref, kdf_ref, chunk_f),
                                                         (nc - 1 - i, sb_ref, stb_ref, kdb_ref, chunk_b)):
            off = pl.multiple_of(c * c_len, c_len)
            all_ref[c] = cur_ref[...].astype(BF16)
            kc = k_ref[0, pl.ds(off, c_len), :].astype(F32)
            vc = v_ref[0, pl.ds(off, c_len), :]
            kd = (kc * kdec_ref[...]).astype(BF16)
            upd = lax.dot_general(kd, vc, tn_dims, preferred_element_type=F32)
            cur_ref[...] = cur_ref[...] * chunk_dec + upd
        return carry

    lax.fori_loop(0, nc, state_body, 0, unroll=4)

    def out_body(g, carry):
        offs = [pl.multiple_of((g * RET_GROUP + u) * c_len, c_len) for u in range(RET_GROUP)]
        for u, off in enumerate(offs):
            qb = q_ref[0, pl.ds(off, c_len), :]
            kb = k_ref[0, pl.ds(off, c_len), :]
            qc = qb.astype(F32)
            s = lax.dot_general(qb, kb, nt_dims, preferred_element_type=F32)
            p_buf[u] = (s * dm_ref[...]).astype(BF16)
            qf_buf[u] = (qc * qdf_ref[...]).astype(BF16)
            qb_buf[u] = (qc * qdb_ref[...]).astype(BF16)
        for u, off in enumerate(offs):
            c = g * RET_GROUP + u
            out = jnp.dot(p_buf[u], v_ref[0, pl.ds(off, c_len), :], preferred_element_type=F32)
            out = out + jnp.dot(qf_buf[u], sf_ref[c], preferred_element_type=F32)
            out_buf[u] = out + jnp.dot(qb_buf[u], sb_ref[c], preferred_element_type=F32)
        for u, off in enumerate(offs):
            out = out_buf[u]
            mu = jnp.mean(out, axis=-1, keepdims=True)
            cen = out - mu
            var = jnp.mean(cen * cen, axis=-1, keepdims=True)
            y = cen * lax.rsqrt(var + GN_EPS)
            gate = g_ref[0, pl.ds(off, c_len), :].astype(F32)
            o_ref[0, pl.ds(off, c_len), :] = (gate * y).astype(o_ref.dtype)
        return carry

    lax.fori_loop(0, nc // RET_GROUP, out_body, 0)


def _retention(qk3, vn3, gr3, decay_logit):
    b, s, _ = qk3.shape
    nc = s // RET_CHUNK
    return pl.pallas_call(
        functools.partial(_retention_kernel, nc=nc),
        out_shape=jax.ShapeDtypeStruct((b, s, RET_HEADS * RET_DV), BF16),
        grid=(b, RET_HEADS),
        in_specs=[
            pl.BlockSpec(memory_space=pltpu.SMEM),
            pl.BlockSpec((1, s, RET_DK), lambda bi, h: (bi, 0, h)),
            pl.BlockSpec((1, s, RET_DK), lambda bi, h: (bi, 0, RET_HEADS + h)),
            pl.BlockSpec((1, s, RET_DV), lambda bi, h: (bi, 0, h)),
            pl.BlockSpec((1, s, RET_DV), lambda bi, h: (bi, 0, h)),
        ],
        out_specs=pl.BlockSpec((1, s, RET_DV), lambda bi, h: (bi, 0, h)),
        scratch_shapes=[
            pltpu.VMEM((nc, RET_DK, RET_DV), BF16),
            pltpu.VMEM((nc, RET_DK, RET_DV), BF16),
            pltpu.VMEM((RET_DK, RET_DV), F32),
            pltpu.VMEM((RET_DK, RET_DV), F32),
            pltpu.VMEM((RET_CHUNK, RET_CHUNK), F32),
            pltpu.VMEM((RET_CHUNK, RET_DK), F32),
            pltpu.VMEM((RET_CHUNK, RET_DK), F32),
            pltpu.VMEM((RET_CHUNK, RET_DK), F32),
            pltpu.VMEM((RET_CHUNK, RET_DK), F32),
            pltpu.VMEM((RET_GROUP, RET_CHUNK, RET_CHUNK), BF16),
            pltpu.VMEM((RET_GROUP, RET_CHUNK, RET_DK), BF16),
            pltpu.VMEM((RET_GROUP, RET_CHUNK, RET_DK), BF16),
            pltpu.VMEM((RET_GROUP, RET_CHUNK, RET_DV), F32),
        ],
        compiler_params=_cparams(("arbitrary", "arbitrary")),
        name="retention",
    )(decay_logit, qk3, qk3, vn3, gr3)


NA_ROW_OFFS = 2 * NA_KH - 1
NA_COL_OFFS = 2 * NA_KW - 1
NA_PATTERNS = 3


def _na_row_offsets(rows):
    n_tiles = rows // NA_QROWS
    offs = np.full((NA_PATTERNS, NA_QROWS, NA_KROWS), NA_ROW_OFFS, np.int32)
    for p, t in enumerate((0, 1, n_tiles - 1)):
        kstart = int(np.clip(NA_QROWS * t - NA_KH // 2, 0, rows - NA_KROWS))
        for rr in range(NA_QROWS):
            r = NA_QROWS * t + rr
            rs = int(np.clip(r - NA_KH // 2, 0, rows - NA_KH))
            for i in range(NA_KROWS):
                krow = kstart + i
                if rs <= krow < rs + NA_KH:
                    offs[p, rr, i] = krow - r + (NA_KH - 1)
    return offs


def _na_build_bias(rpb_ref, e_ref, bias_ref, head0, rows):
    wide = 2 * GRID_W
    lane = lax.broadcasted_iota(jnp.int32, (GRID_W, wide), 1)
    col = lax.broadcasted_iota(jnp.int32, (GRID_W, wide), 0)
    kcol = jnp.where(lane < GRID_W, lane, lane - GRID_W)
    cstart = jnp.clip(col - NA_KW // 2, 0, GRID_W - NA_KW)
    col_ok = (kcol >= cstart) & (kcol < cstart + NA_KW)
    coff = kcol - col + (NA_KW - 1)
    neg = jnp.full((GRID_W, wide), NA_NEG, F32)
    left = lane < GRID_W
    offs = _na_row_offsets(rows)
    for hh in range(2):
        base = (head0 + hh) * (NA_ROW_OFFS * NA_COL_OFFS)
        for ro in range(NA_ROW_OFFS):
            acc = neg
            for d in range(NA_COL_OFFS):
                acc = jnp.where(coff == d, rpb_ref[base + ro * NA_COL_OFFS + d] * LOG2E, acc)
            e_ref[hh, ro] = jnp.where(col_ok, acc, neg)
        e_ref[hh, NA_ROW_OFFS] = neg
        for p in range(NA_PATTERNS):
            for rr in range(NA_QROWS):
                for ip in range(NA_KROWS // 2):
                    tile = jnp.where(left, e_ref[hh, int(offs[p, rr, 2 * ip])], e_ref[hh, int(offs[p, rr, 2 * ip + 1])])
                    bias_ref[hh, p, rr * GRID_W:(rr + 1) * GRID_W, ip * wide:(ip + 1) * wide] = tile


NA_GROUP = 8


def _na_kernel(rpb_ref, q_ref, k_ref, v_ref, o_ref, bias_ref, e_ref, s_buf, p_buf, *, n_tiles, rows):
    nq = NA_QROWS * GRID_W
    nk = NA_KROWS * GRID_W
    nt_dims = (((1,), (1,)), ((), ()))
    first = lax.broadcasted_iota(jnp.int32, (nq, 2 * NA_HD), 1) < NA_HD
    kfirst = lax.broadcasted_iota(jnp.int32, (nk, 2 * NA_HD), 1) < NA_HD

    @pl.when(pl.program_id(1) == 0)
    def _():
        _na_build_bias(rpb_ref, e_ref, bias_ref, 2 * pl.program_id(0), rows)

    def body(g, carry):
        offs = []
        for u in range(NA_GROUP):
            t = g * NA_GROUP + u
            qoff = pl.multiple_of(t * nq, nq)
            krow0 = jnp.clip(NA_QROWS * t - NA_KH // 2, 0, rows - NA_KROWS)
            koff = pl.multiple_of(krow0 * GRID_W, nq)
            pat = jnp.where(t == 0, 0, jnp.where(t == n_tiles - 1, 2, 1))
            offs.append((qoff, koff, pat))
        for u, (qoff, koff, pat) in enumerate(offs):
            q2 = q_ref[0, pl.ds(qoff, nq), :]
            k2 = k_ref[0, pl.ds(koff, nk), :]
            for hh in range(2):
                qm = jnp.where(first if hh == 0 else jnp.logical_not(first), q2, jnp.zeros_like(q2))
                s_buf[2 * u + hh] = lax.dot_general(qm, k2, nt_dims, preferred_element_type=F32) + bias_ref[hh, pat]
        for n in range(2 * NA_GROUP):
            s = s_buf[n]
            p_buf[n] = jnp.exp2(s - jnp.max(s, axis=-1, keepdims=True)).astype(BF16)
        for u, (qoff, koff, pat) in enumerate(offs):
            v2 = v_ref[0, pl.ds(koff, nk), :]
            outs = []
            for hh in range(2):
                vv = jnp.where(kfirst if hh == 0 else jnp.logical_not(kfirst), v2, jnp.ones_like(v2))
                o = jnp.dot(p_buf[2 * u + hh], vv, preferred_element_type=F32)
                outs.append(o / pltpu.roll(o, NA_HD, axis=1))
            o_ref[0, pl.ds(qoff, nq), :] = jnp.where(first, outs[0], outs[1]).astype(o_ref.dtype)
        return carry

    lax.fori_loop(0, n_tiles // NA_GROUP, body, 0)


def _na(vn3, rpb_flat):
    b, s, _ = vn3.shape
    rows = s // GRID_W
    n_tiles = rows // NA_QROWS
    w2 = 2 * NA_HD
    nq = NA_QROWS * GRID_W
    nk = NA_KROWS * GRID_W
    return pl.pallas_call(
        functools.partial(_na_kernel, n_tiles=n_tiles, rows=rows),
        out_shape=jax.ShapeDtypeStruct((b, s, NA_HEADS * NA_HD), BF16),
        grid=(NA_HEADS // 2, b),
        in_specs=[
            pl.BlockSpec(memory_space=pltpu.SMEM),
            pl.BlockSpec((1, s, w2), lambda hp, bi: (bi, 0, VN_Q // w2 + hp)),
            pl.BlockSpec((1, s, w2), lambda hp, bi: (bi, 0, VN_K // w2 + hp)),
            pl.BlockSpec((1, s, w2), lambda hp, bi: (bi, 0, VN_V // w2 + hp)),
        ],
        out_specs=pl.BlockSpec((1, s, w2), lambda hp, bi: (bi, 0, hp)),
        scratch_shapes=[
            pltpu.VMEM((2, NA_PATTERNS, nq, nk), F32),
            pltpu.VMEM((2, NA_ROW_OFFS + 1, GRID_W, 2 * GRID_W), F32),
            pltpu.VMEM((2 * NA_GROUP, nq, nk), F32),
            pltpu.VMEM((2 * NA_GROUP, nq, nk), BF16),
        ],
        compiler_params=_cparams(("arbitrary", "arbitrary")),
        name="natten",
    )(rpb_flat, vn3, vn3, vn3)


MIX_BM = 512
MIX_SUB = 256


def _layernorm_rows(z, g, b):
    mu = jnp.mean(z, axis=-1, keepdims=True)
    cen = z - mu
    var = jnp.mean(cen * cen, axis=-1, keepdims=True)
    return cen * lax.rsqrt(var + LN_EPS) * g + b


def _mixout_kernel(ret_ref, na_ref, g1_ref, g2_ref, x_ref, wr32_ref, wn32_ref, wo32_ref, lg_ref, lb_ref, o_ref,
                   wr_ref, wn_ref, wo_ref):
    @pl.when(pl.program_id(0) == 0)
    def _():
        wr_ref[...] = wr32_ref[0].astype(BF16)
        wn_ref[...] = wn32_ref[0].astype(BF16)
        wo_ref[...] = wo32_ref[0].astype(BF16)

    for r in range(0, MIX_BM, MIX_SUB):
        rows = slice(r, r + MIX_SUB)
        y_ret = jnp.dot(ret_ref[rows, :], wr_ref[...], preferred_element_type=F32)
        y_na = jnp.dot(na_ref[rows, :], wn_ref[...], preferred_element_type=F32)
        merged = g1_ref[rows, :].astype(F32) * y_ret + g2_ref[rows, :].astype(F32) * y_na
        mix = jnp.dot(merged.astype(BF16), wo_ref[...], preferred_element_type=F32)
        z = DEEPNORM_ALPHA * x_ref[rows, :] + mix
        o_ref[rows, :] = _layernorm_rows(z, lg_ref[...], lb_ref[...])


def _mixout(ret2d, na2d, gates2d, x2d, wr, wn, wo, layer, lg, lb):
    t = x2d.shape[0]
    const = lambda i: (0, 0)
    resident = lambda rows: pl.BlockSpec((1, rows, D_MODEL), lambda i: (layer, 0, 0), pipeline_mode=pl.Buffered(1))
    return pl.pallas_call(
        _mixout_kernel,
        out_shape=jax.ShapeDtypeStruct((t, D_MODEL), F32),
        grid=(t // MIX_BM,),
        in_specs=[
            pl.BlockSpec((MIX_BM, RET_HEADS * RET_DV), lambda i: (i, 0)),
            pl.BlockSpec((MIX_BM, NA_HEADS * NA_HD), lambda i: (i, 0)),
            pl.BlockSpec((MIX_BM, D_MODEL), lambda i: (i, 0)),
            pl.BlockSpec((MIX_BM, D_MODEL), lambda i: (i, 1)),
            pl.BlockSpec((MIX_BM, D_MODEL), lambda i: (i, 0)),
            resident(RET_HEADS * RET_DV),
            resident(NA_HEADS * NA_HD),
            resident(D_MODEL),
            pl.BlockSpec((1, D_MODEL), const),
            pl.BlockSpec((1, D_MODEL), const),
        ],
        out_specs=pl.BlockSpec((MIX_BM, D_MODEL), lambda i: (i, 0)),
        scratch_shapes=[
            pltpu.VMEM((RET_HEADS * RET_DV, D_MODEL), BF16),
            pltpu.VMEM((NA_HEADS * NA_HD, D_MODEL), BF16),
            pltpu.VMEM((D_MODEL, D_MODEL), BF16),
        ],
        compiler_params=_cparams(("arbitrary",)),
        name="mixout",
    )(ret2d, na2d, gates2d, gates2d, x2d, wr, wn, wo, lg, lb)


ROUTE_BM = 512


def _route_tile(x, rw, rb):
    nt_dims = (((1,), (1,)), ((), ()))
    x_hi = x.astype(BF16)
    x_lo = (x - x_hi.astype(F32)).astype(BF16)
    rw_hi = rw.astype(BF16)
    rw_lo = (rw - rw_hi.astype(F32)).astype(BF16)
    both = lax.dot_general(jnp.concatenate([rw_hi, rw_lo], axis=0), x_hi, nt_dims, preferred_element_type=F32)
    logits = (both[:N_EXPERTS] + both[N_EXPERTS:]
              + lax.dot_general(rw_hi, x_lo, nt_dims, preferred_element_type=F32))
    scores = jax.nn.sigmoid(logits)
    sel = scores + rb
    p = [scores[m * N_GROUPS:(m + 1) * N_GROUPS] for m in range(EXPERTS_PER_GROUP)]
    s = [sel[m * N_GROUPS:(m + 1) * N_GROUPS] for m in range(EXPERTS_PER_GROUP)]
    one = jnp.ones_like(s[0])
    zero = jnp.zeros_like(s[0])
    chosen = []
    for m in range(EXPERTS_PER_GROUP):
        rank = zero
        for j in range(EXPERTS_PER_GROUP):
            if j == m:
                continue
            beats = (s[j] >= s[m]) if j < m else (s[j] > s[m])
            rank = rank + jnp.where(beats, one, zero)
        chosen.append(rank < 2.0)
    group_score = zero
    for m in range(EXPERTS_PER_GROUP):
        group_score = group_score + jnp.where(chosen[m], s[m], zero)
    gid = lax.broadcasted_iota(jnp.int32, group_score.shape, 0)
    gmax = jnp.max(group_score, axis=0, keepdims=True)
    gbest = jnp.min(jnp.where(group_score == gmax, gid, N_GROUPS), axis=0, keepdims=True)
    in_best = gid == gbest
    picked = [chosen[m] & in_best for m in range(EXPERTS_PER_GROUP)]
    before = zero
    wa = zero
    wb = zero
    ma = zero
    mb = zero
    for m in range(EXPERTS_PER_GROUP):
        is_a = picked[m] & (before == 0.0)
        is_b = picked[m] & (before == 1.0)
        wa = wa + jnp.where(is_a, p[m], zero)
        wb = wb + jnp.where(is_b, p[m], zero)
        ma = ma + jnp.where(is_a, float(m), 0.0)
        mb = mb + jnp.where(is_b, float(m), 0.0)
        before = before + jnp.where(chosen[m], one, zero)
    wa = jnp.sum(wa, axis=0, keepdims=True)
    wb = jnp.sum(wb, axis=0, keepdims=True)
    ma = jnp.sum(ma, axis=0, keepdims=True)
    mb = jnp.sum(mb, axis=0, keepdims=True)
    denom = wa + wb
    w_lo = wa / denom
    w_hi = wb / denom
    pair = jnp.where(ma == 0.0, jnp.where(mb == 1.0, 0.0, mb),
                     jnp.where(ma == 1.0, jnp.where(mb == 2.0, 1.0, 4.0), 5.0))
    keep = pair == 0.0
    return (gbest * N_PAIRS + pair.astype(jnp.int32),
            jnp.where(keep, w_lo, w_hi), jnp.where(keep, w_hi, w_lo))


CLASS_ROWS = 64
META_LANES = 128


DISPATCH_X = D_MODEL // 2
DISPATCH_W = DISPATCH_X + META_LANES


def _router_kernel(x_ref, rw_ref, rb_ref, tri_ref, dest_ref, xw_ref, meta_ref, cls_s, rank_s, cnt_s):
    i = pl.program_id(0)
    nt = pl.num_programs(0) - 1
    bm = ROUTE_BM
    reps = bm // META_LANES
    cid = lax.broadcasted_iota(jnp.int32, (CLASS_ROWS, bm), 0)

    @pl.when(i == 0)
    def _():
        cnt_s[...] = jnp.zeros_like(cnt_s)

    @pl.when(i < nt)
    def _():
        x = x_ref[...]
        cls, w_a, w_b = _route_tile(x, rw_ref[...], rb_ref[...])
        cls_s[i] = cls
        w_rows = jnp.concatenate([w_a, w_b, jnp.zeros((META_LANES - 2, bm), F32)], axis=0)
        xw_ref[:, :DISPATCH_X] = _pack_bf16_pair(x[:, :DISPATCH_X], x[:, DISPATCH_X:])
        xw_ref[:, DISPATCH_X:] = lax.bitcast_convert_type(w_rows.T, U32)
        onehot = (cid == cls).astype(BF16)
        before = jnp.dot(onehot, tri_ref[...], preferred_element_type=F32)
        carry = jnp.concatenate([cnt_s[...]] * reps, axis=1)
        rank = jnp.sum(jnp.where(cid == cls, before + carry, 0.0), axis=0, keepdims=True)
        rank_s[i] = rank.astype(jnp.int32)
        cnt_s[...] = cnt_s[...] + jnp.dot(onehot, jnp.ones((bm, META_LANES), BF16), preferred_element_type=F32)

    @pl.when(i == nt)
    def _():
        cnt = cnt_s[...]
        nblk = jnp.floor((cnt + (MOE_BLOCK - 1.0)) * (1.0 / MOE_BLOCK))
        rr = lax.broadcasted_iota(jnp.int32, (CLASS_ROWS, CLASS_ROWS), 0)
        cc = lax.broadcasted_iota(jnp.int32, (CLASS_ROWS, CLASS_ROWS), 1)
        lower = (cc < rr).astype(BF16)
        start_blk = jnp.dot(lower, nblk.astype(BF16), preferred_element_type=F32)
        start_t = jnp.concatenate([start_blk] * reps, axis=1)

        def dest_tile(t, carry):
            start = jnp.sum(jnp.where(cid == cls_s[t], start_t, 0.0), axis=0, keepdims=True)
            dest_ref[t] = (start * float(MOE_BLOCK)).astype(jnp.int32) + rank_s[t]
            return carry

        lax.fori_loop(0, nt, dest_tile, 0)

        end_blk = start_blk + nblk
        n_used = jnp.max(end_blk, axis=0, keepdims=True)
        blk = lax.broadcasted_iota(jnp.int32, (CLASS_ROWS, META_LANES), 1).astype(F32)
        blk = jnp.minimum(blk, n_used - 1.0)
        bcls = jnp.sum(jnp.where(end_blk <= blk, 1.0, 0.0), axis=0, keepdims=True)
        grp = jnp.zeros_like(bcls)
        for g in range(1, N_GROUPS):
            grp = grp + jnp.where(bcls >= float(g * N_PAIRS), 1.0, 0.0)
        pair = bcls - grp * float(N_PAIRS)
        slot_a = jnp.zeros_like(pair)
        slot_b = jnp.zeros_like(pair)
        for k in range(N_PAIRS):
            slot_a = jnp.where(pair == float(k), float(PAIR_SLOT_A[k]), slot_a)
            slot_b = jnp.where(pair == float(k), float(PAIR_SLOT_B[k]), slot_b)
        rows = [grp * float(EXPERTS_PER_GROUP) + slot_a, grp * float(EXPERTS_PER_GROUP) + slot_b, n_used]
        rows = rows + [jnp.zeros_like(bcls)] * (8 - len(rows))
        meta_ref[...] = jnp.concatenate(rows, axis=0).astype(jnp.int32)


def _router(x2d, rw_t, rb_t):
    t = x2d.shape[0]
    nt = t // ROUTE_BM
    tri = jnp.asarray(np.triu(np.ones((ROUTE_BM, ROUTE_BM), np.float32), 1), BF16)
    tile = lambda i: (jnp.minimum(i, nt - 1), 0)
    const = lambda i: (0, 0)
    return pl.pallas_call(
        _router_kernel,
        out_shape=(jax.ShapeDtypeStruct((nt, 1, ROUTE_BM), jnp.int32),
                   jax.ShapeDtypeStruct((t, DISPATCH_W), U32),
                   jax.ShapeDtypeStruct((8, META_LANES), jnp.int32)),
        grid=(nt + 1,),
        in_specs=[
            pl.BlockSpec((ROUTE_BM, D_MODEL), tile),
            pl.BlockSpec((N_EXPERTS, D_MODEL), const),
            pl.BlockSpec((N_EXPERTS, 1), const),
            pl.BlockSpec((ROUTE_BM, ROUTE_BM), const),
        ],
        out_specs=(pl.BlockSpec((nt, 1, ROUTE_BM), lambda i: (0, 0, 0)),
                   pl.BlockSpec((ROUTE_BM, DISPATCH_W), tile),
                   pl.BlockSpec((8, META_LANES), const)),
        scratch_shapes=[
            pltpu.VMEM((nt, 1, ROUTE_BM), jnp.int32),
            pltpu.VMEM((nt, 1, ROUTE_BM), jnp.int32),
            pltpu.VMEM((CLASS_ROWS, META_LANES), F32),
        ],
        compiler_params=_cparams(("arbitrary",)),
        name="router",
    )(x2d, rw_t, rb_t, tri)


ROW_BM = 512


def _start_rows(make_copy):
    for r in range(ROW_BM):
        make_copy(r).start(priority=r % 2)


def _dispatch_kernel(dest_ref, x_ref, zeros_ref, xs_ref, sem):
    del zeros_ref
    base = pl.program_id(0) * ROW_BM
    _start_rows(lambda r: pltpu.make_async_copy(x_ref.at[pl.ds(r, 1), :],
                                                xs_ref.at[pl.ds(dest_ref[base + r], 1), :], sem))
    pltpu.make_async_copy(x_ref, xs_ref.at[pl.ds(0, ROW_BM), :], sem).wait()


def _dispatch(dest, x_rows, n_rows):
    t, width = x_rows.shape
    grid_spec = pltpu.PrefetchScalarGridSpec(
        num_scalar_prefetch=1,
        grid=(t // ROW_BM,),
        in_specs=[
            pl.BlockSpec((ROW_BM, width), lambda i, dest: (i, 0)),
            pl.BlockSpec(memory_space=pl.ANY),
        ],
        out_specs=pl.BlockSpec(memory_space=pl.ANY),
        scratch_shapes=[pltpu.SemaphoreType.DMA],
    )
    return pl.pallas_call(
        _dispatch_kernel,
        out_shape=jax.ShapeDtypeStruct((n_rows, width), x_rows.dtype),
        grid_spec=grid_spec,
        input_output_aliases={2: 0},
        compiler_params=_cparams(("arbitrary",)),
        name="dispatch",
    )(dest, x_rows, jnp.zeros((n_rows, width), x_rows.dtype))


EXPERT_RING = 3
EXPERT_AHEAD = EXPERT_RING - 1


def _experts_kernel(ea_ref, eb_ref, nb_ref, x_ref, wg_hbm, wu_hbm, wd_hbm, o_ref,
                    ring_g, ring_u, ring_d, sems, loads_ref, hg_buf, hu_buf, act_buf, *, layer, n_blocks):
    j = pl.program_id(0)
    nb = nb_ref[0]
    experts_of = (ea_ref, eb_ref)

    def is_load(s, step):
        step = jnp.minimum(step, n_blocks - 1)
        return ((step == 0) | (experts_of[s][step] != experts_of[s][jnp.maximum(step - 1, 0)])).astype(jnp.int32)

    def weight_copies(s, step, buf):
        e = experts_of[s][jnp.minimum(step, n_blocks - 1)]
        return (pltpu.make_async_copy(wg_hbm.at[layer, e], ring_g.at[s, buf], sems.at[s, buf]),
                pltpu.make_async_copy(wu_hbm.at[layer, e], ring_u.at[s, buf], sems.at[s, buf]),
                pltpu.make_async_copy(wd_hbm.at[layer, e], ring_d.at[s, buf], sems.at[s, buf]))

    def start(s, step, load_number):
        for c in weight_copies(s, step, lax.rem(load_number - 1, EXPERT_RING)):
            c.start()

    @pl.when(j < nb)
    def _():
        bufs = []
        for s in range(2):
            loads_now = jnp.where(j == 0, 0, loads_ref[s]) + is_load(s, j)
            loads_ref[s] = loads_now
            buf = lax.rem(loads_now - 1, EXPERT_RING)
            bufs.append(buf)

            for first in range(EXPERT_AHEAD):
                ahead_loads = 1 + sum(is_load(s, k) for k in range(1, first + 1))

                @pl.when((j == 0) & (first < nb) & (is_load(s, first) == 1))
                def _(first=first, ahead_loads=ahead_loads):
                    start(s, first, ahead_loads)

            ahead = j + EXPERT_AHEAD
            ahead_loads = loads_now + sum(is_load(s, j + k) for k in range(1, EXPERT_AHEAD + 1))

            @pl.when((ahead < nb) & (is_load(s, ahead) == 1))
            def _(ahead=ahead, ahead_loads=ahead_loads):
                start(s, ahead, ahead_loads)

            @pl.when(is_load(s, j) == 1)
            def _(buf=buf):
                for c in weight_copies(s, j, buf):
                    c.wait()

        x = jnp.concatenate(_unpack_bf16_pair(x_ref[:, :DISPATCH_X]), axis=1).astype(BF16)
        gate_w = lax.bitcast_convert_type(x_ref[:, DISPATCH_X:], F32)
        for s in range(2):
            hg_buf[s] = jnp.dot(x, ring_g[s, bufs[s]].astype(BF16), preferred_element_type=F32)
            hu_buf[s] = jnp.dot(x, ring_u[s, bufs[s]].astype(BF16), preferred_element_type=F32)
        for s in range(2):
            hg = hg_buf[s]
            act_buf[s] = (hg * _sigmoid(hg) * hu_buf[s]).astype(BF16)
        ys = [jnp.dot(act_buf[s], ring_d[s, bufs[s]].astype(BF16), preferred_element_type=F32) for s in range(2)]
        y = gate_w[:, 0:1] * ys[0] + gate_w[:, 1:2] * ys[1]
        o_ref[...] = _pack_bf16_pair(y[:, :DISPATCH_X], y[:, DISPATCH_X:])

    @pl.when(j >= nb)
    def _():
        o_ref[...] = jnp.zeros_like(o_ref)


def _experts(blk_a, blk_b, n_used, xs, wg, wu, wd, layer):
    n_rows = xs.shape[0]
    n_blocks = n_rows // MOE_BLOCK
    in_row_map = lambda j, ea, eb, nb: (jnp.maximum(jnp.minimum(j, nb[0] - 1), 0), 0)
    row_map = lambda j, ea, eb, nb: (j, 0)
    grid_spec = pltpu.PrefetchScalarGridSpec(
        num_scalar_prefetch=3,
        grid=(n_blocks,),
        in_specs=[
            pl.BlockSpec((MOE_BLOCK, DISPATCH_W), in_row_map),
            pl.BlockSpec(memory_space=pl.ANY),
            pl.BlockSpec(memory_space=pl.ANY),
            pl.BlockSpec(memory_space=pl.ANY),
        ],
        out_specs=pl.BlockSpec((MOE_BLOCK, DISPATCH_X), row_map),
        scratch_shapes=[
            pltpu.VMEM((2, EXPERT_RING, D_MODEL, D_EXPERT), F32),
            pltpu.VMEM((2, EXPERT_RING, D_MODEL, D_EXPERT), F32),
            pltpu.VMEM((2, EXPERT_RING, D_EXPERT, D_MODEL), F32),
            pltpu.SemaphoreType.DMA((2, EXPERT_RING)),
            pltpu.SMEM((2,), jnp.int32),
            pltpu.VMEM((2, MOE_BLOCK, D_EXPERT), F32),
            pltpu.VMEM((2, MOE_BLOCK, D_EXPERT), F32),
            pltpu.VMEM((2, MOE_BLOCK, D_EXPERT), BF16),
        ],
    )
    return pl.pallas_call(
        functools.partial(_experts_kernel, layer=layer, n_blocks=n_blocks),
        out_shape=jax.ShapeDtypeStruct((n_rows, DISPATCH_X), U32),
        grid_spec=grid_spec,
        compiler_params=_cparams(("arbitrary",)),
        name="experts",
    )(blk_a, blk_b, n_used, xs, wg, wu, wd)


COMBINE_CHUNK = 64


def _combine_ln_kernel(dest_ref, x_ref, lg_ref, lb_ref, ys_ref, o_ref, obf_ref, ybuf, sem):
    i = pl.program_id(0)
    n = pl.num_programs(0)
    slot = i % 2

    def row_copy(tile, to_slot, r):
        return pltpu.make_async_copy(ys_ref.at[pl.ds(dest_ref[tile * ROW_BM + r], 1), :],
                                     ybuf.at[to_slot, pl.ds(r, 1), :], sem.at[to_slot])

    def wait_tile(of_slot):
        pltpu.make_async_copy(ys_ref.at[pl.ds(0, ROW_BM), :], ybuf.at[of_slot], sem.at[of_slot]).wait()

    @pl.when(i == 0)
    def _():
        _start_rows(lambda r: row_copy(0, 0, r))

    wait_tile(slot)
    nxt = lax.rem(i + 1, n)
    for c in range(0, ROW_BM, COMBINE_CHUNK):
        rows = slice(c, c + COMBINE_CHUNK)
        y = jnp.concatenate(_unpack_bf16_pair(ybuf[slot, rows, :]), axis=1)
        z = DEEPNORM_ALPHA * x_ref[rows, :] + y
        out = _layernorm_rows(z, lg_ref[...], lb_ref[...])
        o_ref[rows, :] = out
        obf_ref[rows, :] = out.astype(BF16)
        for r in range(c, c + COMBINE_CHUNK):
            row_copy(nxt, 1 - slot, r).start(priority=r % 2)

    @pl.when(i == n - 1)
    def _():
        wait_tile(1 - slot)


def _combine_ln(dest, x2d, ys, lg, lb):
    t = x2d.shape[0]
    const = lambda i, dest: (0, 0)
    grid_spec = pltpu.PrefetchScalarGridSpec(
        num_scalar_prefetch=1,
        grid=(t // ROW_BM,),
        in_specs=[
            pl.BlockSpec((ROW_BM, D_MODEL), lambda i, dest: (i, 0)),
            pl.BlockSpec((1, D_MODEL), const),
            pl.BlockSpec((1, D_MODEL), const),
            pl.BlockSpec(memory_space=pl.ANY),
        ],
        out_specs=(pl.BlockSpec((ROW_BM, D_MODEL), lambda i, dest: (i, 0)),
                   pl.BlockSpec((ROW_BM, D_MODEL), lambda i, dest: (i, 0))),
        scratch_shapes=[pltpu.VMEM((2, ROW_BM, DISPATCH_X), U32), pltpu.SemaphoreType.DMA((2,))],
    )
    return pl.pallas_call(
        _combine_ln_kernel,
        out_shape=(jax.ShapeDtypeStruct((t, D_MODEL), F32), jax.ShapeDtypeStruct((t, D_MODEL), BF16)),
        grid_spec=grid_spec,
        compiler_params=_cparams(("arbitrary",)),
        name="combine_ln",
    )(dest, x2d, lg, lb, ys)


def _moe(x2d, rw_t, rb_t, wg, wu, wd, layer, lg, lb):
    t = x2d.shape[0]
    n_blocks = (t + N_CLASSES * (MOE_BLOCK - 1)) // MOE_BLOCK
    assert n_blocks <= META_LANES
    dest3, xw, meta = _router(x2d, rw_t, rb_t)
    dest = dest3.reshape(t)
    xs = _dispatch(dest, xw, n_blocks * MOE_BLOCK)
    ys = _experts(meta[0, :n_blocks], meta[1, :n_blocks], meta[2, :1], xs, wg, wu, wd, layer)
    return _combine_ln(dest, x2d, ys, lg, lb)


def _rope_tables(seq):
    half = RET_DK // 2
    pos = jnp.arange(seq, dtype=F32)
    freqs = ROPE_BASE ** (-jnp.arange(half, dtype=F32) / half)
    ang = pos[:, None] * freqs[None, :]
    return jnp.cos(ang), jnp.sin(ang)


def kernel(x, w_in, ret_decay_logit, w_ret_o, na_rpb, w_na_o, w_out, ln_mix_g, ln_mix_b, router_w, router_bias,
           w_exp_gate, w_exp_up, w_exp_down, ln_ffn_g, ln_ffn_b):
    b, s, d = x.shape
    depth = w_in.shape[0]
    t = b * s
    rows = s // GRID_W
    cos, sin = _rope_tables(s)
    rw_t = router_w.astype(F32).T.reshape(N_GROUPS, EXPERTS_PER_GROUP, d).transpose(1, 0, 2).reshape(N_EXPERTS, d)
    rb_t = router_bias.astype(F32).reshape(N_GROUPS, EXPERTS_PER_GROUP).T.reshape(N_EXPERTS, 1)
    x2d = x.reshape(t, d)
    x_bf = x2d.astype(BF16)
    w_in = w_in.astype(F32)
    for l in range(depth):
        qk = _inproj(x_bf, w_in, l, cos, sin, s, "rotary").reshape(b, s, -1)
        vn = _inproj(x_bf, w_in, l, cos, sin, s, "scale").reshape(b, s, -1)
        gr = _inproj(x_bf, w_in, l, cos, sin, s, "silu").reshape(b, s, -1)
        gates = _inproj(x_bf, w_in, l, cos, sin, s, "sigmoid")
        ret = _retention(qk, vn, gr, ret_decay_logit[l].astype(F32))
        na = _na(vn, na_rpb[l].astype(F32).reshape(-1))
        x2d = _mixout(ret.reshape(t, -1), na.reshape(t, -1), gates, x2d,
                      w_ret_o.astype(F32), w_na_o.astype(F32), w_out.astype(F32), l,
                      ln_mix_g[l].reshape(1, d).astype(F32), ln_mix_b[l].reshape(1, d).astype(F32))
        x2d, x_bf = _moe(x2d, rw_t, rb_t, w_exp_gate, w_exp_up, w_exp_down, l,
                         ln_ffn_g[l].reshape(1, d).astype(F32), ln_ffn_b[l].reshape(1, d).astype(F32))
    return x2d.reshape(b, s, d)
```
